```python
import jax, jax.numpy as jnp
from jax import lax
import numpy as np

D_MODEL = 1024
BATCH = 32
SEQ = 256
DEPTH = 2
DEC_BATCH = 4
DEC_SEQ = 1024
PAST_LEN = 512

GRID_W = 64
EPS = 1e-6
H_RET = 4
DK_RET = 64
DV_RET = 128
RET_W = H_RET * DV_RET
RET_CHUNK = 64
H_MLA = 8
Q_LORA = 384
KV_LORA = 256
D_NOPE = 64
D_ROPE = 32
D_VMLA = 64
MLA_W = H_MLA * D_VMLA
ROPE_BASE = 10000.0
Q_BLOCK = 128
F_GROUPS = 4
F_GROUP_W = 128
FOURIER_W = F_GROUPS * F_GROUP_W
N_BRANCH = 3
BRANCH_W = 512
SPLITS = [H_RET * DK_RET, H_RET * DK_RET, RET_W, RET_W,
          Q_LORA, KV_LORA, D_ROPE, MLA_W,
          FOURIER_W, FOURIER_W,
          N_BRANCH * D_MODEL]
IN_W = 256 + 256 + 512 + 512 + 384 + 256 + 32 + 512 + 512 + 512 + 3 * D_MODEL

kernel_name = "hybrid_diffusion_retention_mla_fourier_step"


def rms_norm(x, g):
    xf = x.astype(jnp.float32)
    y = xf * lax.rsqrt(jnp.mean(xf * xf, axis=-1, keepdims=True) + EPS)
    return (y * g.astype(jnp.float32)).astype(x.dtype)


def head_norm(o):
    of = o.astype(jnp.float32)
    mu = jnp.mean(of, axis=-1, keepdims=True)
    var = jnp.mean((of - mu) ** 2, axis=-1, keepdims=True)
    return ((of - mu) * lax.rsqrt(var + EPS)).astype(o.dtype)


def axial_rope(x):
    T = x.shape[1]
    rows = T // GRID_W
    row = jnp.repeat(jnp.arange(rows), GRID_W)
    col = jnp.tile(jnp.arange(GRID_W), rows)
    half = D_ROPE // 2
    nfreq = half // 2
    inv = ROPE_BASE ** (-jnp.arange(nfreq, dtype=jnp.float32) / nfreq)

    def rot(xa, pos):
        ang = pos.astype(jnp.float32)[:, None] * inv[None, :]
        cos = jnp.cos(ang)[None, :, None, :]
        sin = jnp.sin(ang)[None, :, None, :]
        x1, x2 = xa[..., :nfreq], xa[..., nfreq:]
        return jnp.concatenate([x1 * cos - x2 * sin, x1 * sin + x2 * cos], axis=-1)

    xf = x.astype(jnp.float32)
    out = jnp.concatenate([rot(xf[..., :half], row), rot(xf[..., half:], col)], axis=-1)
    return out.astype(x.dtype)


def retention_dir(q, k, v, log_gamma, s0):
    B, T, H, dk = q.shape
    dv = v.shape[-1]
    C = RET_CHUNK
    n = T // C
    dt = q.dtype
    lg = log_gamma.astype(jnp.float32)
    i = jnp.arange(C, dtype=jnp.float32)
    diff = i[:, None] - i[None, :]
    dmask = jnp.where(diff[None] >= 0,
                      jnp.exp(jnp.maximum(diff, 0.0)[None] * lg[:, None, None]), 0.0).astype(dt)
    xi = jnp.exp((i[:, None] + 1.0) * lg[None, :]).astype(dt)
    zeta = jnp.exp((C - 1.0 - i)[:, None] * lg[None, :]).astype(dt)
    chunk_decay = jnp.exp(C * lg).astype(dt)
    qc = q.reshape(B, n, C, H, dk)
    kc = k.reshape(B, n, C, H, dk)
    vc = v.reshape(B, n, C, H, dv)
    scores = jnp.einsum('bnihd,bnjhd->bnhij', qc, kc) * dmask[None, None]
    o_intra = jnp.einsum('bnhij,bnjhe->bnihe', scores, vc)
    kv = jnp.einsum('bnjhd,jh,bnjhe->nbhde', kc, zeta, vc)

    def step(s, kv_n):
        return s * chunk_decay[None, :, None, None] + kv_n, s

    s_final, s_before = lax.scan(step, s0.astype(dt), kv)
    o_cross = jnp.einsum('bnihd,nbhde->bnihe', qc, s_before) * xi[None, None, :, :, None]
    return (o_intra + o_cross).reshape(B, T, H, dv), s_final


def mla_attend(q_nope, q_rope, k_nope, k_rope, v):
    B, Tq, H, _ = q_nope.shape
    nb = Tq // Q_BLOCK
    scale = (D_NOPE + D_ROPE) ** -0.5

    def to_blocks(t):
        return t.reshape(B, nb, Q_BLOCK, H, t.shape[-1]).transpose(1, 0, 2, 3, 4)

    def block(qs):
        qn, qr = qs
        s = jnp.einsum('bqhd,bkhd->bhqk', qn, k_nope) + jnp.einsum('bqhd,bkd->bhqk', qr, k_rope)
        p = jax.nn.softmax(s.astype(jnp.float32) * scale, axis=-1).astype(v.dtype)
        return jnp.einsum('bhqk,bkhd->bqhd', p, v)

    o = lax.map(block, (to_blocks(q_nope), to_blocks(q_rope)))
    return o.transpose(1, 0, 2, 3, 4).reshape(B, Tq, H, v.shape[-1])


def fourier_mix(u):
    B, T, _ = u.shape
    ug = u.astype(jnp.float32).reshape(B, T, F_GROUPS, F_GROUP_W)
    f = jnp.fft.fft2(ug, axes=(1, 3)).real * ((T * F_GROUP_W) ** -0.5)
    return f.reshape(B, T, FOURIER_W).astype(u.dtype)


def layer(x, cvec, ctx, norm_g, w_mod, b_mod, w_in, ret_logit, q_norm_g, w_q_up,
          kv_norm_g, w_kv_up, w_branch, w_out):
    B, T, _ = x.shape
    mod = jax.nn.silu(cvec) @ w_mod + b_mod
    shift, scale, gate = jnp.split(mod[:, None, :], 3, axis=-1)
    h = rms_norm(x, norm_g) * (1 + scale) + shift
    split_at = np.cumsum(SPLITS)[:-1].tolist()
    rq, rk, rv, rz, q_lat, kv_lat, k_r, mz, fu, fz, gl = jnp.split(h @ w_in, split_at, axis=-1)

    rq = rq.reshape(B, T, H_RET, DK_RET)
    rk = rk.reshape(B, T, H_RET, DK_RET) * (DK_RET ** -0.5)
    rv = rv.reshape(B, T, H_RET, DV_RET)
    log_g = jax.nn.log_sigmoid(ret_logit.astype(jnp.float32))
    if ctx is None:
        s0f = jnp.zeros((B, H_RET, DK_RET, DV_RET), x.dtype)
        s0b = s0f
    else:
        ctx_ckv, ctx_kr, ctx_ret = ctx
        s0f, s0b = ctx_ret[:, 0], ctx_ret[:, 1]
    o_fw, s_f = retention_dir(rq, rk, rv, log_g[0], s0f)
    o_bw, s_b = retention_dir(rq[:, ::-1], rk[:, ::-1], rv[:, ::-1], log_g[1], s0b)
    o_r = head_norm(o_fw + o_bw[:, ::-1]).reshape(B, T, RET_W)

    q = (rms_norm(q_lat, q_norm_g) @ w_q_up).reshape(B, T, H_MLA, D_NOPE + D_ROPE)
    q_nope, q_rope = q[..., :D_NOPE], q[..., D_NOPE:]
    ckv = rms_norm(kv_lat, kv_norm_g)
    if ctx is None:
        keys_ckv, keys_kr = ckv, k_r
    else:
        q_rope = axial_rope(q_rope)
        keys_ckv = jnp.concatenate([ctx_ckv.astype(ckv.dtype), ckv], axis=1)
        keys_kr = jnp.concatenate([ctx_kr.astype(k_r.dtype),
                                   axial_rope(k_r[:, :, None, :])[:, :, 0]], axis=1)
    kv = (keys_ckv @ w_kv_up).reshape(B, -1, H_MLA, D_NOPE + D_VMLA)
    o_m = mla_attend(q_nope, q_rope, kv[..., :D_NOPE], keys_kr, kv[..., D_NOPE:]).reshape(B, T, MLA_W)

    o_f = fourier_mix(fu)

    branches = jnp.stack([o_r * jax.nn.silu(rz), o_m * jax.nn.silu(mz), o_f * jax.nn.silu(fz)], axis=2)
    proj = jnp.einsum('btnw,nwd->btnd', branches, w_branch)
    merged = jnp.sum(proj * jax.nn.sigmoid(gl.reshape(B, T, N_BRANCH, D_MODEL)), axis=2)
    x = x + gate * (merged @ w_out)
    new = (ckv, k_r, jnp.stack([s_f, s_b], axis=1)) if ctx is None else None
    return x, new


def setup_inputs(seed: int = 0) -> dict:
    key = jax.random.key(seed)
    ks = jax.random.split(key, 20)
    f32 = jnp.float32
    nrm = lambda k, shape, s: jax.random.normal(k, shape, f32) * s
    gam = 1.0 - 2.0 ** (-5.0 - jnp.arange(H_RET, dtype=f32))
    logit = jnp.log(gam) - jnp.log1p(-gam)
    return {
        "x_prompt": nrm(ks[0], (BATCH, SEQ, D_MODEL), 1.0),
        "x_sample": nrm(ks[1], (DEC_BATCH, DEC_SEQ, D_MODEL), 1.0),
        "cache_ckv": nrm(ks[2], (DEC_BATCH, DEPTH, PAST_LEN, KV_LORA), 1.0),
        "cache_krope": nrm(ks[3], (DEC_BATCH, DEPTH, PAST_LEN, D_ROPE), 1.0),
        "state_ret": nrm(ks[4], (DEC_BATCH, DEPTH, 2, H_RET, DK_RET, DV_RET), 0.5),
        "c": nrm(ks[5], (DEC_BATCH, D_MODEL), 1.0),
        "c_ctx": nrm(ks[6], (D_MODEL,), 1.0),
        "norm_g": 1.0 + nrm(ks[7], (DEPTH, D_MODEL), 0.02),
        "w_mod": nrm(ks[8], (DEPTH, D_MODEL, 3 * D_MODEL), 0.3 * D_MODEL ** -0.5),
        "b_mod": nrm(ks[9], (DEPTH, 3 * D_MODEL), 0.01),
        "w_in": nrm(ks[10], (DEPTH, D_MODEL, IN_W), D_MODEL ** -0.5),
        "ret_decay_logit": logit[None, None, :] + nrm(ks[11], (DEPTH, 2, H_RET), 0.05),
        "q_norm_g": 1.0 + nrm(ks[12], (DEPTH, Q_LORA), 0.02),
        "w_q_up": nrm(ks[13], (DEPTH, Q_LORA, H_MLA * (D_NOPE + D_ROPE)), Q_LORA ** -0.5),
        "kv_norm_g": 1.0 + nrm(ks[14], (DEPTH, KV_LORA), 0.02),
        "w_kv_up": nrm(ks[15], (DEPTH, KV_LORA, H_MLA * (D_NOPE + D_VMLA)), KV_LORA ** -0.5),
        "w_branch": nrm(ks[16], (DEPTH, N_BRANCH, BRANCH_W, D_MODEL), BRANCH_W ** -0.5),
        "w_out": nrm(ks[17], (DEPTH, D_MODEL, D_MODEL), D_MODEL ** -0.5),
        "final_norm_g": 1.0 + nrm(ks[18], (D_MODEL,), 0.02),
    }


def reference(x_prompt, x_sample, cache_ckv, cache_krope, state_ret, c, c_ctx, norm_g, w_mod,
              b_mod, w_in, ret_decay_logit, q_norm_g, w_q_up, kv_norm_g, w_kv_up, w_branch,
              w_out, final_norm_g):
    h = x_prompt
    ckvs, krs, rets = [], [], []
    for l in range(DEPTH):
        h, (ckv, kr, st) = layer(h, c_ctx[None, :], None, norm_g[l], w_mod[l], b_mod[l], w_in[l],
                                 ret_decay_logit[l], q_norm_g[l], w_q_up[l], kv_norm_g[l],
                                 w_kv_up[l], w_branch[l], w_out[l])
        ckvs.append(ckv)
        krs.append(kr)
        rets.append(st)
    y_prompt = rms_norm(h, final_norm_g)
    new_ckv = jnp.stack(ckvs, axis=1)
    new_krope = jnp.stack(krs, axis=1)
    new_ret = jnp.stack(rets, axis=1)

    z = x_sample
    for l in range(DEPTH):
        ctx = (cache_ckv[:, l], cache_krope[:, l], state_ret[:, l])
        z, _ = layer(z, c, ctx, norm_g[l], w_mod[l], b_mod[l], w_in[l], ret_decay_logit[l],
                     q_norm_g[l], w_q_up[l], kv_norm_g[l], w_kv_up[l], w_branch[l], w_out[l])
    y_sample = rms_norm(z, final_norm_g)
    return (y_prompt, y_sample, new_ckv, new_krope, new_ret)
```

```python
import functools

import numpy as np
import jax
import jax.numpy as jnp
from jax import lax
from jax.experimental import pallas as pl
from jax.experimental.pallas import tpu as pltpu

F32 = jnp.float32
BF16 = jnp.bfloat16

D_MODEL = 1024
BATCH = 32
SEQ = 256
DEPTH = 2
DEC_BATCH = 4
DEC_SEQ = 1024
PAST_LEN = 512
GRID_W = 64
EPS = 1e-6
H_RET = 4
DK_RET = 64
DV_RET = 128
RET_W = H_RET * DV_RET
H_MLA = 8
Q_LORA = 384
KV_LORA = 256
D_NOPE = 64
D_ROPE = 32
D_VMLA = 64
MLA_W = H_MLA * D_VMLA
ROPE_BASE = 10000.0
F_GROUPS = 4
F_GROUP_W = 128
FOURIER_W = F_GROUPS * F_GROUP_W
N_BRANCH = 3
BRANCH_W = 512

LANE = 128
N_P = BATCH * SEQ
N_S = DEC_BATCH * DEC_SEQ
N_TOK = N_P + N_S
TM = 512
MOD_ROWS = 8
ROPE_LANE0 = D_NOPE
VMEM_LIMIT = 56 * 1024 * 1024

SEG_QKV = (0, 1024)
SEG_Z = (1024, 2560)
SEG_G = (2560, 5632)
SEG_QL = (5632, 6016)
SEG_KV = (6016, 6272)
SEG_KR = (6272, 6400)
SEG_FU = (6400, 6912)
IN_WP = 6912


def _cparams(sem):
    return pltpu.CompilerParams(dimension_semantics=sem, vmem_limit_bytes=VMEM_LIMIT)


def _tile_mod_row(i):
    n_p_tiles = N_P // TM
    per_b = DEC_SEQ // TM
    return jnp.where(i < n_p_tiles, 0, 1 + (i - n_p_tiles) // per_b)


def _mod_kernel(c_ref, w_ref, b_ref, o_ref):
    cv = c_ref[...]
    s = cv * jax.nn.sigmoid(cv)
    o_ref[0] = jnp.dot(s, w_ref[0], preferred_element_type=F32,
                       precision=lax.Precision.HIGHEST) + b_ref[0]


def _mod_call(cv, w_mod, b_mod):
    nb = 3 * D_MODEL // 1024
    return pl.pallas_call(
        _mod_kernel,
        grid=(DEPTH, nb),
        in_specs=[
            pl.BlockSpec((MOD_ROWS, D_MODEL), lambda l, j: (0, 0)),
            pl.BlockSpec((1, D_MODEL, 1024), lambda l, j: (l, 0, j)),
            pl.BlockSpec((1, 1, 1024), lambda l, j: (l, 0, j)),
        ],
        out_specs=pl.BlockSpec((1, MOD_ROWS, 1024), lambda l, j: (l, 0, j)),
        out_shape=jax.ShapeDtypeStruct((DEPTH, MOD_ROWS, 3 * D_MODEL), F32),
        compiler_params=_cparams(("arbitrary", "arbitrary")),
        name="mod",
    )(cv, w_mod, b_mod.reshape(DEPTH, 1, 3 * D_MODEL))


def _rms(x, g):
    return x * lax.rsqrt(jnp.mean(x * x, axis=-1, keepdims=True) + EPS) * g


def _in_kernel(x_ref, mod_ref, g_ref, qg_ref, kvg_ref, w_ref,
               qkv_ref, sz_ref, sg_ref, qn_ref, ckv_ref, kr_ref, fu_ref):
    x = x_ref[...]
    shift = mod_ref[0, :, 0:D_MODEL]
    scale = mod_ref[0, :, D_MODEL:2 * D_MODEL]
    h = _rms(x, g_ref[...]) * (1.0 + scale) + shift
    hb = h.astype(BF16)

    def mm(a, b):
        return jnp.dot(hb, w_ref[:, a:b], preferred_element_type=F32)

    a0 = SEG_QKV[0]
    p = mm(a0, a0 + 1024)
    qw = H_RET * DK_RET
    qkv_ref[:, 0:qw] = p[:, 0:qw].astype(BF16)
    qkv_ref[:, qw:2 * qw] = (p[:, qw:2 * qw] * (DK_RET ** -0.5)).astype(BF16)
    qkv_ref[:, 2 * qw:] = p[:, 2 * qw:].astype(BF16)
    for j in range(3):
        a = SEG_Z[0] + j * BRANCH_W
        p = mm(a, a + BRANCH_W)
        sz_ref[:, j * BRANCH_W:(j + 1) * BRANCH_W] = (p * jax.nn.sigmoid(p)).astype(BF16)
    for j in range(N_BRANCH):
        a = SEG_G[0] + j * D_MODEL
        p = mm(a, a + D_MODEL)
        sg_ref[:, j * D_MODEL:(j + 1) * D_MODEL] = jax.nn.sigmoid(p).astype(BF16)
    qn_ref[...] = _rms(mm(*SEG_QL), qg_ref[...]).astype(BF16)
    ckv_ref[...] = _rms(mm(*SEG_KV), kvg_ref[...])
    kr_ref[...] = mm(*SEG_KR)
    fu_ref[...] = mm(*SEG_FU).astype(BF16)


def _in_call(x, mod_l, g, qg, kvg, w_p):
    row = lambda w: pl.BlockSpec((TM, w), lambda i: (i, 0))
    full = lambda a: pl.BlockSpec(a.shape, lambda i: (0,) * a.ndim)
    outs = [(1024, BF16), (3 * BRANCH_W, BF16), (N_BRANCH * D_MODEL, BF16), (Q_LORA, BF16),
            (KV_LORA, F32), (LANE, F32), (FOURIER_W, BF16)]
    return pl.pallas_call(
        _in_kernel,
        grid=(N_TOK // TM,),
        in_specs=[
            row(D_MODEL),
            pl.BlockSpec((1, 1, 3 * D_MODEL), lambda i: (_tile_mod_row(i), 0, 0)),
            full(g), full(qg), full(kvg), full(w_p),
        ],
        out_specs=[row(w) for w, _ in outs],
        out_shape=[jax.ShapeDtypeStruct((N_TOK, w), dt) for w, dt in outs],
        compiler_params=_cparams(("arbitrary",)),
        name="in_proj",
    )(x, mod_l, g, qg, kvg, w_p)


def _ret_kernel(T, has_state, *refs):
    if has_state:
        qkv_ref, rz_ref, lg_ref, st_ref, _, br_ref, mask_ref = refs
    else:
        qkv_ref, rz_ref, lg_ref, br_ref, sf_ref, mask_ref = refs
    lg = jax.nn.log_sigmoid(lg_ref[...])

    @pl.when(pl.program_id(0) == 0)
    def _():
        ii = lax.broadcasted_iota(jnp.int32, (T, T), 0)
        jj = lax.broadcasted_iota(jnp.int32, (T, T), 1)
        d = (ii - jj).astype(F32)
        ad = jnp.abs(d)
        for h in range(H_RET):
            lf = lg[0:1, h:h + 1]
            lb = lg[1:2, h:h + 1]
            mask_ref[h] = jnp.where(d > 0, jnp.exp(ad * lf),
                                    jnp.where(d < 0, jnp.exp(ad * lb), 2.0))

    qw = H_RET * DK_RET
    for h in range(H_RET):
        lf = lg[0:1, h:h + 1]
        lb = lg[1:2, h:h + 1]
        q = qkv_ref[:, h * DK_RET:(h + 1) * DK_RET]
        k = qkv_ref[:, qw + h * DK_RET:qw + (h + 1) * DK_RET]
        v = qkv_ref[:, 2 * qw + h * DV_RET:2 * qw + (h + 1) * DV_RET]
        s = lax.dot_general(q, k, (((1,), (1,)), ((), ())), preferred_element_type=F32)
        s = s * mask_ref[h]
        o = jnp.dot(s.astype(BF16), v, preferred_element_type=F32)
        if has_state:
            rows = lax.broadcasted_iota(jnp.int32, (T, DV_RET), 0).astype(F32)
            o = o + jnp.exp((rows + 1.0) * lf) * jnp.dot(
                q, st_ref[0, h].astype(BF16), preferred_element_type=F32)
            o = o + jnp.exp((T - rows) * lb) * jnp.dot(
                q, st_ref[1, h].astype(BF16), preferred_element_type=F32)
        mu = jnp.mean(o, axis=-1, keepdims=True)
        c = o - mu
        var = jnp.mean(c * c, axis=-1, keepdims=True)
        on = c * lax.rsqrt(var + EPS)
        gate = rz_ref[:, h * DV_RET:(h + 1) * DV_RET].astype(F32)
        br_ref[:, h * DV_RET:(h + 1) * DV_RET] = (on * gate).astype(BF16)
        if not has_state:
            rows = lax.broadcasted_iota(jnp.int32, (T, DK_RET), 0).astype(F32)
            kf = (k.astype(F32) * jnp.exp((T - 1.0 - rows) * lf)).astype(BF16)
            kb = (k.astype(F32) * jnp.exp(rows * lb)).astype(BF16)
            tn = (((0,), (0,)), ((), ()))
            sf_ref[0, 0, h] = lax.dot_general(kf, v, tn, preferred_element_type=F32)
            sf_ref[0, 1, h] = lax.dot_general(kb, v, tn, preferred_element_type=F32)


def _ret_call(T, nb, row0, qkv, sz, logit_l, layer, state=None, prev=None):
    blk0 = row0 // T
    has_state = state is not None
    in_specs = [
        pl.BlockSpec((T, 1024), lambda b: (blk0 + b, 0)),
        pl.BlockSpec((T, BRANCH_W), lambda b: (blk0 + b, 0)),
        pl.BlockSpec((8, LANE), lambda b: (0, 0)),
    ]
    args = [qkv, sz, logit_l]
    br_shape = jax.ShapeDtypeStruct((N_TOK, RET_W), BF16)
    br_spec = pl.BlockSpec((T, RET_W), lambda b: (blk0 + b, 0))
    if has_state:
        in_specs.append(pl.BlockSpec((None, None, 2, H_RET, DK_RET, DV_RET),
                                     lambda b: (b, layer, 0, 0, 0, 0)))
        in_specs.append(pl.BlockSpec(memory_space=pl.ANY))
        args += [state, prev]
        out_specs, out_shape, aliases = br_spec, br_shape, {4: 0}
    else:
        out_specs = [br_spec, pl.BlockSpec((1, 2, H_RET, DK_RET, DV_RET),
                                           lambda b: (b, 0, 0, 0, 0))]
        out_shape = [br_shape, jax.ShapeDtypeStruct((nb, 2, H_RET, DK_RET, DV_RET), F32)]
        aliases = {}
    return pl.pallas_call(
        functools.partial(_ret_kernel, T, has_state),
        grid=(nb,),
        in_specs=in_specs,
        out_specs=out_specs,
        out_shape=out_shape,
        input_output_aliases=aliases,
        scratch_shapes=[pltpu.VMEM((H_RET, T, T), F32)],
        compiler_params=_cparams(("arbitrary",)),
        name="ret_s" if has_state else "ret_p",
    )(*args)


TQ = 256


def _rope(x, cos, sa, sb):
    half = D_ROPE // 4
    return x * cos + pltpu.roll(x, LANE - half, 1) * sa + pltpu.roll(x, half, 1) * sb


def _mla_kernel(T, has_cache, *refs):
    if has_cache:
        (qn_ref, ckv_ref, kr_ref, mz_ref, wq_ref, wk_ref, wv_ref, cckv_ref, ckr_ref,
         cos_ref, sa_ref, sb_ref, _, bm_ref, q_scr, k_scr, v_scr) = refs
    else:
        (qn_ref, ckv_ref, kr_ref, mz_ref, wq_ref, wk_ref, wv_ref,
         bm_ref, q_scr, k_scr, v_scr) = refs
    P = PAST_LEN if has_cache else 0
    scale = (D_NOPE + D_ROPE) ** -0.5

    qn = qn_ref[...]
    ckv = ckv_ref[...].astype(BF16)
    kr = kr_ref[...]
    if has_cache:
        kr = _rope(kr, cos_ref[...], sa_ref[...], sb_ref[...])
        cc = cckv_ref[...].astype(BF16)
    for h in range(H_MLA):
        sl = slice(h * LANE, (h + 1) * LANE)
        qh = jnp.dot(qn, wq_ref[:, sl], preferred_element_type=F32)
        if has_cache:
            qh = _rope(qh, cos_ref[...], sa_ref[...], sb_ref[...])
            kc = jnp.dot(cc, wk_ref[:, sl], preferred_element_type=F32)
            k_scr[h, 0:P] = (kc + ckr_ref[...]).astype(BF16)
        q_scr[h] = qh.astype(BF16)
        kn = jnp.dot(ckv, wk_ref[:, sl], preferred_element_type=F32)
        k_scr[h, P:P + T] = (kn + kr).astype(BF16)
    for p in range(H_MLA // 2):
        sl = slice(p * LANE, (p + 1) * LANE)
        if has_cache:
            v_scr[p, 0:P] = jnp.dot(cc, wv_ref[:, sl], preferred_element_type=F32).astype(BF16)
        v_scr[p, P:P + T] = jnp.dot(ckv, wv_ref[:, sl], preferred_element_type=F32).astype(BF16)

    lane = lax.broadcasted_iota(jnp.int32, (TQ, LANE), 1)

    def attend(r0):
        rs = pl.ds(r0, TQ)
        for p in range(H_MLA // 2):
            halves = []
            for h in (2 * p, 2 * p + 1):
                s = lax.dot_general(q_scr[h, rs], k_scr[h], (((1,), (1,)), ((), ())),
                                    preferred_element_type=F32) * scale
                e = jnp.exp(s - jnp.max(s, axis=-1, keepdims=True))
                l = jnp.sum(e, axis=-1, keepdims=True)
                halves.append(jnp.dot(e.astype(BF16), v_scr[p], preferred_element_type=F32) / l)
            o = jnp.where(lane < D_VMLA, halves[0], halves[1])
            gate = mz_ref[rs, p * LANE:(p + 1) * LANE].astype(F32)
            bm_ref[rs, p * LANE:(p + 1) * LANE] = (o * gate).astype(BF16)

    if T == TQ:
        attend(0)
    else:
        def body(qb, carry):
            attend(pl.multiple_of(qb * TQ, TQ))
            return carry
        lax.fori_loop(0, T // TQ, body, 0)


def _mla_call(T, nb, row0, qn, ckv, kr, sz, wq, wk, wv, layer, cache=None, prev=None):
    blk0 = row0 // T
    has_cache = cache is not None
    row = lambda w, c=0: pl.BlockSpec((T, w), lambda b: (blk0 + b, c))
    full = lambda a: pl.BlockSpec(a.shape, lambda b: (0,) * a.ndim)
    in_specs = [row(Q_LORA), row(KV_LORA), row(LANE), row(BRANCH_W, 1), full(wq), full(wk), full(wv)]
    args = [qn, ckv, kr, sz, wq, wk, wv]
    aliases = {}
    if has_cache:
        cache_ckv, cache_kr, cos, sa, sb = cache
        in_specs += [
            pl.BlockSpec((None, None, PAST_LEN, KV_LORA), lambda b: (b, layer, 0, 0)),
            pl.BlockSpec((None, None, PAST_LEN, LANE), lambda b: (b, layer, 0, 0)),
            full(cos), full(sa), full(sb),
            pl.BlockSpec(memory_space=pl.ANY),
        ]
        args += [cache_ckv, cache_kr, cos, sa, sb, prev]
        aliases = {len(args) - 1: 0}
    Tk = T + (PAST_LEN if has_cache else 0)
    return pl.pallas_call(
        functools.partial(_mla_kernel, T, has_cache),
        grid=(nb,),
        in_specs=in_specs,
        out_specs=pl.BlockSpec((T, MLA_W), lambda b: (blk0 + b, 0)),
        out_shape=jax.ShapeDtypeStruct((N_TOK, MLA_W), BF16),
        input_output_aliases=aliases,
        scratch_shapes=[pltpu.VMEM((H_MLA, T, LANE), BF16),
                        pltpu.VMEM((H_MLA, Tk, LANE), BF16),
                        pltpu.VMEM((H_MLA // 2, Tk, LANE), BF16)],
        compiler_params=_cparams(("arbitrary",)),
        name="mla_s" if has_cache else "mla_p",
    )(*args)


def _fourier_kernel(T, *refs):
    if len(refs) == 7:
        fu_ref, fz_ref, csw_ref, cts_ref, _, bf_ref, ab_scr = refs
    else:
        fu_ref, fz_ref, csw_ref, cts_ref, bf_ref, ab_scr = refs
    for g in range(F_GROUPS):
        sl = slice(g * F_GROUP_W, (g + 1) * F_GROUP_W)
        z = jnp.dot(fu_ref[:, sl], csw_ref[...], preferred_element_type=F32)
        ab_scr[0:T, sl] = z[:, :F_GROUP_W].astype(BF16)
        ab_scr[T:2 * T, sl] = z[:, F_GROUP_W:].astype(BF16)
    o = jnp.dot(cts_ref[...], ab_scr[...], preferred_element_type=F32)
    o = o * ((T * F_GROUP_W) ** -0.5)
    bf_ref[...] = (o * fz_ref[...].astype(F32)).astype(BF16)


def _fourier_call(T, nb, row0, fu, sz, csw, cts, prev=None):
    blk0 = row0 // T
    full = lambda a: pl.BlockSpec(a.shape, lambda b: (0,) * a.ndim)
    in_specs = [pl.BlockSpec((T, FOURIER_W), lambda b: (blk0 + b, 0)),
                pl.BlockSpec((T, BRANCH_W), lambda b: (blk0 + b, 2)),
                full(csw), full(cts)]
    args = [fu, sz, csw, cts]
    aliases = {}
    if prev is not None:
        in_specs.append(pl.BlockSpec(memory_space=pl.ANY))
        args.append(prev)
        aliases = {4: 0}
    return pl.pallas_call(
        functools.partial(_fourier_kernel, T),
        grid=(nb,),
        in_specs=in_specs,
        out_specs=pl.BlockSpec((T, FOURIER_W), lambda b: (blk0 + b, 0)),
        out_shape=jax.ShapeDtypeStruct((N_TOK, FOURIER_W), BF16),
        input_output_aliases=aliases,
        scratch_shapes=[pltpu.VMEM((2 * T, FOURIER_W), BF16)],
        compiler_params=_cparams(("arbitrary",)),
        name="fourier_s" if prev is not None else "fourier_p",
    )(*args)


def _merge_kernel(final, *refs):
    if final:
        br_ref, bm_ref, bf_ref, sg_ref, x_ref, mod_ref, wb_ref, wo_ref, fg_ref, o_ref = refs
    else:
        br_ref, bm_ref, bf_ref, sg_ref, x_ref, mod_ref, wb_ref, wo_ref, o_ref = refs
    merged = None
    for n, b_ref in enumerate((br_ref, bm_ref, bf_ref)):
        proj = jnp.dot(b_ref[...], wb_ref[n], preferred_element_type=F32)
        term = proj * sg_ref[:, n * D_MODEL:(n + 1) * D_MODEL].astype(F32)
        merged = term if merged is None else merged + term
    gate = mod_ref[0, :, 2 * D_MODEL:3 * D_MODEL]
    y = x_ref[...] + gate * jnp.dot(merged.astype(BF16), wo_ref[...], preferred_element_type=F32)
    if final:
        y = _rms(y, fg_ref[...])
    o_ref[...] = y


def _merge_call(br, bm, bf, sg, x, mod_l, wb, wo, final_g=None):
    final = final_g is not None
    row = lambda w: pl.BlockSpec((TM, w), lambda i: (i, 0))
    full = lambda a: pl.BlockSpec(a.shape, lambda i: (0,) * a.ndim)
    in_specs = [row(BRANCH_W), row(BRANCH_W), row(BRANCH_W), row(N_BRANCH * D_MODEL), row(D_MODEL),
                pl.BlockSpec((1, 1, 3 * D_MODEL), lambda i: (_tile_mod_row(i), 0, 0)),
                full(wb), full(wo)]
    args = [br, bm, bf, sg, x, mod_l, wb, wo]
    if final:
        in_specs.append(full(final_g))
        args.append(final_g)
    return pl.pallas_call(
        functools.partial(_merge_kernel, final),
        grid=(N_TOK // TM,),
        in_specs=in_specs,
        out_specs=row(D_MODEL),
        out_shape=jax.ShapeDtypeStruct((N_TOK, D_MODEL), F32),
        compiler_params=_cparams(("arbitrary",)),
        name="merge_final" if final else "merge",
    )(*args)


def _dft_tables(T):
    def cs(n):
        kt = (np.arange(n)[:, None] * np.arange(n)[None, :]) % n
        ang = 2.0 * np.pi * kt.astype(np.float64) / n
        return np.cos(ang), np.sin(ang)
    ct, st = cs(T)
    cw, sw = cs(F_GROUP_W)
    cts = np.concatenate([ct, -st], axis=1).astype(np.float32)
    csw = np.concatenate([cw, sw], axis=1).astype(np.float32)
    return cts, csw


def _rope_tables(T):
    half = D_ROPE // 2
    nfreq = half // 2
    inv = ROPE_BASE ** (-np.arange(nfreq, dtype=np.float64) / nfreq)
    t = np.arange(T)
    pos = np.stack([t // GRID_W, t % GRID_W], axis=0).astype(np.float64)
    cos = np.ones((T, LANE), np.float64)
    sa = np.zeros((T, LANE), np.float64)
    sb = np.zeros((T, LANE), np.float64)
    for part in range(2):
        ang = pos[part][:, None] * inv[None, :]
        l1 = ROPE_LANE0 + part * half
        l2 = l1 + nfreq
        cos[:, l1:l1 + nfreq] = np.cos(ang)
        cos[:, l2:l2 + nfreq] = np.cos(ang)
        sa[:, l1:l1 + nfreq] = -np.sin(ang)
        sb[:, l2:l2 + nfreq] = np.sin(ang)
    return tuple(jnp.asarray(a.astype(np.float32)) for a in (cos, sa, sb))


def _permute_w_in(w):
    z = lambda n: jnp.zeros((D_MODEL, n), w.dtype)
    cols = [w[:, 0:1024],
            w[:, 1024:1536], w[:, 2208:2720], w[:, 3232:3744],
            w[:, 3744:6816],
            w[:, 1536:1920],
            w[:, 1920:2176],
            z(ROPE_LANE0), w[:, 2176:2208], z(LANE - ROPE_LANE0 - D_ROPE),
            w[:, 2720:3232]]
    return jnp.concatenate(cols, axis=1).astype(BF16)


def _pad_heads(w, n_heads, width, lo, hi):
    k = w.shape[0]
    wh = w.reshape(k, n_heads, width)[:, :, lo:hi]
    wh = jnp.pad(wh, ((0, 0), (0, 0), (0, LANE - (hi - lo))))
    return wh.reshape(k, n_heads * LANE).astype(BF16)


def kernel(x_prompt, x_sample, cache_ckv, cache_krope, state_ret, c, c_ctx, norm_g, w_mod, b_mod,
           w_in, ret_decay_logit, q_norm_g, w_q_up, kv_norm_g, w_kv_up, w_branch, w_out,
           final_norm_g):
    cv = jnp.concatenate([c_ctx[None, :], c, jnp.zeros((MOD_ROWS - 1 - DEC_BATCH, D_MODEL), F32)], axis=0)
    mod = _mod_call(cv, w_mod, b_mod)

    x = jnp.concatenate([x_prompt.reshape(N_P, D_MODEL), x_sample.reshape(N_S, D_MODEL)], axis=0)
    cache_kr = jnp.pad(cache_krope, ((0, 0), (0, 0), (0, 0), (ROPE_LANE0, LANE - ROPE_LANE0 - D_ROPE)))
    rope = _rope_tables(DEC_SEQ)
    dft = {}
    for T in (SEQ, DEC_SEQ):
        cts, csw = _dft_tables(T)
        dft[T] = (jnp.asarray(cts).astype(BF16), jnp.asarray(csw).astype(BF16))
    logit = jnp.pad(ret_decay_logit, ((0, 0), (0, 8 - 2), (0, LANE - H_RET)))

    ckvs, krs, rets = [], [], []
    for l in range(DEPTH):
        mod_l = mod[l].reshape(MOD_ROWS, 1, 3 * D_MODEL)
        w_p = _permute_w_in(w_in[l])
        wq = _pad_heads(w_q_up[l], H_MLA, D_NOPE + D_ROPE, 0, D_NOPE + D_ROPE)
        wk = _pad_heads(w_kv_up[l], H_MLA, D_NOPE + D_VMLA, 0, D_NOPE)
        wv = w_kv_up[l].reshape(KV_LORA, H_MLA, D_NOPE + D_VMLA)[:, :, D_NOPE:]
        wv = wv.reshape(KV_LORA, MLA_W).astype(BF16)
        qkv, sz, sg, qn, ckv, kr, fu = _in_call(
            x, mod_l, norm_g[l][None, :], q_norm_g[l][None, :], kv_norm_g[l][None, :], w_p)

        br, st = _ret_call(SEQ, BATCH, 0, qkv, sz, logit[l], l)
        br = _ret_call(DEC_SEQ, DEC_BATCH, N_P, qkv, sz, logit[l], l, state=state_ret, prev=br)
        bm = _mla_call(SEQ, BATCH, 0, qn, ckv, kr, sz, wq, wk, wv, l)
        bm = _mla_call(DEC_SEQ, DEC_BATCH, N_P, qn, ckv, kr, sz, wq, wk, wv, l,
                       cache=(cache_ckv, cache_kr) + rope, prev=bm)
        bf = _fourier_call(SEQ, BATCH, 0, fu, sz, dft[SEQ][1], dft[SEQ][0])
        bf = _fourier_call(DEC_SEQ, DEC_BATCH, N_P, fu, sz, dft[DEC_SEQ][1], dft[DEC_SEQ][0], prev=bf)

        x = _merge_call(br, bm, bf, sg, x, mod_l, w_branch[l].astype(BF16), w_out[l].astype(BF16),
                        final_g=final_norm_g[None, :] if l == DEPTH - 1 else None)
        ckvs.append(ckv[:N_P].reshape(BATCH, SEQ, KV_LORA))
        krs.append(kr[:N_P, ROPE_LANE0:ROPE_LANE0 + D_ROPE].reshape(BATCH, SEQ, D_ROPE))
        rets.append(st)

    y_prompt = x[:N_P].reshape(BATCH, SEQ, D_MODEL)
    y_sample = x[N_P:].reshape(DEC_BATCH, DEC_SEQ, D_MODEL)
    return (y_prompt, y_sample, jnp.stack(ckvs, axis=1), jnp.stack(krs, axis=1), jnp.stack(rets, axis=1))
```

```python
import functools

import numpy as np
import jax
import jax.numpy as jnp
from jax import lax
from jax.experimental import pallas as pl
from jax.experimental.pallas import tpu as pltpu

F32 = jnp.float32
BF16 = jnp.bfloat16

D_MODEL = 1024
BATCH = 32
SEQ = 256
DEPTH = 2
DEC_BATCH = 4
DEC_SEQ = 1024
PAST_LEN = 512
GRID_W = 64
EPS = 1e-6
H_RET = 4
DK_RET = 64
DV_RET = 128
RET_W = H_RET * DV_RET
H_MLA = 8
Q_LORA = 384
KV_LORA = 256
D_NOPE = 64
D_ROPE = 32
D_VMLA = 64
MLA_W = H_MLA * D_VMLA
ROPE_BASE = 10000.0
F_GROUPS = 4
F_GROUP_W = 128
FOURIER_W = F_GROUPS * F_GROUP_W
N_BRANCH = 3
BRANCH_W = 512

LANE = 128
N_P = BATCH * SEQ
N_S = DEC_BATCH * DEC_SEQ
N_TOK = N_P + N_S
TM = 512
MOD_ROWS = 8
ROPE_LANE0 = D_NOPE
VMEM_LIMIT = 56 * 1024 * 1024

SEG_QKV = (0, 1024)
SEG_Z = (1024, 2560)
SEG_G = (2560, 5632)
SEG_QL = (5632, 6016)
SEG_KV = (6016, 6272)
SEG_KR = (6272, 6400)
SEG_FU = (6400, 6912)
IN_WP = 6912


def _cparams(sem):
    return pltpu.CompilerParams(dimension_semantics=sem, vmem_limit_bytes=VMEM_LIMIT)


def _tile_mod_row(i):
    n_p_tiles = N_P // TM
    per_b = DEC_SEQ // TM
    return jnp.where(i < n_p_tiles, 0, 1 + (i - n_p_tiles) // per_b)


def _mod_kernel(c_ref, w_ref, b_ref, o_ref):
    cv = c_ref[...]
    s = cv * jax.nn.sigmoid(cv)
    o_ref[0] = jnp.dot(s, w_ref[0], preferred_element_type=F32,
                       precision=lax.Precision.HIGHEST) + b_ref[0]


def _mod_call(cv, w_mod, b_mod):
    nb = 3 * D_MODEL // 1024
    return pl.pallas_call(
        _mod_kernel,
        grid=(DEPTH, nb),
        in_specs=[
            pl.BlockSpec((MOD_ROWS, D_MODEL), lambda l, j: (0, 0)),
            pl.BlockSpec((1, D_MODEL, 1024), lambda l, j: (l, 0, j)),
            pl.BlockSpec((1, 1, 1024), lambda l, j: (l, 0, j)),
        ],
        out_specs=pl.BlockSpec((1, MOD_ROWS, 1024), lambda l, j: (l, 0, j)),
        out_shape=jax.ShapeDtypeStruct((DEPTH, MOD_ROWS, 3 * D_MODEL), F32),
        compiler_params=_cparams(("arbitrary", "arbitrary")),
        name="mod",
    )(cv, w_mod, b_mod.reshape(DEPTH, 1, 3 * D_MODEL))


def _rms(x, g):
    return x * lax.rsqrt(jnp.mean(x * x, axis=-1, keepdims=True) + EPS) * g


def _in_kernel(chained, *refs):
    xp_ref, xs_ref, mod_ref, g_ref, qg_ref, kvg_ref, w_ref = refs[:7]
    (qkv_ref, sz_ref, sg_ref, qn_ref, ckv_ref, kr_ref, fu_ref,
     nckv_ref, nkr_ref) = refs[7 + 2 * chained:]
    is_prompt = pl.program_id(0) < N_P // TM
    x = jnp.where(is_prompt, xp_ref[...], xs_ref[...])
    shift = mod_ref[0, :, 0:D_MODEL]
    scale = mod_ref[0, :, D_MODEL:2 * D_MODEL]
    h = _rms(x, g_ref[...]) * (1.0 + scale) + shift
    hb = h.astype(BF16)

    def mm(a, b):
        return jnp.dot(hb, w_ref[:, a:b], preferred_element_type=F32)

    a0 = SEG_QKV[0]
    p = mm(a0, a0 + 1024)
    qw = H_RET * DK_RET
    qkv_ref[:, 0:qw] = p[:, 0:qw].astype(BF16)
    qkv_ref[:, qw:2 * qw] = (p[:, qw:2 * qw] * (DK_RET ** -0.5)).astype(BF16)
    qkv_ref[:, 2 * qw:] = p[:, 2 * qw:].astype(BF16)
    for j in range(3):
        a = SEG_Z[0] + j * BRANCH_W
        p = mm(a, a + BRANCH_W)
        sz_ref[:, j * BRANCH_W:(j + 1) * BRANCH_W] = (p * jax.nn.sigmoid(p)).astype(BF16)
    for j in range(N_BRANCH):
        a = SEG_G[0] + j * D_MODEL
        p = mm(a, a + D_MODEL)
        sg_ref[:, j * D_MODEL:(j + 1) * D_MODEL] = jax.nn.sigmoid(p).astype(BF16)
    qn_ref[...] = _rms(mm(*SEG_QL), qg_ref[...]).astype(BF16)
    ckv = _rms(mm(*SEG_KV), kvg_ref[...])
    ckv_ref[...] = ckv.astype(BF16)
    kr = mm(*SEG_KR)
    kr_ref[...] = kr
    fu_ref[...] = mm(*SEG_FU).astype(BF16)

    @pl.when(is_prompt)
    def _():
        for s in range(TM // SEQ):
            nckv_ref[s] = ckv[s * SEQ:(s + 1) * SEQ]
            nkr_ref[s] = kr[s * SEQ:(s + 1) * SEQ, ROPE_LANE0:ROPE_LANE0 + D_ROPE]


def _split_rows(width):
    n_p_tiles = N_P // TM
    return [pl.BlockSpec((TM, width), lambda i: (jnp.minimum(i, n_p_tiles - 1), 0)),
            pl.BlockSpec((TM, width), lambda i: (jnp.maximum(i - n_p_tiles, 0), 0))]


def _in_call(xp, xs, mod_l, g, qg, kvg, w_p, layer, prev=None):
    row = lambda w: pl.BlockSpec((TM, w), lambda i: (i, 0))
    full = lambda a: pl.BlockSpec(a.shape, lambda i: (0,) * a.ndim)
    chained = prev is not None
    outs = [(1024, BF16), (3 * BRANCH_W, BF16), (N_BRANCH * D_MODEL, BF16), (Q_LORA, BF16),
            (KV_LORA, BF16), (LANE, F32), (FOURIER_W, BF16)]
    last_p = N_P // TM - 1
    new_spec = lambda w: pl.BlockSpec((TM // SEQ, None, SEQ, w),
                                      lambda i: (jnp.minimum(i, last_p), layer, 0, 0))
    in_specs = _split_rows(D_MODEL) + [
        pl.BlockSpec((1, 1, 3 * D_MODEL), lambda i: (_tile_mod_row(i), 0, 0)),
        full(g), full(qg), full(kvg), full(w_p)]
    args = [xp, xs, mod_l, g, qg, kvg, w_p]
    aliases = {}
    if chained:
        in_specs += [pl.BlockSpec(memory_space=pl.ANY)] * 2
        args += list(prev)
        aliases = {7: len(outs), 8: len(outs) + 1}
    return pl.pallas_call(
        functools.partial(_in_kernel, chained),
        grid=(N_TOK // TM,),
        in_specs=in_specs,
        out_specs=[row(w) for w, _ in outs] + [new_spec(KV_LORA), new_spec(D_ROPE)],
        out_shape=[jax.ShapeDtypeStruct((N_TOK, w), dt) for w, dt in outs]
        + [jax.ShapeDtypeStruct((BATCH, DEPTH, SEQ, KV_LORA), F32),
           jax.ShapeDtypeStruct((BATCH, DEPTH, SEQ, D_ROPE), F32)],
        input_output_aliases=aliases,
        compiler_params=_cparams(("arbitrary",)),
        name="in_proj",
    )(*args)


def _ret_kernel(T, has_state, *refs):
    qkv_ref, rz_ref, lg_ref = refs[:3]
    if has_state:
        st_ref = refs[3]
        br_ref, mask_ref = refs[-2:]
    else:
        br_ref, sf_ref, mask_ref = refs[-3:]
    lg = jax.nn.log_sigmoid(lg_ref[...])

    @pl.when(pl.program_id(0) == 0)
    def _():
        ii = lax.broadcasted_iota(jnp.int32, (T, T), 0)
        jj = lax.broadcasted_iota(jnp.int32, (T, T), 1)
        d = (ii - jj).astype(F32)
        ad = jnp.abs(d)
        for h in range(H_RET):
            lf = lg[0:1, h:h + 1]
            lb = lg[1:2, h:h + 1]
            mask_ref[h] = jnp.where(d > 0, jnp.exp(ad * lf),
                                    jnp.where(d < 0, jnp.exp(ad * lb), 2.0))

    qw = H_RET * DK_RET
    for h in range(H_RET):
        lf = lg[0:1, h:h + 1]
        lb = lg[1:2, h:h + 1]
        q = qkv_ref[:, h * DK_RET:(h + 1) * DK_RET]
        k = qkv_ref[:, qw + h * DK_RET:qw + (h + 1) * DK_RET]
        v = qkv_ref[:, 2 * qw + h * DV_RET:2 * qw + (h + 1) * DV_RET]
        s = lax.dot_general(q, k, (((1,), (1,)), ((), ())), preferred_element_type=F32)
        s = s * mask_ref[h]
        o = jnp.dot(s.astype(BF16), v, preferred_element_type=F32)
        if has_state:
            rows = lax.broadcasted_iota(jnp.int32, (T, DV_RET), 0).astype(F32)
            o = o + jnp.exp((rows + 1.0) * lf) * jnp.dot(
                q, st_ref[0, h].astype(BF16), preferred_element_type=F32)
            o = o + jnp.exp((T - rows) * lb) * jnp.dot(
                q, st_ref[1, h].astype(BF16), preferred_element_type=F32)
        mu = jnp.mean(o, axis=-1, keepdims=True)
        c = o - mu
        var = jnp.mean(c * c, axis=-1, keepdims=True)
        on = c * lax.rsqrt(var + EPS)
        gate = rz_ref[:, h * DV_RET:(h + 1) * DV_RET].astype(F32)
        br_ref[:, h * DV_RET:(h + 1) * DV_RET] = (on * gate).astype(BF16)
        if not has_state:
            rows = lax.broadcasted_iota(jnp.int32, (T, DK_RET), 0).astype(F32)
            kf = (k.astype(F32) * jnp.exp((T - 1.0 - rows) * lf)).astype(BF16)
            kb = (k.astype(F32) * jnp.exp(rows * lb)).astype(BF16)
            tn = (((0,), (0,)), ((), ()))
            sf_ref[0, h] = lax.dot_general(kf, v, tn, preferred_element_type=F32)
            sf_ref[1, h] = lax.dot_general(kb, v, tn, preferred_element_type=F32)


def _ret_call(T, nb, row0, qkv, sz, logit_l, layer, state=None, prev=None):
    blk0 = row0 // T
    has_state = state is not None
    in_specs = [
        pl.BlockSpec((T, 1024), lambda b: (blk0 + b, 0)),
        pl.BlockSpec((T, BRANCH_W), lambda b: (blk0 + b, 0)),
        pl.BlockSpec((8, LANE), lambda b: (0, 0)),
    ]
    args = [qkv, sz, logit_l]
    br_shape = jax.ShapeDtypeStruct((N_TOK, RET_W), BF16)
    br_spec = pl.BlockSpec((T, RET_W), lambda b: (blk0 + b, 0))
    if has_state:
        in_specs.append(pl.BlockSpec((None, None, 2, H_RET, DK_RET, DV_RET),
                                     lambda b: (b, layer, 0, 0, 0, 0)))
        in_specs.append(pl.BlockSpec(memory_space=pl.ANY))
        args += [state, prev]
        out_specs, out_shape, aliases = br_spec, br_shape, {4: 0}
    else:
        out_specs = [br_spec, pl.BlockSpec((None, None, 2, H_RET, DK_RET, DV_RET),
                                           lambda b: (b, layer, 0, 0, 0, 0))]
        out_shape = [br_shape, jax.ShapeDtypeStruct((nb, DEPTH, 2, H_RET, DK_RET, DV_RET), F32)]
        aliases = {}
        if prev is not None:
            in_specs.append(pl.BlockSpec(memory_space=pl.ANY))
            args.append(prev)
            aliases = {3: 1}
    return pl.pallas_call(
        functools.partial(_ret_kernel, T, has_state),
        grid=(nb,),
        in_specs=in_specs,
        out_specs=out_specs,
        out_shape=out_shape,
        input_output_aliases=aliases,
        scratch_shapes=[pltpu.VMEM((H_RET, T, T), F32)],
        compiler_params=_cparams(("arbitrary",)),
        name="ret_s" if has_state else "ret_p",
    )(*args)


TQ = 256


def _rope(x, cos, sa, sb):
    half = D_ROPE // 4
    return x * cos + pltpu.roll(x, LANE - half, 1) * sa + pltpu.roll(x, half, 1) * sb


def _mla_kernel(T, has_cache, *refs):
    if has_cache:
        (qn_ref, ckv_ref, kr_ref, mz_ref, wq_ref, wk_ref, wv_ref, cckv_ref, ckr_ref,
         cos_ref, sa_ref, sb_ref, _, bm_ref, q_scr, k_scr, v_scr) = refs
    else:
        (qn_ref, ckv_ref, kr_ref, mz_ref, wq_ref, wk_ref, wv_ref,
         bm_ref, q_scr, k_scr, v_scr) = refs
    P = PAST_LEN if has_cache else 0
    scale = (D_NOPE + D_ROPE) ** -0.5

    qn = qn_ref[...]
    ckv = ckv_ref[...]
    kr = kr_ref[...]
    if has_cache:
        kr = _rope(kr, cos_ref[...], sa_ref[...], sb_ref[...])
        cc = cckv_ref[...].astype(BF16)
    for h in range(H_MLA):
        sl = slice(h * LANE, (h + 1) * LANE)
        qh = jnp.dot(qn, wq_ref[:, sl], preferred_element_type=F32)
        if has_cache:
            qh = _rope(qh, cos_ref[...], sa_ref[...], sb_ref[...])
            kc = jnp.dot(cc, wk_ref[:, sl], preferred_element_type=F32)
            k_scr[h, 0:P] = (kc + ckr_ref[...]).astype(BF16)
        q_scr[h] = qh.astype(BF16)
        kn = jnp.dot(ckv, wk_ref[:, sl], preferred_element_type=F32)
        k_scr[h, P:P + T] = (kn + kr).astype(BF16)
    for p in range(H_MLA // 2):
        sl = slice(p * LANE, (p + 1) * LANE)
        if has_cache:
            v_scr[p, 0:P] = jnp.dot(cc, wv_ref[:, sl], preferred_element_type=F32).astype(BF16)
        v_scr[p, P:P + T] = jnp.dot(ckv, wv_ref[:, sl], preferred_element_type=F32).astype(BF16)

    lane = lax.broadcasted_iota(jnp.int32, (TQ, LANE), 1)

    def attend(r0):
        rs = pl.ds(r0, TQ)
        for p in range(H_MLA // 2):
            halves = []
            for h in (2 * p, 2 * p + 1):
                s = lax.dot_general(q_scr[h, rs], k_scr[h], (((1,), (1,)), ((), ())),
                                    preferred_element_type=F32) * scale
                e = jnp.exp(s - jnp.max(s, axis=-1, keepdims=True))
                l = jnp.sum(e, axis=-1, keepdims=True)
                halves.append(jnp.dot(e.astype(BF16), v_scr[p], preferred_element_type=F32) / l)
            o = jnp.where(lane < D_VMLA, halves[0], halves[1])
            gate = mz_ref[rs, p * LANE:(p + 1) * LANE].astype(F32)
            bm_ref[rs, p * LANE:(p + 1) * LANE] = (o * gate).astype(BF16)

    if T == TQ:
        attend(0)
    else:
        def body(qb, carry):
            attend(pl.multiple_of(qb * TQ, TQ))
            return carry
        lax.fori_loop(0, T // TQ, body, 0)


def _mla_call(T, nb, row0, qn, ckv, kr, sz, wq, wk, wv, layer, cache=None, prev=None):
    blk0 = row0 // T
    has_cache = cache is not None
    row = lambda w, c=0: pl.BlockSpec((T, w), lambda b: (blk0 + b, c))
    full = lambda a: pl.BlockSpec(a.shape, lambda b: (0,) * a.ndim)
    in_specs = [row(Q_LORA), row(KV_LORA), row(LANE), row(BRANCH_W, 1), full(wq), full(wk), full(wv)]
    args = [qn, ckv, kr, sz, wq, wk, wv]
    aliases = {}
    if has_cache:
        cache_ckv, cache_kr, cos, sa, sb = cache
        in_specs += [
            pl.BlockSpec((None, None, PAST_LEN, KV_LORA), lambda b: (b, layer, 0, 0)),
            pl.BlockSpec((None, None, PAST_LEN, LANE), lambda b: (b, layer, 0, 0)),
            full(cos), full(sa), full(sb),
            pl.BlockSpec(memory_space=pl.ANY),
        ]
        args += [cache_ckv, cache_kr, cos, sa, sb, prev]
        aliases = {len(args) - 1: 0}
    Tk = T + (PAST_LEN if has_cache else 0)
    return pl.pallas_call(
        functools.partial(_mla_kernel, T, has_cache),
        grid=(nb,),
        in_specs=in_specs,
        out_specs=pl.BlockSpec((T, MLA_W), lambda b: (blk0 + b, 0)),
        out_shape=jax.ShapeDtypeStruct((N_TOK, MLA_W), BF16),
        input_output_aliases=aliases,
        scratch_shapes=[pltpu.VMEM((H_MLA, T, LANE), BF16),
                        pltpu.VMEM((H_MLA, Tk, LANE), BF16),
                        pltpu.VMEM((H_MLA // 2, Tk, LANE), BF16)],
        compiler_params=_cparams(("arbitrary",)),
        name="mla_s" if has_cache else "mla_p",
    )(*args)


def _fourier_kernel(T, *refs):
    if len(refs) == 7:
        fu_ref, fz_ref, csw_ref, cts_ref, _, bf_ref, ab_scr = refs
    else:
        fu_ref, fz_ref, csw_ref, cts_ref, bf_ref, ab_scr = refs
    for g in range(F_GROUPS):
        sl = slice(g * F_GROUP_W, (g + 1) * F_GROUP_W)
        z = jnp.dot(fu_ref[:, sl], csw_ref[...], preferred_element_type=F32)
        ab_scr[0:T, sl] = z[:, :F_GROUP_W].astype(BF16)
        ab_scr[T:2 * T, sl] = z[:, F_GROUP_W:].astype(BF16)
    o = jnp.dot(cts_ref[...], ab_scr[...], preferred_element_type=F32)
    o = o * ((T * F_GROUP_W) ** -0.5)
    bf_ref[...] = (o * fz_ref[...].astype(F32)).astype(BF16)


def _fourier_call(T, nb, row0, fu, sz, csw, cts, prev=None):
    blk0 = row0 // T
    full = lambda a: pl.BlockSpec(a.shape, lambda b: (0,) * a.ndim)
    in_specs = [pl.BlockSpec((T, FOURIER_W), lambda b: (blk0 + b, 0)),
                pl.BlockSpec((T, BRANCH_W), lambda b: (blk0 + b, 2)),
                full(csw), full(cts)]
    args = [fu, sz, csw, cts]
    aliases = {}
    if prev is not None:
        in_specs.append(pl.BlockSpec(memory_space=pl.ANY))
        args.append(prev)
        aliases = {4: 0}
    return pl.pallas_call(
        functools.partial(_fourier_kernel, T),
        grid=(nb,),
        in_specs=in_specs,
        out_specs=pl.BlockSpec((T, FOURIER_W), lambda b: (blk0 + b, 0)),
        out_shape=jax.ShapeDtypeStruct((N_TOK, FOURIER_W), BF16),
        input_output_aliases=aliases,
        scratch_shapes=[pltpu.VMEM((2 * T, FOURIER_W), BF16)],
        compiler_params=_cparams(("arbitrary",)),
        name="fourier_s" if prev is not None else "fourier_p",
    )(*args)


def _merge_kernel(final, *refs):
    if final:
        (br_ref, bm_ref, bf_ref, sg_ref, xp_ref, xs_ref, mod_ref, wb_ref, wo_ref, fg_ref,
         yp_ref, ys_ref) = refs
    else:
        (br_ref, bm_ref, bf_ref, sg_ref, xp_ref, xs_ref, mod_ref, wb_ref, wo_ref,
         yp_ref, ys_ref) = refs
    is_prompt = pl.program_id(0) < N_P // TM
    merged = None
    for n, b_ref in enumerate((br_ref, bm_ref, bf_ref)):
        proj = jnp.dot(b_ref[...], wb_ref[n], preferred_element_type=F32)
        term = proj * sg_ref[:, n * D_MODEL:(n + 1) * D_MODEL].astype(F32)
        merged = term if merged is None else merged + term
    gate = mod_ref[0, :, 2 * D_MODEL:3 * D_MODEL]
    x = jnp.where(is_prompt, xp_ref[...], xs_ref[...])
    y = x + gate * jnp.dot(merged.astype(BF16), wo_ref[...], preferred_element_type=F32)
    if final:
        y = _rms(y, fg_ref[...])

    @pl.when(is_prompt)
    def _():
        yp_ref[...] = y

    @pl.when(jnp.logical_not(is_prompt))
    def _():
        ys_ref[...] = y


def _merge_call(br, bm, bf, sg, xp, xs, mod_l, wb, wo, final_g=None):
    final = final_g is not None
    row = lambda w: pl.BlockSpec((TM, w), lambda i: (i, 0))
    full = lambda a: pl.BlockSpec(a.shape, lambda i: (0,) * a.ndim)
    in_specs = [row(BRANCH_W), row(BRANCH_W), row(BRANCH_W), row(N_BRANCH * D_MODEL)]
    in_specs += _split_rows(D_MODEL)
    in_specs += [pl.BlockSpec((1, 1, 3 * D_MODEL), lambda i: (_tile_mod_row(i), 0, 0)),
                 full(wb), full(wo)]
    args = [br, bm, bf, sg, xp, xs, mod_l, wb, wo]
    if final:
        in_specs.append(full(final_g))
        args.append(final_g)
    return pl.pallas_call(
        functools.partial(_merge_kernel, final),
        grid=(N_TOK // TM,),
        in_specs=in_specs,
        out_specs=_split_rows(D_MODEL),
        out_shape=[jax.ShapeDtypeStruct((N_P, D_MODEL), F32),
                   jax.ShapeDtypeStruct((N_S, D_MODEL), F32)],
        compiler_params=_cparams(("arbitrary",)),
        name="merge_final" if final else "merge",
    )(*args)


def _dft_tables(T):
    def cs(n):
        kt = (np.arange(n)[:, None] * np.arange(n)[None, :]) % n
        ang = 2.0 * np.pi * kt.astype(np.float64) / n
        return np.cos(ang), np.sin(ang)
    ct, st = cs(T)
    cw, sw = cs(F_GROUP_W)
    cts = np.concatenate([ct, -st], axis=1).astype(np.float32)
    csw = np.concatenate([cw, sw], axis=1).astype(np.float32)
    return cts, csw


def _rope_tables(T):
    half = D_ROPE // 2
    nfreq = half // 2
    inv = ROPE_BASE ** (-np.arange(nfreq, dtype=np.float64) / nfreq)
    t = np.arange(T)
    pos = np.stack([t // GRID_W, t % GRID_W], axis=0).astype(np.float64)
    cos = np.ones((T, LANE), np.float64)
    sa = np.zeros((T, LANE), np.float64)
    sb = np.zeros((T, LANE), np.float64)
    for part in range(2):
        ang = pos[part][:, None] * inv[None, :]
        l1 = ROPE_LANE0 + part * half
        l2 = l1 + nfreq
        cos[:, l1:l1 + nfreq] = np.cos(ang)
        cos[:, l2:l2 + nfreq] = np.cos(ang)
        sa[:, l1:l1 + nfreq] = -np.sin(ang)
        sb[:, l2:l2 + nfreq] = np.sin(ang)
    return tuple(jnp.asarray(a.astype(np.float32)) for a in (cos, sa, sb))


def _permute_w_in(w):
    z = lambda n: jnp.zeros((D_MODEL, n), w.dtype)
    cols = [w[:, 0:1024],
            w[:, 1024:1536], w[:, 2208:2720], w[:, 3232:3744],
            w[:, 3744:6816],
            w[:, 1536:1920],
            w[:, 1920:2176],
            z(ROPE_LANE0), w[:, 2176:2208], z(LANE - ROPE_LANE0 - D_ROPE),
            w[:, 2720:3232]]
    return jnp.concatenate(cols, axis=1).astype(BF16)


def _pad_heads(w, n_heads, width, lo, hi):
    k = w.shape[0]
    wh = w.reshape(k, n_heads, width)[:, :, lo:hi]
    wh = jnp.pad(wh, ((0, 0), (0, 0), (0, LANE - (hi - lo))))
    return wh.reshape(k, n_heads * LANE).astype(BF16)


def kernel(x_prompt, x_sample, cache_ckv, cache_krope, state_ret, c, c_ctx, norm_g, w_mod, b_mod,
           w_in, ret_decay_logit, q_norm_g, w_q_up, kv_norm_g, w_kv_up, w_branch, w_out,
           final_norm_g):
    cv = jnp.concatenate([c_ctx[None, :], c, jnp.zeros((MOD_ROWS - 1 - DEC_BATCH, D_MODEL), F32)], axis=0)
    mod = _mod_call(cv, w_mod, b_mod)

    xp = x_prompt.reshape(N_P, D_MODEL)
    xs = x_sample.reshape(N_S, D_MODEL)
    cache_kr = jnp.pad(cache_krope, ((0, 0), (0, 0), (0, 0), (ROPE_LANE0, LANE - ROPE_LANE0 - D_ROPE)))
    rope = _rope_tables(DEC_SEQ)
    dft = {}
    for T in (SEQ, DEC_SEQ):
        cts, csw = _dft_tables(T)
        dft[T] = (jnp.asarray(cts).astype(BF16), jnp.asarray(csw).astype(BF16))
    logit = jnp.pad(ret_decay_logit, ((0, 0), (0, 8 - 2), (0, LANE - H_RET)))

    new_lat = None
    new_ret = None
    for l in range(DEPTH):
        mod_l = mod[l].reshape(MOD_ROWS, 1, 3 * D_MODEL)
        w_p = _permute_w_in(w_in[l])
        wq = _pad_heads(w_q_up[l], H_MLA, D_NOPE + D_ROPE, 0, D_NOPE + D_ROPE)
        wk = _pad_heads(w_kv_up[l], H_MLA, D_NOPE + D_VMLA, 0, D_NOPE)
        wv = w_kv_up[l].reshape(KV_LORA, H_MLA, D_NOPE + D_VMLA)[:, :, D_NOPE:]
        wv = wv.reshape(KV_LORA, MLA_W).astype(BF16)
        qkv, sz, sg, qn, ckv, kr, fu, *new_lat = _in_call(
            xp, xs, mod_l, norm_g[l][None, :], q_norm_g[l][None, :], kv_norm_g[l][None, :], w_p,
            l, prev=new_lat)

        br, new_ret = _ret_call(SEQ, BATCH, 0, qkv, sz, logit[l], l, prev=new_ret)
        br = _ret_call(DEC_SEQ, DEC_BATCH, N_P, qkv, sz, logit[l], l, state=state_ret, prev=br)
        bm = _mla_call(SEQ, BATCH, 0, qn, ckv, kr, sz, wq, wk, wv, l)
        bm = _mla_call(DEC_SEQ, DEC_BATCH, N_P, qn, ckv, kr, sz, wq, wk, wv, l,
                       cache=(cache_ckv, cache_kr) + rope, prev=bm)
        bf = _fourier_call(SEQ, BATCH, 0, fu, sz, dft[SEQ][1], dft[SEQ][0])
        bf = _fourier_call(DEC_SEQ, DEC_BATCH, N_P, fu, sz, dft[DEC_SEQ][1], dft[DEC_SEQ][0], prev=bf)

        xp, xs = _merge_call(br, bm, bf, sg, xp, xs, mod_l, w_branch[l].astype(BF16),
                             w_out[l].astype(BF16),
                             final_g=final_norm_g[None, :] if l == DEPTH - 1 else None)

    y_prompt = xp.reshape(BATCH, SEQ, D_MODEL)
    y_sample = xs.reshape(DEC_BATCH, DEC_SEQ, D_MODEL)
    return (y_prompt, y_sample, new_lat[0], new_lat[1], new_ret)
```

```python
import collections
import functools

import numpy as np
import jax
import jax.numpy as jnp
from jax import lax
from jax.experimental import pallas as pl
from jax.experimental.pallas import tpu as pltpu

F32 = jnp.float32
BF16 = jnp.bfloat16

D_MODEL = 1024
BATCH = 32
SEQ = 256
DEPTH = 2
DEC_BATCH = 4
DEC_SEQ = 1024
PAST_LEN = 512
GRID_W = 64
EPS = 1e-6
H_RET = 4
DK_RET = 64
DV_RET = 128
RET_W = H_RET * DV_RET
H_MLA = 8
Q_LORA = 384
KV_LORA = 256
D_NOPE = 64
D_ROPE = 32
D_VMLA = 64
MLA_W = H_MLA * D_VMLA
ROPE_BASE = 10000.0
F_GROUPS = 4
F_GROUP_W = 128
FOURIER_W = F_GROUPS * F_GROUP_W
N_BRANCH = 3
BRANCH_W = 512

LANE = 128
TM = 512
TQ = 256
MOD_ROWS = 8
ROPE_LANE0 = D_NOPE
VMEM_LIMIT = 56 * 1024 * 1024

Group = collections.namedtuple("Group", "tag T nb cached")
PROMPT = Group("p", SEQ, BATCH, False)
SAMPLE = Group("s", DEC_SEQ, DEC_BATCH, True)

SEG_QKV = (0, 1024)
SEG_Z = (1024, 2560)
SEG_G = (2560, 5632)
SEG_QL = (5632, 6016)
SEG_KV = (6016, 6272)
SEG_KR = (6272, 6400)
SEG_FU = (6400, 6912)
IN_WP = 6912


def _cparams(n_axes=1):
    return pltpu.CompilerParams(dimension_semantics=("arbitrary",) * n_axes,
                                vmem_limit_bytes=VMEM_LIMIT)


def _layer_spec(a, layer):
    shape = a.shape[1:]
    return pl.BlockSpec((None,) + shape, lambda i: (layer,) + (0,) * len(shape))


def _mod_spec(grp, layer):
    per_b = grp.T // TM
    row = (lambda i: 1 + i // per_b) if grp.cached else (lambda i: 0)
    return pl.BlockSpec((None, None, 1, 3 * D_MODEL), lambda i: (layer, row(i), 0, 0))


def _grow_spec(rows, depth, tail):
    return pl.BlockSpec((rows, depth) + tail, lambda i: (i,) + (0,) * (1 + len(tail)))


def _mod_kernel(c_ref, w_ref, b_ref, o_ref):
    cv = c_ref[...]
    s = cv * jax.nn.sigmoid(cv)
    o_ref[0] = jnp.dot(s, w_ref[0], preferred_element_type=F32,
                       precision=lax.Precision.HIGHEST) + b_ref[0]


def _mod_call(cv, w_mod, b_mod):
    nb = 3 * D_MODEL // 1024
    return pl.pallas_call(
        _mod_kernel,
        grid=(DEPTH, nb),
        in_specs=[
            pl.BlockSpec((MOD_ROWS, D_MODEL), lambda l, j: (0, 0)),
            pl.BlockSpec((1, D_MODEL, 1024), lambda l, j: (l, 0, j)),
            pl.BlockSpec((1, 1, 1024), lambda l, j: (l, 0, j)),
        ],
        out_specs=pl.BlockSpec((1, MOD_ROWS, 1024), lambda l, j: (l, 0, j)),
        out_shape=jax.ShapeDtypeStruct((DEPTH, MOD_ROWS, 3 * D_MODEL), F32),
        compiler_params=_cparams(2),
        name="mod",
    )(cv, w_mod, b_mod.reshape(DEPTH, 1, 3 * D_MODEL))


def _rms(x, g):
    return x * lax.rsqrt(jnp.mean(x * x, axis=-1, keepdims=True) + EPS) * g


def _in_kernel(cached, layer, *refs):
    x_ref, mod_ref, g_ref, qg_ref, kvg_ref, w_ref = refs[:6]
    n_in = 6 if (cached or layer == 0) else 8
    qkv_ref, sz_ref, sg_ref, qn_ref, ckv_ref, kr_ref, fu_ref = refs[n_in:n_in + 7]
    shift = mod_ref[:, 0:D_MODEL]
    scale = mod_ref[:, D_MODEL:2 * D_MODEL]
    h = _rms(x_ref[...], g_ref[...]) * (1.0 + scale) + shift
    hb = h.astype(BF16)

    def mm(a, b):
        return jnp.dot(hb, w_ref[:, a:b], preferred_element_type=F32)

    a0 = SEG_QKV[0]
    p = mm(a0, a0 + 1024)
    qw = H_RET * DK_RET
    qkv_ref[:, 0:qw] = p[:, 0:qw].astype(BF16)
    qkv_ref[:, qw:2 * qw] = (p[:, qw:2 * qw] * (DK_RET ** -0.5)).astype(BF16)
    qkv_ref[:, 2 * qw:] = p[:, 2 * qw:].astype(BF16)
    for j in range(3):
        a = SEG_Z[0] + j * BRANCH_W
        p = mm(a, a + BRANCH_W)
        sz_ref[:, j * BRANCH_W:(j + 1) * BRANCH_W] = (p * jax.nn.sigmoid(p)).astype(BF16)
    for j in range(N_BRANCH):
        a = SEG_G[0] + j * D_MODEL
        p = mm(a, a + D_MODEL)
        sg_ref[:, j * D_MODEL:(j + 1) * D_MODEL] = jax.nn.sigmoid(p).astype(BF16)
    qn_ref[...] = _rms(mm(*SEG_QL), qg_ref[...]).astype(BF16)
    ckv = _rms(mm(*SEG_KV), kvg_ref[...])
    ckv_ref[...] = ckv.astype(BF16)
    kr = mm(*SEG_KR)
    kr_ref[...] = kr
    fu_ref[...] = mm(*SEG_FU).astype(BF16)

    if not cached:
        nckv_ref, nkr_ref = refs[n_in + 7:]
        if layer:
            pckv_ref, pkr_ref = refs[6:8]
            nckv_ref[:, 0:layer] = pckv_ref[...]
            nkr_ref[:, 0:layer] = pkr_ref[...]
        for s in range(TM // SEQ):
            nckv_ref[s, layer] = ckv[s * SEQ:(s + 1) * SEQ]
            nkr_ref[s, layer] = kr[s * SEQ:(s + 1) * SEQ, ROPE_LANE0:ROPE_LANE0 + D_ROPE]


def _in_call(grp, layer, x, mod, g, qg, kvg, w_p, prev=None):
    n = grp.T * grp.nb
    row = lambda w: pl.BlockSpec((TM, w), lambda i: (i, 0))
    outs = [(1024, BF16), (3 * BRANCH_W, BF16), (N_BRANCH * D_MODEL, BF16), (Q_LORA, BF16),
            (KV_LORA, BF16), (LANE, F32), (FOURIER_W, BF16)]
    in_specs = [row(D_MODEL), _mod_spec(grp, layer)]
    in_specs += [_layer_spec(a, layer) for a in (g, qg, kvg, w_p)]
    args = [x, mod, g, qg, kvg, w_p]
    out_specs = [row(w) for w, _ in outs]
    out_shape = [jax.ShapeDtypeStruct((n, w), dt) for w, dt in outs]
    if not grp.cached:
        spt = TM // SEQ
        if layer:
            in_specs += [_grow_spec(spt, layer, (SEQ, KV_LORA)), _grow_spec(spt, layer, (SEQ, D_ROPE))]
            args += list(prev)
        out_specs += [_grow_spec(spt, layer + 1, (SEQ, KV_LORA)),
                      _grow_spec(spt, layer + 1, (SEQ, D_ROPE))]
        out_shape += [jax.ShapeDtypeStruct((grp.nb, layer + 1, SEQ, KV_LORA), F32),
                      jax.ShapeDtypeStruct((grp.nb, layer + 1, SEQ, D_ROPE), F32)]
    return pl.pallas_call(
        functools.partial(_in_kernel, grp.cached, layer),
        grid=(n // TM,),
        in_specs=in_specs,
        out_specs=out_specs,
        out_shape=out_shape,
        compiler_params=_cparams(),
        name="in_proj_" + grp.tag,
    )(*args)


def _ret_kernel(T, cached, layer, *refs):
    qkv_ref, rz_ref, lg_ref = refs[:3]
    if cached:
        st_ref, br_ref, mask_ref = refs[3:]
    else:
        br_ref, sf_ref, mask_ref = refs[-3:]
        if layer:
            sf_ref[0, 0:layer] = refs[3][0]
    lg = jax.nn.log_sigmoid(lg_ref[...])

    @pl.when(pl.program_id(0) == 0)
    def _():
        ii = lax.broadcasted_iota(jnp.int32, (T, T), 0)
        jj = lax.broadcasted_iota(jnp.int32, (T, T), 1)
        d = (ii - jj).astype(F32)
        ad = jnp.abs(d)
        for h in range(H_RET):
            lf = lg[0:1, h:h + 1]
            lb = lg[1:2, h:h + 1]
            mask_ref[h] = jnp.where(d > 0, jnp.exp(ad * lf),
                                    jnp.where(d < 0, jnp.exp(ad * lb), 2.0))

    qw = H_RET * DK_RET
    for h in range(H_RET):
        lf = lg[0:1, h:h + 1]
        lb = lg[1:2, h:h + 1]
        q = qkv_ref[:, h * DK_RET:(h + 1) * DK_RET]
        k = qkv_ref[:, qw + h * DK_RET:qw + (h + 1) * DK_RET]
        v = qkv_ref[:, 2 * qw + h * DV_RET:2 * qw + (h + 1) * DV_RET]
        s = lax.dot_general(q, k, (((1,), (1,)), ((), ())), preferred_element_type=F32)
        s = s * mask_ref[h]
        o = jnp.dot(s.astype(BF16), v, preferred_element_type=F32)
        if cached:
            rows = lax.broadcasted_iota(jnp.int32, (T, DV_RET), 0).astype(F32)
            o = o + jnp.exp((rows + 1.0) * lf) * jnp.dot(
                q, st_ref[0, h].astype(BF16), preferred_element_type=F32)
            o = o + jnp.exp((T - rows) * lb) * jnp.dot(
                q, st_ref[1, h].astype(BF16), preferred_element_type=F32)
        mu = jnp.mean(o, axis=-1, keepdims=True)
        c = o - mu
        var = jnp.mean(c * c, axis=-1, keepdims=True)
        on = c * lax.rsqrt(var + EPS)
        gate = rz_ref[:, h * DV_RET:(h + 1) * DV_RET].astype(F32)
        br_ref[:, h * DV_RET:(h + 1) * DV_RET] = (on * gate).astype(BF16)
        if not cached:
            rows = lax.broadcasted_iota(jnp.int32, (T, DK_RET), 0).astype(F32)
            kf = (k.astype(F32) * jnp.exp((T - 1.0 - rows) * lf)).astype(BF16)
            kb = (k.astype(F32) * jnp.exp(rows * lb)).astype(BF16)
            tn = (((0,), (0,)), ((), ()))
            sf_ref[0, layer, 0, h] = lax.dot_general(kf, v, tn, preferred_element_type=F32)
            sf_ref[0, layer, 1, h] = lax.dot_general(kb, v, tn, preferred_element_type=F32)


def _ret_call(grp, layer, qkv, sz, logit, state=None, prev=None):
    T = grp.T
    st_tail = (2, H_RET, DK_RET, DV_RET)
    in_specs = [pl.BlockSpec((T, 1024), lambda b: (b, 0)),
                pl.BlockSpec((T, BRANCH_W), lambda b: (b, 0)),
                _layer_spec(logit, layer)]
    args = [qkv, sz, logit]
    out_specs = [pl.BlockSpec((T, RET_W), lambda b: (b, 0))]
    out_shape = [jax.ShapeDtypeStruct((T * grp.nb, RET_W), BF16)]
    if grp.cached:
        in_specs.append(pl.BlockSpec((None, None) + st_tail, lambda b: (b, layer, 0, 0, 0, 0)))
        args.append(state)
    else:
        if layer:
            in_specs.append(_grow_spec(1, layer, st_tail))
            args.append(prev)
        out_specs.append(_grow_spec(1, layer + 1, st_tail))
        out_shape.append(jax.ShapeDtypeStruct((grp.nb, layer + 1) + st_tail, F32))
    return pl.pallas_call(
        functools.partial(_ret_kernel, T, grp.cached, layer),
        grid=(grp.nb,),
        in_specs=in_specs,
        out_specs=out_specs,
        out_shape=out_shape,
        scratch_shapes=[pltpu.VMEM((H_RET, T, T), F32)],
        compiler_params=_cparams(),
        name="ret_" + grp.tag,
    )(*args)


def _rope(x, cos, sa, sb):
    half = D_ROPE // 4
    return x * cos + pltpu.roll(x, LANE - half, 1) * sa + pltpu.roll(x, half, 1) * sb


def _mla_kernel(T, cached, *refs):
    if cached:
        (qn_ref, ckv_ref, kr_ref, mz_ref, wq_ref, wk_ref, wv_ref, cckv_ref, ckr_ref,
         cos_ref, sa_ref, sb_ref, bm_ref, q_scr, k_scr, v_scr) = refs
    else:
        (qn_ref, ckv_ref, kr_ref, mz_ref, wq_ref, wk_ref, wv_ref,
         bm_ref, q_scr, k_scr, v_scr) = refs
    P = PAST_LEN if cached else 0
    scale = (D_NOPE + D_ROPE) ** -0.5

    qn = qn_ref[...]
    ckv = ckv_ref[...]
    kr = kr_ref[...]
    if cached:
        kr = _rope(kr, cos_ref[...], sa_ref[...], sb_ref[...])
        cc = cckv_ref[...].astype(BF16)
    for h in range(H_MLA):
        sl = slice(h * LANE, (h + 1) * LANE)
        qh = jnp.dot(qn, wq_ref[:, sl], preferred_element_type=F32)
        if cached:
            qh = _rope(qh, cos_ref[...], sa_ref[...], sb_ref[...])
            kc = jnp.dot(cc, wk_ref[:, sl], preferred_element_type=F32)
            k_scr[h, 0:P] = (kc + ckr_ref[...]).astype(BF16)
        q_scr[h] = qh.astype(BF16)
        kn = jnp.dot(ckv, wk_ref[:, sl], preferred_element_type=F32)
        k_scr[h, P:P + T] = (kn + kr).astype(BF16)
    for p in range(H_MLA // 2):
        sl = slice(p * LANE, (p + 1) * LANE)
        if cached:
            v_scr[p, 0:P] = jnp.dot(cc, wv_ref[:, sl], preferred_element_type=F32).astype(BF16)
        v_scr[p, P:P + T] = jnp.dot(ckv, wv_ref[:, sl], preferred_element_type=F32).astype(BF16)

    lane = lax.broadcasted_iota(jnp.int32, (TQ, LANE), 1)

    def attend(r0):
        rs = pl.ds(r0, TQ)
        for p in range(H_MLA // 2):
            halves = []
            for h in (2 * p, 2 * p + 1):
                s = lax.dot_general(q_scr[h, rs], k_scr[h], (((1,), (1,)), ((), ())),
                                    preferred_element_type=F32) * scale
                e = jnp.exp(s - jnp.max(s, axis=-1, keepdims=True))
                l = jnp.sum(e, axis=-1, keepdims=True)
                halves.append(jnp.dot(e.astype(BF16), v_scr[p], preferred_element_type=F32) / l)
            o = jnp.where(lane < D_VMLA, halves[0], halves[1])
            gate = mz_ref[rs, p * LANE:(p + 1) * LANE].astype(F32)
            bm_ref[rs, p * LANE:(p + 1) * LANE] = (o * gate).astype(BF16)

    if T == TQ:
        attend(0)
    else:
        def body(qb, carry):
            attend(pl.multiple_of(qb * TQ, TQ))
            return carry
        lax.fori_loop(0, T // TQ, body, 0)


def _mla_call(grp, layer, qn, ckv, kr, sz, wq, wk, wv, cache=None):
    T = grp.T
    row = lambda w, c=0: pl.BlockSpec((T, w), lambda b: (b, c))
    full = lambda a: pl.BlockSpec(a.shape, lambda b: (0,) * a.ndim)
    in_specs = [row(Q_LORA), row(KV_LORA), row(LANE), row(BRANCH_W, 1)]
    in_specs += [_layer_spec(a, layer) for a in (wq, wk, wv)]
    args = [qn, ckv, kr, sz, wq, wk, wv]
    if grp.cached:
        cache_ckv, cache_kr, cos, sa, sb = cache
        in_specs += [
            pl.BlockSpec((None, None, PAST_LEN, KV_LORA), lambda b: (b, layer, 0, 0)),
            pl.BlockSpec((None, None, PAST_LEN, LANE), lambda b: (b, layer, 0, 0)),
            full(cos), full(sa), full(sb),
        ]
        args += [cache_ckv, cache_kr, cos, sa, sb]
    Tk = T + (PAST_LEN if grp.cached else 0)
    return pl.pallas_call(
        functools.partial(_mla_kernel, T, grp.cached),
        grid=(grp.nb,),
        in_specs=in_specs,
        out_specs=pl.BlockSpec((T, MLA_W), lambda b: (b, 0)),
        out_shape=jax.ShapeDtypeStruct((T * grp.nb, MLA_W), BF16),
        scratch_shapes=[pltpu.VMEM((H_MLA, T, LANE), BF16),
                        pltpu.VMEM((H_MLA, Tk, LANE), BF16),
                        pltpu.VMEM((H_MLA // 2, Tk, LANE), BF16)],
        compiler_params=_cparams(),
        name="mla_" + grp.tag,
    )(*args)


def _fourier_kernel(T, fu_ref, fz_ref, csw_ref, cts_ref, bf_ref, ab_scr):
    for g in range(F_GROUPS):
        sl = slice(g * F_GROUP_W, (g + 1) * F_GROUP_W)
        z = jnp.dot(fu_ref[:, sl], csw_ref[...], preferred_element_type=F32)
        ab_scr[0:T, sl] = z[:, :F_GROUP_W].astype(BF16)
        ab_scr[T:2 * T, sl] = z[:, F_GROUP_W:].astype(BF16)
    o = jnp.dot(cts_ref[...], ab_scr[...], preferred_element_type=F32)
    o = o * ((T * F_GROUP_W) ** -0.5)
    bf_ref[...] = (o * fz_ref[...].astype(F32)).astype(BF16)


def _fourier_call(grp, fu, sz, csw, cts):
    T = grp.T
    full = lambda a: pl.BlockSpec(a.shape, lambda b: (0,) * a.ndim)
    return pl.pallas_call(
        functools.partial(_fourier_kernel, T),
        grid=(grp.nb,),
        in_specs=[pl.BlockSpec((T, FOURIER_W), lambda b: (b, 0)),
                  pl.BlockSpec((T, BRANCH_W), lambda b: (b, 2)),
                  full(csw), full(cts)],
        out_specs=pl.BlockSpec((T, FOURIER_W), lambda b: (b, 0)),
        out_shape=jax.ShapeDtypeStruct((T * grp.nb, FOURIER_W), BF16),
        scratch_shapes=[pltpu.VMEM((2 * T, FOURIER_W), BF16)],
        compiler_params=_cparams(),
        name="fourier_" + grp.tag,
    )(fu, sz, csw, cts)


def _merge_kernel(final, *refs):
    br_ref, bm_ref, bf_ref, sg_ref, x_ref, mod_ref, wb_ref, wo_ref = refs[:8]
    y_ref = refs[-1]
    merged = None
    for n, b_ref in enumerate((br_ref, bm_ref, bf_ref)):
        proj = jnp.dot(b_ref[...], wb_ref[n], preferred_element_type=F32)
        term = proj * sg_ref[:, n * D_MODEL:(n + 1) * D_MODEL].astype(F32)
        merged = term if merged is None else merged + term
    gate = mod_ref[:, 2 * D_MODEL:3 * D_MODEL]
    y = x_ref[...] + gate * jnp.dot(merged.astype(BF16), wo_ref[...], preferred_element_type=F32)
    if final:
        y = _rms(y, refs[8][...])
    y_ref[...] = y


def _merge_call(grp, layer, br, bm, bf, sg, x, mod, wb, wo, final_g=None):
    final = final_g is not None
    n = grp.T * grp.nb
    row = lambda w: pl.BlockSpec((TM, w), lambda i: (i, 0))
    in_specs = [row(BRANCH_W), row(BRANCH_W), row(BRANCH_W), row(N_BRANCH * D_MODEL), row(D_MODEL),
                _mod_spec(grp, layer), _layer_spec(wb, layer), _layer_spec(wo, layer)]
    args = [br, bm, bf, sg, x, mod, wb, wo]
    if final:
        in_specs.append(pl.BlockSpec(final_g.shape, lambda i: (0, 0)))
        args.append(final_g)
    return pl.pallas_call(
        functools.partial(_merge_kernel, final),
        grid=(n // TM,),
        in_specs=in_specs,
        out_specs=row(D_MODEL),
        out_shape=jax.ShapeDtypeStruct((n, D_MODEL), F32),
        compiler_params=_cparams(),
        name="merge_" + grp.tag,
    )(*args)


def _dft_tables(T):
    def cs(n):
        kt = (np.arange(n)[:, None] * np.arange(n)[None, :]) % n
        ang = 2.0 * np.pi * kt.astype(np.float64) / n
        return np.cos(ang), np.sin(ang)
    ct, st = cs(T)
    cw, sw = cs(F_GROUP_W)
    cts = np.concatenate([ct, -st], axis=1).astype(np.float32)
    csw = np.concatenate([cw, sw], axis=1).astype(np.float32)
    return cts, csw


def _rope_tables(T):
    half = D_ROPE // 2
    nfreq = half // 2
    inv = ROPE_BASE ** (-np.arange(nfreq, dtype=np.float64) / nfreq)
    t = np.arange(T)
    pos = np.stack([t // GRID_W, t % GRID_W], axis=0).astype(np.float64)
    cos = np.ones((T, LANE), np.float64)
    sa = np.zeros((T, LANE), np.float64)
    sb = np.zeros((T, LANE), np.float64)
    for part in range(2):
        ang = pos[part][:, None] * inv[None, :]
        l1 = ROPE_LANE0 + part * half
        l2 = l1 + nfreq
        cos[:, l1:l1 + nfreq] = np.cos(ang)
        cos[:, l2:l2 + nfreq] = np.cos(ang)
        sa[:, l1:l1 + nfreq] = -np.sin(ang)
        sb[:, l2:l2 + nfreq] = np.sin(ang)
    return tuple(jnp.asarray(a.astype(np.float32)) for a in (cos, sa, sb))


def _permute_w_in(w):
    z = lambda n: jnp.zeros(w.shape[:2] + (n,), w.dtype)
    cols = [w[..., 0:1024],
            w[..., 1024:1536], w[..., 2208:2720], w[..., 3232:3744],
            w[..., 3744:6816],
            w[..., 1536:1920],
            w[..., 1920:2176],
            z(ROPE_LANE0), w[..., 2176:2208], z(LANE - ROPE_LANE0 - D_ROPE),
            w[..., 2720:3232]]
    return jnp.concatenate(cols, axis=-1).astype(BF16)


def _pad_heads(w, n_heads, width, lo, hi):
    d, k = w.shape[:2]
    wh = w.reshape(d, k, n_heads, width)[..., lo:hi]
    wh = jnp.pad(wh, ((0, 0), (0, 0), (0, 0), (0, LANE - (hi - lo))))
    return wh.reshape(d, k, n_heads * LANE).astype(BF16)


def kernel(x_prompt, x_sample, cache_ckv, cache_krope, state_ret, c, c_ctx, norm_g, w_mod, b_mod,
           w_in, ret_decay_logit, q_norm_g, w_q_up, kv_norm_g, w_kv_up, w_branch, w_out,
           final_norm_g):
    cv = jnp.concatenate([c_ctx[None, :], c, jnp.zeros((MOD_ROWS - 1 - DEC_BATCH, D_MODEL), F32)], axis=0)
    mod = _mod_call(cv, w_mod, b_mod).reshape(DEPTH, MOD_ROWS, 1, 3 * D_MODEL)

    cache_kr = jnp.pad(cache_krope, ((0, 0), (0, 0), (0, 0), (ROPE_LANE0, LANE - ROPE_LANE0 - D_ROPE)))
    rope = _rope_tables(DEC_SEQ)
    dft = {}
    for T in (SEQ, DEC_SEQ):
        cts, csw = _dft_tables(T)
        dft[T] = (jnp.asarray(csw).astype(BF16), jnp.asarray(cts).astype(BF16))
    logit = jnp.pad(ret_decay_logit, ((0, 0), (0, 8 - 2), (0, LANE - H_RET)))
    g, qg, kvg = norm_g[:, None, :], q_norm_g[:, None, :], kv_norm_g[:, None, :]
    w_p = _permute_w_in(w_in)
    wq = _pad_heads(w_q_up, H_MLA, D_NOPE + D_ROPE, 0, D_NOPE + D_ROPE)
    wk = _pad_heads(w_kv_up, H_MLA, D_NOPE + D_VMLA, 0, D_NOPE)
    wv = w_kv_up.reshape(DEPTH, KV_LORA, H_MLA, D_NOPE + D_VMLA)[..., D_NOPE:]
    wv = wv.reshape(DEPTH, KV_LORA, MLA_W).astype(BF16)
    wb = w_branch.astype(BF16)
    wo = w_out.astype(BF16)
    final_g = final_norm_g[None, :]

    x = {PROMPT: x_prompt.reshape(BATCH * SEQ, D_MODEL), SAMPLE: x_sample.reshape(DEC_BATCH * DEC_SEQ, D_MODEL)}
    new_lat, new_ret = None, None
    for l in range(DEPTH):
        for grp in (PROMPT, SAMPLE):
            res = _in_call(grp, l, x[grp], mod, g, qg, kvg, w_p, prev=new_lat)
            qkv, sz, sg, qn, ckv, kr, fu = res[:7]
            if grp.cached:
                (br,) = _ret_call(grp, l, qkv, sz, logit, state=state_ret)
                bm = _mla_call(grp, l, qn, ckv, kr, sz, wq, wk, wv, cache=(cache_ckv, cache_kr) + rope)
            else:
                new_lat = res[7:]
                br, new_ret = _ret_call(grp, l, qkv, sz, logit, prev=new_ret)
                bm = _mla_call(grp, l, qn, ckv, kr, sz, wq, wk, wv)
            bf = _fourier_call(grp, fu, sz, *dft[grp.T])
            x[grp] = _merge_call(grp, l, br, bm, bf, sg, x[grp], mod, wb, wo,
                                 final_g=final_g if l == DEPTH - 1 else None)

    y_prompt = x[PROMPT].reshape(BATCH, SEQ, D_MODEL)
    y_sample = x[SAMPLE].reshape(DEC_BATCH, DEC_SEQ, D_MODEL)
    return (y_prompt, y_sample, new_lat[0], new_lat[1], new_ret)
```

```python
import collections
import functools

import numpy as np
import jax
import jax.numpy as jnp
from jax import lax
from jax.experimental import pallas as pl
from jax.experimental.pallas import tpu as pltpu

F32 = jnp.float32
BF16 = jnp.bfloat16

D_MODEL = 1024
BATCH = 32
SEQ = 256
DEPTH = 2
DEC_BATCH = 4
DEC_SEQ = 1024
PAST_LEN = 512
GRID_W = 64
EPS = 1e-6
H_RET = 4
DK_RET = 64
DV_RET = 128
RET_W = H_RET * DV_RET
H_MLA = 8
Q_LORA = 384
KV_LORA = 256
D_NOPE = 64
D_ROPE = 32
D_VMLA = 64
MLA_W = H_MLA * D_VMLA
ROPE_BASE = 10000.0
F_GROUPS = 4
F_GROUP_W = 128
FOURIER_W = F_GROUPS * F_GROUP_W
N_BRANCH = 3
BRANCH_W = 512

LANE = 128
TM = 512
TQ = 256
MOD_ROWS = 8
ROPE_LANE0 = D_NOPE
VMEM_LIMIT = 56 * 1024 * 1024
ST_TAIL = (2, H_RET, DK_RET, DV_RET)

SEG_QKV = (0, 1024)
SEG_Z = (1024, 2560)
SEG_G = (2560, 5632)
SEG_QL = (5632, 6016)
SEG_KV = (6016, 6272)
SEG_KR = (6272, 6400)
SEG_FU = (6400, 6912)
IN_WP = 6912

NT = (((1,), (1,)), ((), ()))
TN = (((0,), (0,)), ((), ()))


def _cparams(n_axes=1):
    return pltpu.CompilerParams(dimension_semantics=("arbitrary",) * n_axes,
                                vmem_limit_bytes=VMEM_LIMIT)


def _layer_spec(a, layer):
    shape = a.shape[1:]
    return pl.BlockSpec((None,) + shape, lambda i: (layer,) + (0,) * len(shape))


def _full_spec(a):
    return pl.BlockSpec(a.shape, lambda i: (0,) * a.ndim)


def _mod_spec(layer, row):
    return pl.BlockSpec((None, None, 1, 3 * D_MODEL), lambda i: (layer, row(i), 0, 0))


def _grow_spec(depth, tail):
    return pl.BlockSpec((1, depth) + tail, lambda i: (i,) + (0,) * (1 + len(tail)))


def _dot(a, b):
    return jnp.dot(a, b, preferred_element_type=F32)


def _rms(x, g):
    return x * lax.rsqrt(jnp.mean(x * x, axis=-1, keepdims=True) + EPS) * g


def _mod_kernel(c_ref, w_ref, b_ref, o_ref):
    cv = c_ref[...]
    s = cv * jax.nn.sigmoid(cv)
    o_ref[0] = jnp.dot(s, w_ref[0], preferred_element_type=F32,
                       precision=lax.Precision.HIGHEST) + b_ref[0]


def _mod_call(cv, w_mod, b_mod):
    nb = 3 * D_MODEL // 1024
    return pl.pallas_call(
        _mod_kernel,
        grid=(DEPTH, nb),
        in_specs=[
            pl.BlockSpec((MOD_ROWS, D_MODEL), lambda l, j: (0, 0)),
            pl.BlockSpec((1, D_MODEL, 1024), lambda l, j: (l, 0, j)),
            pl.BlockSpec((1, 1, 1024), lambda l, j: (l, 0, j)),
        ],
        out_specs=pl.BlockSpec((1, MOD_ROWS, 1024), lambda l, j: (l, 0, j)),
        out_shape=jax.ShapeDtypeStruct((DEPTH, MOD_ROWS, 3 * D_MODEL), F32),
        compiler_params=_cparams(2),
        name="mod",
    )(cv, w_mod, b_mod.reshape(DEPTH, 1, 3 * D_MODEL))


Proj = collections.namedtuple("Proj", "q k v sz sg qn ckv kr fu")


def _project(x, mod_ref, g_ref, qg_ref, kvg_ref, w_ref):
    shift = mod_ref[:, 0:D_MODEL]
    scale = mod_ref[:, D_MODEL:2 * D_MODEL]
    hb = (_rms(x, g_ref[...]) * (1.0 + scale) + shift).astype(BF16)

    def mm(a, b):
        return _dot(hb, w_ref[:, a:b])

    qw = H_RET * DK_RET
    p = mm(*SEG_QKV)
    q = p[:, 0:qw].astype(BF16)
    k = (p[:, qw:2 * qw] * (DK_RET ** -0.5)).astype(BF16)
    v = p[:, 2 * qw:].astype(BF16)
    sz = []
    for j in range(3):
        a = SEG_Z[0] + j * BRANCH_W
        p = mm(a, a + BRANCH_W)
        sz.append(p * jax.nn.sigmoid(p))
    sg = []
    for j in range(N_BRANCH):
        a = SEG_G[0] + j * D_MODEL
        sg.append(jax.nn.sigmoid(mm(a, a + D_MODEL)))
    qn = _rms(mm(*SEG_QL), qg_ref[...]).astype(BF16)
    ckv = _rms(mm(*SEG_KV), kvg_ref[...])
    kr = mm(*SEG_KR)
    fu = mm(*SEG_FU).astype(BF16)
    return Proj(q, k, v, sz, sg, qn, ckv, kr, fu)


def _init_masks(T, lg, mask_ref):
    ii = lax.broadcasted_iota(jnp.int32, (T, T), 0)
    jj = lax.broadcasted_iota(jnp.int32, (T, T), 1)
    d = (ii - jj).astype(F32)
    ad = jnp.abs(d)
    for h in range(H_RET):
        lf = lg[0:1, h:h + 1]
        lb = lg[1:2, h:h + 1]
        mask_ref[h] = jnp.where(d > 0, jnp.exp(ad * lf), jnp.where(d < 0, jnp.exp(ad * lb), 2.0))


def _ret_head(T, q, k, v, mask, lf, lb, st=None):
    s = lax.dot_general(q, k, NT, preferred_element_type=F32) * mask
    o = _dot(s.astype(BF16), v)
    if st is not None:
        rows = lax.broadcasted_iota(jnp.int32, (T, DV_RET), 0).astype(F32)
        o = o + jnp.exp((rows + 1.0) * lf) * _dot(q, st[0].astype(BF16))
        o = o + jnp.exp((T - rows) * lb) * _dot(q, st[1].astype(BF16))
    mu = jnp.mean(o, axis=-1, keepdims=True)
    c = o - mu
    var = jnp.mean(c * c, axis=-1, keepdims=True)
    return c * lax.rsqrt(var + EPS)


def _ret_states(T, k, v, lf, lb):
    rows = lax.broadcasted_iota(jnp.int32, (T, DK_RET), 0).astype(F32)
    kf = (k.astype(F32) * jnp.exp((T - 1.0 - rows) * lf)).astype(BF16)
    kb = (k.astype(F32) * jnp.exp(rows * lb)).astype(BF16)
    return (lax.dot_general(kf, v, TN, preferred_element_type=F32),
            lax.dot_general(kb, v, TN, preferred_element_type=F32))


def _rope(x, cos, sa, sb):
    half = D_ROPE // 4
    return x * cos + pltpu.roll(x, LANE - half, 1) * sa + pltpu.roll(x, half, 1) * sb


def _attend_pair(q2, k2, vp, lane):
    scale = (D_NOPE + D_ROPE) ** -0.5
    halves = []
    for qh, kh in zip(q2, k2):
        s = lax.dot_general(qh, kh, NT, preferred_element_type=F32) * scale
        e = jnp.exp(s - jnp.max(s, axis=-1, keepdims=True))
        l = jnp.sum(e, axis=-1, keepdims=True)
        halves.append(_dot(e.astype(BF16), vp) / l)
    return jnp.where(lane < D_VMLA, halves[0], halves[1])


def _fourier(T, fu, csw_ref, cts_ref):
    a, b = [], []
    for g in range(F_GROUPS):
        z = _dot(fu[:, g * F_GROUP_W:(g + 1) * F_GROUP_W], csw_ref[...])
        a.append(z[:, :F_GROUP_W])
        b.append(z[:, F_GROUP_W:])
    a = jnp.concatenate(a, axis=1).astype(BF16)
    b = jnp.concatenate(b, axis=1).astype(BF16)
    o = _dot(cts_ref[:, 0:T], a) + _dot(cts_ref[:, T:2 * T], b)
    return o * ((T * F_GROUP_W) ** -0.5)


def _merge(branches, sg, x, mod_ref, wb_ref, wo_ref, fg_ref):
    merged = None
    for n, b in enumerate(branches):
        term = _dot(b, wb_ref[n]) * sg[n]
        merged = term if merged is None else merged + term
    gate = mod_ref[:, 2 * D_MODEL:3 * D_MODEL]
    y = x + gate * _dot(merged.astype(BF16), wo_ref[...])
    if fg_ref is not None:
        y = _rms(y, fg_ref[...])
    return y


def _prompt_kernel(layer, final, *refs):
    (x_ref, mod_ref, g_ref, qg_ref, kvg_ref, w_ref, lg_ref, wq_ref, wk_ref, wv_ref,
     csw_ref, cts_ref, wb_ref, wo_ref) = refs[:14]
    n_in = 14
    fg_ref = None
    if final:
        fg_ref = refs[n_in]
        n_in += 1
    if layer:
        pckv_ref, pkr_ref, pst_ref = refs[n_in:n_in + 3]
        n_in += 3
    y_ref, nckv_ref, nkr_ref, nst_ref, mask_ref = refs[n_in:]
    T = SEQ
    lg = jax.nn.log_sigmoid(lg_ref[...])

    @pl.when(pl.program_id(0) == 0)
    def _():
        _init_masks(T, lg, mask_ref)

    x = x_ref[...]
    pr = _project(x, mod_ref, g_ref, qg_ref, kvg_ref, w_ref)

    if layer:
        nckv_ref[0, 0:layer] = pckv_ref[0]
        nkr_ref[0, 0:layer] = pkr_ref[0]
        nst_ref[0, 0:layer] = pst_ref[0]
    nckv_ref[0, layer] = pr.ckv
    nkr_ref[0, layer] = pr.kr[:, ROPE_LANE0:ROPE_LANE0 + D_ROPE]

    o_r = []
    for h in range(H_RET):
        lf = lg[0:1, h:h + 1]
        lb = lg[1:2, h:h + 1]
        q = pr.q[:, h * DK_RET:(h + 1) * DK_RET]
        k = pr.k[:, h * DK_RET:(h + 1) * DK_RET]
        v = pr.v[:, h * DV_RET:(h + 1) * DV_RET]
        o_r.append(_ret_head(T, q, k, v, mask_ref[h], lf, lb))
        sf, sb = _ret_states(T, k, v, lf, lb)
        nst_ref[0, layer, 0, h] = sf
        nst_ref[0, layer, 1, h] = sb
    b_r = (jnp.concatenate(o_r, axis=1) * pr.sz[0]).astype(BF16)

    ckvb = pr.ckv.astype(BF16)
    qs, ks = [], []
    for h in range(H_MLA):
        sl = slice(h * LANE, (h + 1) * LANE)
        qs.append(_dot(pr.qn, wq_ref[:, sl]).astype(BF16))
        ks.append((_dot(ckvb, wk_ref[:, sl]) + pr.kr).astype(BF16))
    lane = lax.broadcasted_iota(jnp.int32, (T, LANE), 1)
    o_m = []
    for p in range(H_MLA // 2):
        vp = _dot(ckvb, wv_ref[:, p * LANE:(p + 1) * LANE]).astype(BF16)
        o_m.append(_attend_pair(qs[2 * p:2 * p + 2], ks[2 * p:2 * p + 2], vp, lane))
    b_m = (jnp.concatenate(o_m, axis=1) * pr.sz[1]).astype(BF16)

    b_f = (_fourier(T, pr.fu, csw_ref, cts_ref) * pr.sz[2]).astype(BF16)

    y_ref[...] = _merge((b_r, b_m, b_f), pr.sg, x, mod_ref, wb_ref, wo_ref, fg_ref)


def _prompt_call(layer, x, mod, g, qg, kvg, w_p, logit, wq, wk, wv, csw, cts, wb, wo,
                 final_g=None, prev=None):
    T = SEQ
    row = pl.BlockSpec((T, D_MODEL), lambda i: (i, 0))
    in_specs = [row, _mod_spec(layer, lambda i: 0)]
    in_specs += [_layer_spec(a, layer) for a in (g, qg, kvg, w_p, logit, wq, wk, wv)]
    in_specs += [_full_spec(csw), _full_spec(cts), _layer_spec(wb, layer), _layer_spec(wo, layer)]
    args = [x, mod, g, qg, kvg, w_p, logit, wq, wk, wv, csw, cts, wb, wo]
    if final_g is not None:
        in_specs.append(_full_spec(final_g))
        args.append(final_g)
    tails = ((SEQ, KV_LORA), (SEQ, D_ROPE), ST_TAIL)
    if layer:
        in_specs += [_grow_spec(layer, t) for t in tails]
        args += list(prev)
    return pl.pallas_call(
        functools.partial(_prompt_kernel, layer, final_g is not None),
        grid=(BATCH,),
        in_specs=in_specs,
        out_specs=[row] + [_grow_spec(layer + 1, t) for t in tails],
        out_shape=[jax.ShapeDtypeStruct((BATCH * T, D_MODEL), F32)]
        + [jax.ShapeDtypeStruct((BATCH, layer + 1) + t, F32) for t in tails],
        scratch_shapes=[pltpu.VMEM((H_RET, T, T), F32)],
        compiler_params=_cparams(),
        name="prompt_layer",
    )(*args)


def _in_kernel(x_ref, mod_ref, g_ref, qg_ref, kvg_ref, w_ref,
               qkv_ref, sz_ref, sg_ref, qn_ref, ckv_ref, kr_ref, fu_ref):
    pr = _project(x_ref[...], mod_ref, g_ref, qg_ref, kvg_ref, w_ref)
    qw = H_RET * DK_RET
    qkv_ref[:, 0:qw] = pr.q
    qkv_ref[:, qw:2 * qw] = pr.k
    qkv_ref[:, 2 * qw:] = pr.v
    for j in range(3):
        sz_ref[:, j * BRANCH_W:(j + 1) * BRANCH_W] = pr.sz[j].astype(BF16)
    for j in range(N_BRANCH):
        sg_ref[:, j * D_MODEL:(j + 1) * D_MODEL] = pr.sg[j].astype(BF16)
    qn_ref[...] = pr.qn
    ckv_ref[...] = pr.ckv.astype(BF16)
    kr_ref[...] = pr.kr
    fu_ref[...] = pr.fu


def _in_call(layer, x, mod, g, qg, kvg, w_p):
    n = x.shape[0]
    per_b = DEC_SEQ // TM
    row = lambda w: pl.BlockSpec((TM, w), lambda i: (i, 0))
    outs = [(1024, BF16), (3 * BRANCH_W, BF16), (N_BRANCH * D_MODEL, BF16), (Q_LORA, BF16),
            (KV_LORA, BF16), (LANE, F32), (FOURIER_W, BF16)]
    return pl.pallas_call(
        _in_kernel,
        grid=(n // TM,),
        in_specs=[row(D_MODEL), _mod_spec(layer, lambda i: 1 + i // per_b)]
        + [_layer_spec(a, layer) for a in (g, qg, kvg, w_p)],
        out_specs=[row(w) for w, _ in outs],
        out_shape=[jax.ShapeDtypeStruct((n, w), dt) for w, dt in outs],
        compiler_params=_cparams(),
        name="in_proj_s",
    )(x, mod, g, qg, kvg, w_p)


def _ret_kernel(T, qkv_ref, rz_ref, lg_ref, st_ref, br_ref, mask_ref):
    lg = jax.nn.log_sigmoid(lg_ref[...])

    @pl.when(pl.program_id(0) == 0)
    def _():
        _init_masks(T, lg, mask_ref)

    qw = H_RET * DK_RET
    for h in range(H_RET):
        q = qkv_ref[:, h * DK_RET:(h + 1) * DK_RET]
        k = qkv_ref[:, qw + h * DK_RET:qw + (h + 1) * DK_RET]
        v = qkv_ref[:, 2 * qw + h * DV_RET:2 * qw + (h + 1) * DV_RET]
        on = _ret_head(T, q, k, v, mask_ref[h], lg[0:1, h:h + 1], lg[1:2, h:h + 1],
                       st=(st_ref[0, h], st_ref[1, h]))
        gate = rz_ref[:, h * DV_RET:(h + 1) * DV_RET].astype(F32)
        br_ref[:, h * DV_RET:(h + 1) * DV_RET] = (on * gate).astype(BF16)


def _ret_call(layer, qkv, sz, logit, state):
    T = DEC_SEQ
    return pl.pallas_call(
        functools.partial(_ret_kernel, T),
        grid=(DEC_BATCH,),
        in_specs=[pl.BlockSpec((T, 1024), lambda b: (b, 0)),
                  pl.BlockSpec((T, BRANCH_W), lambda b: (b, 0)),
                  _layer_spec(logit, layer),
                  pl.BlockSpec((None, None) + ST_TAIL, lambda b: (b, layer, 0, 0, 0, 0))],
        out_specs=pl.BlockSpec((T, RET_W), lambda b: (b, 0)),
        out_shape=jax.ShapeDtypeStruct((T * DEC_BATCH, RET_W), BF16),
        scratch_shapes=[pltpu.VMEM((H_RET, T, T), F32)],
        compiler_params=_cparams(),
        name="ret_s",
    )(qkv, sz, logit, state)


def _mla_kernel(T, qn_ref, ckv_ref, kr_ref, mz_ref, wq_ref, wk_ref, wv_ref, cckv_ref, ckr_ref,
                cos_ref, sa_ref, sb_ref, bm_ref, q_scr, k_scr, v_scr):
    P = PAST_LEN
    qn = qn_ref[...]
    ckv = ckv_ref[...]
    kr = _rope(kr_ref[...], cos_ref[...], sa_ref[...], sb_ref[...])
    cc = cckv_ref[...].astype(BF16)
    for h in range(H_MLA):
        sl = slice(h * LANE, (h + 1) * LANE)
        qh = _rope(_dot(qn, wq_ref[:, sl]), cos_ref[...], sa_ref[...], sb_ref[...])
        q_scr[h] = qh.astype(BF16)
        k_scr[h, 0:P] = (_dot(cc, wk_ref[:, sl]) + ckr_ref[...]).astype(BF16)
        k_scr[h, P:P + T] = (_dot(ckv, wk_ref[:, sl]) + kr).astype(BF16)
    for p in range(H_MLA // 2):
        sl = slice(p * LANE, (p + 1) * LANE)
        v_scr[p, 0:P] = _dot(cc, wv_ref[:, sl]).astype(BF16)
        v_scr[p, P:P + T] = _dot(ckv, wv_ref[:, sl]).astype(BF16)

    lane = lax.broadcasted_iota(jnp.int32, (TQ, LANE), 1)

    def body(qb, carry):
        rs = pl.ds(pl.multiple_of(qb * TQ, TQ), TQ)
        for p in range(H_MLA // 2):
            hs = (2 * p, 2 * p + 1)
            o = _attend_pair([q_scr[h, rs] for h in hs], [k_scr[h] for h in hs], v_scr[p], lane)
            gate = mz_ref[rs, p * LANE:(p + 1) * LANE].astype(F32)
            bm_ref[rs, p * LANE:(p + 1) * LANE] = (o * gate).astype(BF16)
        return carry

    lax.fori_loop(0, T // TQ, body, 0)


def _mla_call(layer, qn, ckv, kr, sz, wq, wk, wv, cache_ckv, cache_kr, cos, sa, sb):
    T = DEC_SEQ
    Tk = T + PAST_LEN
    row = lambda w, c=0: pl.BlockSpec((T, w), lambda b: (b, c))
    return pl.pallas_call(
        functools.partial(_mla_kernel, T),
        grid=(DEC_BATCH,),
        in_specs=[row(Q_LORA), row(KV_LORA), row(LANE), row(BRANCH_W, 1)]
        + [_layer_spec(a, layer) for a in (wq, wk, wv)]
        + [pl.BlockSpec((None, None, PAST_LEN, KV_LORA), lambda b: (b, layer, 0, 0)),
           pl.BlockSpec((None, None, PAST_LEN, LANE), lambda b: (b, layer, 0, 0)),
           _full_spec(cos), _full_spec(sa), _full_spec(sb)],
        out_specs=pl.BlockSpec((T, MLA_W), lambda b: (b, 0)),
        out_shape=jax.ShapeDtypeStruct((T * DEC_BATCH, MLA_W), BF16),
        scratch_shapes=[pltpu.VMEM((H_MLA, T, LANE), BF16),
                        pltpu.VMEM((H_MLA, Tk, LANE), BF16),
                        pltpu.VMEM((H_MLA // 2, Tk, LANE), BF16)],
        compiler_params=_cparams(),
        name="mla_s",
    )(qn, ckv, kr, sz, wq, wk, wv, cache_ckv, cache_kr, cos, sa, sb)


def _fourier_kernel(T, fu_ref, fz_ref, csw_ref, cts_ref, bf_ref):
    o = _fourier(T, fu_ref[...], csw_ref, cts_ref)
    bf_ref[...] = (o * fz_ref[...].astype(F32)).astype(BF16)


def _fourier_call(fu, sz, csw, cts):
    T = DEC_SEQ
    return pl.pallas_call(
        functools.partial(_fourier_kernel, T),
        grid=(DEC_BATCH,),
        in_specs=[pl.BlockSpec((T, FOURIER_W), lambda b: (b, 0)),
                  pl.BlockSpec((T, BRANCH_W), lambda b: (b, 2)),
                  _full_spec(csw), _full_spec(cts)],
        out_specs=pl.BlockSpec((T, FOURIER_W), lambda b: (b, 0)),
        out_shape=jax.ShapeDtypeStruct((T * DEC_BATCH, FOURIER_W), BF16),
        compiler_params=_cparams(),
        name="fourier_s",
    )(fu, sz, csw, cts)


def _merge_kernel(final, *refs):
    br_ref, bm_ref, bf_ref, sg_ref, x_ref, mod_ref, wb_ref, wo_ref = refs[:8]
    fg_ref = refs[8] if final else None
    sg = [sg_ref[:, n * D_MODEL:(n + 1) * D_MODEL].astype(F32) for n in range(N_BRANCH)]
    refs[-1][...] = _merge((br_ref[...], bm_ref[...], bf_ref[...]), sg, x_ref[...],
                           mod_ref, wb_ref, wo_ref, fg_ref)


def _merge_call(layer, br, bm, bf, sg, x, mod, wb, wo, final_g=None):
    final = final_g is not None
    n = x.shape[0]
    per_b = DEC_SEQ // TM
    row = lambda w: pl.BlockSpec((TM, w), lambda i: (i, 0))
    in_specs = [row(BRANCH_W), row(BRANCH_W), row(BRANCH_W), row(N_BRANCH * D_MODEL), row(D_MODEL),
                _mod_spec(layer, lambda i: 1 + i // per_b),
                _layer_spec(wb, layer), _layer_spec(wo, layer)]
    args = [br, bm, bf, sg, x, mod, wb, wo]
    if final:
        in_specs.append(_full_spec(final_g))
        args.append(final_g)
    return pl.pallas_call(
        functools.partial(_merge_kernel, final),
        grid=(n // TM,),
        in_specs=in_specs,
        out_specs=row(D_MODEL),
        out_shape=jax.ShapeDtypeStruct((n, D_MODEL), F32),
        compiler_params=_cparams(),
        name="merge_s",
    )(*args)


def _dft_tables(T):
    def cs(n):
        kt = (np.arange(n)[:, None] * np.arange(n)[None, :]) % n
        ang = 2.0 * np.pi * kt.astype(np.float64) / n
        return np.cos(ang), np.sin(ang)
    ct, st = cs(T)
    cw, sw = cs(F_GROUP_W)
    cts = np.concatenate([ct, -st], axis=1).astype(np.float32)
    csw = np.concatenate([cw, sw], axis=1).astype(np.float32)
    return jnp.asarray(csw).astype(BF16), jnp.asarray(cts).astype(BF16)


def _rope_tables(T):
    half = D_ROPE // 2
    nfreq = half // 2
    inv = ROPE_BASE ** (-np.arange(nfreq, dtype=np.float64) / nfreq)
    t = np.arange(T)
    pos = np.stack([t // GRID_W, t % GRID_W], axis=0).astype(np.float64)
    cos = np.ones((T, LANE), np.float64)
    sa = np.zeros((T, LANE), np.float64)
    sb = np.zeros((T, LANE), np.float64)
    for part in range(2):
        ang = pos[part][:, None] * inv[None, :]
        l1 = ROPE_LANE0 + part * half
        l2 = l1 + nfreq
        cos[:, l1:l1 + nfreq] = np.cos(ang)
        cos[:, l2:l2 + nfreq] = np.cos(ang)
        sa[:, l1:l1 + nfreq] = -np.sin(ang)
        sb[:, l2:l2 + nfreq] = np.sin(ang)
    return tuple(jnp.asarray(a.astype(np.float32)) for a in (cos, sa, sb))


def _permute_w_in(w):
    z = lambda n: jnp.zeros(w.shape[:2] + (n,), w.dtype)
    cols = [w[..., 0:1024],
            w[..., 1024:1536], w[..., 2208:2720], w[..., 3232:3744],
            w[..., 3744:6816],
            w[..., 1536:1920],
            w[..., 1920:2176],
            z(ROPE_LANE0), w[..., 2176:2208], z(LANE - ROPE_LANE0 - D_ROPE),
            w[..., 2720:3232]]
    return jnp.concatenate(cols, axis=-1).astype(BF16)


def _pad_heads(w, n_heads, width, lo, hi):
    d, k = w.shape[:2]
    wh = w.reshape(d, k, n_heads, width)[..., lo:hi]
    wh = jnp.pad(wh, ((0, 0), (0, 0), (0, 0), (0, LANE - (hi - lo))))
    return wh.reshape(d, k, n_heads * LANE).astype(BF16)


def kernel(x_prompt, x_sample, cache_ckv, cache_krope, state_ret, c, c_ctx, norm_g, w_mod, b_mod,
           w_in, ret_decay_logit, q_norm_g, w_q_up, kv_norm_g, w_kv_up, w_branch, w_out,
           final_norm_g):
    cv = jnp.concatenate([c_ctx[None, :], c, jnp.zeros((MOD_ROWS - 1 - DEC_BATCH, D_MODEL), F32)], axis=0)
    mod = _mod_call(cv, w_mod, b_mod).reshape(DEPTH, MOD_ROWS, 1, 3 * D_MODEL)

    cache_kr = jnp.pad(cache_krope, ((0, 0), (0, 0), (0, 0), (ROPE_LANE0, LANE - ROPE_LANE0 - D_ROPE)))
    rope = _rope_tables(DEC_SEQ)
    dft_p = _dft_tables(SEQ)
    dft_s = _dft_tables(DEC_SEQ)
    logit = jnp.pad(ret_decay_logit, ((0, 0), (0, 8 - 2), (0, LANE - H_RET)))
    g, qg, kvg = norm_g[:, None, :], q_norm_g[:, None, :], kv_norm_g[:, None, :]
    w_p = _permute_w_in(w_in)
    wq = _pad_heads(w_q_up, H_MLA, D_NOPE + D_ROPE, 0, D_NOPE + D_ROPE)
    wk = _pad_heads(w_kv_up, H_MLA, D_NOPE + D_VMLA, 0, D_NOPE)
    wv = w_kv_up.reshape(DEPTH, KV_LORA, H_MLA, D_NOPE + D_VMLA)[..., D_NOPE:]
    wv = wv.reshape(DEPTH, KV_LORA, MLA_W).astype(BF16)
    wb = w_branch.astype(BF16)
    wo = w_out.astype(BF16)
    final_g = final_norm_g[None, :]

    xp = x_prompt.reshape(BATCH * SEQ, D_MODEL)
    xs = x_sample.reshape(DEC_BATCH * DEC_SEQ, D_MODEL)
    new_ctx = None
    for l in range(DEPTH):
        fg = final_g if l == DEPTH - 1 else None
        xp, *new_ctx = _prompt_call(l, xp, mod, g, qg, kvg, w_p, logit, wq, wk, wv, *dft_p, wb, wo,
                                    final_g=fg, prev=new_ctx)

        qkv, sz, sg, qn, ckv, kr, fu = _in_call(l, xs, mod, g, qg, kvg, w_p)
        br = _ret_call(l, qkv, sz, logit, state_ret)
        bm = _mla_call(l, qn, ckv, kr, sz, wq, wk, wv, cache_ckv, cache_kr, *rope)
        bf = _fourier_call(fu, sz, *dft_s)
        xs = _merge_call(l, br, bm, bf, sg, xs, mod, wb, wo, final_g=fg)

    y_prompt = xp.reshape(BATCH, SEQ, D_MODEL)
    y_sample = xs.reshape(DEC_BATCH, DEC_SEQ, D_MODEL)
    return (y_prompt, y_sample, *new_ctx)
```

```python
import collections
import functools

import numpy as np
import jax
import jax.numpy as jnp
from jax import lax
from jax.experimental import pallas as pl
from jax.experimental.pallas import tpu as pltpu

F32 = jnp.float32
BF16 = jnp.bfloat16

D_MODEL = 1024
BATCH = 32
SEQ = 256
DEPTH = 2
DEC_BATCH = 4
DEC_SEQ = 1024
PAST_LEN = 512
GRID_W = 64
EPS = 1e-6
H_RET = 4
DK_RET = 64
DV_RET = 128
RET_W = H_RET * DV_RET
H_MLA = 8
Q_LORA = 384
KV_LORA = 256
D_NOPE = 64
D_ROPE = 32
D_VMLA = 64
MLA_W = H_MLA * D_VMLA
ROPE_BASE = 10000.0
F_GROUPS = 4
F_GROUP_W = 128
FOURIER_W = F_GROUPS * F_GROUP_W
N_BRANCH = 3
BRANCH_W = 512

LANE = 128
TM = 512
TQ = 256
MOD_ROWS = 8
ROPE_LANE0 = D_NOPE
VMEM_LIMIT = 56 * 1024 * 1024
ST_TAIL = (2, H_RET, DK_RET, DV_RET)

SEG_QKV = (0, 1024)
SEG_Z = (1024, 2560)
SEG_G = (2560, 5632)
SEG_QL = (5632, 6016)
SEG_KV = (6016, 6272)
SEG_KR = (6272, 6400)
SEG_FU = (6400, 6912)
IN_WP = 6912

NT = (((1,), (1,)), ((), ()))
TN = (((0,), (0,)), ((), ()))


def _cparams(n_axes=1):
    return pltpu.CompilerParams(dimension_semantics=("arbitrary",) * n_axes,
                                vmem_limit_bytes=VMEM_LIMIT)


def _layer_spec(a, layer):
    shape = a.shape[1:]
    return pl.BlockSpec((None,) + shape, lambda i: (layer,) + (0,) * len(shape))


def _full_spec(a):
    return pl.BlockSpec(a.shape, lambda i: (0,) * a.ndim)


def _mod_spec(layer, row):
    return pl.BlockSpec((None, None, 1, 3 * D_MODEL), lambda i: (layer, row(i), 0, 0))


def _grow_spec(depth, tail):
    return pl.BlockSpec((1, depth) + tail, lambda i: (i,) + (0,) * (1 + len(tail)))


def _dot(a, b):
    return jnp.dot(a, b, preferred_element_type=F32)


def _rms(x, g):
    return x * lax.rsqrt(jnp.mean(x * x, axis=-1, keepdims=True) + EPS) * g


def _mod_kernel(c_ref, w_ref, b_ref, o_ref):
    cv = c_ref[...]
    s = cv * jax.nn.sigmoid(cv)
    o_ref[0] = jnp.dot(s, w_ref[0], preferred_element_type=F32,
                       precision=lax.Precision.HIGHEST) + b_ref[0]


def _mod_call(cv, w_mod, b_mod):
    nb = 3 * D_MODEL // 1024
    return pl.pallas_call(
        _mod_kernel,
        grid=(DEPTH, nb),
        in_specs=[
            pl.BlockSpec((MOD_ROWS, D_MODEL), lambda l, j: (0, 0)),
            pl.BlockSpec((1, D_MODEL, 1024), lambda l, j: (l, 0, j)),
            pl.BlockSpec((1, 1, 1024), lambda l, j: (l, 0, j)),
        ],
        out_specs=pl.BlockSpec((1, MOD_ROWS, 1024), lambda l, j: (l, 0, j)),
        out_shape=jax.ShapeDtypeStruct((DEPTH, MOD_ROWS, 3 * D_MODEL), F32),
        compiler_params=_cparams(2),
        name="mod",
    )(cv, w_mod, b_mod.reshape(DEPTH, 1, 3 * D_MODEL))


Proj = collections.namedtuple("Proj", "q k v sz sg qn ckv kr fu")


def _project(x, mod_ref, g_ref, qg_ref, kvg_ref, w_ref):
    shift = mod_ref[:, 0:D_MODEL]
    scale = mod_ref[:, D_MODEL:2 * D_MODEL]
    hb = (_rms(x, g_ref[...]) * (1.0 + scale) + shift).astype(BF16)

    def mm(a, b):
        return _dot(hb, w_ref[:, a:b])

    qw = H_RET * DK_RET
    p = mm(*SEG_QKV)
    q = p[:, 0:qw].astype(BF16)
    k = (p[:, qw:2 * qw] * (DK_RET ** -0.5)).astype(BF16)
    v = p[:, 2 * qw:].astype(BF16)
    sz = []
    for j in range(3):
        a = SEG_Z[0] + j * BRANCH_W
        p = mm(a, a + BRANCH_W)
        sz.append(p * jax.nn.sigmoid(p))
    sg = []
    for j in range(N_BRANCH):
        a = SEG_G[0] + j * D_MODEL
        sg.append(jax.nn.sigmoid(mm(a, a + D_MODEL)))
    qn = _rms(mm(*SEG_QL), qg_ref[...]).astype(BF16)
    ckv = _rms(mm(*SEG_KV), kvg_ref[...])
    kr = mm(*SEG_KR)
    fu = mm(*SEG_FU).astype(BF16)
    return Proj(q, k, v, sz, sg, qn, ckv, kr, fu)


def _init_masks(T, lg, mask_ref):
    ii = lax.broadcasted_iota(jnp.int32, (T, T), 0)
    jj = lax.broadcasted_iota(jnp.int32, (T, T), 1)
    d = (ii - jj).astype(F32)
    ad = jnp.abs(d)
    for h in range(H_RET):
        lf = lg[0:1, h:h + 1]
        lb = lg[1:2, h:h + 1]
        mask_ref[h] = jnp.where(d > 0, jnp.exp(ad * lf), jnp.where(d < 0, jnp.exp(ad * lb), 2.0))


def _ret_head(T, q, k, v, mask, lf, lb, st=None):
    s = lax.dot_general(q, k, NT, preferred_element_type=F32) * mask
    o = _dot(s.astype(BF16), v)
    if st is not None:
        rows = lax.broadcasted_iota(jnp.int32, (T, DV_RET), 0).astype(F32)
        o = o + jnp.exp((rows + 1.0) * lf) * _dot(q, st[0].astype(BF16))
        o = o + jnp.exp((T - rows) * lb) * _dot(q, st[1].astype(BF16))
    mu = jnp.mean(o, axis=-1, keepdims=True)
    c = o - mu
    var = jnp.mean(c * c, axis=-1, keepdims=True)
    return c * lax.rsqrt(var + EPS)


def _ret_states(T, k, v, lf, lb):
    rows = lax.broadcasted_iota(jnp.int32, (T, DK_RET), 0).astype(F32)
    kf = (k.astype(F32) * jnp.exp((T - 1.0 - rows) * lf)).astype(BF16)
    kb = (k.astype(F32) * jnp.exp(rows * lb)).astype(BF16)
    return (lax.dot_general(kf, v, TN, preferred_element_type=F32),
            lax.dot_general(kb, v, TN, preferred_element_type=F32))


def _rope(x, cos, sa, sb):
    half = D_ROPE // 4
    return x * cos + pltpu.roll(x, LANE - half, 1) * sa + pltpu.roll(x, half, 1) * sb


def _attend_pair(q2, k2, vp, lane):
    scale = (D_NOPE + D_ROPE) ** -0.5
    halves = []
    for qh, kh in zip(q2, k2):
        s = lax.dot_general(qh, kh, NT, preferred_element_type=F32) * scale
        e = jnp.exp(s - jnp.max(s, axis=-1, keepdims=True))
        l = jnp.sum(e, axis=-1, keepdims=True)
        halves.append(_dot(e.astype(BF16), vp) / l)
    return jnp.where(lane < D_VMLA, halves[0], halves[1])


def _fourier(T, fu, csw_ref, cts_ref):
    a, b = [], []
    for g in range(F_GROUPS):
        z = _dot(fu[:, g * F_GROUP_W:(g + 1) * F_GROUP_W], csw_ref[...])
        a.append(z[:, :F_GROUP_W])
        b.append(z[:, F_GROUP_W:])
    a = jnp.concatenate(a, axis=1).astype(BF16)
    b = jnp.concatenate(b, axis=1).astype(BF16)
    o = _dot(cts_ref[:, 0:T], a) + _dot(cts_ref[:, T:2 * T], b)
    return o * ((T * F_GROUP_W) ** -0.5)


def _merge(branches, sg, x, mod_ref, wb_ref, wo_ref, fg_ref):
    merged = None
    for n, b in enumerate(branches):
        term = _dot(b, wb_ref[n]) * sg[n]
        merged = term if merged is None else merged + term
    gate = mod_ref[:, 2 * D_MODEL:3 * D_MODEL]
    y = x + gate * _dot(merged.astype(BF16), wo_ref[...])
    if fg_ref is not None:
        y = _rms(y, fg_ref[...])
    return y


def _prompt_kernel(layer, final, *refs):
    (x_ref, mod_ref, g_ref, qg_ref, kvg_ref, w_ref, lg_ref, wq_ref, wk_ref, wv_ref,
     csw_ref, cts_ref, wb_ref, wo_ref) = refs[:14]
    n_in = 14
    fg_ref = None
    if final:
        fg_ref = refs[n_in]
        n_in += 1
    if layer:
        pckv_ref, pkr_ref, pst_ref = refs[n_in:n_in + 3]
        n_in += 3
    y_ref, nckv_ref, nkr_ref, nst_ref, mask_ref = refs[n_in:]
    T = SEQ
    lg = jax.nn.log_sigmoid(lg_ref[...])

    @pl.when(pl.program_id(0) == 0)
    def _():
        _init_masks(T, lg, mask_ref)

    x = x_ref[...]
    pr = _project(x, mod_ref, g_ref, qg_ref, kvg_ref, w_ref)

    if layer:
        nckv_ref[0, 0:layer] = pckv_ref[0]
        nkr_ref[0, 0:layer] = pkr_ref[0]
        nst_ref[0, 0:layer] = pst_ref[0]
    nckv_ref[0, layer] = pr.ckv
    nkr_ref[0, layer] = pr.kr[:, ROPE_LANE0:ROPE_LANE0 + D_ROPE]

    o_r = []
    for h in range(H_RET):
        lf = lg[0:1, h:h + 1]
        lb = lg[1:2, h:h + 1]
        q = pr.q[:, h * DK_RET:(h + 1) * DK_RET]
        k = pr.k[:, h * DK_RET:(h + 1) * DK_RET]
        v = pr.v[:, h * DV_RET:(h + 1) * DV_RET]
        o_r.append(_ret_head(T, q, k, v, mask_ref[h], lf, lb))
        sf, sb = _ret_states(T, k, v, lf, lb)
        nst_ref[0, layer, 0, h] = sf
        nst_ref[0, layer, 1, h] = sb
    b_r = (jnp.concatenate(o_r, axis=1) * pr.sz[0]).astype(BF16)

    ckvb = pr.ckv.astype(BF16)
    q_all = _dot(pr.qn, wq_ref[...])
    k_all = _dot(ckvb, wk_ref[...])
    v_all = _dot(ckvb, wv_ref[...])
    slot = lambda a, i: a[:, i * LANE:(i + 1) * LANE]
    qs = [slot(q_all, h).astype(BF16) for h in range(H_MLA)]
    ks = [(slot(k_all, h) + pr.kr).astype(BF16) for h in range(H_MLA)]
    lane = lax.broadcasted_iota(jnp.int32, (T, LANE), 1)
    o_m = []
    for p in range(H_MLA // 2):
        vp = slot(v_all, p).astype(BF16)
        o_m.append(_attend_pair(qs[2 * p:2 * p + 2], ks[2 * p:2 * p + 2], vp, lane))
    b_m = (jnp.concatenate(o_m, axis=1) * pr.sz[1]).astype(BF16)

    b_f = (_fourier(T, pr.fu, csw_ref, cts_ref) * pr.sz[2]).astype(BF16)

    y_ref[...] = _merge((b_r, b_m, b_f), pr.sg, x, mod_ref, wb_ref, wo_ref, fg_ref)


def _prompt_call(layer, x, mod, g, qg, kvg, w_p, logit, wq, wk, wv, csw, cts, wb, wo,
                 final_g=None, prev=None):
    T = SEQ
    row = pl.BlockSpec((T, D_MODEL), lambda i: (i, 0))
    in_specs = [row, _mod_spec(layer, lambda i: 0)]
    in_specs += [_layer_spec(a, layer) for a in (g, qg, kvg, w_p, logit, wq, wk, wv)]
    in_specs += [_full_spec(csw), _full_spec(cts), _layer_spec(wb, layer), _layer_spec(wo, layer)]
    args = [x, mod, g, qg, kvg, w_p, logit, wq, wk, wv, csw, cts, wb, wo]
    if final_g is not None:
        in_specs.append(_full_spec(final_g))
        args.append(final_g)
    tails = ((SEQ, KV_LORA), (SEQ, D_ROPE), ST_TAIL)
    if layer:
        in_specs += [_grow_spec(layer, t) for t in tails]
        args += list(prev)
    return pl.pallas_call(
        functools.partial(_prompt_kernel, layer, final_g is not None),
        grid=(BATCH,),
        in_specs=in_specs,
        out_specs=[row] + [_grow_spec(layer + 1, t) for t in tails],
        out_shape=[jax.ShapeDtypeStruct((BATCH * T, D_MODEL), F32)]
        + [jax.ShapeDtypeStruct((BATCH, layer + 1) + t, F32) for t in tails],
        scratch_shapes=[pltpu.VMEM((H_RET, T, T), F32)],
        compiler_params=_cparams(),
        name="prompt_layer",
    )(*args)


def _in_kernel(x_ref, mod_ref, g_ref, qg_ref, kvg_ref, w_ref,
               qkv_ref, sz_ref, sg_ref, qn_ref, ckv_ref, kr_ref, fu_ref):
    pr = _project(x_ref[...], mod_ref, g_ref, qg_ref, kvg_ref, w_ref)
    qw = H_RET * DK_RET
    qkv_ref[:, 0:qw] = pr.q
    qkv_ref[:, qw:2 * qw] = pr.k
    qkv_ref[:, 2 * qw:] = pr.v
    for j in range(3):
        sz_ref[:, j * BRANCH_W:(j + 1) * BRANCH_W] = pr.sz[j].astype(BF16)
    for j in range(N_BRANCH):
        sg_ref[:, j * D_MODEL:(j + 1) * D_MODEL] = pr.sg[j].astype(BF16)
    qn_ref[...] = pr.qn
    ckv_ref[...] = pr.ckv.astype(BF16)
    kr_ref[...] = pr.kr
    fu_ref[...] = pr.fu


def _in_call(layer, x, mod, g, qg, kvg, w_p):
    n = x.shape[0]
    per_b = DEC_SEQ // TM
    row = lambda w: pl.BlockSpec((TM, w), lambda i: (i, 0))
    outs = [(1024, BF16), (3 * BRANCH_W, BF16), (N_BRANCH * D_MODEL, BF16), (Q_LORA, BF16),
            (KV_LORA, BF16), (LANE, F32), (FOURIER_W, BF16)]
    return pl.pallas_call(
        _in_kernel,
        grid=(n // TM,),
        in_specs=[row(D_MODEL), _mod_spec(layer, lambda i: 1 + i // per_b)]
        + [_layer_spec(a, layer) for a in (g, qg, kvg, w_p)],
        out_specs=[row(w) for w, _ in outs],
        out_shape=[jax.ShapeDtypeStruct((n, w), dt) for w, dt in outs],
        compiler_params=_cparams(),
        name="in_proj_s",
    )(x, mod, g, qg, kvg, w_p)


def _ret_kernel(T, qkv_ref, rz_ref, lg_ref, st_ref, br_ref, mask_ref):
    lg = jax.nn.log_sigmoid(lg_ref[...])

    @pl.when(pl.program_id(0) == 0)
    def _():
        _init_masks(T, lg, mask_ref)

    qw = H_RET * DK_RET
    for h in range(H_RET):
        q = qkv_ref[:, h * DK_RET:(h + 1) * DK_RET]
        k = qkv_ref[:, qw + h * DK_RET:qw + (h + 1) * DK_RET]
        v = qkv_ref[:, 2 * qw + h * DV_RET:2 * qw + (h + 1) * DV_RET]
        on = _ret_head(T, q, k, v, mask_ref[h], lg[0:1, h:h + 1], lg[1:2, h:h + 1],
                       st=(st_ref[0, h], st_ref[1, h]))
        gate = rz_ref[:, h * DV_RET:(h + 1) * DV_RET].astype(F32)
        br_ref[:, h * DV_RET:(h + 1) * DV_RET] = (on * gate).astype(BF16)


def _ret_call(layer, qkv, sz, logit, state):
    T = DEC_SEQ
    return pl.pallas_call(
        functools.partial(_ret_kernel, T),
        grid=(DEC_BATCH,),
        in_specs=[pl.BlockSpec((T, 1024), lambda b: (b, 0)),
                  pl.BlockSpec((T, BRANCH_W), lambda b: (b, 0)),
                  _layer_spec(logit, layer),
                  pl.BlockSpec((None, None) + ST_TAIL, lambda b: (b, layer, 0, 0, 0, 0))],
        out_specs=pl.BlockSpec((T, RET_W), lambda b: (b, 0)),
        out_shape=jax.ShapeDtypeStruct((T * DEC_BATCH, RET_W), BF16),
        scratch_shapes=[pltpu.VMEM((H_RET, T, T), F32)],
        compiler_params=_cparams(),
        name="ret_s",
    )(qkv, sz, logit, state)


def _mla_kernel(T, qn_ref, ckv_ref, kr_ref, mz_ref, wq_ref, wk_ref, wv_ref, cckv_ref, ckr_ref,
                cos_ref, sa_ref, sb_ref, bm_ref, q_scr, k_scr, v_scr):
    P = PAST_LEN
    qn = qn_ref[...]
    ckv = ckv_ref[...]
    kr = _rope(kr_ref[...], cos_ref[...], sa_ref[...], sb_ref[...])
    cc = cckv_ref[...].astype(BF16)
    for h in range(H_MLA):
        sl = slice(h * LANE, (h + 1) * LANE)
        qh = _rope(_dot(qn, wq_ref[:, sl]), cos_ref[...], sa_ref[...], sb_ref[...])
        q_scr[h] = qh.astype(BF16)
        k_scr[h, 0:P] = (_dot(cc, wk_ref[:, sl]) + ckr_ref[...]).astype(BF16)
        k_scr[h, P:P + T] = (_dot(ckv, wk_ref[:, sl]) + kr).astype(BF16)
    for p in range(H_MLA // 2):
        sl = slice(p * LANE, (p + 1) * LANE)
        v_scr[p, 0:P] = _dot(cc, wv_ref[:, sl]).astype(BF16)
        v_scr[p, P:P + T] = _dot(ckv, wv_ref[:, sl]).astype(BF16)

    lane = lax.broadcasted_iota(jnp.int32, (TQ, LANE), 1)

    def body(qb, carry):
        rs = pl.ds(pl.multiple_of(qb * TQ, TQ), TQ)
        for p in range(H_MLA // 2):
            hs = (2 * p, 2 * p + 1)
            o = _attend_pair([q_scr[h, rs] for h in hs], [k_scr[h] for h in hs], v_scr[p], lane)
            gate = mz_ref[rs, p * LANE:(p + 1) * LANE].astype(F32)
            bm_ref[rs, p * LANE:(p + 1) * LANE] = (o * gate).astype(BF16)
        return carry

    lax.fori_loop(0, T // TQ, body, 0)


def _mla_call(layer, qn, ckv, kr, sz, wq, wk, wv, cache_ckv, cache_kr, cos, sa, sb):
    T = DEC_SEQ
    Tk = T + PAST_LEN
    row = lambda w, c=0: pl.BlockSpec((T, w), lambda b: (b, c))
    return pl.pallas_call(
        functools.partial(_mla_kernel, T),
        grid=(DEC_BATCH,),
        in_specs=[row(Q_LORA), row(KV_LORA), row(LANE), row(BRANCH_W, 1)]
        + [_layer_spec(a, layer) for a in (wq, wk, wv)]
        + [pl.BlockSpec((None, None, PAST_LEN, KV_LORA), lambda b: (b, layer, 0, 0)),
           pl.BlockSpec((None, None, PAST_LEN, LANE), lambda b: (b, layer, 0, 0)),
           _full_spec(cos), _full_spec(sa), _full_spec(sb)],
        out_specs=pl.BlockSpec((T, MLA_W), lambda b: (b, 0)),
        out_shape=jax.ShapeDtypeStruct((T * DEC_BATCH, MLA_W), BF16),
        scratch_shapes=[pltpu.VMEM((H_MLA, T, LANE), BF16),
                        pltpu.VMEM((H_MLA, Tk, LANE), BF16),
                        pltpu.VMEM((H_MLA // 2, Tk, LANE), BF16)],
        compiler_params=_cparams(),
        name="mla_s",
    )(qn, ckv, kr, sz, wq, wk, wv, cache_ckv, cache_kr, cos, sa, sb)


def _fourier_kernel(T, fu_ref, fz_ref, csw_ref, cts_ref, bf_ref):
    o = _fourier(T, fu_ref[...], csw_ref, cts_ref)
    bf_ref[...] = (o * fz_ref[...].astype(F32)).astype(BF16)


def _fourier_call(fu, sz, csw, cts):
    T = DEC_SEQ
    return pl.pallas_call(
        functools.partial(_fourier_kernel, T),
        grid=(DEC_BATCH,),
        in_specs=[pl.BlockSpec((T, FOURIER_W), lambda b: (b, 0)),
                  pl.BlockSpec((T, BRANCH_W), lambda b: (b, 2)),
                  _full_spec(csw), _full_spec(cts)],
        out_specs=pl.BlockSpec((T, FOURIER_W), lambda b: (b, 0)),
        out_shape=jax.ShapeDtypeStruct((T * DEC_BATCH, FOURIER_W), BF16),
        compiler_params=_cparams(),
        name="fourier_s",
    )(fu, sz, csw, cts)


def _merge_kernel(final, *refs):
    br_ref, bm_ref, bf_ref, sg_ref, x_ref, mod_ref, wb_ref, wo_ref = refs[:8]
    fg_ref = refs[8] if final else None
    sg = [sg_ref[:, n * D_MODEL:(n + 1) * D_MODEL].astype(F32) for n in range(N_BRANCH)]
    refs[-1][...] = _merge((br_ref[...], bm_ref[...], bf_ref[...]), sg, x_ref[...],
                           mod_ref, wb_ref, wo_ref, fg_ref)


def _merge_call(layer, br, bm, bf, sg, x, mod, wb, wo, final_g=None):
    final = final_g is not None
    n = x.shape[0]
    per_b = DEC_SEQ // TM
    row = lambda w: pl.BlockSpec((TM, w), lambda i: (i, 0))
    in_specs = [row(BRANCH_W), row(BRANCH_W), row(BRANCH_W), row(N_BRANCH * D_MODEL), row(D_MODEL),
                _mod_spec(layer, lambda i: 1 + i // per_b),
                _layer_spec(wb, layer), _layer_spec(wo, layer)]
    args = [br, bm, bf, sg, x, mod, wb, wo]
    if final:
        in_specs.append(_full_spec(final_g))
        args.append(final_g)
    return pl.pallas_call(
        functools.partial(_merge_kernel, final),
        grid=(n // TM,),
        in_specs=in_specs,
        out_specs=row(D_MODEL),
        out_shape=jax.ShapeDtypeStruct((n, D_MODEL), F32),
        compiler_params=_cparams(),
        name="merge_s",
    )(*args)


def _dft_tables(T):
    def cs(n):
        kt = (np.arange(n)[:, None] * np.arange(n)[None, :]) % n
        ang = 2.0 * np.pi * kt.astype(np.float64) / n
        return np.cos(ang), np.sin(ang)
    ct, st = cs(T)
    cw, sw = cs(F_GROUP_W)
    cts = np.concatenate([ct, -st], axis=1).astype(np.float32)
    csw = np.concatenate([cw, sw], axis=1).astype(np.float32)
    return jnp.asarray(csw).astype(BF16), jnp.asarray(cts).astype(BF16)


def _rope_tables(T):
    half = D_ROPE // 2
    nfreq = half // 2
    inv = ROPE_BASE ** (-np.arange(nfreq, dtype=np.float64) / nfreq)
    t = np.arange(T)
    pos = np.stack([t // GRID_W, t % GRID_W], axis=0).astype(np.float64)
    cos = np.ones((T, LANE), np.float64)
    sa = np.zeros((T, LANE), np.float64)
    sb = np.zeros((T, LANE), np.float64)
    for part in range(2):
        ang = pos[part][:, None] * inv[None, :]
        l1 = ROPE_LANE0 + part * half
        l2 = l1 + nfreq
        cos[:, l1:l1 + nfreq] = np.cos(ang)
        cos[:, l2:l2 + nfreq] = np.cos(ang)
        sa[:, l1:l1 + nfreq] = -np.sin(ang)
        sb[:, l2:l2 + nfreq] = np.sin(ang)
    return tuple(jnp.asarray(a.astype(np.float32)) for a in (cos, sa, sb))


W_IN_RUNS = ((0, 1024),
             (1024, 512), (2208, 512), (3232, 512),
             (3744, 3072),
             (1536, 384),
             (1920, 256),
             (None, ROPE_LANE0), (2176, D_ROPE), (None, LANE - ROPE_LANE0 - D_ROPE),
             (2720, 512))
W_PREP_ROWS = 256


def _w_prep_kernel(w_ref, o_ref):
    a = 0
    for src, width in W_IN_RUNS:
        if src is None:
            o_ref[:, a:a + width] = jnp.zeros((W_PREP_ROWS, width), BF16)
        else:
            o_ref[:, a:a + width] = w_ref[:, src:src + width].astype(BF16)
        a += width


def _permute_w_in(w):
    depth, d, in_w = w.shape
    return pl.pallas_call(
        _w_prep_kernel,
        grid=(depth, d // W_PREP_ROWS),
        in_specs=[pl.BlockSpec((None, W_PREP_ROWS, in_w), lambda l, r: (l, r, 0))],
        out_specs=pl.BlockSpec((None, W_PREP_ROWS, IN_WP), lambda l, r: (l, r, 0)),
        out_shape=jax.ShapeDtypeStruct((depth, d, IN_WP), BF16),
        compiler_params=_cparams(2),
        name="w_prep",
    )(w)


def _pad_heads(w, n_heads, width, lo, hi):
    d, k = w.shape[:2]
    wh = w.reshape(d, k, n_heads, width)[..., lo:hi]
    wh = jnp.pad(wh, ((0, 0), (0, 0), (0, 0), (0, LANE - (hi - lo))))
    return wh.reshape(d, k, n_heads * LANE).astype(BF16)


def kernel(x_prompt, x_sample, cache_ckv, cache_krope, state_ret, c, c_ctx, norm_g, w_mod, b_mod,
           w_in, ret_decay_logit, q_norm_g, w_q_up, kv_norm_g, w_kv_up, w_branch, w_out,
           final_norm_g):
    cv = jnp.concatenate([c_ctx[None, :], c, jnp.zeros((MOD_ROWS - 1 - DEC_BATCH, D_MODEL), F32)], axis=0)
    mod = _mod_call(cv, w_mod, b_mod).reshape(DEPTH, MOD_ROWS, 1, 3 * D_MODEL)

    cache_kr = jnp.pad(cache_krope, ((0, 0), (0, 0), (0, 0), (ROPE_LANE0, LANE - ROPE_LANE0 - D_ROPE)))
    rope = _rope_tables(DEC_SEQ)
    dft_p = _dft_tables(SEQ)
    dft_s = _dft_tables(DEC_SEQ)
    logit = jnp.pad(ret_decay_logit, ((0, 0), (0, 8 - 2), (0, LANE - H_RET)))
    g, qg, kvg = norm_g[:, None, :], q_norm_g[:, None, :], kv_norm_g[:, None, :]
    w_p = _permute_w_in(w_in)
    wq = _pad_heads(w_q_up, H_MLA, D_NOPE + D_ROPE, 0, D_NOPE + D_ROPE)
    wk = _pad_heads(w_kv_up, H_MLA, D_NOPE + D_VMLA, 0, D_NOPE)
    wv = w_kv_up.reshape(DEPTH, KV_LORA, H_MLA, D_NOPE + D_VMLA)[..., D_NOPE:]
    wv = wv.reshape(DEPTH, KV_LORA, MLA_W).astype(BF16)
    wb = w_branch.astype(BF16)
    wo = w_out.astype(BF16)
    final_g = final_norm_g[None, :]

    xp = x_prompt.reshape(BATCH * SEQ, D_MODEL)
    xs = x_sample.reshape(DEC_BATCH * DEC_SEQ, D_MODEL)
    new_ctx = None
    for l in range(DEPTH):
        fg = final_g if l == DEPTH - 1 else None
        xp, *new_ctx = _prompt_call(l, xp, mod, g, qg, kvg, w_p, logit, wq, wk, wv, *dft_p, wb, wo,
                                    final_g=fg, prev=new_ctx)

        qkv, sz, sg, qn, ckv, kr, fu = _in_call(l, xs, mod, g, qg, kvg, w_p)
        br = _ret_call(l, qkv, sz, logit, state_ret)
        bm = _mla_call(l, qn, ckv, kr, sz, wq, wk, wv, cache_ckv, cache_kr, *rope)
        bf = _fourier_call(fu, sz, *dft_s)
        xs = _merge_call(l, br, bm, bf, sg, xs, mod, wb, wo, final_g=fg)

    y_prompt = xp.reshape(BATCH, SEQ, D_MODEL)
    y_sample = xs.reshape(DEC_BATCH, DEC_SEQ, D_MODEL)
    return (y_prompt, y_sample, *new_ctx)
```

```python
import collections
import functools

import numpy as np
import jax
import jax.numpy as jnp
from jax import lax
from jax.experimental import pallas as pl
from jax.experimental.pallas import tpu as pltpu

F32 = jnp.float32
BF16 = jnp.bfloat16

D_MODEL = 1024
BATCH = 32
SEQ = 256
DEPTH = 2
DEC_BATCH = 4
DEC_SEQ = 1024
PAST_LEN = 512
GRID_W = 64
EPS = 1e-6
H_RET = 4
DK_RET = 64
DV_RET = 128
RET_W = H_RET * DV_RET
H_MLA = 8
Q_LORA = 384
KV_LORA = 256
D_NOPE = 64
D_ROPE = 32
D_VMLA = 64
MLA_W = H_MLA * D_VMLA
ROPE_BASE = 10000.0
F_GROUPS = 4
F_GROUP_W = 128
FOURIER_W = F_GROUPS * F_GROUP_W
N_BRANCH = 3
BRANCH_W = 512

LANE = 128
TM = 512
TQ = 256
MOD_ROWS = 8
ROPE_LANE0 = D_NOPE
VMEM_LIMIT = 56 * 1024 * 1024
ST_TAIL = (2, H_RET, DK_RET, DV_RET)

SEG_QKV = (0, 1024)
SEG_Z = (1024, 2560)
SEG_G = (2560, 5632)
SEG_QL = (5632, 6016)
SEG_KV = (6016, 6272)
SEG_KR = (6272, 6400)
SEG_FU = (6400, 6912)
IN_WP = 6912

NT = (((1,), (1,)), ((), ()))
TN = (((0,), (0,)), ((), ()))


def _cparams(n_axes=1):
    return pltpu.CompilerParams(dimension_semantics=("arbitrary",) * n_axes,
                                vmem_limit_bytes=VMEM_LIMIT)


def _layer_spec(a, layer):
    shape = a.shape[1:]
    return pl.BlockSpec((None,) + shape, lambda i: (layer,) + (0,) * len(shape))


def _full_spec(a):
    return pl.BlockSpec(a.shape, lambda i: (0,) * a.ndim)


def _mod_spec(layer, row):
    return pl.BlockSpec((None, None, 1, 3 * D_MODEL), lambda i: (layer, row(i), 0, 0))


def _grow_spec(depth, tail):
    return pl.BlockSpec((1, depth) + tail, lambda i: (i,) + (0,) * (1 + len(tail)))


def _dot(a, b):
    return jnp.dot(a, b, preferred_element_type=F32)


def _rms(x, g):
    return x * lax.rsqrt(jnp.mean(x * x, axis=-1, keepdims=True) + EPS) * g


def _mod_kernel(c_ref, w_ref, b_ref, o_ref):
    cv = c_ref[...]
    s = cv * jax.nn.sigmoid(cv)
    o_ref[0] = jnp.dot(s, w_ref[0], preferred_element_type=F32,
                       precision=lax.Precision.HIGHEST) + b_ref[0]


def _mod_call(cv, w_mod, b_mod):
    nb = 3 * D_MODEL // 1024
    return pl.pallas_call(
        _mod_kernel,
        grid=(DEPTH, nb),
        in_specs=[
            pl.BlockSpec((MOD_ROWS, D_MODEL), lambda l, j: (0, 0)),
            pl.BlockSpec((1, D_MODEL, 1024), lambda l, j: (l, 0, j)),
            pl.BlockSpec((1, 1, 1024), lambda l, j: (l, 0, j)),
        ],
        out_specs=pl.BlockSpec((1, MOD_ROWS, 1024), lambda l, j: (l, 0, j)),
        out_shape=jax.ShapeDtypeStruct((DEPTH, MOD_ROWS, 3 * D_MODEL), F32),
        compiler_params=_cparams(2),
        name="mod",
    )(cv, w_mod, b_mod.reshape(DEPTH, 1, 3 * D_MODEL))


Proj = collections.namedtuple("Proj", "q k v sz sg qn ckv kr fu")


def _project(x, mod_ref, g_ref, qg_ref, kvg_ref, w_ref):
    shift = mod_ref[:, 0:D_MODEL]
    scale = mod_ref[:, D_MODEL:2 * D_MODEL]
    hb = (_rms(x, g_ref[...]) * (1.0 + scale) + shift).astype(BF16)

    def mm(a, b):
        return _dot(hb, w_ref[:, a:b])

    qw = H_RET * DK_RET
    p = mm(*SEG_QKV)
    q = p[:, 0:qw].astype(BF16)
    k = (p[:, qw:2 * qw] * (DK_RET ** -0.5)).astype(BF16)
    v = p[:, 2 * qw:].astype(BF16)
    sz = []
    for j in range(3):
        a = SEG_Z[0] + j * BRANCH_W
        p = mm(a, a + BRANCH_W)
        sz.append(p * jax.nn.sigmoid(p))
    sg = []
    for j in range(N_BRANCH):
        a = SEG_G[0] + j * D_MODEL
        sg.append(jax.nn.sigmoid(mm(a, a + D_MODEL)))
    qn = _rms(mm(*SEG_QL), qg_ref[...]).astype(BF16)
    ckv = _rms(mm(*SEG_KV), kvg_ref[...])
    kr = mm(*SEG_KR)
    fu = mm(*SEG_FU).astype(BF16)
    return Proj(q, k, v, sz, sg, qn, ckv, kr, fu)


def _init_masks(T, lg, mask_ref):
    ii = lax.broadcasted_iota(jnp.int32, (T, T), 0)
    jj = lax.broadcasted_iota(jnp.int32, (T, T), 1)
    d = (ii - jj).astype(F32)
    ad = jnp.abs(d)
    for h in range(H_RET):
        lf = lg[0:1, h:h + 1]
        lb = lg[1:2, h:h + 1]
        mask_ref[h] = jnp.where(d > 0, jnp.exp(ad * lf), jnp.where(d < 0, jnp.exp(ad * lb), 2.0))


def _ret_head(T, q, k, v, mask, lf, lb, st=None):
    s = lax.dot_general(q, k, NT, preferred_element_type=F32) * mask
    o = _dot(s.astype(BF16), v)
    if st is not None:
        rows = lax.broadcasted_iota(jnp.int32, (T, DV_RET), 0).astype(F32)
        o = o + jnp.exp((rows + 1.0) * lf) * _dot(q, st[0].astype(BF16))
        o = o + jnp.exp((T - rows) * lb) * _dot(q, st[1].astype(BF16))
    mu = jnp.mean(o, axis=-1, keepdims=True)
    c = o - mu
    var = jnp.mean(c * c, axis=-1, keepdims=True)
    return c * lax.rsqrt(var + EPS)


def _ret_states(T, k, v, lf, lb):
    rows = lax.broadcasted_iota(jnp.int32, (T, DK_RET), 0).astype(F32)
    kf = (k.astype(F32) * jnp.exp((T - 1.0 - rows) * lf)).astype(BF16)
    kb = (k.astype(F32) * jnp.exp(rows * lb)).astype(BF16)
    return (lax.dot_general(kf, v, TN, preferred_element_type=F32),
            lax.dot_general(kb, v, TN, preferred_element_type=F32))


def _rope(x, cos, sa, sb):
    half = D_ROPE // 4
    return x * cos + pltpu.roll(x, LANE - half, 1) * sa + pltpu.roll(x, half, 1) * sb


def _attend_pair(q2, k2, vp, lane):
    scale = (D_NOPE + D_ROPE) ** -0.5
    halves = []
    for qh, kh in zip(q2, k2):
        s = lax.dot_general(qh, kh, NT, preferred_element_type=F32) * scale
        e = jnp.exp(s - jnp.max(s, axis=-1, keepdims=True))
        l = jnp.sum(e, axis=-1, keepdims=True)
        halves.append(_dot(e.astype(BF16), vp) / l)
    return jnp.where(lane < D_VMLA, halves[0], halves[1])


def _fourier(T, fu, csw_ref, cts_ref):
    a, b = [], []
    for g in range(F_GROUPS):
        z = _dot(fu[:, g * F_GROUP_W:(g + 1) * F_GROUP_W], csw_ref[...])
        a.append(z[:, :F_GROUP_W])
        b.append(z[:, F_GROUP_W:])
    a = jnp.concatenate(a, axis=1).astype(BF16)
    b = jnp.concatenate(b, axis=1).astype(BF16)
    o = _dot(cts_ref[:, 0:T], a) + _dot(cts_ref[:, T:2 * T], b)
    return o * ((T * F_GROUP_W) ** -0.5)


def _merge(branches, sg, x, mod_ref, wb_ref, wo_ref, fg_ref):
    merged = None
    for n, b in enumerate(branches):
        term = _dot(b, wb_ref[n]) * sg[n]
        merged = term if merged is None else merged + term
    gate = mod_ref[:, 2 * D_MODEL:3 * D_MODEL]
    y = x + gate * _dot(merged.astype(BF16), wo_ref[...])
    if fg_ref is not None:
        y = _rms(y, fg_ref[...])
    return y


def _prompt_kernel(layer, final, *refs):
    (x_ref, mod_ref, g_ref, qg_ref, kvg_ref, w_ref, lg_ref, wq_ref, wk_ref, wv_ref,
     csw_ref, cts_ref, wb_ref, wo_ref) = refs[:14]
    n_in = 14
    fg_ref = None
    if final:
        fg_ref = refs[n_in]
        n_in += 1
    if layer:
        pckv_ref, pkr_ref, pst_ref = refs[n_in:n_in + 3]
        n_in += 3
    y_ref, nckv_ref, nkr_ref, nst_ref, mask_ref = refs[n_in:]
    T = SEQ
    lg = jax.nn.log_sigmoid(lg_ref[...])

    @pl.when(pl.program_id(0) == 0)
    def _():
        _init_masks(T, lg, mask_ref)

    x = x_ref[...]
    pr = _project(x, mod_ref, g_ref, qg_ref, kvg_ref, w_ref)

    if layer:
        nckv_ref[0, 0:layer] = pckv_ref[0]
        nkr_ref[0, 0:layer] = pkr_ref[0]
        nst_ref[0, 0:layer] = pst_ref[0]
    nckv_ref[0, layer] = pr.ckv
    nkr_ref[0, layer] = pr.kr.T[ROPE_LANE0:ROPE_LANE0 + D_ROPE]

    o_r = []
    for h in range(H_RET):
        lf = lg[0:1, h:h + 1]
        lb = lg[1:2, h:h + 1]
        q = pr.q[:, h * DK_RET:(h + 1) * DK_RET]
        k = pr.k[:, h * DK_RET:(h + 1) * DK_RET]
        v = pr.v[:, h * DV_RET:(h + 1) * DV_RET]
        o_r.append(_ret_head(T, q, k, v, mask_ref[h], lf, lb))
        sf, sb = _ret_states(T, k, v, lf, lb)
        nst_ref[0, layer, 0, h] = sf
        nst_ref[0, layer, 1, h] = sb
    b_r = (jnp.concatenate(o_r, axis=1) * pr.sz[0]).astype(BF16)

    ckvb = pr.ckv.astype(BF16)
    q_all = _dot(pr.qn, wq_ref[...])
    k_all = _dot(ckvb, wk_ref[...])
    v_all = _dot(ckvb, wv_ref[...])
    slot = lambda a, i: a[:, i * LANE:(i + 1) * LANE]
    qs = [slot(q_all, h).astype(BF16) for h in range(H_MLA)]
    ks = [(slot(k_all, h) + pr.kr).astype(BF16) for h in range(H_MLA)]
    lane = lax.broadcasted_iota(jnp.int32, (T, LANE), 1)
    o_m = []
    for p in range(H_MLA // 2):
        vp = slot(v_all, p).astype(BF16)
        o_m.append(_attend_pair(qs[2 * p:2 * p + 2], ks[2 * p:2 * p + 2], vp, lane))
    b_m = (jnp.concatenate(o_m, axis=1) * pr.sz[1]).astype(BF16)

    b_f = (_fourier(T, pr.fu, csw_ref, cts_ref) * pr.sz[2]).astype(BF16)

    y_ref[...] = _merge((b_r, b_m, b_f), pr.sg, x, mod_ref, wb_ref, wo_ref, fg_ref)


def _prompt_call(layer, x, mod, g, qg, kvg, w_p, logit, wq, wk, wv, csw, cts, wb, wo,
                 final_g=None, prev=None):
    T = SEQ
    row = pl.BlockSpec((T, D_MODEL), lambda i: (i, 0))
    in_specs = [row, _mod_spec(layer, lambda i: 0)]
    in_specs += [_layer_spec(a, layer) for a in (g, qg, kvg, w_p, logit, wq, wk, wv)]
    in_specs += [_full_spec(csw), _full_spec(cts), _layer_spec(wb, layer), _layer_spec(wo, layer)]
    args = [x, mod, g, qg, kvg, w_p, logit, wq, wk, wv, csw, cts, wb, wo]
    if final_g is not None:
        in_specs.append(_full_spec(final_g))
        args.append(final_g)
    tails = ((SEQ, KV_LORA), (D_ROPE, SEQ), ST_TAIL)
    if layer:
        in_specs += [_grow_spec(layer, t) for t in tails]
        args += list(prev)
    return pl.pallas_call(
        functools.partial(_prompt_kernel, layer, final_g is not None),
        grid=(BATCH,),
        in_specs=in_specs,
        out_specs=[row] + [_grow_spec(layer + 1, t) for t in tails],
        out_shape=[jax.ShapeDtypeStruct((BATCH * T, D_MODEL), F32)]
        + [jax.ShapeDtypeStruct((BATCH, layer + 1) + t, F32) for t in tails],
        scratch_shapes=[pltpu.VMEM((H_RET, T, T), F32)],
        compiler_params=_cparams(),
        name="prompt_layer",
    )(*args)


def _in_kernel(x_ref, mod_ref, g_ref, qg_ref, kvg_ref, w_ref,
               qkv_ref, sz_ref, sg_ref, qn_ref, ckv_ref, kr_ref, fu_ref):
    pr = _project(x_ref[...], mod_ref, g_ref, qg_ref, kvg_ref, w_ref)
    qw = H_RET * DK_RET
    qkv_ref[:, 0:qw] = pr.q
    qkv_ref[:, qw:2 * qw] = pr.k
    qkv_ref[:, 2 * qw:] = pr.v
    for j in range(3):
        sz_ref[:, j * BRANCH_W:(j + 1) * BRANCH_W] = pr.sz[j].astype(BF16)
    for j in range(N_BRANCH):
        sg_ref[:, j * D_MODEL:(j + 1) * D_MODEL] = pr.sg[j].astype(BF16)
    qn_ref[...] = pr.qn
    ckv_ref[...] = pr.ckv.astype(BF16)
    kr_ref[...] = pr.kr
    fu_ref[...] = pr.fu


def _in_call(layer, x, mod, g, qg, kvg, w_p):
    n = x.shape[0]
    per_b = DEC_SEQ // TM
    row = lambda w: pl.BlockSpec((TM, w), lambda i: (i, 0))
    outs = [(1024, BF16), (3 * BRANCH_W, BF16), (N_BRANCH * D_MODEL, BF16), (Q_LORA, BF16),
            (KV_LORA, BF16), (LANE, F32), (FOURIER_W, BF16)]
    return pl.pallas_call(
        _in_kernel,
        grid=(n // TM,),
        in_specs=[row(D_MODEL), _mod_spec(layer, lambda i: 1 + i // per_b)]
        + [_layer_spec(a, layer) for a in (g, qg, kvg, w_p)],
        out_specs=[row(w) for w, _ in outs],
        out_shape=[jax.ShapeDtypeStruct((n, w), dt) for w, dt in outs],
        compiler_params=_cparams(),
        name="in_proj_s",
    )(x, mod, g, qg, kvg, w_p)


def _ret_kernel(T, qkv_ref, rz_ref, lg_ref, st_ref, br_ref, mask_ref):
    lg = jax.nn.log_sigmoid(lg_ref[...])

    @pl.when(pl.program_id(0) == 0)
    def _():
        _init_masks(T, lg, mask_ref)

    qw = H_RET * DK_RET
    for h in range(H_RET):
        q = qkv_ref[:, h * DK_RET:(h + 1) * DK_RET]
        k = qkv_ref[:, qw + h * DK_RET:qw + (h + 1) * DK_RET]
        v = qkv_ref[:, 2 * qw + h * DV_RET:2 * qw + (h + 1) * DV_RET]
        on = _ret_head(T, q, k, v, mask_ref[h], lg[0:1, h:h + 1], lg[1:2, h:h + 1],
                       st=(st_ref[0, h], st_ref[1, h]))
        gate = rz_ref[:, h * DV_RET:(h + 1) * DV_RET].astype(F32)
        br_ref[:, h * DV_RET:(h + 1) * DV_RET] = (on * gate).astype(BF16)


def _ret_call(layer, qkv, sz, logit, state):
    T = DEC_SEQ
    return pl.pallas_call(
        functools.partial(_ret_kernel, T),
        grid=(DEC_BATCH,),
        in_specs=[pl.BlockSpec((T, 1024), lambda b: (b, 0)),
                  pl.BlockSpec((T, BRANCH_W), lambda b: (b, 0)),
                  _layer_spec(logit, layer),
                  pl.BlockSpec((None, None) + ST_TAIL, lambda b: (b, layer, 0, 0, 0, 0))],
        out_specs=pl.BlockSpec((T, RET_W), lambda b: (b, 0)),
        out_shape=jax.ShapeDtypeStruct((T * DEC_BATCH, RET_W), BF16),
        scratch_shapes=[pltpu.VMEM((H_RET, T, T), F32)],
        compiler_params=_cparams(),
        name="ret_s",
    )(qkv, sz, logit, state)


def _mla_kernel(T, qn_ref, ckv_ref, kr_ref, mz_ref, wq_ref, wk_ref, wv_ref, cckv_ref, ckr_ref,
                cos_ref, sa_ref, sb_ref, bm_ref, q_scr, k_scr, v_scr):
    P = PAST_LEN
    qn = qn_ref[...]
    ckv = ckv_ref[...]
    kr = _rope(kr_ref[...], cos_ref[...], sa_ref[...], sb_ref[...])
    cc = cckv_ref[...].astype(BF16)
    for h in range(H_MLA):
        sl = slice(h * LANE, (h + 1) * LANE)
        qh = _rope(_dot(qn, wq_ref[:, sl]), cos_ref[...], sa_ref[...], sb_ref[...])
        q_scr[h] = qh.astype(BF16)
        k_scr[h, 0:P] = (_dot(cc, wk_ref[:, sl]) + ckr_ref[...]).astype(BF16)
        k_scr[h, P:P + T] = (_dot(ckv, wk_ref[:, sl]) + kr).astype(BF16)
    for p in range(H_MLA // 2):
        sl = slice(p * LANE, (p + 1) * LANE)
        v_scr[p, 0:P] = _dot(cc, wv_ref[:, sl]).astype(BF16)
        v_scr[p, P:P + T] = _dot(ckv, wv_ref[:, sl]).astype(BF16)

    lane = lax.broadcasted_iota(jnp.int32, (TQ, LANE), 1)

    def body(qb, carry):
        rs = pl.ds(pl.multiple_of(qb * TQ, TQ), TQ)
        for p in range(H_MLA // 2):
            hs = (2 * p, 2 * p + 1)
            o = _attend_pair([q_scr[h, rs] for h in hs], [k_scr[h] for h in hs], v_scr[p], lane)
            gate = mz_ref[rs, p * LANE:(p + 1) * LANE].astype(F32)
            bm_ref[rs, p * LANE:(p + 1) * LANE] = (o * gate).astype(BF16)
        return carry

    lax.fori_loop(0, T // TQ, body, 0)


def _mla_call(layer, qn, ckv, kr, sz, wq, wk, wv, cache_ckv, cache_kr, cos, sa, sb):
    T = DEC_SEQ
    Tk = T + PAST_LEN
    row = lambda w, c=0: pl.BlockSpec((T, w), lambda b: (b, c))
    return pl.pallas_call(
        functools.partial(_mla_kernel, T),
        grid=(DEC_BATCH,),
        in_specs=[row(Q_LORA), row(KV_LORA), row(LANE), row(BRANCH_W, 1)]
        + [_layer_spec(a, layer) for a in (wq, wk, wv)]
        + [pl.BlockSpec((None, None, PAST_LEN, KV_LORA), lambda b: (b, layer, 0, 0)),
           pl.BlockSpec((None, None, PAST_LEN, LANE), lambda b: (b, layer, 0, 0)),
           _full_spec(cos), _full_spec(sa), _full_spec(sb)],
        out_specs=pl.BlockSpec((T, MLA_W), lambda b: (b, 0)),
        out_shape=jax.ShapeDtypeStruct((T * DEC_BATCH, MLA_W), BF16),
        scratch_shapes=[pltpu.VMEM((H_MLA, T, LANE), BF16),
                        pltpu.VMEM((H_MLA, Tk, LANE), BF16),
                        pltpu.VMEM((H_MLA // 2, Tk, LANE), BF16)],
        compiler_params=_cparams(),
        name="mla_s",
    )(qn, ckv, kr, sz, wq, wk, wv, cache_ckv, cache_kr, cos, sa, sb)


def _fourier_kernel(T, fu_ref, fz_ref, csw_ref, cts_ref, bf_ref):
    o = _fourier(T, fu_ref[...], csw_ref, cts_ref)
    bf_ref[...] = (o * fz_ref[...].astype(F32)).astype(BF16)


def _fourier_call(fu, sz, csw, cts):
    T = DEC_SEQ
    return pl.pallas_call(
        functools.partial(_fourier_kernel, T),
        grid=(DEC_BATCH,),
        in_specs=[pl.BlockSpec((T, FOURIER_W), lambda b: (b, 0)),
                  pl.BlockSpec((T, BRANCH_W), lambda b: (b, 2)),
                  _full_spec(csw), _full_spec(cts)],
        out_specs=pl.BlockSpec((T, FOURIER_W), lambda b: (b, 0)),
        out_shape=jax.ShapeDtypeStruct((T * DEC_BATCH, FOURIER_W), BF16),
        compiler_params=_cparams(),
        name="fourier_s",
    )(fu, sz, csw, cts)


def _merge_kernel(final, *refs):
    br_ref, bm_ref, bf_ref, sg_ref, x_ref, mod_ref, wb_ref, wo_ref = refs[:8]
    fg_ref = refs[8] if final else None
    sg = [sg_ref[:, n * D_MODEL:(n + 1) * D_MODEL].astype(F32) for n in range(N_BRANCH)]
    refs[-1][...] = _merge((br_ref[...], bm_ref[...], bf_ref[...]), sg, x_ref[...],
                           mod_ref, wb_ref, wo_ref, fg_ref)


def _merge_call(layer, br, bm, bf, sg, x, mod, wb, wo, final_g=None):
    final = final_g is not None
    n = x.shape[0]
    per_b = DEC_SEQ // TM
    row = lambda w: pl.BlockSpec((TM, w), lambda i: (i, 0))
    in_specs = [row(BRANCH_W), row(BRANCH_W), row(BRANCH_W), row(N_BRANCH * D_MODEL), row(D_MODEL),
                _mod_spec(layer, lambda i: 1 + i // per_b),
                _layer_spec(wb, layer), _layer_spec(wo, layer)]
    args = [br, bm, bf, sg, x, mod, wb, wo]
    if final:
        in_specs.append(_full_spec(final_g))
        args.append(final_g)
    return pl.pallas_call(
        functools.partial(_merge_kernel, final),
        grid=(n // TM,),
        in_specs=in_specs,
        out_specs=row(D_MODEL),
        out_shape=jax.ShapeDtypeStruct((n, D_MODEL), F32),
        compiler_params=_cparams(),
        name="merge_s",
    )(*args)


def _dft_tables(T):
    def cs(n):
        kt = (np.arange(n)[:, None] * np.arange(n)[None, :]) % n
        ang = 2.0 * np.pi * kt.astype(np.float64) / n
        return np.cos(ang), np.sin(ang)
    ct, st = cs(T)
    cw, sw = cs(F_GROUP_W)
    cts = np.concatenate([ct, -st], axis=1).astype(np.float32)
    csw = np.concatenate([cw, sw], axis=1).astype(np.float32)
    return jnp.asarray(csw).astype(BF16), jnp.asarray(cts).astype(BF16)


def _rope_tables(T):
    half = D_ROPE // 2
    nfreq = half // 2
    inv = ROPE_BASE ** (-np.arange(nfreq, dtype=np.float64) / nfreq)
    t = np.arange(T)
    pos = np.stack([t // GRID_W, t % GRID_W], axis=0).astype(np.float64)
    cos = np.ones((T, LANE), np.float64)
    sa = np.zeros((T, LANE), np.float64)
    sb = np.zeros((T, LANE), np.float64)
    for part in range(2):
        ang = pos[part][:, None] * inv[None, :]
        l1 = ROPE_LANE0 + part * half
        l2 = l1 + nfreq
        cos[:, l1:l1 + nfreq] = np.cos(ang)
        cos[:, l2:l2 + nfreq] = np.cos(ang)
        sa[:, l1:l1 + nfreq] = -np.sin(ang)
        sb[:, l2:l2 + nfreq] = np.sin(ang)
    return tuple(jnp.asarray(a.astype(np.float32)) for a in (cos, sa, sb))


KR_SRC = 2176
W_IN_RUNS = ((0, 1024),
             (1024, 512), (2208, 512), (3232, 512),
             (3744, 3072),
             (1536, 384),
             (1920, 256),
             (KR_SRC - ROPE_LANE0, LANE),
             (2720, 512))
W_PREP_TILES = 6
KR_TILE = SEG_KR[0] // LANE


def _w_prep_kernel(tbl_ref, *refs):
    o_ref = refs[-1]
    j = pl.program_id(1)
    lane = lax.broadcasted_iota(jnp.int32, (D_MODEL, LANE), 1)
    rotary = (lane >= ROPE_LANE0) & (lane < ROPE_LANE0 + D_ROPE)
    for t, w_ref in enumerate(refs[:-1]):
        wt = w_ref[0].T
        if t == KR_TILE % W_PREP_TILES:
            wt = jnp.where((j != KR_TILE // W_PREP_TILES) | rotary, wt, 0.0)
        o_ref[:, t * LANE:(t + 1) * LANE] = wt.astype(BF16)


def _permute_w_in(w):
    depth, d, in_w = w.shape
    starts = np.concatenate([src + np.arange(0, width, LANE) for src, width in W_IN_RUNS])
    assert starts.size * LANE == IN_WP and starts.size % W_PREP_TILES == 0
    tbl = jnp.asarray(starts, jnp.int32)
    tile_spec = lambda t: pl.BlockSpec(
        (pl.Element(1), pl.Element(LANE), pl.Element(d)),
        lambda l, j, tbl_ref: (l, pl.multiple_of(tbl_ref[j * W_PREP_TILES + t], D_ROPE), 0))
    wide = W_PREP_TILES * LANE
    return pl.pallas_call(
        _w_prep_kernel,
        grid_spec=pltpu.PrefetchScalarGridSpec(
            num_scalar_prefetch=1,
            grid=(depth, IN_WP // wide),
            in_specs=[tile_spec(t) for t in range(W_PREP_TILES)],
            out_specs=pl.BlockSpec((None, d, wide), lambda l, j, tbl_ref: (l, 0, j)),
        ),
        out_shape=jax.ShapeDtypeStruct((depth, d, IN_WP), BF16),
        compiler_params=_cparams(2),
        name="w_prep",
    )(tbl, *([jnp.swapaxes(w, 1, 2)] * W_PREP_TILES))


def _pad_heads(w, n_heads, width, lo, hi):
    d, k = w.shape[:2]
    wh = w.reshape(d, k, n_heads, width)[..., lo:hi]
    wh = jnp.pad(wh, ((0, 0), (0, 0), (0, 0), (0, LANE - (hi - lo))))
    return wh.reshape(d, k, n_heads * LANE).astype(BF16)


def kernel(x_prompt, x_sample, cache_ckv, cache_krope, state_ret, c, c_ctx, norm_g, w_mod, b_mod,
           w_in, ret_decay_logit, q_norm_g, w_q_up, kv_norm_g, w_kv_up, w_branch, w_out,
           final_norm_g):
    cv = jnp.concatenate([c_ctx[None, :], c, jnp.zeros((MOD_ROWS - 1 - DEC_BATCH, D_MODEL), F32)], axis=0)
    mod = _mod_call(cv, w_mod, b_mod).reshape(DEPTH, MOD_ROWS, 1, 3 * D_MODEL)

    cache_kr = jnp.pad(cache_krope, ((0, 0), (0, 0), (0, 0), (ROPE_LANE0, LANE - ROPE_LANE0 - D_ROPE)))
    rope = _rope_tables(DEC_SEQ)
    dft_p = _dft_tables(SEQ)
    dft_s = _dft_tables(DEC_SEQ)
    logit = jnp.pad(ret_decay_logit, ((0, 0), (0, 8 - 2), (0, LANE - H_RET)))
    g, qg, kvg = norm_g[:, None, :], q_norm_g[:, None, :], kv_norm_g[:, None, :]
    w_p = _permute_w_in(w_in)
    wq = _pad_heads(w_q_up, H_MLA, D_NOPE + D_ROPE, 0, D_NOPE + D_ROPE)
    wk = _pad_heads(w_kv_up, H_MLA, D_NOPE + D_VMLA, 0, D_NOPE)
    wv = w_kv_up.reshape(DEPTH, KV_LORA, H_MLA, D_NOPE + D_VMLA)[..., D_NOPE:]
    wv = wv.reshape(DEPTH, KV_LORA, MLA_W).astype(BF16)
    wb = w_branch.astype(BF16)
    wo = w_out.astype(BF16)
    final_g = final_norm_g[None, :]

    xp = x_prompt.reshape(BATCH * SEQ, D_MODEL)
    xs = x_sample.reshape(DEC_BATCH * DEC_SEQ, D_MODEL)
    new_ctx = None
    for l in range(DEPTH):
        fg = final_g if l == DEPTH - 1 else None
        xp, *new_ctx = _prompt_call(l, xp, mod, g, qg, kvg, w_p, logit, wq, wk, wv, *dft_p, wb, wo,
                                    final_g=fg, prev=new_ctx)

        qkv, sz, sg, qn, ckv, kr, fu = _in_call(l, xs, mod, g, qg, kvg, w_p)
        br = _ret_call(l, qkv, sz, logit, state_ret)
        bm = _mla_call(l, qn, ckv, kr, sz, wq, wk, wv, cache_ckv, cache_kr, *rope)
        bf = _fourier_call(fu, sz, *dft_s)
        xs = _merge_call(l, br, bm, bf, sg, xs, mod, wb, wo, final_g=fg)

    y_prompt = xp.reshape(BATCH, SEQ, D_MODEL)
    y_sample = xs.reshape(DEC_BATCH, DEC_SEQ, D_MODEL)
    new_ckv, new_krope_t, new_ret = new_ctx
    return (y_prompt, y_sample, new_ckv, jnp.swapaxes(new_krope_t, 2, 3), new_ret)
```

```python
import collections
import functools
import math

import numpy as np
import jax
import jax.numpy as jnp
from jax import lax
from jax.experimental import pallas as pl
from jax.experimental.pallas import tpu as pltpu

F32 = jnp.float32
BF16 = jnp.bfloat16

D_MODEL = 1024
BATCH = 32
SEQ = 256
DEPTH = 2
DEC_BATCH = 4
DEC_SEQ = 1024
PAST_LEN = 512
GRID_W = 64
EPS = 1e-6
H_RET = 4
DK_RET = 64
DV_RET = 128
RET_W = H_RET * DV_RET
H_MLA = 8
Q_LORA = 384
KV_LORA = 256
D_NOPE = 64
D_ROPE = 32
D_VMLA = 64
MLA_W = H_MLA * D_VMLA
ROPE_BASE = 10000.0
F_GROUPS = 4
F_GROUP_W = 128
FOURIER_W = F_GROUPS * F_GROUP_W
N_BRANCH = 3
BRANCH_W = 512

LANE = 128
TM = 512
TQ = 256
MOD_ROWS = 8
ROPE_LANE0 = D_NOPE
VMEM_LIMIT = 56 * 1024 * 1024
ST_TAIL = (2, H_RET, DK_RET, DV_RET)

SEG_QKV = (0, 1024)
SEG_Z = (1024, 2560)
SEG_G = (2560, 5632)
SEG_QL = (5632, 6016)
SEG_KV = (6016, 6272)
SEG_KR = (6272, 6400)
SEG_FU = (6400, 6912)
IN_WP = 6912

NT = (((1,), (1,)), ((), ()))
TN = (((0,), (0,)), ((), ()))


def _cparams(n_axes=1):
    return pltpu.CompilerParams(dimension_semantics=("arbitrary",) * n_axes,
                                vmem_limit_bytes=VMEM_LIMIT)


def _layer_spec(a, layer):
    shape = a.shape[1:]
    return pl.BlockSpec((None,) + shape, lambda i: (layer,) + (0,) * len(shape))


def _full_spec(a):
    return pl.BlockSpec(a.shape, lambda i: (0,) * a.ndim)


def _mod_spec(layer, row):
    return pl.BlockSpec((None, None, 1, 3 * D_MODEL), lambda i: (layer, row(i), 0, 0))


def _grow_spec(depth, tail):
    return pl.BlockSpec((1, depth) + tail, lambda i: (i,) + (0,) * (1 + len(tail)))


def _dot(a, b):
    return jnp.dot(a, b, preferred_element_type=F32)


def _rms(x, g):
    return x * lax.rsqrt(jnp.mean(x * x, axis=-1, keepdims=True) + EPS) * g


def _mod_kernel(c_ref, w_ref, b_ref, o_ref):
    cv = c_ref[...]
    s = cv * jax.nn.sigmoid(cv)
    o_ref[0] = jnp.dot(s, w_ref[0], preferred_element_type=F32,
                       precision=lax.Precision.HIGHEST) + b_ref[0]


def _mod_call(cv, w_mod, b_mod):
    nb = 3 * D_MODEL // 1024
    return pl.pallas_call(
        _mod_kernel,
        grid=(DEPTH, nb),
        in_specs=[
            pl.BlockSpec((MOD_ROWS, D_MODEL), lambda l, j: (0, 0)),
            pl.BlockSpec((1, D_MODEL, 1024), lambda l, j: (l, 0, j)),
            pl.BlockSpec((1, 1, 1024), lambda l, j: (l, 0, j)),
        ],
        out_specs=pl.BlockSpec((1, MOD_ROWS, 1024), lambda l, j: (l, 0, j)),
        out_shape=jax.ShapeDtypeStruct((DEPTH, MOD_ROWS, 3 * D_MODEL), F32),
        compiler_params=_cparams(2),
        name="mod",
    )(cv, w_mod, b_mod.reshape(DEPTH, 1, 3 * D_MODEL))


Proj = collections.namedtuple("Proj", "q k v sz sg qn ckv kr fu")


def _project(x, mod_ref, g_ref, qg_ref, kvg_ref, w_ref):
    shift = mod_ref[:, 0:D_MODEL]
    scale = mod_ref[:, D_MODEL:2 * D_MODEL]
    hb = (_rms(x, g_ref[...]) * (1.0 + scale) + shift).astype(BF16)

    def mm(a, b):
        return _dot(hb, w_ref[:, a:b])

    qw = H_RET * DK_RET
    p = mm(*SEG_QKV)
    q = p[:, 0:qw].astype(BF16)
    k = (p[:, qw:2 * qw] * (DK_RET ** -0.5)).astype(BF16)
    v = p[:, 2 * qw:].astype(BF16)
    sz = []
    for j in range(3):
        a = SEG_Z[0] + j * BRANCH_W
        p = mm(a, a + BRANCH_W)
        sz.append(p * jax.nn.sigmoid(p))
    sg = []
    for j in range(N_BRANCH):
        a = SEG_G[0] + j * D_MODEL
        sg.append(jax.nn.sigmoid(mm(a, a + D_MODEL)))
    qn = _rms(mm(*SEG_QL), qg_ref[...]).astype(BF16)
    ckv = _rms(mm(*SEG_KV), kvg_ref[...])
    kr = mm(*SEG_KR)
    fu = mm(*SEG_FU).astype(BF16)
    return Proj(q, k, v, sz, sg, qn, ckv, kr, fu)


def _init_masks(T, lg, mask_ref):
    ii = lax.broadcasted_iota(jnp.int32, (T, T), 0)
    jj = lax.broadcasted_iota(jnp.int32, (T, T), 1)
    d = (ii - jj).astype(F32)
    ad = jnp.abs(d)
    for h in range(H_RET):
        lf = lg[0:1, h:h + 1]
        lb = lg[1:2, h:h + 1]
        mask_ref[h] = jnp.where(d > 0, jnp.exp(ad * lf), jnp.where(d < 0, jnp.exp(ad * lb), 2.0))


def _ret_head(T, q, k, v, mask, lf, lb, st=None):
    s = lax.dot_general(q, k, NT, preferred_element_type=F32) * mask
    o = _dot(s.astype(BF16), v)
    if st is not None:
        rows = lax.broadcasted_iota(jnp.int32, (T, DV_RET), 0).astype(F32)
        o = o + jnp.exp((rows + 1.0) * lf) * _dot(q, st[0].astype(BF16))
        o = o + jnp.exp((T - rows) * lb) * _dot(q, st[1].astype(BF16))
    mu = jnp.mean(o, axis=-1, keepdims=True)
    c = o - mu
    var = jnp.mean(c * c, axis=-1, keepdims=True)
    return c * lax.rsqrt(var + EPS)


def _ret_states(T, k, v, lf, lb):
    rows = lax.broadcasted_iota(jnp.int32, (T, DK_RET), 0).astype(F32)
    kf = (k.astype(F32) * jnp.exp((T - 1.0 - rows) * lf)).astype(BF16)
    kb = (k.astype(F32) * jnp.exp(rows * lb)).astype(BF16)
    return (lax.dot_general(kf, v, TN, preferred_element_type=F32),
            lax.dot_general(kb, v, TN, preferred_element_type=F32))


def _rope(x, cos, sa, sb):
    half = D_ROPE // 4
    return x * cos + pltpu.roll(x, LANE - half, 1) * sa + pltpu.roll(x, half, 1) * sb


Q_SCALE = (D_NOPE + D_ROPE) ** -0.5 * math.log2(math.e)


def _value_slots(vp):
    first = lax.broadcasted_iota(jnp.int32, vp.shape, 1) < D_VMLA
    return (jnp.where(first, vp, 1.0).astype(BF16), jnp.where(first, 1.0, vp).astype(BF16))


def _attend_pair(q2, k2, v2):
    outs = []
    for qh, kh, vh in zip(q2, k2, v2):
        s = lax.dot_general(qh, kh, NT, preferred_element_type=F32)
        e = jnp.exp2((s - jnp.max(s, axis=-1, keepdims=True)).astype(BF16))
        outs.append(_dot(e, vh))
    first = lax.broadcasted_iota(jnp.int32, outs[0].shape, 1) < D_VMLA
    o = jnp.where(first, outs[0], outs[1])
    l = jnp.where(first, pltpu.roll(outs[0], D_VMLA, 1), pltpu.roll(outs[1], D_VMLA, 1))
    return o / l


def _fourier(T, fu, csw_ref, cts_ref):
    a, b = [], []
    for g in range(F_GROUPS):
        z = _dot(fu[:, g * F_GROUP_W:(g + 1) * F_GROUP_W], csw_ref[...])
        a.append(z[:, :F_GROUP_W])
        b.append(z[:, F_GROUP_W:])
    a = jnp.concatenate(a, axis=1).astype(BF16)
    b = jnp.concatenate(b, axis=1).astype(BF16)
    o = _dot(cts_ref[:, 0:T], a) + _dot(cts_ref[:, T:2 * T], b)
    return o * ((T * F_GROUP_W) ** -0.5)


def _merge(branches, sg, x, mod_ref, wb_ref, wo_ref, fg_ref):
    merged = None
    for n, b in enumerate(branches):
        term = _dot(b, wb_ref[n]) * sg[n]
        merged = term if merged is None else merged + term
    gate = mod_ref[:, 2 * D_MODEL:3 * D_MODEL]
    y = x + gate * _dot(merged.astype(BF16), wo_ref[...])
    if fg_ref is not None:
        y = _rms(y, fg_ref[...])
    return y


def _prompt_kernel(layer, final, *refs):
    (x_ref, mod_ref, g_ref, qg_ref, kvg_ref, w_ref, lg_ref, wq_ref, wk_ref, wv_ref,
     csw_ref, cts_ref, wb_ref, wo_ref) = refs[:14]
    n_in = 14
    fg_ref = None
    if final:
        fg_ref = refs[n_in]
        n_in += 1
    if layer:
        pckv_ref, pkr_ref, pst_ref = refs[n_in:n_in + 3]
        n_in += 3
    y_ref, nckv_ref, nkr_ref, nst_ref, mask_ref = refs[n_in:]
    T = SEQ
    lg = jax.nn.log_sigmoid(lg_ref[...])

    @pl.when(pl.program_id(0) == 0)
    def _():
        _init_masks(T, lg, mask_ref)

    x = x_ref[...]
    pr = _project(x, mod_ref, g_ref, qg_ref, kvg_ref, w_ref)

    if layer:
        nckv_ref[0, 0:layer] = pckv_ref[0]
        nkr_ref[0, 0:layer] = pkr_ref[0]
        nst_ref[0, 0:layer] = pst_ref[0]
    nckv_ref[0, layer] = pr.ckv
    nkr_ref[0, layer] = pr.kr.T[ROPE_LANE0:ROPE_LANE0 + D_ROPE]

    o_r = []
    for h in range(H_RET):
        lf = lg[0:1, h:h + 1]
        lb = lg[1:2, h:h + 1]
        q = pr.q[:, h * DK_RET:(h + 1) * DK_RET]
        k = pr.k[:, h * DK_RET:(h + 1) * DK_RET]
        v = pr.v[:, h * DV_RET:(h + 1) * DV_RET]
        o_r.append(_ret_head(T, q, k, v, mask_ref[h], lf, lb))
        sf, sb = _ret_states(T, k, v, lf, lb)
        nst_ref[0, layer, 0, h] = sf
        nst_ref[0, layer, 1, h] = sb
    b_r = (jnp.concatenate(o_r, axis=1) * pr.sz[0]).astype(BF16)

    ckvb = pr.ckv.astype(BF16)
    q_all = _dot(pr.qn, wq_ref[...])
    k_all = _dot(ckvb, wk_ref[...])
    v_all = _dot(ckvb, wv_ref[...])
    slot = lambda a, i: a[:, i * LANE:(i + 1) * LANE]
    qs = [(slot(q_all, h) * Q_SCALE).astype(BF16) for h in range(H_MLA)]
    ks = [(slot(k_all, h) + pr.kr).astype(BF16) for h in range(H_MLA)]
    o_m = []
    for p in range(H_MLA // 2):
        o_m.append(_attend_pair(qs[2 * p:2 * p + 2], ks[2 * p:2 * p + 2],
                                _value_slots(slot(v_all, p))))
    b_m = (jnp.concatenate(o_m, axis=1) * pr.sz[1]).astype(BF16)

    b_f = (_fourier(T, pr.fu, csw_ref, cts_ref) * pr.sz[2]).astype(BF16)

    y_ref[...] = _merge((b_r, b_m, b_f), pr.sg, x, mod_ref, wb_ref, wo_ref, fg_ref)


def _prompt_call(layer, x, mod, g, qg, kvg, w_p, logit, wq, wk, wv, csw, cts, wb, wo,
                 final_g=None, prev=None):
    T = SEQ
    row = pl.BlockSpec((T, D_MODEL), lambda i: (i, 0))
    in_specs = [row, _mod_spec(layer, lambda i: 0)]
    in_specs += [_layer_spec(a, layer) for a in (g, qg, kvg, w_p, logit, wq, wk, wv)]
    in_specs += [_full_spec(csw), _full_spec(cts), _layer_spec(wb, layer), _layer_spec(wo, layer)]
    args = [x, mod, g, qg, kvg, w_p, logit, wq, wk, wv, csw, cts, wb, wo]
    if final_g is not None:
        in_specs.append(_full_spec(final_g))
        args.append(final_g)
    tails = ((SEQ, KV_LORA), (D_ROPE, SEQ), ST_TAIL)
    if layer:
        in_specs += [_grow_spec(layer, t) for t in tails]
        args += list(prev)
    return pl.pallas_call(
        functools.partial(_prompt_kernel, layer, final_g is not None),
        grid=(BATCH,),
        in_specs=in_specs,
        out_specs=[row] + [_grow_spec(layer + 1, t) for t in tails],
        out_shape=[jax.ShapeDtypeStruct((BATCH * T, D_MODEL), F32)]
        + [jax.ShapeDtypeStruct((BATCH, layer + 1) + t, F32) for t in tails],
        scratch_shapes=[pltpu.VMEM((H_RET, T, T), F32)],
        compiler_params=_cparams(),
        name="prompt_layer",
    )(*args)


def _in_kernel(x_ref, mod_ref, g_ref, qg_ref, kvg_ref, w_ref,
               qkv_ref, sz_ref, sg_ref, qn_ref, ckv_ref, kr_ref, fu_ref):
    pr = _project(x_ref[...], mod_ref, g_ref, qg_ref, kvg_ref, w_ref)
    qw = H_RET * DK_RET
    qkv_ref[:, 0:qw] = pr.q
    qkv_ref[:, qw:2 * qw] = pr.k
    qkv_ref[:, 2 * qw:] = pr.v
    for j in range(3):
        sz_ref[:, j * BRANCH_W:(j + 1) * BRANCH_W] = pr.sz[j].astype(BF16)
    for j in range(N_BRANCH):
        sg_ref[:, j * D_MODEL:(j + 1) * D_MODEL] = pr.sg[j].astype(BF16)
    qn_ref[...] = pr.qn
    ckv_ref[...] = pr.ckv.astype(BF16)
    kr_ref[...] = pr.kr
    fu_ref[...] = pr.fu


def _in_call(layer, x, mod, g, qg, kvg, w_p):
    n = x.shape[0]
    per_b = DEC_SEQ // TM
    row = lambda w: pl.BlockSpec((TM, w), lambda i: (i, 0))
    outs = [(1024, BF16), (3 * BRANCH_W, BF16), (N_BRANCH * D_MODEL, BF16), (Q_LORA, BF16),
            (KV_LORA, BF16), (LANE, F32), (FOURIER_W, BF16)]
    return pl.pallas_call(
        _in_kernel,
        grid=(n // TM,),
        in_specs=[row(D_MODEL), _mod_spec(layer, lambda i: 1 + i // per_b)]
        + [_layer_spec(a, layer) for a in (g, qg, kvg, w_p)],
        out_specs=[row(w) for w, _ in outs],
        out_shape=[jax.ShapeDtypeStruct((n, w), dt) for w, dt in outs],
        compiler_params=_cparams(),
        name="in_proj_s",
    )(x, mod, g, qg, kvg, w_p)


def _ret_kernel(T, qkv_ref, rz_ref, lg_ref, st_ref, br_ref, mask_ref):
    lg = jax.nn.log_sigmoid(lg_ref[...])

    @pl.when(pl.program_id(0) == 0)
    def _():
        _init_masks(T, lg, mask_ref)

    qw = H_RET * DK_RET
    for h in range(H_RET):
        q = qkv_ref[:, h * DK_RET:(h + 1) * DK_RET]
        k = qkv_ref[:, qw + h * DK_RET:qw + (h + 1) * DK_RET]
        v = qkv_ref[:, 2 * qw + h * DV_RET:2 * qw + (h + 1) * DV_RET]
        on = _ret_head(T, q, k, v, mask_ref[h], lg[0:1, h:h + 1], lg[1:2, h:h + 1],
                       st=(st_ref[0, h], st_ref[1, h]))
        gate = rz_ref[:, h * DV_RET:(h + 1) * DV_RET].astype(F32)
        br_ref[:, h * DV_RET:(h + 1) * DV_RET] = (on * gate).astype(BF16)


def _ret_call(layer, qkv, sz, logit, state):
    T = DEC_SEQ
    return pl.pallas_call(
        functools.partial(_ret_kernel, T),
        grid=(DEC_BATCH,),
        in_specs=[pl.BlockSpec((T, 1024), lambda b: (b, 0)),
                  pl.BlockSpec((T, BRANCH_W), lambda b: (b, 0)),
                  _layer_spec(logit, layer),
                  pl.BlockSpec((None, None) + ST_TAIL, lambda b: (b, layer, 0, 0, 0, 0))],
        out_specs=pl.BlockSpec((T, RET_W), lambda b: (b, 0)),
        out_shape=jax.ShapeDtypeStruct((T * DEC_BATCH, RET_W), BF16),
        scratch_shapes=[pltpu.VMEM((H_RET, T, T), F32)],
        compiler_params=_cparams(),
        name="ret_s",
    )(qkv, sz, logit, state)


def _mla_kernel(T, qn_ref, ckv_ref, kr_ref, mz_ref, wq_ref, wk_ref, wv_ref, cckv_ref, ckr_ref,
                cos_ref, sa_ref, sb_ref, bm_ref, q_scr, k_scr, v_scr):
    P = PAST_LEN
    qn = qn_ref[...]
    ckv = ckv_ref[...]
    kr = _rope(kr_ref[...], cos_ref[...], sa_ref[...], sb_ref[...])
    cc = cckv_ref[...].astype(BF16)
    for h in range(H_MLA):
        sl = slice(h * LANE, (h + 1) * LANE)
        qh = _rope(_dot(qn, wq_ref[:, sl]), cos_ref[...], sa_ref[...], sb_ref[...])
        q_scr[h] = (qh * Q_SCALE).astype(BF16)
        k_scr[h, 0:P] = (_dot(cc, wk_ref[:, sl]) + ckr_ref[...]).astype(BF16)
        k_scr[h, P:P + T] = (_dot(ckv, wk_ref[:, sl]) + kr).astype(BF16)
    for p in range(H_MLA // 2):
        sl = slice(p * LANE, (p + 1) * LANE)
        for rows, src in ((slice(0, P), cc), (slice(P, P + T), ckv)):
            v_scr[2 * p, rows], v_scr[2 * p + 1, rows] = _value_slots(_dot(src, wv_ref[:, sl]))

    def body(qb, carry):
        rs = pl.ds(pl.multiple_of(qb * TQ, TQ), TQ)
        for p in range(H_MLA // 2):
            hs = (2 * p, 2 * p + 1)
            o = _attend_pair([q_scr[h, rs] for h in hs], [k_scr[h] for h in hs],
                             [v_scr[h] for h in hs])
            gate = mz_ref[rs, p * LANE:(p + 1) * LANE].astype(F32)
            bm_ref[rs, p * LANE:(p + 1) * LANE] = (o * gate).astype(BF16)
        return carry

    lax.fori_loop(0, T // TQ, body, 0)


def _mla_call(layer, qn, ckv, kr, sz, wq, wk, wv, cache_ckv, cache_kr, cos, sa, sb):
    T = DEC_SEQ
    Tk = T + PAST_LEN
    row = lambda w, c=0: pl.BlockSpec((T, w), lambda b: (b, c))
    return pl.pallas_call(
        functools.partial(_mla_kernel, T),
        grid=(DEC_BATCH,),
        in_specs=[row(Q_LORA), row(KV_LORA), row(LANE), row(BRANCH_W, 1)]
        + [_layer_spec(a, layer) for a in (wq, wk, wv)]
        + [pl.BlockSpec((None, None, PAST_LEN, KV_LORA), lambda b: (b, layer, 0, 0)),
           pl.BlockSpec((None, None, PAST_LEN, LANE), lambda b: (b, layer, 0, 0)),
           _full_spec(cos), _full_spec(sa), _full_spec(sb)],
        out_specs=pl.BlockSpec((T, MLA_W), lambda b: (b, 0)),
        out_shape=jax.ShapeDtypeStruct((T * DEC_BATCH, MLA_W), BF16),
        scratch_shapes=[pltpu.VMEM((H_MLA, T, LANE), BF16),
                        pltpu.VMEM((H_MLA, Tk, LANE), BF16),
                        pltpu.VMEM((H_MLA, Tk, LANE), BF16)],
        compiler_params=_cparams(),
        name="mla_s",
    )(qn, ckv, kr, sz, wq, wk, wv, cache_ckv, cache_kr, cos, sa, sb)


def _fourier_kernel(T, fu_ref, fz_ref, csw_ref, cts_ref, bf_ref):
    o = _fourier(T, fu_ref[...], csw_ref, cts_ref)
    bf_ref[...] = (o * fz_ref[...].astype(F32)).astype(BF16)


def _fourier_call(fu, sz, csw, cts):
    T = DEC_SEQ
    return pl.pallas_call(
        functools.partial(_fourier_kernel, T),
        grid=(DEC_BATCH,),
        in_specs=[pl.BlockSpec((T, FOURIER_W), lambda b: (b, 0)),
                  pl.BlockSpec((T, BRANCH_W), lambda b: (b, 2)),
                  _full_spec(csw), _full_spec(cts)],
        out_specs=pl.BlockSpec((T, FOURIER_W), lambda b: (b, 0)),
        out_shape=jax.ShapeDtypeStruct((T * DEC_BATCH, FOURIER_W), BF16),
        compiler_params=_cparams(),
        name="fourier_s",
    )(fu, sz, csw, cts)


def _merge_kernel(final, *refs):
    br_ref, bm_ref, bf_ref, sg_ref, x_ref, mod_ref, wb_ref, wo_ref = refs[:8]
    fg_ref = refs[8] if final else None
    sg = [sg_ref[:, n * D_MODEL:(n + 1) * D_MODEL].astype(F32) for n in range(N_BRANCH)]
    refs[-1][...] = _merge((br_ref[...], bm_ref[...], bf_ref[...]), sg, x_ref[...],
                           mod_ref, wb_ref, wo_ref, fg_ref)


def _merge_call(layer, br, bm, bf, sg, x, mod, wb, wo, final_g=None):
    final = final_g is not None
    n = x.shape[0]
    per_b = DEC_SEQ // TM
    row = lambda w: pl.BlockSpec((TM, w), lambda i: (i, 0))
    in_specs = [row(BRANCH_W), row(BRANCH_W), row(BRANCH_W), row(N_BRANCH * D_MODEL), row(D_MODEL),
                _mod_spec(layer, lambda i: 1 + i // per_b),
                _layer_spec(wb, layer), _layer_spec(wo, layer)]
    args = [br, bm, bf, sg, x, mod, wb, wo]
    if final:
        in_specs.append(_full_spec(final_g))
        args.append(final_g)
    return pl.pallas_call(
        functools.partial(_merge_kernel, final),
        grid=(n // TM,),
        in_specs=in_specs,
        out_specs=row(D_MODEL),
        out_shape=jax.ShapeDtypeStruct((n, D_MODEL), F32),
        compiler_params=_cparams(),
        name="merge_s",
    )(*args)


def _dft_tables(T):
    def cs(n):
        kt = (np.arange(n)[:, None] * np.arange(n)[None, :]) % n
        ang = 2.0 * np.pi * kt.astype(np.float64) / n
        return np.cos(ang), np.sin(ang)
    ct, st = cs(T)
    cw, sw = cs(F_GROUP_W)
    cts = np.concatenate([ct, -st], axis=1).astype(np.float32)
    csw = np.concatenate([cw, sw], axis=1).astype(np.float32)
    return jnp.asarray(csw).astype(BF16), jnp.asarray(cts).astype(BF16)


def _rope_tables(T):
    half = D_ROPE // 2
    nfreq = half // 2
    inv = ROPE_BASE ** (-np.arange(nfreq, dtype=np.float64) / nfreq)
    t = np.arange(T)
    pos = np.stack([t // GRID_W, t % GRID_W], axis=0).astype(np.float64)
    cos = np.ones((T, LANE), np.float64)
    sa = np.zeros((T, LANE), np.float64)
    sb = np.zeros((T, LANE), np.float64)
    for part in range(2):
        ang = pos[part][:, None] * inv[None, :]
        l1 = ROPE_LANE0 + part * half
        l2 = l1 + nfreq
        cos[:, l1:l1 + nfreq] = np.cos(ang)
        cos[:, l2:l2 + nfreq] = np.cos(ang)
        sa[:, l1:l1 + nfreq] = -np.sin(ang)
        sb[:, l2:l2 + nfreq] = np.sin(ang)
    return tuple(jnp.asarray(a.astype(np.float32)) for a in (cos, sa, sb))


KR_SRC = 2176
W_IN_RUNS = ((0, 1024),
             (1024, 512), (2208, 512), (3232, 512),
             (3744, 3072),
             (1536, 384),
             (1920, 256),
             (KR_SRC - ROPE_LANE0, LANE),
             (2720, 512))
W_PREP_TILES = 6
KR_TILE = SEG_KR[0] // LANE


def _w_prep_kernel(tbl_ref, *refs):
    o_ref = refs[-1]
    j = pl.program_id(1)
    lane = lax.broadcasted_iota(jnp.int32, (D_MODEL, LANE), 1)
    rotary = (lane >= ROPE_LANE0) & (lane < ROPE_LANE0 + D_ROPE)
    for t, w_ref in enumerate(refs[:-1]):
        wt = w_ref[0].T
        if t == KR_TILE % W_PREP_TILES:
            wt = jnp.where((j != KR_TILE // W_PREP_TILES) | rotary, wt, 0.0)
        o_ref[:, t * LANE:(t + 1) * LANE] = wt.astype(BF16)


def _permute_w_in(w):
    depth, d, in_w = w.shape
    starts = np.concatenate([src + np.arange(0, width, LANE) for src, width in W_IN_RUNS])
    assert starts.size * LANE == IN_WP and starts.size % W_PREP_TILES == 0
    tbl = jnp.asarray(starts, jnp.int32)
    tile_spec = lambda t: pl.BlockSpec(
        (pl.Element(1), pl.Element(LANE), pl.Element(d)),
        lambda l, j, tbl_ref: (l, pl.multiple_of(tbl_ref[j * W_PREP_TILES + t], D_ROPE), 0))
    wide = W_PREP_TILES * LANE
    return pl.pallas_call(
        _w_prep_kernel,
        grid_spec=pltpu.PrefetchScalarGridSpec(
            num_scalar_prefetch=1,
            grid=(depth, IN_WP // wide),
            in_specs=[tile_spec(t) for t in range(W_PREP_TILES)],
            out_specs=pl.BlockSpec((None, d, wide), lambda l, j, tbl_ref: (l, 0, j)),
        ),
        out_shape=jax.ShapeDtypeStruct((depth, d, IN_WP), BF16),
        compiler_params=_cparams(2),
        name="w_prep",
    )(tbl, *([jnp.swapaxes(w, 1, 2)] * W_PREP_TILES))


def _pad_heads(w, n_heads, width, lo, hi):
    d, k = w.shape[:2]
    wh = w.reshape(d, k, n_heads, width)[..., lo:hi]
    wh = jnp.pad(wh, ((0, 0), (0, 0), (0, 0), (0, LANE - (hi - lo))))
    return wh.reshape(d, k, n_heads * LANE).astype(BF16)


def kernel(x_prompt, x_sample, cache_ckv, cache_krope, state_ret, c, c_ctx, norm_g, w_mod, b_mod,
           w_in, ret_decay_logit, q_norm_g, w_q_up, kv_norm_g, w_kv_up, w_branch, w_out,
           final_norm_g):
    cv = jnp.concatenate([c_ctx[None, :], c, jnp.zeros((MOD_ROWS - 1 - DEC_BATCH, D_MODEL), F32)], axis=0)
    mod = _mod_call(cv, w_mod, b_mod).reshape(DEPTH, MOD_ROWS, 1, 3 * D_MODEL)

    cache_kr = jnp.pad(cache_krope, ((0, 0), (0, 0), (0, 0), (ROPE_LANE0, LANE - ROPE_LANE0 - D_ROPE)))
    rope = _rope_tables(DEC_SEQ)
    dft_p = _dft_tables(SEQ)
    dft_s = _dft_tables(DEC_SEQ)
    logit = jnp.pad(ret_decay_logit, ((0, 0), (0, 8 - 2), (0, LANE - H_RET)))
    g, qg, kvg = norm_g[:, None, :], q_norm_g[:, None, :], kv_norm_g[:, None, :]
    w_p = _permute_w_in(w_in)
    wq = _pad_heads(w_q_up, H_MLA, D_NOPE + D_ROPE, 0, D_NOPE + D_ROPE)
    wk = _pad_heads(w_kv_up, H_MLA, D_NOPE + D_VMLA, 0, D_NOPE)
    wv = w_kv_up.reshape(DEPTH, KV_LORA, H_MLA, D_NOPE + D_VMLA)[..., D_NOPE:]
    wv = wv.reshape(DEPTH, KV_LORA, MLA_W).astype(BF16)
    wb = w_branch.astype(BF16)
    wo = w_out.astype(BF16)
    final_g = final_norm_g[None, :]

    xp = x_prompt.reshape(BATCH * SEQ, D_MODEL)
    xs = x_sample.reshape(DEC_BATCH * DEC_SEQ, D_MODEL)
    new_ctx = None
    for l in range(DEPTH):
        fg = final_g if l == DEPTH - 1 else None
        xp, *new_ctx = _prompt_call(l, xp, mod, g, qg, kvg, w_p, logit, wq, wk, wv, *dft_p, wb, wo,
                                    final_g=fg, prev=new_ctx)

        qkv, sz, sg, qn, ckv, kr, fu = _in_call(l, xs, mod, g, qg, kvg, w_p)
        br = _ret_call(l, qkv, sz, logit, state_ret)
        bm = _mla_call(l, qn, ckv, kr, sz, wq, wk, wv, cache_ckv, cache_kr, *rope)
        bf = _fourier_call(fu, sz, *dft_s)
        xs = _merge_call(l, br, bm, bf, sg, xs, mod, wb, wo, final_g=fg)

    y_prompt = xp.reshape(BATCH, SEQ, D_MODEL)
    y_sample = xs.reshape(DEC_BATCH, DEC_SEQ, D_MODEL)
    new_ckv, new_krope_t, new_ret = new_ctx
    return (y_prompt, y_sample, new_ckv, jnp.swapaxes(new_krope_t, 2, 3), new_ret)
```

```python
import collections
import functools
import math

import numpy as np
import jax
import jax.numpy as jnp
from jax import lax
from jax.experimental import pallas as pl
from jax.experimental.pallas import tpu as pltpu

F32 = jnp.float32
BF16 = jnp.bfloat16

D_MODEL = 1024
BATCH = 32
SEQ = 256
DEPTH = 2
DEC_BATCH = 4
DEC_SEQ = 1024
PAST_LEN = 512
GRID_W = 64
EPS = 1e-6
H_RET = 4
DK_RET = 64
DV_RET = 128
RET_W = H_RET * DV_RET
H_MLA = 8
Q_LORA = 384
KV_LORA = 256
D_NOPE = 64
D_ROPE = 32
D_VMLA = 64
MLA_W = H_MLA * D_VMLA
ROPE_BASE = 10000.0
F_GROUPS = 4
F_GROUP_W = 128
FOURIER_W = F_GROUPS * F_GROUP_W
N_BRANCH = 3
BRANCH_W = 512

LANE = 128
TM = 512
TQ = 1024
MOD_ROWS = 8
ROPE_LANE0 = D_NOPE
VMEM_LIMIT = 56 * 1024 * 1024
ST_TAIL = (2, H_RET, DK_RET, DV_RET)

SEG_QKV = (0, 1024)
SEG_Z = (1024, 2560)
SEG_G = (2560, 5632)
SEG_QL = (5632, 6016)
SEG_KV = (6016, 6272)
SEG_KR = (6272, 6400)
SEG_FU = (6400, 6912)
IN_WP = 6912

NT = (((1,), (1,)), ((), ()))
TN = (((0,), (0,)), ((), ()))


def _cparams(n_axes=1):
    return pltpu.CompilerParams(dimension_semantics=("arbitrary",) * n_axes,
                                vmem_limit_bytes=VMEM_LIMIT)


def _layer_spec(a, layer):
    shape = a.shape[1:]
    return pl.BlockSpec((None,) + shape, lambda i: (layer,) + (0,) * len(shape))


def _full_spec(a):
    return pl.BlockSpec(a.shape, lambda i: (0,) * a.ndim)


def _mod_spec(layer, row):
    return pl.BlockSpec((None, None, 1, 3 * D_MODEL), lambda i: (layer, row(i), 0, 0))


def _grow_spec(depth, tail):
    return pl.BlockSpec((1, depth) + tail, lambda i: (i,) + (0,) * (1 + len(tail)))


def _dot(a, b):
    return jnp.dot(a, b, preferred_element_type=F32)


def _rms(x, g):
    return x * lax.rsqrt(jnp.mean(x * x, axis=-1, keepdims=True) + EPS) * g


def _mod_kernel(c_ref, w_ref, b_ref, o_ref):
    cv = c_ref[...]
    s = cv * jax.nn.sigmoid(cv)
    o_ref[0] = jnp.dot(s, w_ref[0], preferred_element_type=F32,
                       precision=lax.Precision.HIGHEST) + b_ref[0]


def _mod_call(cv, w_mod, b_mod):
    nb = 3 * D_MODEL // 1024
    return pl.pallas_call(
        _mod_kernel,
        grid=(DEPTH, nb),
        in_specs=[
            pl.BlockSpec((MOD_ROWS, D_MODEL), lambda l, j: (0, 0)),
            pl.BlockSpec((1, D_MODEL, 1024), lambda l, j: (l, 0, j)),
            pl.BlockSpec((1, 1, 1024), lambda l, j: (l, 0, j)),
        ],
        out_specs=pl.BlockSpec((1, MOD_ROWS, 1024), lambda l, j: (l, 0, j)),
        out_shape=jax.ShapeDtypeStruct((DEPTH, MOD_ROWS, 3 * D_MODEL), F32),
        compiler_params=_cparams(2),
        name="mod",
    )(cv, w_mod, b_mod.reshape(DEPTH, 1, 3 * D_MODEL))


Proj = collections.namedtuple("Proj", "q k v sz sg qn ckv kr fu")


def _project(x, mod_ref, g_ref, qg_ref, kvg_ref, w_ref):
    shift = mod_ref[:, 0:D_MODEL]
    scale = mod_ref[:, D_MODEL:2 * D_MODEL]
    hb = (_rms(x, g_ref[...]) * (1.0 + scale) + shift).astype(BF16)

    def mm(a, b):
        return _dot(hb, w_ref[:, a:b])

    qw = H_RET * DK_RET
    p = mm(*SEG_QKV)
    q = p[:, 0:qw].astype(BF16)
    k = (p[:, qw:2 * qw] * (DK_RET ** -0.5)).astype(BF16)
    v = p[:, 2 * qw:].astype(BF16)
    sz = []
    for j in range(3):
        a = SEG_Z[0] + j * BRANCH_W
        p = mm(a, a + BRANCH_W)
        sz.append(p * jax.nn.sigmoid(p))
    sg = []
    for j in range(N_BRANCH):
        a = SEG_G[0] + j * D_MODEL
        sg.append(jax.nn.sigmoid(mm(a, a + D_MODEL)))
    qn = _rms(mm(*SEG_QL), qg_ref[...]).astype(BF16)
    ckv = _rms(mm(*SEG_KV), kvg_ref[...])
    kr = mm(*SEG_KR)
    fu = mm(*SEG_FU).astype(BF16)
    return Proj(q, k, v, sz, sg, qn, ckv, kr, fu)


def _init_masks(T, lg, mask_ref):
    ii = lax.broadcasted_iota(jnp.int32, (T, T), 0)
    jj = lax.broadcasted_iota(jnp.int32, (T, T), 1)
    d = (ii - jj).astype(F32)
    ad = jnp.abs(d)
    for h in range(H_RET):
        lf = lg[0:1, h:h + 1]
        lb = lg[1:2, h:h + 1]
        mask_ref[h] = jnp.where(d > 0, jnp.exp(ad * lf), jnp.where(d < 0, jnp.exp(ad * lb), 2.0))


def _ret_head(T, q, k, v, mask, lf, lb, st=None):
    s = lax.dot_general(q, k, NT, preferred_element_type=F32) * mask
    o = _dot(s.astype(BF16), v)
    if st is not None:
        rows = lax.broadcasted_iota(jnp.int32, (T, DV_RET), 0).astype(F32)
        o = o + jnp.exp((rows + 1.0) * lf) * _dot(q, st[0].astype(BF16))
        o = o + jnp.exp((T - rows) * lb) * _dot(q, st[1].astype(BF16))
    mu = jnp.mean(o, axis=-1, keepdims=True)
    c = o - mu
    var = jnp.mean(c * c, axis=-1, keepdims=True)
    return c * lax.rsqrt(var + EPS)


def _ret_states(T, k, v, lf, lb):
    rows = lax.broadcasted_iota(jnp.int32, (T, DK_RET), 0).astype(F32)
    kf = (k.astype(F32) * jnp.exp((T - 1.0 - rows) * lf)).astype(BF16)
    kb = (k.astype(F32) * jnp.exp(rows * lb)).astype(BF16)
    return (lax.dot_general(kf, v, TN, preferred_element_type=F32),
            lax.dot_general(kb, v, TN, preferred_element_type=F32))


def _rope(x, cos, sa, sb):
    half = D_ROPE // 4
    return x * cos + pltpu.roll(x, LANE - half, 1) * sa + pltpu.roll(x, half, 1) * sb


Q_SCALE = (D_NOPE + D_ROPE) ** -0.5 * math.log2(math.e)


def _value_slots(vp):
    first = lax.broadcasted_iota(jnp.int32, vp.shape, 1) < D_VMLA
    return (jnp.where(first, vp, 1.0).astype(BF16), jnp.where(first, 1.0, vp).astype(BF16))


def _attend_pair(q2, k2, v2):
    outs = []
    for qh, kh, vh in zip(q2, k2, v2):
        s = lax.dot_general(qh, kh, NT, preferred_element_type=F32)
        e = jnp.exp2((s - jnp.max(s, axis=-1, keepdims=True)).astype(BF16))
        outs.append(_dot(e, vh))
    first = lax.broadcasted_iota(jnp.int32, outs[0].shape, 1) < D_VMLA
    o = jnp.where(first, outs[0], outs[1])
    l = jnp.where(first, pltpu.roll(outs[0], D_VMLA, 1), pltpu.roll(outs[1], D_VMLA, 1))
    return o / l


def _fourier(T, fu, csw_ref, cts_ref):
    a, b = [], []
    for g in range(F_GROUPS):
        z = _dot(fu[:, g * F_GROUP_W:(g + 1) * F_GROUP_W], csw_ref[...])
        a.append(z[:, :F_GROUP_W])
        b.append(z[:, F_GROUP_W:])
    a = jnp.concatenate(a, axis=1).astype(BF16)
    b = jnp.concatenate(b, axis=1).astype(BF16)
    o = _dot(cts_ref[:, 0:T], a) + _dot(cts_ref[:, T:2 * T], b)
    return o * ((T * F_GROUP_W) ** -0.5)


def _merge(branches, sg, x, mod_ref, wb_ref, wo_ref, fg_ref):
    merged = None
    for n, b in enumerate(branches):
        term = _dot(b, wb_ref[n]) * sg[n]
        merged = term if merged is None else merged + term
    gate = mod_ref[:, 2 * D_MODEL:3 * D_MODEL]
    y = x + gate * _dot(merged.astype(BF16), wo_ref[...])
    if fg_ref is not None:
        y = _rms(y, fg_ref[...])
    return y


def _prompt_kernel(layer, final, *refs):
    (x_ref, mod_ref, g_ref, qg_ref, kvg_ref, w_ref, lg_ref, wq_ref, wk_ref, wv_ref,
     csw_ref, cts_ref, wb_ref, wo_ref) = refs[:14]
    n_in = 14
    fg_ref = None
    if final:
        fg_ref = refs[n_in]
        n_in += 1
    if layer:
        pckv_ref, pkr_ref, pst_ref = refs[n_in:n_in + 3]
        n_in += 3
    y_ref, nckv_ref, nkr_ref, nst_ref, mask_ref = refs[n_in:]
    T = SEQ
    lg = jax.nn.log_sigmoid(lg_ref[...])

    @pl.when(pl.program_id(0) == 0)
    def _():
        _init_masks(T, lg, mask_ref)

    x = x_ref[...]
    pr = _project(x, mod_ref, g_ref, qg_ref, kvg_ref, w_ref)

    if layer:
        nckv_ref[0, 0:layer] = pckv_ref[0]
        nkr_ref[0, 0:layer] = pkr_ref[0]
        nst_ref[0, 0:layer] = pst_ref[0]
    nckv_ref[0, layer] = pr.ckv
    nkr_ref[0, layer] = pr.kr.T[ROPE_LANE0:ROPE_LANE0 + D_ROPE]

    o_r = []
    for h in range(H_RET):
        lf = lg[0:1, h:h + 1]
        lb = lg[1:2, h:h + 1]
        q = pr.q[:, h * DK_RET:(h + 1) * DK_RET]
        k = pr.k[:, h * DK_RET:(h + 1) * DK_RET]
        v = pr.v[:, h * DV_RET:(h + 1) * DV_RET]
        o_r.append(_ret_head(T, q, k, v, mask_ref[h], lf, lb))
        sf, sb = _ret_states(T, k, v, lf, lb)
        nst_ref[0, layer, 0, h] = sf
        nst_ref[0, layer, 1, h] = sb
    b_r = (jnp.concatenate(o_r, axis=1) * pr.sz[0]).astype(BF16)

    ckvb = pr.ckv.astype(BF16)
    q_all = _dot(pr.qn, wq_ref[...])
    k_all = _dot(ckvb, wk_ref[...])
    v_all = _dot(ckvb, wv_ref[...])
    slot = lambda a, i: a[:, i * LANE:(i + 1) * LANE]
    qs = [(slot(q_all, h) * Q_SCALE).astype(BF16) for h in range(H_MLA)]
    ks = [(slot(k_all, h) + pr.kr).astype(BF16) for h in range(H_MLA)]
    o_m = []
    for p in range(H_MLA // 2):
        o_m.append(_attend_pair(qs[2 * p:2 * p + 2], ks[2 * p:2 * p + 2],
                                _value_slots(slot(v_all, p))))
    b_m = (jnp.concatenate(o_m, axis=1) * pr.sz[1]).astype(BF16)

    b_f = (_fourier(T, pr.fu, csw_ref, cts_ref) * pr.sz[2]).astype(BF16)

    y_ref[...] = _merge((b_r, b_m, b_f), pr.sg, x, mod_ref, wb_ref, wo_ref, fg_ref)


def _prompt_call(layer, x, mod, g, qg, kvg, w_p, logit, wq, wk, wv, csw, cts, wb, wo,
                 final_g=None, prev=None):
    T = SEQ
    row = pl.BlockSpec((T, D_MODEL), lambda i: (i, 0))
    in_specs = [row, _mod_spec(layer, lambda i: 0)]
    in_specs += [_layer_spec(a, layer) for a in (g, qg, kvg, w_p, logit, wq, wk, wv)]
    in_specs += [_full_spec(csw), _full_spec(cts), _layer_spec(wb, layer), _layer_spec(wo, layer)]
    args = [x, mod, g, qg, kvg, w_p, logit, wq, wk, wv, csw, cts, wb, wo]
    if final_g is not None:
        in_specs.append(_full_spec(final_g))
        args.append(final_g)
    tails = ((SEQ, KV_LORA), (D_ROPE, SEQ), ST_TAIL)
    if layer:
        in_specs += [_grow_spec(layer, t) for t in tails]
        args += list(prev)
    return pl.pallas_call(
        functools.partial(_prompt_kernel, layer, final_g is not None),
        grid=(BATCH,),
        in_specs=in_specs,
        out_specs=[row] + [_grow_spec(layer + 1, t) for t in tails],
        out_shape=[jax.ShapeDtypeStruct((BATCH * T, D_MODEL), F32)]
        + [jax.ShapeDtypeStruct((BATCH, layer + 1) + t, F32) for t in tails],
        scratch_shapes=[pltpu.VMEM((H_RET, T, T), F32)],
        compiler_params=_cparams(),
        name="prompt_layer",
    )(*args)


def _in_kernel(x_ref, mod_ref, g_ref, qg_ref, kvg_ref, w_ref,
               qkv_ref, sz_ref, sg_ref, qn_ref, ckv_ref, kr_ref, fu_ref):
    pr = _project(x_ref[...], mod_ref, g_ref, qg_ref, kvg_ref, w_ref)
    qw = H_RET * DK_RET
    qkv_ref[:, 0:qw] = pr.q
    qkv_ref[:, qw:2 * qw] = pr.k
    qkv_ref[:, 2 * qw:] = pr.v
    for j in range(3):
        sz_ref[:, j * BRANCH_W:(j + 1) * BRANCH_W] = pr.sz[j].astype(BF16)
    for j in range(N_BRANCH):
        sg_ref[:, j * D_MODEL:(j + 1) * D_MODEL] = pr.sg[j].astype(BF16)
    qn_ref[...] = pr.qn
    ckv_ref[...] = pr.ckv.astype(BF16)
    kr_ref[...] = pr.kr
    fu_ref[...] = pr.fu


def _in_call(layer, x, mod, g, qg, kvg, w_p):
    n = x.shape[0]
    per_b = DEC_SEQ // TM
    row = lambda w: pl.BlockSpec((TM, w), lambda i: (i, 0))
    outs = [(1024, BF16), (3 * BRANCH_W, BF16), (N_BRANCH * D_MODEL, BF16), (Q_LORA, BF16),
            (KV_LORA, BF16), (LANE, F32), (FOURIER_W, BF16)]
    return pl.pallas_call(
        _in_kernel,
        grid=(n // TM,),
        in_specs=[row(D_MODEL), _mod_spec(layer, lambda i: 1 + i // per_b)]
        + [_layer_spec(a, layer) for a in (g, qg, kvg, w_p)],
        out_specs=[row(w) for w, _ in outs],
        out_shape=[jax.ShapeDtypeStruct((n, w), dt) for w, dt in outs],
        compiler_params=_cparams(),
        name="in_proj_s",
    )(x, mod, g, qg, kvg, w_p)


def _ret_kernel(T, qkv_ref, rz_ref, lg_ref, st_ref, br_ref, mask_ref):
    lg = jax.nn.log_sigmoid(lg_ref[...])

    @pl.when(pl.program_id(0) == 0)
    def _():
        _init_masks(T, lg, mask_ref)

    qw = H_RET * DK_RET
    for h in range(H_RET):
        q = qkv_ref[:, h * DK_RET:(h + 1) * DK_RET]
        k = qkv_ref[:, qw + h * DK_RET:qw + (h + 1) * DK_RET]
        v = qkv_ref[:, 2 * qw + h * DV_RET:2 * qw + (h + 1) * DV_RET]
        on = _ret_head(T, q, k, v, mask_ref[h], lg[0:1, h:h + 1], lg[1:2, h:h + 1],
                       st=(st_ref[0, h], st_ref[1, h]))
        gate = rz_ref[:, h * DV_RET:(h + 1) * DV_RET].astype(F32)
        br_ref[:, h * DV_RET:(h + 1) * DV_RET] = (on * gate).astype(BF16)


def _ret_call(layer, qkv, sz, logit, state):
    T = DEC_SEQ
    return pl.pallas_call(
        functools.partial(_ret_kernel, T),
        grid=(DEC_BATCH,),
        in_specs=[pl.BlockSpec((T, 1024), lambda b: (b, 0)),
                  pl.BlockSpec((T, BRANCH_W), lambda b: (b, 0)),
                  _layer_spec(logit, layer),
                  pl.BlockSpec((None, None) + ST_TAIL, lambda b: (b, layer, 0, 0, 0, 0))],
        out_specs=pl.BlockSpec((T, RET_W), lambda b: (b, 0)),
        out_shape=jax.ShapeDtypeStruct((T * DEC_BATCH, RET_W), BF16),
        scratch_shapes=[pltpu.VMEM((H_RET, T, T), F32)],
        compiler_params=_cparams(),
        name="ret_s",
    )(qkv, sz, logit, state)


def _mla_kernel(T, qn_ref, ckv_ref, kr_ref, mz_ref, wq_ref, wk_ref, wv_ref, cckv_ref, ckr_ref,
                cos_ref, sa_ref, sb_ref, bm_ref, q_scr, k_scr, v_scr):
    P = PAST_LEN
    qn = qn_ref[...]
    ckv = ckv_ref[...]
    kr = _rope(kr_ref[...], cos_ref[...], sa_ref[...], sb_ref[...])
    cc = cckv_ref[...].astype(BF16)
    for p in range(H_MLA // 2):
        sl = slice(2 * p * LANE, 2 * (p + 1) * LANE)
        q2 = _dot(qn, wq_ref[:, sl])
        kc2 = _dot(cc, wk_ref[:, sl])
        kn2 = _dot(ckv, wk_ref[:, sl])
        for i in range(2):
            h = 2 * p + i
            hl = slice(i * LANE, (i + 1) * LANE)
            qh = _rope(q2[:, hl], cos_ref[...], sa_ref[...], sb_ref[...])
            q_scr[h] = (qh * Q_SCALE).astype(BF16)
            k_scr[h, 0:P] = (kc2[:, hl] + ckr_ref[...]).astype(BF16)
            k_scr[h, P:P + T] = (kn2[:, hl] + kr).astype(BF16)
    for pp in range(H_MLA // 4):
        sl = slice(2 * pp * LANE, 2 * (pp + 1) * LANE)
        for rows, src in ((slice(0, P), cc), (slice(P, P + T), ckv)):
            v2 = _dot(src, wv_ref[:, sl])
            for i in range(2):
                h = 4 * pp + 2 * i
                v_scr[h, rows], v_scr[h + 1, rows] = _value_slots(v2[:, i * LANE:(i + 1) * LANE])

    def body(qb, carry):
        rs = pl.ds(pl.multiple_of(qb * TQ, TQ), TQ)
        for p in range(H_MLA // 2):
            hs = (2 * p, 2 * p + 1)
            o = _attend_pair([q_scr[h, rs] for h in hs], [k_scr[h] for h in hs],
                             [v_scr[h] for h in hs])
            gate = mz_ref[rs, p * LANE:(p + 1) * LANE].astype(F32)
            bm_ref[rs, p * LANE:(p + 1) * LANE] = (o * gate).astype(BF16)
        return carry

    lax.fori_loop(0, T // TQ, body, 0)


def _mla_call(layer, qn, ckv, kr, sz, wq, wk, wv, cache_ckv, cache_kr, cos, sa, sb):
    T = DEC_SEQ
    Tk = T + PAST_LEN
    row = lambda w, c=0: pl.BlockSpec((T, w), lambda b: (b, c))
    return pl.pallas_call(
        functools.partial(_mla_kernel, T),
        grid=(DEC_BATCH,),
        in_specs=[row(Q_LORA), row(KV_LORA), row(LANE), row(BRANCH_W, 1)]
        + [_layer_spec(a, layer) for a in (wq, wk, wv)]
        + [pl.BlockSpec((None, None, PAST_LEN, KV_LORA), lambda b: (b, layer, 0, 0)),
           pl.BlockSpec((None, None, PAST_LEN, LANE), lambda b: (b, layer, 0, 0)),
           _full_spec(cos), _full_spec(sa), _full_spec(sb)],
        out_specs=pl.BlockSpec((T, MLA_W), lambda b: (b, 0)),
        out_shape=jax.ShapeDtypeStruct((T * DEC_BATCH, MLA_W), BF16),
        scratch_shapes=[pltpu.VMEM((H_MLA, T, LANE), BF16),
                        pltpu.VMEM((H_MLA, Tk, LANE), BF16),
                        pltpu.VMEM((H_MLA, Tk, LANE), BF16)],
        compiler_params=_cparams(),
        name="mla_s",
    )(qn, ckv, kr, sz, wq, wk, wv, cache_ckv, cache_kr, cos, sa, sb)


def _fourier_kernel(T, fu_ref, fz_ref, csw_ref, cts_ref, bf_ref):
    o = _fourier(T, fu_ref[...], csw_ref, cts_ref)
    bf_ref[...] = (o * fz_ref[...].astype(F32)).astype(BF16)


def _fourier_call(fu, sz, csw, cts):
    T = DEC_SEQ
    return pl.pallas_call(
        functools.partial(_fourier_kernel, T),
        grid=(DEC_BATCH,),
        in_specs=[pl.BlockSpec((T, FOURIER_W), lambda b: (b, 0)),
                  pl.BlockSpec((T, BRANCH_W), lambda b: (b, 2)),
                  _full_spec(csw), _full_spec(cts)],
        out_specs=pl.BlockSpec((T, FOURIER_W), lambda b: (b, 0)),
        out_shape=jax.ShapeDtypeStruct((T * DEC_BATCH, FOURIER_W), BF16),
        compiler_params=_cparams(),
        name="fourier_s",
    )(fu, sz, csw, cts)


def _merge_kernel(final, *refs):
    br_ref, bm_ref, bf_ref, sg_ref, x_ref, mod_ref, wb_ref, wo_ref = refs[:8]
    fg_ref = refs[8] if final else None
    sg = [sg_ref[:, n * D_MODEL:(n + 1) * D_MODEL].astype(F32) for n in range(N_BRANCH)]
    refs[-1][...] = _merge((br_ref[...], bm_ref[...], bf_ref[...]), sg, x_ref[...],
                           mod_ref, wb_ref, wo_ref, fg_ref)


def _merge_call(layer, br, bm, bf, sg, x, mod, wb, wo, final_g=None):
    final = final_g is not None
    n = x.shape[0]
    per_b = DEC_SEQ // TM
    row = lambda w: pl.BlockSpec((TM, w), lambda i: (i, 0))
    in_specs = [row(BRANCH_W), row(BRANCH_W), row(BRANCH_W), row(N_BRANCH * D_MODEL), row(D_MODEL),
                _mod_spec(layer, lambda i: 1 + i // per_b),
                _layer_spec(wb, layer), _layer_spec(wo, layer)]
    args = [br, bm, bf, sg, x, mod, wb, wo]
    if final:
        in_specs.append(_full_spec(final_g))
        args.append(final_g)
    return pl.pallas_call(
        functools.partial(_merge_kernel, final),
        grid=(n // TM,),
        in_specs=in_specs,
        out_specs=row(D_MODEL),
        out_shape=jax.ShapeDtypeStruct((n, D_MODEL), F32),
        compiler_params=_cparams(),
        name="merge_s",
    )(*args)


def _dft_tables(T):
    def cs(n):
        kt = (np.arange(n)[:, None] * np.arange(n)[None, :]) % n
        ang = 2.0 * np.pi * kt.astype(np.float64) / n
        return np.cos(ang), np.sin(ang)
    ct, st = cs(T)
    cw, sw = cs(F_GROUP_W)
    cts = np.concatenate([ct, -st], axis=1).astype(np.float32)
    csw = np.concatenate([cw, sw], axis=1).astype(np.float32)
    return jnp.asarray(csw).astype(BF16), jnp.asarray(cts).astype(BF16)


def _rope_tables(T):
    half = D_ROPE // 2
    nfreq = half // 2
    inv = ROPE_BASE ** (-np.arange(nfreq, dtype=np.float64) / nfreq)
    t = np.arange(T)
    pos = np.stack([t // GRID_W, t % GRID_W], axis=0).astype(np.float64)
    cos = np.ones((T, LANE), np.float64)
    sa = np.zeros((T, LANE), np.float64)
    sb = np.zeros((T, LANE), np.float64)
    for part in range(2):
        ang = pos[part][:, None] * inv[None, :]
        l1 = ROPE_LANE0 + part * half
        l2 = l1 + nfreq
        cos[:, l1:l1 + nfreq] = np.cos(ang)
        cos[:, l2:l2 + nfreq] = np.cos(ang)
        sa[:, l1:l1 + nfreq] = -np.sin(ang)
        sb[:, l2:l2 + nfreq] = np.sin(ang)
    return tuple(jnp.asarray(a.astype(np.float32)) for a in (cos, sa, sb))


KR_SRC = 2176
W_IN_RUNS = ((0, 1024),
             (1024, 512), (2208, 512), (3232, 512),
             (3744, 3072),
             (1536, 384),
             (1920, 256),
             (KR_SRC - ROPE_LANE0, LANE),
             (2720, 512))
W_PREP_TILES = 6
KR_TILE = SEG_KR[0] // LANE


def _w_prep_kernel(tbl_ref, *refs):
    o_ref = refs[-1]
    j = pl.program_id(1)
    lane = lax.broadcasted_iota(jnp.int32, (D_MODEL, LANE), 1)
    rotary = (lane >= ROPE_LANE0) & (lane < ROPE_LANE0 + D_ROPE)
    for t, w_ref in enumerate(refs[:-1]):
        wt = w_ref[0].T
        if t == KR_TILE % W_PREP_TILES:
            wt = jnp.where((j != KR_TILE // W_PREP_TILES) | rotary, wt, 0.0)
        o_ref[:, t * LANE:(t + 1) * LANE] = wt.astype(BF16)


def _permute_w_in(w):
    depth, d, in_w = w.shape
    starts = np.concatenate([src + np.arange(0, width, LANE) for src, width in W_IN_RUNS])
    assert starts.size * LANE == IN_WP and starts.size % W_PREP_TILES == 0
    tbl = jnp.asarray(starts, jnp.int32)
    tile_spec = lambda t: pl.BlockSpec(
        (pl.Element(1), pl.Element(LANE), pl.Element(d)),
        lambda l, j, tbl_ref: (l, pl.multiple_of(tbl_ref[j * W_PREP_TILES + t], D_ROPE), 0))
    wide = W_PREP_TILES * LANE
    return pl.pallas_call(
        _w_prep_kernel,
        grid_spec=pltpu.PrefetchScalarGridSpec(
            num_scalar_prefetch=1,
            grid=(depth, IN_WP // wide),
            in_specs=[tile_spec(t) for t in range(W_PREP_TILES)],
            out_specs=pl.BlockSpec((None, d, wide), lambda l, j, tbl_ref: (l, 0, j)),
        ),
        out_shape=jax.ShapeDtypeStruct((depth, d, IN_WP), BF16),
        compiler_params=_cparams(2),
        name="w_prep",
    )(tbl, *([jnp.swapaxes(w, 1, 2)] * W_PREP_TILES))


def _pad_heads(w, n_heads, width, lo, hi):
    d, k = w.shape[:2]
    wh = w.reshape(d, k, n_heads, width)[..., lo:hi]
    wh = jnp.pad(wh, ((0, 0), (0, 0), (0, 0), (0, LANE - (hi - lo))))
    return wh.reshape(d, k, n_heads * LANE).astype(BF16)


def kernel(x_prompt, x_sample, cache_ckv, cache_krope, state_ret, c, c_ctx, norm_g, w_mod, b_mod,
           w_in, ret_decay_logit, q_norm_g, w_q_up, kv_norm_g, w_kv_up, w_branch, w_out,
           final_norm_g):
    cv = jnp.concatenate([c_ctx[None, :], c, jnp.zeros((MOD_ROWS - 1 - DEC_BATCH, D_MODEL), F32)], axis=0)
    mod = _mod_call(cv, w_mod, b_mod).reshape(DEPTH, MOD_ROWS, 1, 3 * D_MODEL)

    cache_kr = jnp.pad(cache_krope, ((0, 0), (0, 0), (0, 0), (ROPE_LANE0, LANE - ROPE_LANE0 - D_ROPE)))
    rope = _rope_tables(DEC_SEQ)
    dft_p = _dft_tables(SEQ)
    dft_s = _dft_tables(DEC_SEQ)
    logit = jnp.pad(ret_decay_logit, ((0, 0), (0, 8 - 2), (0, LANE - H_RET)))
    g, qg, kvg = norm_g[:, None, :], q_norm_g[:, None, :], kv_norm_g[:, None, :]
    w_p = _permute_w_in(w_in)
    wq = _pad_heads(w_q_up, H_MLA, D_NOPE + D_ROPE, 0, D_NOPE + D_ROPE)
    wk = _pad_heads(w_kv_up, H_MLA, D_NOPE + D_VMLA, 0, D_NOPE)
    wv = w_kv_up.reshape(DEPTH, KV_LORA, H_MLA, D_NOPE + D_VMLA)[..., D_NOPE:]
    wv = wv.reshape(DEPTH, KV_LORA, MLA_W).astype(BF16)
    wb = w_branch.astype(BF16)
    wo = w_out.astype(BF16)
    final_g = final_norm_g[None, :]

    xp = x_prompt.reshape(BATCH * SEQ, D_MODEL)
    xs = x_sample.reshape(DEC_BATCH * DEC_SEQ, D_MODEL)
    new_ctx = None
    for l in range(DEPTH):
        fg = final_g if l == DEPTH - 1 else None
        xp, *new_ctx = _prompt_call(l, xp, mod, g, qg, kvg, w_p, logit, wq, wk, wv, *dft_p, wb, wo,
                                    final_g=fg, prev=new_ctx)

        qkv, sz, sg, qn, ckv, kr, fu = _in_call(l, xs, mod, g, qg, kvg, w_p)
        br = _ret_call(l, qkv, sz, logit, state_ret)
        bm = _mla_call(l, qn, ckv, kr, sz, wq, wk, wv, cache_ckv, cache_kr, *rope)
        bf = _fourier_call(fu, sz, *dft_s)
        xs = _merge_call(l, br, bm, bf, sg, xs, mod, wb, wo, final_g=fg)

    y_prompt = xp.reshape(BATCH, SEQ, D_MODEL)
    y_sample = xs.reshape(DEC_BATCH, DEC_SEQ, D_MODEL)
    new_ckv, new_krope_t, new_ret = new_ctx
    return (y_prompt, y_sample, new_ckv, jnp.swapaxes(new_krope_t, 2, 3), new_ret)
```

```python
import collections
import functools
import math

import numpy as np
import jax
import jax.numpy as jnp
from jax import lax
from jax.experimental import pallas as pl
from jax.experimental.pallas import tpu as pltpu

F32 = jnp.float32
BF16 = jnp.bfloat16

D_MODEL = 1024
BATCH = 32
SEQ = 256
DEPTH = 2
DEC_BATCH = 4
DEC_SEQ = 1024
PAST_LEN = 512
GRID_W = 64
EPS = 1e-6
H_RET = 4
DK_RET = 64
DV_RET = 128
RET_W = H_RET * DV_RET
H_MLA = 8
Q_LORA = 384
KV_LORA = 256
D_NOPE = 64
D_ROPE = 32
D_VMLA = 64
MLA_W = H_MLA * D_VMLA
ROPE_BASE = 10000.0
F_GROUPS = 4
F_GROUP_W = 128
FOURIER_W = F_GROUPS * F_GROUP_W
N_BRANCH = 3
BRANCH_W = 512

LANE = 128
TM = 512
PROMPT_SPS = 2
TQ = 1024
MOD_ROWS = 8
ROPE_LANE0 = D_NOPE
VMEM_LIMIT = 56 * 1024 * 1024
ST_TAIL = (2, H_RET, DK_RET, DV_RET)

SEG_QKV = (0, 1024)
SEG_Z = (1024, 2560)
SEG_G = (2560, 5632)
SEG_QL = (5632, 6016)
SEG_KV = (6016, 6272)
SEG_KR = (6272, 6400)
SEG_FU = (6400, 6912)
IN_WP = 6912

NT = (((1,), (1,)), ((), ()))
TN = (((0,), (0,)), ((), ()))


def _cparams(n_axes=1):
    return pltpu.CompilerParams(dimension_semantics=("arbitrary",) * n_axes,
                                vmem_limit_bytes=VMEM_LIMIT)


def _layer_spec(a, layer):
    shape = a.shape[1:]
    return pl.BlockSpec((None,) + shape, lambda i: (layer,) + (0,) * len(shape))


def _full_spec(a):
    return pl.BlockSpec(a.shape, lambda i: (0,) * a.ndim)


def _mod_spec(layer, row):
    return pl.BlockSpec((None, None, 1, 3 * D_MODEL), lambda i: (layer, row(i), 0, 0))


def _grow_spec(depth, tail):
    return pl.BlockSpec((PROMPT_SPS, depth) + tail, lambda i: (i,) + (0,) * (1 + len(tail)))


def _dot(a, b):
    return jnp.dot(a, b, preferred_element_type=F32)


def _rms(x, g):
    return x * lax.rsqrt(jnp.mean(x * x, axis=-1, keepdims=True) + EPS) * g


def _mod_kernel(c_ref, w_ref, b_ref, o_ref):
    cv = c_ref[...]
    s = cv * jax.nn.sigmoid(cv)
    o_ref[0] = jnp.dot(s, w_ref[0], preferred_element_type=F32,
                       precision=lax.Precision.HIGHEST) + b_ref[0]


def _mod_call(cv, w_mod, b_mod):
    nb = 3 * D_MODEL // 1024
    return pl.pallas_call(
        _mod_kernel,
        grid=(DEPTH, nb),
        in_specs=[
            pl.BlockSpec((MOD_ROWS, D_MODEL), lambda l, j: (0, 0)),
            pl.BlockSpec((1, D_MODEL, 1024), lambda l, j: (l, 0, j)),
            pl.BlockSpec((1, 1, 1024), lambda l, j: (l, 0, j)),
        ],
        out_specs=pl.BlockSpec((1, MOD_ROWS, 1024), lambda l, j: (l, 0, j)),
        out_shape=jax.ShapeDtypeStruct((DEPTH, MOD_ROWS, 3 * D_MODEL), F32),
        compiler_params=_cparams(2),
        name="mod",
    )(cv, w_mod, b_mod.reshape(DEPTH, 1, 3 * D_MODEL))


Proj = collections.namedtuple("Proj", "q k v sz sg qn ckv kr fu")


def _project(x, mod_ref, g_ref, qg_ref, kvg_ref, w_ref):
    shift = mod_ref[:, 0:D_MODEL]
    scale = mod_ref[:, D_MODEL:2 * D_MODEL]
    hb = (_rms(x, g_ref[...]) * (1.0 + scale) + shift).astype(BF16)

    def mm(a, b):
        return _dot(hb, w_ref[:, a:b])

    qw = H_RET * DK_RET
    p = mm(*SEG_QKV)
    q = p[:, 0:qw].astype(BF16)
    k = (p[:, qw:2 * qw] * (DK_RET ** -0.5)).astype(BF16)
    v = p[:, 2 * qw:].astype(BF16)
    sz = []
    for j in range(3):
        a = SEG_Z[0] + j * BRANCH_W
        p = mm(a, a + BRANCH_W)
        sz.append(p * jax.nn.sigmoid(p))
    sg = []
    for j in range(N_BRANCH):
        a = SEG_G[0] + j * D_MODEL
        sg.append(jax.nn.sigmoid(mm(a, a + D_MODEL)))
    qn = _rms(mm(*SEG_QL), qg_ref[...]).astype(BF16)
    ckv = _rms(mm(*SEG_KV), kvg_ref[...])
    kr = mm(*SEG_KR)
    fu = mm(*SEG_FU).astype(BF16)
    return Proj(q, k, v, sz, sg, qn, ckv, kr, fu)


def _init_masks(T, lg, mask_ref):
    ii = lax.broadcasted_iota(jnp.int32, (T, T), 0)
    jj = lax.broadcasted_iota(jnp.int32, (T, T), 1)
    d = (ii - jj).astype(F32)
    ad = jnp.abs(d)
    for h in range(H_RET):
        lf = lg[0:1, h:h + 1]
        lb = lg[1:2, h:h + 1]
        mask_ref[h] = jnp.where(d > 0, jnp.exp(ad * lf), jnp.where(d < 0, jnp.exp(ad * lb), 2.0))


def _ret_head(T, q, k, v, mask, lf, lb, st=None):
    s = lax.dot_general(q, k, NT, preferred_element_type=F32) * mask
    o = _dot(s.astype(BF16), v)
    if st is not None:
        rows = lax.broadcasted_iota(jnp.int32, (T, DV_RET), 0).astype(F32)
        o = o + jnp.exp((rows + 1.0) * lf) * _dot(q, st[0].astype(BF16))
        o = o + jnp.exp((T - rows) * lb) * _dot(q, st[1].astype(BF16))
    mu = jnp.mean(o, axis=-1, keepdims=True)
    c = o - mu
    var = jnp.mean(c * c, axis=-1, keepdims=True)
    return c * lax.rsqrt(var + EPS)


def _ret_states(T, k, v, lf, lb):
    rows = lax.broadcasted_iota(jnp.int32, (T, DK_RET), 0).astype(F32)
    kf = (k.astype(F32) * jnp.exp((T - 1.0 - rows) * lf)).astype(BF16)
    kb = (k.astype(F32) * jnp.exp(rows * lb)).astype(BF16)
    return (lax.dot_general(kf, v, TN, preferred_element_type=F32),
            lax.dot_general(kb, v, TN, preferred_element_type=F32))


def _rope(x, cos, sa, sb):
    half = D_ROPE // 4
    return x * cos + pltpu.roll(x, LANE - half, 1) * sa + pltpu.roll(x, half, 1) * sb


Q_SCALE = (D_NOPE + D_ROPE) ** -0.5 * math.log2(math.e)


def _value_slots(vp):
    first = lax.broadcasted_iota(jnp.int32, vp.shape, 1) < D_VMLA
    return (jnp.where(first, vp, 1.0).astype(BF16), jnp.where(first, 1.0, vp).astype(BF16))


def _attend_pair(q2, k2, v2):
    outs = []
    for qh, kh, vh in zip(q2, k2, v2):
        s = lax.dot_general(qh, kh, NT, preferred_element_type=F32)
        e = jnp.exp2((s - jnp.max(s, axis=-1, keepdims=True)).astype(BF16))
        outs.append(_dot(e, vh))
    first = lax.broadcasted_iota(jnp.int32, outs[0].shape, 1) < D_VMLA
    o = jnp.where(first, outs[0], outs[1])
    l = jnp.where(first, pltpu.roll(outs[0], D_VMLA, 1), pltpu.roll(outs[1], D_VMLA, 1))
    return o / l


def _fourier(T, fu, csw_ref, cts_ref):
    a, b = [], []
    for g in range(F_GROUPS):
        z = _dot(fu[:, g * F_GROUP_W:(g + 1) * F_GROUP_W], csw_ref[...])
        a.append(z[:, :F_GROUP_W])
        b.append(z[:, F_GROUP_W:])
    a = jnp.concatenate(a, axis=1).astype(BF16)
    b = jnp.concatenate(b, axis=1).astype(BF16)
    o = _dot(cts_ref[:, 0:T], a) + _dot(cts_ref[:, T:2 * T], b)
    return o * ((T * F_GROUP_W) ** -0.5)


def _merge(branches, sg, x, mod_ref, wb_ref, wo_ref, fg_ref):
    merged = None
    for n, b in enumerate(branches):
        term = _dot(b, wb_ref[n]) * sg[n]
        merged = term if merged is None else merged + term
    gate = mod_ref[:, 2 * D_MODEL:3 * D_MODEL]
    y = x + gate * _dot(merged.astype(BF16), wo_ref[...])
    if fg_ref is not None:
        y = _rms(y, fg_ref[...])
    return y


def _prompt_kernel(layer, final, *refs):
    (x_ref, mod_ref, g_ref, qg_ref, kvg_ref, w_ref, lg_ref, wq_ref, wk_ref, wv_ref,
     csw_ref, cts_ref, wb_ref, wo_ref) = refs[:14]
    n_in = 14
    fg_ref = None
    if final:
        fg_ref = refs[n_in]
        n_in += 1
    if layer:
        pckv_ref, pkr_ref, pst_ref = refs[n_in:n_in + 3]
        n_in += 3
    y_ref, nckv_ref, nkr_ref, nst_ref, mask_ref = refs[n_in:]
    T = SEQ
    lg = jax.nn.log_sigmoid(lg_ref[...])

    @pl.when(pl.program_id(0) == 0)
    def _():
        _init_masks(T, lg, mask_ref)

    slot = lambda a, i: a[:, i * LANE:(i + 1) * LANE]
    for s in range(PROMPT_SPS):
        rows = slice(s * T, (s + 1) * T)
        x = x_ref[rows]
        pr = _project(x, mod_ref, g_ref, qg_ref, kvg_ref, w_ref)

        if layer:
            nckv_ref[s, 0:layer] = pckv_ref[s]
            nkr_ref[s, 0:layer] = pkr_ref[s]
            nst_ref[s, 0:layer] = pst_ref[s]
        nckv_ref[s, layer] = pr.ckv
        nkr_ref[s, layer] = pr.kr.T[ROPE_LANE0:ROPE_LANE0 + D_ROPE]

        o_r = []
        for h in range(H_RET):
            lf = lg[0:1, h:h + 1]
            lb = lg[1:2, h:h + 1]
            q = pr.q[:, h * DK_RET:(h + 1) * DK_RET]
            k = pr.k[:, h * DK_RET:(h + 1) * DK_RET]
            v = pr.v[:, h * DV_RET:(h + 1) * DV_RET]
            o_r.append(_ret_head(T, q, k, v, mask_ref[h], lf, lb))
            sf, sb = _ret_states(T, k, v, lf, lb)
            nst_ref[s, layer, 0, h] = sf
            nst_ref[s, layer, 1, h] = sb
        b_r = (jnp.concatenate(o_r, axis=1) * pr.sz[0]).astype(BF16)

        ckvb = pr.ckv.astype(BF16)
        q_all = _dot(pr.qn, wq_ref[...]) * Q_SCALE
        k_all = _dot(ckvb, wk_ref[...])
        v_all = _dot(ckvb, wv_ref[...])
        qs = [slot(q_all, h).astype(BF16) for h in range(H_MLA)]
        ks = [(slot(k_all, h) + pr.kr).astype(BF16) for h in range(H_MLA)]
        o_m = []
        for p in range(H_MLA // 2):
            o_m.append(_attend_pair(qs[2 * p:2 * p + 2], ks[2 * p:2 * p + 2],
                                    _value_slots(slot(v_all, p))))
        b_m = (jnp.concatenate(o_m, axis=1) * pr.sz[1]).astype(BF16)

        b_f = (_fourier(T, pr.fu, csw_ref, cts_ref) * pr.sz[2]).astype(BF16)

        y_ref[rows] = _merge((b_r, b_m, b_f), pr.sg, x, mod_ref, wb_ref, wo_ref, fg_ref)


def _prompt_call(layer, x, mod, g, qg, kvg, w_p, logit, wq, wk, wv, csw, cts, wb, wo,
                 final_g=None, prev=None):
    T = SEQ
    row = pl.BlockSpec((PROMPT_SPS * T, D_MODEL), lambda i: (i, 0))
    in_specs = [row, _mod_spec(layer, lambda i: 0)]
    in_specs += [_layer_spec(a, layer) for a in (g, qg, kvg, w_p, logit, wq, wk, wv)]
    in_specs += [_full_spec(csw), _full_spec(cts), _layer_spec(wb, layer), _layer_spec(wo, layer)]
    args = [x, mod, g, qg, kvg, w_p, logit, wq, wk, wv, csw, cts, wb, wo]
    if final_g is not None:
        in_specs.append(_full_spec(final_g))
        args.append(final_g)
    tails = ((SEQ, KV_LORA), (D_ROPE, SEQ), ST_TAIL)
    if layer:
        in_specs += [_grow_spec(layer, t) for t in tails]
        args += list(prev)
    return pl.pallas_call(
        functools.partial(_prompt_kernel, layer, final_g is not None),
        grid=(BATCH // PROMPT_SPS,),
        in_specs=in_specs,
        out_specs=[row] + [_grow_spec(layer + 1, t) for t in tails],
        out_shape=[jax.ShapeDtypeStruct((BATCH * T, D_MODEL), F32)]
        + [jax.ShapeDtypeStruct((BATCH, layer + 1) + t, F32) for t in tails],
        scratch_shapes=[pltpu.VMEM((H_RET, T, T), F32)],
        compiler_params=_cparams(),
        name="prompt_layer",
    )(*args)


def _in_kernel(x_ref, mod_ref, g_ref, qg_ref, kvg_ref, w_ref,
               qkv_ref, sz_ref, sg_ref, qn_ref, ckv_ref, kr_ref, fu_ref):
    pr = _project(x_ref[...], mod_ref, g_ref, qg_ref, kvg_ref, w_ref)
    qw = H_RET * DK_RET
    qkv_ref[:, 0:qw] = pr.q
    qkv_ref[:, qw:2 * qw] = pr.k
    qkv_ref[:, 2 * qw:] = pr.v
    for j in range(3):
        sz_ref[:, j * BRANCH_W:(j + 1) * BRANCH_W] = pr.sz[j].astype(BF16)
    for j in range(N_BRANCH):
        sg_ref[:, j * D_MODEL:(j + 1) * D_MODEL] = pr.sg[j].astype(BF16)
    qn_ref[...] = pr.qn
    ckv_ref[...] = pr.ckv.astype(BF16)
    kr_ref[...] = pr.kr
    fu_ref[...] = pr.fu


def _in_call(layer, x, mod, g, qg, kvg, w_p):
    n = x.shape[0]
    per_b = DEC_SEQ // TM
    row = lambda w: pl.BlockSpec((TM, w), lambda i: (i, 0))
    outs = [(1024, BF16), (3 * BRANCH_W, BF16), (N_BRANCH * D_MODEL, BF16), (Q_LORA, BF16),
            (KV_LORA, BF16), (LANE, F32), (FOURIER_W, BF16)]
    return pl.pallas_call(
        _in_kernel,
        grid=(n // TM,),
        in_specs=[row(D_MODEL), _mod_spec(layer, lambda i: 1 + i // per_b)]
        + [_layer_spec(a, layer) for a in (g, qg, kvg, w_p)],
        out_specs=[row(w) for w, _ in outs],
        out_shape=[jax.ShapeDtypeStruct((n, w), dt) for w, dt in outs],
        compiler_params=_cparams(),
        name="in_proj_s",
    )(x, mod, g, qg, kvg, w_p)


def _ret_kernel(T, qkv_ref, rz_ref, lg_ref, st_ref, br_ref, mask_ref):
    lg = jax.nn.log_sigmoid(lg_ref[...])

    @pl.when(pl.program_id(0) == 0)
    def _():
        _init_masks(T, lg, mask_ref)

    qw = H_RET * DK_RET
    for h in range(H_RET):
        q = qkv_ref[:, h * DK_RET:(h + 1) * DK_RET]
        k = qkv_ref[:, qw + h * DK_RET:qw + (h + 1) * DK_RET]
        v = qkv_ref[:, 2 * qw + h * DV_RET:2 * qw + (h + 1) * DV_RET]
        on = _ret_head(T, q, k, v, mask_ref[h], lg[0:1, h:h + 1], lg[1:2, h:h + 1],
                       st=(st_ref[0, h], st_ref[1, h]))
        gate = rz_ref[:, h * DV_RET:(h + 1) * DV_RET].astype(F32)
        br_ref[:, h * DV_RET:(h + 1) * DV_RET] = (on * gate).astype(BF16)


def _ret_call(layer, qkv, sz, logit, state):
    T = DEC_SEQ
    return pl.pallas_call(
        functools.partial(_ret_kernel, T),
        grid=(DEC_BATCH,),
        in_specs=[pl.BlockSpec((T, 1024), lambda b: (b, 0)),
                  pl.BlockSpec((T, BRANCH_W), lambda b: (b, 0)),
                  _layer_spec(logit, layer),
                  pl.BlockSpec((None, None) + ST_TAIL, lambda b: (b, layer, 0, 0, 0, 0))],
        out_specs=pl.BlockSpec((T, RET_W), lambda b: (b, 0)),
        out_shape=jax.ShapeDtypeStruct((T * DEC_BATCH, RET_W), BF16),
        scratch_shapes=[pltpu.VMEM((H_RET, T, T), F32)],
        compiler_params=_cparams(),
        name="ret_s",
    )(qkv, sz, logit, state)


def _mla_kernel(T, qn_ref, ckv_ref, kr_ref, mz_ref, wq_ref, wk_ref, wv_ref, cckv_ref, ckr_ref,
                cos_ref, sa_ref, sb_ref, bm_ref, q_scr, k_scr, v_scr):
    P = PAST_LEN
    qn = qn_ref[...]
    ckv = ckv_ref[...]
    kr = _rope(kr_ref[...], cos_ref[...], sa_ref[...], sb_ref[...])
    cc = cckv_ref[...].astype(BF16)
    for p in range(H_MLA // 2):
        sl = slice(2 * p * LANE, 2 * (p + 1) * LANE)
        q2 = _dot(qn, wq_ref[:, sl])
        kc2 = _dot(cc, wk_ref[:, sl])
        kn2 = _dot(ckv, wk_ref[:, sl])
        for i in range(2):
            h = 2 * p + i
            hl = slice(i * LANE, (i + 1) * LANE)
            qh = _rope(q2[:, hl], cos_ref[...], sa_ref[...], sb_ref[...])
            q_scr[h] = (qh * Q_SCALE).astype(BF16)
            k_scr[h, 0:P] = (kc2[:, hl] + ckr_ref[...]).astype(BF16)
            k_scr[h, P:P + T] = (kn2[:, hl] + kr).astype(BF16)
    for pp in range(H_MLA // 4):
        sl = slice(2 * pp * LANE, 2 * (pp + 1) * LANE)
        for rows, src in ((slice(0, P), cc), (slice(P, P + T), ckv)):
            v2 = _dot(src, wv_ref[:, sl])
            for i in range(2):
                h = 4 * pp + 2 * i
                v_scr[h, rows], v_scr[h + 1, rows] = _value_slots(v2[:, i * LANE:(i + 1) * LANE])

    def body(qb, carry):
        rs = pl.ds(pl.multiple_of(qb * TQ, TQ), TQ)
        for p in range(H_MLA // 2):
            hs = (2 * p, 2 * p + 1)
            o = _attend_pair([q_scr[h, rs] for h in hs], [k_scr[h] for h in hs],
                             [v_scr[h] for h in hs])
            gate = mz_ref[rs, p * LANE:(p + 1) * LANE].astype(F32)
            bm_ref[rs, p * LANE:(p + 1) * LANE] = (o * gate).astype(BF16)
        return carry

    lax.fori_loop(0, T // TQ, body, 0)


def _mla_call(layer, qn, ckv, kr, sz, wq, wk, wv, cache_ckv, cache_kr, cos, sa, sb):
    T = DEC_SEQ
    Tk = T + PAST_LEN
    row = lambda w, c=0: pl.BlockSpec((T, w), lambda b: (b, c))
    return pl.pallas_call(
        functools.partial(_mla_kernel, T),
        grid=(DEC_BATCH,),
        in_specs=[row(Q_LORA), row(KV_LORA), row(LANE), row(BRANCH_W, 1)]
        + [_layer_spec(a, layer) for a in (wq, wk, wv)]
        + [pl.BlockSpec((None, None, PAST_LEN, KV_LORA), lambda b: (b, layer, 0, 0)),
           pl.BlockSpec((None, None, PAST_LEN, LANE), lambda b: (b, layer, 0, 0)),
           _full_spec(cos), _full_spec(sa), _full_spec(sb)],
        out_specs=pl.BlockSpec((T, MLA_W), lambda b: (b, 0)),
        out_shape=jax.ShapeDtypeStruct((T * DEC_BATCH, MLA_W), BF16),
        scratch_shapes=[pltpu.VMEM((H_MLA, T, LANE), BF16),
                        pltpu.VMEM((H_MLA, Tk, LANE), BF16),
                        pltpu.VMEM((H_MLA, Tk, LANE), BF16)],
        compiler_params=_cparams(),
        name="mla_s",
    )(qn, ckv, kr, sz, wq, wk, wv, cache_ckv, cache_kr, cos, sa, sb)


def _fourier_kernel(T, fu_ref, fz_ref, csw_ref, cts_ref, bf_ref):
    o = _fourier(T, fu_ref[...], csw_ref, cts_ref)
    bf_ref[...] = (o * fz_ref[...].astype(F32)).astype(BF16)


def _fourier_call(fu, sz, csw, cts):
    T = DEC_SEQ
    return pl.pallas_call(
        functools.partial(_fourier_kernel, T),
        grid=(DEC_BATCH,),
        in_specs=[pl.BlockSpec((T, FOURIER_W), lambda b: (b, 0)),
                  pl.BlockSpec((T, BRANCH_W), lambda b: (b, 2)),
                  _full_spec(csw), _full_spec(cts)],
        out_specs=pl.BlockSpec((T, FOURIER_W), lambda b: (b, 0)),
        out_shape=jax.ShapeDtypeStruct((T * DEC_BATCH, FOURIER_W), BF16),
        compiler_params=_cparams(),
        name="fourier_s",
    )(fu, sz, csw, cts)


def _merge_kernel(final, *refs):
    br_ref, bm_ref, bf_ref, sg_ref, x_ref, mod_ref, wb_ref, wo_ref = refs[:8]
    fg_ref = refs[8] if final else None
    sg = [sg_ref[:, n * D_MODEL:(n + 1) * D_MODEL].astype(F32) for n in range(N_BRANCH)]
    refs[-1][...] = _merge((br_ref[...], bm_ref[...], bf_ref[...]), sg, x_ref[...],
                           mod_ref, wb_ref, wo_ref, fg_ref)


def _merge_call(layer, br, bm, bf, sg, x, mod, wb, wo, final_g=None):
    final = final_g is not None
    n = x.shape[0]
    per_b = DEC_SEQ // TM
    row = lambda w: pl.BlockSpec((TM, w), lambda i: (i, 0))
    in_specs = [row(BRANCH_W), row(BRANCH_W), row(BRANCH_W), row(N_BRANCH * D_MODEL), row(D_MODEL),
                _mod_spec(layer, lambda i: 1 + i // per_b),
                _layer_spec(wb, layer), _layer_spec(wo, layer)]
    args = [br, bm, bf, sg, x, mod, wb, wo]
    if final:
        in_specs.append(_full_spec(final_g))
        args.append(final_g)
    return pl.pallas_call(
        functools.partial(_merge_kernel, final),
        grid=(n // TM,),
        in_specs=in_specs,
        out_specs=row(D_MODEL),
        out_shape=jax.ShapeDtypeStruct((n, D_MODEL), F32),
        compiler_params=_cparams(),
        name="merge_s",
    )(*args)


def _dft_tables(T):
    def cs(n):
        kt = (np.arange(n)[:, None] * np.arange(n)[None, :]) % n
        ang = 2.0 * np.pi * kt.astype(np.float64) / n
        return np.cos(ang), np.sin(ang)
    ct, st = cs(T)
    cw, sw = cs(F_GROUP_W)
    cts = np.concatenate([ct, -st], axis=1).astype(np.float32)
    csw = np.concatenate([cw, sw], axis=1).astype(np.float32)
    return jnp.asarray(csw).astype(BF16), jnp.asarray(cts).astype(BF16)


def _rope_tables(T):
    half = D_ROPE // 2
    nfreq = half // 2
    inv = ROPE_BASE ** (-np.arange(nfreq, dtype=np.float64) / nfreq)
    t = np.arange(T)
    pos = np.stack([t // GRID_W, t % GRID_W], axis=0).astype(np.float64)
    cos = np.ones((T, LANE), np.float64)
    sa = np.zeros((T, LANE), np.float64)
    sb = np.zeros((T, LANE), np.float64)
    for part in range(2):
        ang = pos[part][:, None] * inv[None, :]
        l1 = ROPE_LANE0 + part * half
        l2 = l1 + nfreq
        cos[:, l1:l1 + nfreq] = np.cos(ang)
        cos[:, l2:l2 + nfreq] = np.cos(ang)
        sa[:, l1:l1 + nfreq] = -np.sin(ang)
        sb[:, l2:l2 + nfreq] = np.sin(ang)
    return tuple(jnp.asarray(a.astype(np.float32)) for a in (cos, sa, sb))


KR_SRC = 2176
W_IN_RUNS = ((0, 1024),
             (1024, 512), (2208, 512), (3232, 512),
             (3744, 3072),
             (1536, 384),
             (1920, 256),
             (KR_SRC - ROPE_LANE0, LANE),
             (2720, 512))
W_PREP_TILES = 6
KR_TILE = SEG_KR[0] // LANE


def _w_prep_kernel(tbl_ref, *refs):
    o_ref = refs[-1]
    j = pl.program_id(1)
    lane = lax.broadcasted_iota(jnp.int32, (D_MODEL, LANE), 1)
    rotary = (lane >= ROPE_LANE0) & (lane < ROPE_LANE0 + D_ROPE)
    for t, w_ref in enumerate(refs[:-1]):
        wt = w_ref[0].T
        if t == KR_TILE % W_PREP_TILES:
            wt = jnp.where((j != KR_TILE // W_PREP_TILES) | rotary, wt, 0.0)
        o_ref[:, t * LANE:(t + 1) * LANE] = wt.astype(BF16)


def _permute_w_in(w):
    depth, d, in_w = w.shape
    starts = np.concatenate([src + np.arange(0, width, LANE) for src, width in W_IN_RUNS])
    assert starts.size * LANE == IN_WP and starts.size % W_PREP_TILES == 0
    tbl = jnp.asarray(starts, jnp.int32)
    tile_spec = lambda t: pl.BlockSpec(
        (pl.Element(1), pl.Element(LANE), pl.Element(d)),
        lambda l, j, tbl_ref: (l, pl.multiple_of(tbl_ref[j * W_PREP_TILES + t], D_ROPE), 0))
    wide = W_PREP_TILES * LANE
    return pl.pallas_call(
        _w_prep_kernel,
        grid_spec=pltpu.PrefetchScalarGridSpec(
            num_scalar_prefetch=1,
            grid=(depth, IN_WP // wide),
            in_specs=[tile_spec(t) for t in range(W_PREP_TILES)],
            out_specs=pl.BlockSpec((None, d, wide), lambda l, j, tbl_ref: (l, 0, j)),
        ),
        out_shape=jax.ShapeDtypeStruct((depth, d, IN_WP), BF16),
        compiler_params=_cparams(2),
        name="w_prep",
    )(tbl, *([jnp.swapaxes(w, 1, 2)] * W_PREP_TILES))


def _pad_heads(w, n_heads, width, lo, hi):
    d, k = w.shape[:2]
    wh = w.reshape(d, k, n_heads, width)[..., lo:hi]
    wh = jnp.pad(wh, ((0, 0), (0, 0), (0, 0), (0, LANE - (hi - lo))))
    return wh.reshape(d, k, n_heads * LANE).astype(BF16)


def kernel(x_prompt, x_sample, cache_ckv, cache_krope, state_ret, c, c_ctx, norm_g, w_mod, b_mod,
           w_in, ret_decay_logit, q_norm_g, w_q_up, kv_norm_g, w_kv_up, w_branch, w_out,
           final_norm_g):
    cv = jnp.concatenate([c_ctx[None, :], c, jnp.zeros((MOD_ROWS - 1 - DEC_BATCH, D_MODEL), F32)], axis=0)
    mod = _mod_call(cv, w_mod, b_mod).reshape(DEPTH, MOD_ROWS, 1, 3 * D_MODEL)

    cache_kr = jnp.pad(cache_krope, ((0, 0), (0, 0), (0, 0), (ROPE_LANE0, LANE - ROPE_LANE0 - D_ROPE)))
    rope = _rope_tables(DEC_SEQ)
    dft_p = _dft_tables(SEQ)
    dft_s = _dft_tables(DEC_SEQ)
    logit = jnp.pad(ret_decay_logit, ((0, 0), (0, 8 - 2), (0, LANE - H_RET)))
    g, qg, kvg = norm_g[:, None, :], q_norm_g[:, None, :], kv_norm_g[:, None, :]
    w_p = _permute_w_in(w_in)
    wq = _pad_heads(w_q_up, H_MLA, D_NOPE + D_ROPE, 0, D_NOPE + D_ROPE)
    wk = _pad_heads(w_kv_up, H_MLA, D_NOPE + D_VMLA, 0, D_NOPE)
    wv = w_kv_up.reshape(DEPTH, KV_LORA, H_MLA, D_NOPE + D_VMLA)[..., D_NOPE:]
    wv = wv.reshape(DEPTH, KV_LORA, MLA_W).astype(BF16)
    wb = w_branch.astype(BF16)
    wo = w_out.astype(BF16)
    final_g = final_norm_g[None, :]

    xp = x_prompt.reshape(BATCH * SEQ, D_MODEL)
    xs = x_sample.reshape(DEC_BATCH * DEC_SEQ, D_MODEL)
    new_ctx = None
    for l in range(DEPTH):
        fg = final_g if l == DEPTH - 1 else None
        xp, *new_ctx = _prompt_call(l, xp, mod, g, qg, kvg, w_p, logit, wq, wk, wv, *dft_p, wb, wo,
                                    final_g=fg, prev=new_ctx)

        qkv, sz, sg, qn, ckv, kr, fu = _in_call(l, xs, mod, g, qg, kvg, w_p)
        br = _ret_call(l, qkv, sz, logit, state_ret)
        bm = _mla_call(l, qn, ckv, kr, sz, wq, wk, wv, cache_ckv, cache_kr, *rope)
        bf = _fourier_call(fu, sz, *dft_s)
        xs = _merge_call(l, br, bm, bf, sg, xs, mod, wb, wo, final_g=fg)

    y_prompt = xp.reshape(BATCH, SEQ, D_MODEL)
    y_sample = xs.reshape(DEC_BATCH, DEC_SEQ, D_MODEL)
    new_ckv, new_krope_t, new_ret = new_ctx
    return (y_prompt, y_sample, new_ckv, jnp.swapaxes(new_krope_t, 2, 3), new_ret)
```

```python
import collections
import functools
import math

import numpy as np
import jax
import jax.numpy as jnp
from jax import lax
from jax.experimental import pallas as pl
from jax.experimental.pallas import tpu as pltpu

F32 = jnp.float32
BF16 = jnp.bfloat16

D_MODEL = 1024
BATCH = 32
SEQ = 256
DEPTH = 2
DEC_BATCH = 4
DEC_SEQ = 1024
PAST_LEN = 512
GRID_W = 64
EPS = 1e-6
H_RET = 4
DK_RET = 64
DV_RET = 128
RET_W = H_RET * DV_RET
H_MLA = 8
Q_LORA = 384
KV_LORA = 256
D_NOPE = 64
D_ROPE = 32
D_VMLA = 64
MLA_W = H_MLA * D_VMLA
ROPE_BASE = 10000.0
F_GROUPS = 4
F_GROUP_W = 128
FOURIER_W = F_GROUPS * F_GROUP_W
N_BRANCH = 3
BRANCH_W = 512

LANE = 128
TM = 512
PROMPT_SPS = 2
TQ = 1024
MOD_ROWS = 8
ROPE_LANE0 = D_NOPE
VMEM_LIMIT = 56 * 1024 * 1024
ST_TAIL = (2, H_RET, DK_RET, DV_RET)

SEG_QKV = (0, 1024)
SEG_Z = (1024, 2560)
SEG_G = (2560, 5632)
SEG_QL = (5632, 6016)
SEG_KV = (6016, 6272)
SEG_KR = (6272, 6400)
SEG_FU = (6400, 6912)
IN_WP = 6912

NT = (((1,), (1,)), ((), ()))
TN = (((0,), (0,)), ((), ()))


def _cparams(n_axes=1):
    return pltpu.CompilerParams(dimension_semantics=("arbitrary",) * n_axes,
                                vmem_limit_bytes=VMEM_LIMIT)


def _layer_spec(a, layer):
    shape = a.shape[1:]
    return pl.BlockSpec((None,) + shape, lambda i: (layer,) + (0,) * len(shape))


def _full_spec(a):
    return pl.BlockSpec(a.shape, lambda i: (0,) * a.ndim)


def _mod_spec(layer, row):
    return pl.BlockSpec((None, None, 1, 3 * D_MODEL), lambda i: (layer, row(i), 0, 0))


def _grow_spec(depth, tail):
    return pl.BlockSpec((PROMPT_SPS, depth) + tail, lambda i: (i,) + (0,) * (1 + len(tail)))


def _dot(a, b):
    return jnp.dot(a, b, preferred_element_type=F32)


def _rms(x, g):
    return x * lax.rsqrt(jnp.mean(x * x, axis=-1, keepdims=True) + EPS) * g


def _mod_kernel(c_ref, w_ref, b_ref, o_ref):
    cv = c_ref[...]
    s = cv * jax.nn.sigmoid(cv)
    o_ref[0] = jnp.dot(s, w_ref[0], preferred_element_type=F32,
                       precision=lax.Precision.HIGHEST) + b_ref[0]


def _mod_call(cv, w_mod, b_mod):
    nb = 3 * D_MODEL // 1024
    return pl.pallas_call(
        _mod_kernel,
        grid=(DEPTH, nb),
        in_specs=[
            pl.BlockSpec((MOD_ROWS, D_MODEL), lambda l, j: (0, 0)),
            pl.BlockSpec((1, D_MODEL, 1024), lambda l, j: (l, 0, j)),
            pl.BlockSpec((1, 1, 1024), lambda l, j: (l, 0, j)),
        ],
        out_specs=pl.BlockSpec((1, MOD_ROWS, 1024), lambda l, j: (l, 0, j)),
        out_shape=jax.ShapeDtypeStruct((DEPTH, MOD_ROWS, 3 * D_MODEL), F32),
        compiler_params=_cparams(2),
        name="mod",
    )(cv, w_mod, b_mod.reshape(DEPTH, 1, 3 * D_MODEL))


Proj = collections.namedtuple("Proj", "q k v sz sg qn ckv kr fu")


def _project(x, mod_ref, g_ref, qg_ref, kvg_ref, w_ref):
    shift = mod_ref[:, 0:D_MODEL]
    scale = mod_ref[:, D_MODEL:2 * D_MODEL]
    hb = (_rms(x, g_ref[...]) * (1.0 + scale) + shift).astype(BF16)

    def mm(a, b):
        return _dot(hb, w_ref[:, a:b])

    qw = H_RET * DK_RET
    p = mm(*SEG_QKV)
    q = p[:, 0:qw].astype(BF16)
    k = (p[:, qw:2 * qw] * (DK_RET ** -0.5)).astype(BF16)
    v = p[:, 2 * qw:].astype(BF16)
    sz = []
    for j in range(3):
        a = SEG_Z[0] + j * BRANCH_W
        p = mm(a, a + BRANCH_W)
        sz.append(p * jax.nn.sigmoid(p))
    sg = []
    for j in range(N_BRANCH):
        a = SEG_G[0] + j * D_MODEL
        sg.append(jax.nn.sigmoid(mm(a, a + D_MODEL)))
    qn = _rms(mm(*SEG_QL), qg_ref[...]).astype(BF16)
    ckv = _rms(mm(*SEG_KV), kvg_ref[...])
    kr = mm(*SEG_KR)
    fu = mm(*SEG_FU).astype(BF16)
    return Proj(q, k, v, sz, sg, qn, ckv, kr, fu)


RET_BLK = 256


def _init_masks(n_blk, lg, mask_ref):
    B = RET_BLK
    ii = lax.broadcasted_iota(jnp.int32, (B, B), 0)
    jj = lax.broadcasted_iota(jnp.int32, (B, B), 1)
    d = (ii - jj).astype(F32)
    ad = jnp.abs(d)
    for h in range(H_RET):
        lf = lg[0:1, h:h + 1]
        lb = lg[1:2, h:h + 1]
        mask_ref[h, 0] = jnp.where(d > 0, jnp.exp(ad * lf), jnp.where(d < 0, jnp.exp(ad * lb), 2.0))
        if n_blk > 1:
            mask_ref[h, 1] = jnp.exp((B + d) * lf)
            mask_ref[h, 2] = jnp.exp((B - d) * lb)


def _masked_scores(s, mask, lf, lb):
    B = RET_BLK
    n_blk = s.shape[0] // B
    if n_blk == 1:
        return (s * mask[0]).astype(BF16)
    rows = []
    for i in range(n_blk):
        cols = []
        for j in range(n_blk):
            blk = s[i * B:(i + 1) * B, j * B:(j + 1) * B]
            far = abs(i - j) - 1
            if far < 0:
                m = mask[0]
            else:
                m = mask[1] if j < i else mask[2]
                if far:
                    m = m * jnp.exp((far * B) * (lf if j < i else lb))
            cols.append((blk * m).astype(BF16))
        rows.append(jnp.concatenate(cols, axis=1))
    return jnp.concatenate(rows, axis=0)


def _ret_head(T, q, k, v, mask, lf, lb, st=None):
    s = lax.dot_general(q, k, NT, preferred_element_type=F32)
    o = _dot(_masked_scores(s, mask, lf, lb), v)
    if st is not None:
        rows = lax.broadcasted_iota(jnp.int32, (T, DV_RET), 0).astype(F32)
        o = o + jnp.exp((rows + 1.0) * lf) * _dot(q, st[0].astype(BF16))
        o = o + jnp.exp((T - rows) * lb) * _dot(q, st[1].astype(BF16))
    mu = jnp.mean(o, axis=-1, keepdims=True)
    c = o - mu
    var = jnp.mean(c * c, axis=-1, keepdims=True)
    return c * lax.rsqrt(var + EPS)


def _ret_states(T, k, v, lf, lb):
    rows = lax.broadcasted_iota(jnp.int32, (T, DK_RET), 0).astype(F32)
    kf = (k.astype(F32) * jnp.exp((T - 1.0 - rows) * lf)).astype(BF16)
    kb = (k.astype(F32) * jnp.exp(rows * lb)).astype(BF16)
    return (lax.dot_general(kf, v, TN, preferred_element_type=F32),
            lax.dot_general(kb, v, TN, preferred_element_type=F32))


def _rope(x, cos, sa, sb):
    half = D_ROPE // 4
    return x * cos + pltpu.roll(x, LANE - half, 1) * sa + pltpu.roll(x, half, 1) * sb


Q_SCALE = (D_NOPE + D_ROPE) ** -0.5 * math.log2(math.e)


def _value_slots(vp):
    first = lax.broadcasted_iota(jnp.int32, vp.shape, 1) < D_VMLA
    return (jnp.where(first, vp, 1.0).astype(BF16), jnp.where(first, 1.0, vp).astype(BF16))


def _attend_pair(q2, k2, v2):
    outs = []
    for qh, kh, vh in zip(q2, k2, v2):
        s = lax.dot_general(qh, kh, NT, preferred_element_type=F32)
        e = jnp.exp2((s - jnp.max(s, axis=-1, keepdims=True)).astype(BF16))
        outs.append(_dot(e, vh))
    first = lax.broadcasted_iota(jnp.int32, outs[0].shape, 1) < D_VMLA
    o = jnp.where(first, outs[0], outs[1])
    l = jnp.where(first, pltpu.roll(outs[0], D_VMLA, 1), pltpu.roll(outs[1], D_VMLA, 1))
    return o / l


def _fourier(T, fu, csw_ref, cts_ref):
    a, b = [], []
    for g in range(F_GROUPS):
        z = _dot(fu[:, g * F_GROUP_W:(g + 1) * F_GROUP_W], csw_ref[...])
        a.append(z[:, :F_GROUP_W])
        b.append(z[:, F_GROUP_W:])
    a = jnp.concatenate(a, axis=1).astype(BF16)
    b = jnp.concatenate(b, axis=1).astype(BF16)
    o = _dot(cts_ref[:, 0:T], a) + _dot(cts_ref[:, T:2 * T], b)
    return o * ((T * F_GROUP_W) ** -0.5)


def _merge(branches, sg, x, mod_ref, wb_ref, wo_ref, fg_ref):
    merged = None
    for n, b in enumerate(branches):
        term = _dot(b, wb_ref[n]) * sg[n]
        merged = term if merged is None else merged + term
    gate = mod_ref[:, 2 * D_MODEL:3 * D_MODEL]
    y = x + gate * _dot(merged.astype(BF16), wo_ref[...])
    if fg_ref is not None:
        y = _rms(y, fg_ref[...])
    return y


def _prompt_kernel(layer, final, *refs):
    (x_ref, mod_ref, g_ref, qg_ref, kvg_ref, w_ref, lg_ref, wq_ref, wk_ref, wv_ref,
     csw_ref, cts_ref, wb_ref, wo_ref) = refs[:14]
    n_in = 14
    fg_ref = None
    if final:
        fg_ref = refs[n_in]
        n_in += 1
    if layer:
        pckv_ref, pkr_ref, pst_ref = refs[n_in:n_in + 3]
        n_in += 3
    y_ref, nckv_ref, nkr_ref, nst_ref, mask_ref = refs[n_in:]
    T = SEQ
    lg = jax.nn.log_sigmoid(lg_ref[...])

    @pl.when(pl.program_id(0) == 0)
    def _():
        _init_masks(T // RET_BLK, lg, mask_ref)

    seqs = range(PROMPT_SPS)
    rows = [slice(s * T, (s + 1) * T) for s in seqs]
    slot = lambda a, i: a[:, i * LANE:(i + 1) * LANE]
    xs = [x_ref[r] for r in rows]
    prs = [_project(x, mod_ref, g_ref, qg_ref, kvg_ref, w_ref) for x in xs]

    for s, pr in zip(seqs, prs):
        if layer:
            nckv_ref[s, 0:layer] = pckv_ref[s]
            nkr_ref[s, 0:layer] = pkr_ref[s]
            nst_ref[s, 0:layer] = pst_ref[s]
        nckv_ref[s, layer] = pr.ckv
        nkr_ref[s, layer] = pr.kr.T[ROPE_LANE0:ROPE_LANE0 + D_ROPE]

    o_r = [[] for _ in seqs]
    for h in range(H_RET):
        lf = lg[0:1, h:h + 1]
        lb = lg[1:2, h:h + 1]
        for s, pr in zip(seqs, prs):
            q = pr.q[:, h * DK_RET:(h + 1) * DK_RET]
            k = pr.k[:, h * DK_RET:(h + 1) * DK_RET]
            v = pr.v[:, h * DV_RET:(h + 1) * DV_RET]
            o_r[s].append(_ret_head(T, q, k, v, mask_ref.at[h], lf, lb))
            sf, sb = _ret_states(T, k, v, lf, lb)
            nst_ref[s, layer, 0, h] = sf
            nst_ref[s, layer, 1, h] = sb

    qs, ks, vs = [], [], []
    for pr in prs:
        ckvb = pr.ckv.astype(BF16)
        q_all = _dot(pr.qn, wq_ref[...]) * Q_SCALE
        k_all = _dot(ckvb, wk_ref[...])
        vs.append(_dot(ckvb, wv_ref[...]))
        qs.append([slot(q_all, h).astype(BF16) for h in range(H_MLA)])
        ks.append([(slot(k_all, h) + pr.kr).astype(BF16) for h in range(H_MLA)])
    o_m = [[] for _ in seqs]
    for p in range(H_MLA // 2):
        for s in seqs:
            o_m[s].append(_attend_pair(qs[s][2 * p:2 * p + 2], ks[s][2 * p:2 * p + 2],
                                       _value_slots(slot(vs[s], p))))

    o_f = [_fourier(T, pr.fu, csw_ref, cts_ref) for pr in prs]

    for s, pr in zip(seqs, prs):
        branches = [(o * gate).astype(BF16) for o, gate in zip(
            (jnp.concatenate(o_r[s], axis=1), jnp.concatenate(o_m[s], axis=1), o_f[s]), pr.sz)]
        y_ref[rows[s]] = _merge(branches, pr.sg, xs[s], mod_ref, wb_ref, wo_ref, fg_ref)


def _prompt_call(layer, x, mod, g, qg, kvg, w_p, logit, wq, wk, wv, csw, cts, wb, wo,
                 final_g=None, prev=None):
    T = SEQ
    row = pl.BlockSpec((PROMPT_SPS * T, D_MODEL), lambda i: (i, 0))
    in_specs = [row, _mod_spec(layer, lambda i: 0)]
    in_specs += [_layer_spec(a, layer) for a in (g, qg, kvg, w_p, logit, wq, wk, wv)]
    in_specs += [_full_spec(csw), _full_spec(cts), _layer_spec(wb, layer), _layer_spec(wo, layer)]
    args = [x, mod, g, qg, kvg, w_p, logit, wq, wk, wv, csw, cts, wb, wo]
    if final_g is not None:
        in_specs.append(_full_spec(final_g))
        args.append(final_g)
    tails = ((SEQ, KV_LORA), (D_ROPE, SEQ), ST_TAIL)
    if layer:
        in_specs += [_grow_spec(layer, t) for t in tails]
        args += list(prev)
    return pl.pallas_call(
        functools.partial(_prompt_kernel, layer, final_g is not None),
        grid=(BATCH // PROMPT_SPS,),
        in_specs=in_specs,
        out_specs=[row] + [_grow_spec(layer + 1, t) for t in tails],
        out_shape=[jax.ShapeDtypeStruct((BATCH * T, D_MODEL), F32)]
        + [jax.ShapeDtypeStruct((BATCH, layer + 1) + t, F32) for t in tails],
        scratch_shapes=[pltpu.VMEM((H_RET, 1, RET_BLK, RET_BLK), F32)],
        compiler_params=_cparams(),
        name="prompt_layer",
    )(*args)


def _in_kernel(x_ref, mod_ref, g_ref, qg_ref, kvg_ref, w_ref,
               qkv_ref, sz_ref, sg_ref, qn_ref, ckv_ref, kr_ref, fu_ref):
    pr = _project(x_ref[...], mod_ref, g_ref, qg_ref, kvg_ref, w_ref)
    qw = H_RET * DK_RET
    qkv_ref[:, 0:qw] = pr.q
    qkv_ref[:, qw:2 * qw] = pr.k
    qkv_ref[:, 2 * qw:] = pr.v
    for j in range(3):
        sz_ref[:, j * BRANCH_W:(j + 1) * BRANCH_W] = pr.sz[j].astype(BF16)
    for j in range(N_BRANCH):
        sg_ref[:, j * D_MODEL:(j + 1) * D_MODEL] = pr.sg[j].astype(BF16)
    qn_ref[...] = pr.qn
    ckv_ref[...] = pr.ckv.astype(BF16)
    kr_ref[...] = pr.kr
    fu_ref[...] = pr.fu


def _in_call(layer, x, mod, g, qg, kvg, w_p):
    n = x.shape[0]
    per_b = DEC_SEQ // TM
    row = lambda w: pl.BlockSpec((TM, w), lambda i: (i, 0))
    outs = [(1024, BF16), (3 * BRANCH_W, BF16), (N_BRANCH * D_MODEL, BF16), (Q_LORA, BF16),
            (KV_LORA, BF16), (LANE, F32), (FOURIER_W, BF16)]
    return pl.pallas_call(
        _in_kernel,
        grid=(n // TM,),
        in_specs=[row(D_MODEL), _mod_spec(layer, lambda i: 1 + i // per_b)]
        + [_layer_spec(a, layer) for a in (g, qg, kvg, w_p)],
        out_specs=[row(w) for w, _ in outs],
        out_shape=[jax.ShapeDtypeStruct((n, w), dt) for w, dt in outs],
        compiler_params=_cparams(),
        name="in_proj_s",
    )(x, mod, g, qg, kvg, w_p)


def _ret_fourier_kernel(T, qkv_ref, rz_ref, fz_ref, fu_ref, lg_ref, st_ref, csw_ref, cts_ref,
                        o_ref, mask_ref):
    lg = jax.nn.log_sigmoid(lg_ref[...])

    @pl.when(pl.program_id(0) == 0)
    def _():
        _init_masks(T // RET_BLK, lg, mask_ref)

    qw = H_RET * DK_RET
    for h in range(H_RET):
        q = qkv_ref[:, h * DK_RET:(h + 1) * DK_RET]
        k = qkv_ref[:, qw + h * DK_RET:qw + (h + 1) * DK_RET]
        v = qkv_ref[:, 2 * qw + h * DV_RET:2 * qw + (h + 1) * DV_RET]
        on = _ret_head(T, q, k, v, mask_ref.at[h], lg[0:1, h:h + 1], lg[1:2, h:h + 1],
                       st=(st_ref[0, h], st_ref[1, h]))
        gate = rz_ref[:, h * DV_RET:(h + 1) * DV_RET].astype(F32)
        o_ref[:, h * DV_RET:(h + 1) * DV_RET] = (on * gate).astype(BF16)
    o = _fourier(T, fu_ref[...], csw_ref, cts_ref)
    o_ref[:, RET_W:] = (o * fz_ref[...].astype(F32)).astype(BF16)


def _ret_fourier_call(layer, qkv, sz, fu, logit, state, csw, cts):
    T = DEC_SEQ
    return pl.pallas_call(
        functools.partial(_ret_fourier_kernel, T),
        grid=(DEC_BATCH,),
        in_specs=[pl.BlockSpec((T, 1024), lambda b: (b, 0)),
                  pl.BlockSpec((T, BRANCH_W), lambda b: (b, 0)),
                  pl.BlockSpec((T, BRANCH_W), lambda b: (b, 2)),
                  pl.BlockSpec((T, FOURIER_W), lambda b: (b, 0)),
                  _layer_spec(logit, layer),
                  pl.BlockSpec((None, None) + ST_TAIL, lambda b: (b, layer, 0, 0, 0, 0)),
                  _full_spec(csw), _full_spec(cts)],
        out_specs=pl.BlockSpec((T, RET_W + FOURIER_W), lambda b: (b, 0)),
        out_shape=jax.ShapeDtypeStruct((T * DEC_BATCH, RET_W + FOURIER_W), BF16),
        scratch_shapes=[pltpu.VMEM((H_RET, 3, RET_BLK, RET_BLK), F32)],
        compiler_params=_cparams(),
        name="ret_fourier_s",
    )(qkv, sz, sz, fu, logit, state, csw, cts)


def _mla_kernel(T, qn_ref, ckv_ref, kr_ref, mz_ref, wq_ref, wk_ref, wv_ref, cckv_ref, ckr_ref,
                cos_ref, sa_ref, sb_ref, bm_ref, q_scr, k_scr, v_scr):
    P = PAST_LEN
    qn = qn_ref[...]
    ckv = ckv_ref[...]
    kr = _rope(kr_ref[...], cos_ref[...], sa_ref[...], sb_ref[...])
    cc = cckv_ref[...].astype(BF16)
    for p in range(H_MLA // 2):
        sl = slice(2 * p * LANE, 2 * (p + 1) * LANE)
        q2 = _dot(qn, wq_ref[:, sl])
        kc2 = _dot(cc, wk_ref[:, sl])
        kn2 = _dot(ckv, wk_ref[:, sl])
        for i in range(2):
            h = 2 * p + i
            hl = slice(i * LANE, (i + 1) * LANE)
            qh = _rope(q2[:, hl], cos_ref[...], sa_ref[...], sb_ref[...])
            q_scr[h] = (qh * Q_SCALE).astype(BF16)
            k_scr[h, 0:P] = (kc2[:, hl] + ckr_ref[...]).astype(BF16)
            k_scr[h, P:P + T] = (kn2[:, hl] + kr).astype(BF16)
    for pp in range(H_MLA // 4):
        sl = slice(2 * pp * LANE, 2 * (pp + 1) * LANE)
        for rows, src in ((slice(0, P), cc), (slice(P, P + T), ckv)):
            v2 = _dot(src, wv_ref[:, sl])
            for i in range(2):
                h = 4 * pp + 2 * i
                v_scr[h, rows], v_scr[h + 1, rows] = _value_slots(v2[:, i * LANE:(i + 1) * LANE])

    def body(qb, carry):
        rs = pl.ds(pl.multiple_of(qb * TQ, TQ), TQ)
        for p in range(H_MLA // 2):
            hs = (2 * p, 2 * p + 1)
            o = _attend_pair([q_scr[h, rs] for h in hs], [k_scr[h] for h in hs],
                             [v_scr[h] for h in hs])
            gate = mz_ref[rs, p * LANE:(p + 1) * LANE].astype(F32)
            bm_ref[rs, p * LANE:(p + 1) * LANE] = (o * gate).astype(BF16)
        return carry

    lax.fori_loop(0, T // TQ, body, 0)


def _mla_call(layer, qn, ckv, kr, sz, wq, wk, wv, cache_ckv, cache_kr, cos, sa, sb):
    T = DEC_SEQ
    Tk = T + PAST_LEN
    row = lambda w, c=0: pl.BlockSpec((T, w), lambda b: (b, c))
    return pl.pallas_call(
        functools.partial(_mla_kernel, T),
        grid=(DEC_BATCH,),
        in_specs=[row(Q_LORA), row(KV_LORA), row(LANE), row(BRANCH_W, 1)]
        + [_layer_spec(a, layer) for a in (wq, wk, wv)]
        + [pl.BlockSpec((None, None, PAST_LEN, KV_LORA), lambda b: (b, layer, 0, 0)),
           pl.BlockSpec((None, None, PAST_LEN, LANE), lambda b: (b, layer, 0, 0)),
           _full_spec(cos), _full_spec(sa), _full_spec(sb)],
        out_specs=pl.BlockSpec((T, MLA_W), lambda b: (b, 0)),
        out_shape=jax.ShapeDtypeStruct((T * DEC_BATCH, MLA_W), BF16),
        scratch_shapes=[pltpu.VMEM((H_MLA, T, LANE), BF16),
                        pltpu.VMEM((H_MLA, Tk, LANE), BF16),
                        pltpu.VMEM((H_MLA, Tk, LANE), BF16)],
        compiler_params=_cparams(),
        name="mla_s",
    )(qn, ckv, kr, sz, wq, wk, wv, cache_ckv, cache_kr, cos, sa, sb)


def _merge_kernel(final, *refs):
    br_ref, bm_ref, bf_ref, sg_ref, x_ref, mod_ref, wb_ref, wo_ref = refs[:8]
    fg_ref = refs[8] if final else None
    sg = [sg_ref[:, n * D_MODEL:(n + 1) * D_MODEL].astype(F32) for n in range(N_BRANCH)]
    refs[-1][...] = _merge((br_ref[...], bm_ref[...], bf_ref[...]), sg, x_ref[...],
                           mod_ref, wb_ref, wo_ref, fg_ref)


def _merge_call(layer, brf, bm, sg, x, mod, wb, wo, final_g=None):
    final = final_g is not None
    n = x.shape[0]
    per_b = DEC_SEQ // TM
    row = lambda w, c=0: pl.BlockSpec((TM, w), lambda i: (i, c))
    in_specs = [row(BRANCH_W), row(BRANCH_W), row(BRANCH_W, 1), row(N_BRANCH * D_MODEL), row(D_MODEL),
                _mod_spec(layer, lambda i: 1 + i // per_b),
                _layer_spec(wb, layer), _layer_spec(wo, layer)]
    args = [brf, bm, brf, sg, x, mod, wb, wo]
    if final:
        in_specs.append(_full_spec(final_g))
        args.append(final_g)
    return pl.pallas_call(
        functools.partial(_merge_kernel, final),
        grid=(n // TM,),
        in_specs=in_specs,
        out_specs=row(D_MODEL),
        out_shape=jax.ShapeDtypeStruct((n, D_MODEL), F32),
        compiler_params=_cparams(),
        name="merge_s",
    )(*args)


def _dft_tables(T):
    def cs(n):
        kt = (np.arange(n)[:, None] * np.arange(n)[None, :]) % n
        ang = 2.0 * np.pi * kt.astype(np.float64) / n
        return np.cos(ang), np.sin(ang)
    ct, st = cs(T)
    cw, sw = cs(F_GROUP_W)
    cts = np.concatenate([ct, -st], axis=1).astype(np.float32)
    csw = np.concatenate([cw, sw], axis=1).astype(np.float32)
    return jnp.asarray(csw).astype(BF16), jnp.asarray(cts).astype(BF16)


def _rope_tables(T):
    half = D_ROPE // 2
    nfreq = half // 2
    inv = ROPE_BASE ** (-np.arange(nfreq, dtype=np.float64) / nfreq)
    t = np.arange(T)
    pos = np.stack([t // GRID_W, t % GRID_W], axis=0).astype(np.float64)
    cos = np.ones((T, LANE), np.float64)
    sa = np.zeros((T, LANE), np.float64)
    sb = np.zeros((T, LANE), np.float64)
    for part in range(2):
        ang = pos[part][:, None] * inv[None, :]
        l1 = ROPE_LANE0 + part * half
        l2 = l1 + nfreq
        cos[:, l1:l1 + nfreq] = np.cos(ang)
        cos[:, l2:l2 + nfreq] = np.cos(ang)
        sa[:, l1:l1 + nfreq] = -np.sin(ang)
        sb[:, l2:l2 + nfreq] = np.sin(ang)
    return tuple(jnp.asarray(a.astype(np.float32)) for a in (cos, sa, sb))


KR_SRC = 2176
W_IN_RUNS = ((0, 1024),
             (1024, 512), (2208, 512), (3232, 512),
             (3744, 3072),
             (1536, 384),
             (1920, 256),
             (KR_SRC - ROPE_LANE0, LANE),
             (2720, 512))
W_PREP_TILES = 6
KR_TILE = SEG_KR[0] // LANE


def _w_prep_kernel(tbl_ref, *refs):
    o_ref = refs[-1]
    j = pl.program_id(1)
    lane = lax.broadcasted_iota(jnp.int32, (D_MODEL, LANE), 1)
    rotary = (lane >= ROPE_LANE0) & (lane < ROPE_LANE0 + D_ROPE)
    for t, w_ref in enumerate(refs[:-1]):
        wt = w_ref[0].T
        if t == KR_TILE % W_PREP_TILES:
            wt = jnp.where((j != KR_TILE // W_PREP_TILES) | rotary, wt, 0.0)
        o_ref[:, t * LANE:(t + 1) * LANE] = wt.astype(BF16)


def _permute_w_in(w):
    depth, d, in_w = w.shape
    starts = np.concatenate([src + np.arange(0, width, LANE) for src, width in W_IN_RUNS])
    assert starts.size * LANE == IN_WP and starts.size % W_PREP_TILES == 0
    tbl = jnp.asarray(starts, jnp.int32)
    tile_spec = lambda t: pl.BlockSpec(
        (pl.Element(1), pl.Element(LANE), pl.Element(d)),
        lambda l, j, tbl_ref: (l, pl.multiple_of(tbl_ref[j * W_PREP_TILES + t], D_ROPE), 0))
    wide = W_PREP_TILES * LANE
    return pl.pallas_call(
        _w_prep_kernel,
        grid_spec=pltpu.PrefetchScalarGridSpec(
            num_scalar_prefetch=1,
            grid=(depth, IN_WP // wide),
            in_specs=[tile_spec(t) for t in range(W_PREP_TILES)],
            out_specs=pl.BlockSpec((None, d, wide), lambda l, j, tbl_ref: (l, 0, j)),
        ),
        out_shape=jax.ShapeDtypeStruct((depth, d, IN_WP), BF16),
        compiler_params=_cparams(2),
        name="w_prep",
    )(tbl, *([jnp.swapaxes(w, 1, 2)] * W_PREP_TILES))


def _pad_heads(w, n_heads, width, lo, hi):
    d, k = w.shape[:2]
    wh = w.reshape(d, k, n_heads, width)[..., lo:hi]
    wh = jnp.pad(wh, ((0, 0), (0, 0), (0, 0), (0, LANE - (hi - lo))))
    return wh.reshape(d, k, n_heads * LANE).astype(BF16)


def kernel(x_prompt, x_sample, cache_ckv, cache_krope, state_ret, c, c_ctx, norm_g, w_mod, b_mod,
           w_in, ret_decay_logit, q_norm_g, w_q_up, kv_norm_g, w_kv_up, w_branch, w_out,
           final_norm_g):
    cv = jnp.concatenate([c_ctx[None, :], c, jnp.zeros((MOD_ROWS - 1 - DEC_BATCH, D_MODEL), F32)], axis=0)
    mod = _mod_call(cv, w_mod, b_mod).reshape(DEPTH, MOD_ROWS, 1, 3 * D_MODEL)

    cache_kr = jnp.pad(cache_krope, ((0, 0), (0, 0), (0, 0), (ROPE_LANE0, LANE - ROPE_LANE0 - D_ROPE)))
    rope = _rope_tables(DEC_SEQ)
    dft_p = _dft_tables(SEQ)
    dft_s = _dft_tables(DEC_SEQ)
    logit = jnp.pad(ret_decay_logit, ((0, 0), (0, 8 - 2), (0, LANE - H_RET)))
    g, qg, kvg = norm_g[:, None, :], q_norm_g[:, None, :], kv_norm_g[:, None, :]
    w_p = _permute_w_in(w_in)
    wq = _pad_heads(w_q_up, H_MLA, D_NOPE + D_ROPE, 0, D_NOPE + D_ROPE)
    wk = _pad_heads(w_kv_up, H_MLA, D_NOPE + D_VMLA, 0, D_NOPE)
    wv = w_kv_up.reshape(DEPTH, KV_LORA, H_MLA, D_NOPE + D_VMLA)[..., D_NOPE:]
    wv = wv.reshape(DEPTH, KV_LORA, MLA_W).astype(BF16)
    wb = w_branch.astype(BF16)
    wo = w_out.astype(BF16)
    final_g = final_norm_g[None, :]

    xp = x_prompt.reshape(BATCH * SEQ, D_MODEL)
    xs = x_sample.reshape(DEC_BATCH * DEC_SEQ, D_MODEL)
    new_ctx = None
    for l in range(DEPTH):
        fg = final_g if l == DEPTH - 1 else None
        xp, *new_ctx = _prompt_call(l, xp, mod, g, qg, kvg, w_p, logit, wq, wk, wv, *dft_p, wb, wo,
                                    final_g=fg, prev=new_ctx)

        qkv, sz, sg, qn, ckv, kr, fu = _in_call(l, xs, mod, g, qg, kvg, w_p)
        brf = _ret_fourier_call(l, qkv, sz, fu, logit, state_ret, *dft_s)
        bm = _mla_call(l, qn, ckv, kr, sz, wq, wk, wv, cache_ckv, cache_kr, *rope)
        xs = _merge_call(l, brf, bm, sg, xs, mod, wb, wo, final_g=fg)

    y_prompt = xp.reshape(BATCH, SEQ, D_MODEL)
    y_sample = xs.reshape(DEC_BATCH, DEC_SEQ, D_MODEL)
    new_ckv, new_krope_t, new_ret = new_ctx
    return (y_prompt, y_sample, new_ckv, jnp.swapaxes(new_krope_t, 2, 3), new_ret)
```

```python
import collections
import functools
import math

import numpy as np
import jax
import jax.numpy as jnp
from jax import lax
from jax.experimental import pallas as pl
from jax.experimental.pallas import tpu as pltpu

F32 = jnp.float32
BF16 = jnp.bfloat16

D_MODEL = 1024
BATCH = 32
SEQ = 256
DEPTH = 2
DEC_BATCH = 4
DEC_SEQ = 1024
PAST_LEN = 512
GRID_W = 64
EPS = 1e-6
H_RET = 4
DK_RET = 64
DV_RET = 128
RET_W = H_RET * DV_RET
H_MLA = 8
Q_LORA = 384
KV_LORA = 256
D_NOPE = 64
D_ROPE = 32
D_VMLA = 64
MLA_W = H_MLA * D_VMLA
ROPE_BASE = 10000.0
F_GROUPS = 4
F_GROUP_W = 128
FOURIER_W = F_GROUPS * F_GROUP_W
N_BRANCH = 3
BRANCH_W = 512

LANE = 128
TM = 512
IN_CHAINS = 2
PROMPT_SPS = 2
TQ = 1024
MOD_ROWS = 8
ROPE_LANE0 = D_NOPE
VMEM_LIMIT = 56 * 1024 * 1024
ST_TAIL = (2, H_RET, DK_RET, DV_RET)

SEG_QKV = (0, 1024)
SEG_Z = (1024, 2560)
SEG_G = (2560, 5632)
SEG_QL = (5632, 6016)
SEG_KV = (6016, 6272)
SEG_KR = (6272, 6400)
SEG_FU = (6400, 6912)
IN_WP = 6912

NT = (((1,), (1,)), ((), ()))
TN = (((0,), (0,)), ((), ()))


def _cparams(n_axes=1):
    return pltpu.CompilerParams(dimension_semantics=("arbitrary",) * n_axes,
                                vmem_limit_bytes=VMEM_LIMIT)


def _layer_spec(a, layer):
    shape = a.shape[1:]
    return pl.BlockSpec((None,) + shape, lambda i: (layer,) + (0,) * len(shape))


def _full_spec(a):
    return pl.BlockSpec(a.shape, lambda i: (0,) * a.ndim)


def _mod_spec(layer, row):
    return pl.BlockSpec((None, None, 1, 3 * D_MODEL), lambda i: (layer, row(i), 0, 0))


def _grow_spec(depth, tail):
    return pl.BlockSpec((PROMPT_SPS, depth) + tail, lambda i: (i,) + (0,) * (1 + len(tail)))


def _dot(a, b):
    return jnp.dot(a, b, preferred_element_type=F32)


def _rms(x, g):
    return x * lax.rsqrt(jnp.mean(x * x, axis=-1, keepdims=True) + EPS) * g


def _mod_kernel(c_ref, w_ref, b_ref, o_ref):
    cv = c_ref[...]
    s = cv * jax.nn.sigmoid(cv)
    o_ref[0] = _dot(s.astype(BF16), w_ref[0].astype(BF16)) + b_ref[0]


def _mod_call(cv, w_mod, b_mod):
    nb = 3 * D_MODEL // 1024
    return pl.pallas_call(
        _mod_kernel,
        grid=(DEPTH, nb),
        in_specs=[
            pl.BlockSpec((MOD_ROWS, D_MODEL), lambda l, j: (0, 0)),
            pl.BlockSpec((1, D_MODEL, 1024), lambda l, j: (l, 0, j)),
            pl.BlockSpec((1, 1, 1024), lambda l, j: (l, 0, j)),
        ],
        out_specs=pl.BlockSpec((1, MOD_ROWS, 1024), lambda l, j: (l, 0, j)),
        out_shape=jax.ShapeDtypeStruct((DEPTH, MOD_ROWS, 3 * D_MODEL), F32),
        compiler_params=_cparams(2),
        name="mod",
    )(cv, w_mod, b_mod.reshape(DEPTH, 1, 3 * D_MODEL))


Proj = collections.namedtuple("Proj", "q k v sz sg qn ckv kr fu")


def _project(x, mod_ref, g_ref, qg_ref, kvg_ref, w_ref):
    shift = mod_ref[:, 0:D_MODEL]
    scale = mod_ref[:, D_MODEL:2 * D_MODEL]
    hb = (_rms(x, g_ref[...]) * (1.0 + scale) + shift).astype(BF16)

    def mm(a, b):
        return _dot(hb, w_ref[:, a:b])

    qw = H_RET * DK_RET
    p = mm(*SEG_QKV)
    q = p[:, 0:qw].astype(BF16)
    k = (p[:, qw:2 * qw] * (DK_RET ** -0.5)).astype(BF16)
    v = p[:, 2 * qw:].astype(BF16)
    sz = []
    for j in range(3):
        a = SEG_Z[0] + j * BRANCH_W
        p = mm(a, a + BRANCH_W)
        sz.append(p * jax.nn.sigmoid(p))
    sg = []
    for j in range(N_BRANCH):
        a = SEG_G[0] + j * D_MODEL
        sg.append(jax.nn.sigmoid(mm(a, a + D_MODEL)))
    qn = _rms(mm(*SEG_QL), qg_ref[...]).astype(BF16)
    ckv = _rms(mm(*SEG_KV), kvg_ref[...])
    kr = mm(*SEG_KR)
    fu = mm(*SEG_FU).astype(BF16)
    return Proj(q, k, v, sz, sg, qn, ckv, kr, fu)


RET_BLK = 256


def _init_masks(n_blk, lg, mask_ref):
    B = RET_BLK
    ii = lax.broadcasted_iota(jnp.int32, (B, B), 0)
    jj = lax.broadcasted_iota(jnp.int32, (B, B), 1)
    d = (ii - jj).astype(F32)
    ad = jnp.abs(d)
    for h in range(H_RET):
        lf = lg[0:1, h:h + 1]
        lb = lg[1:2, h:h + 1]
        mask_ref[h, 0] = jnp.where(d > 0, jnp.exp(ad * lf), jnp.where(d < 0, jnp.exp(ad * lb), 2.0))
        if n_blk > 1:
            mask_ref[h, 1] = jnp.exp((B + d) * lf)
            mask_ref[h, 2] = jnp.exp((B - d) * lb)


def _masked_scores(s, mask, lf, lb):
    B = RET_BLK
    n_blk = s.shape[0] // B
    if n_blk == 1:
        return (s * mask[0]).astype(BF16)
    rows = []
    for i in range(n_blk):
        cols = []
        for j in range(n_blk):
            blk = s[i * B:(i + 1) * B, j * B:(j + 1) * B]
            far = abs(i - j) - 1
            if far < 0:
                m = mask[0]
            else:
                m = mask[1] if j < i else mask[2]
                if far:
                    m = m * jnp.exp((far * B) * (lf if j < i else lb))
            cols.append((blk * m).astype(BF16))
        rows.append(jnp.concatenate(cols, axis=1))
    return jnp.concatenate(rows, axis=0)


def _ret_head(T, q, k, v, mask, lf, lb, st=None):
    s = lax.dot_general(q, k, NT, preferred_element_type=F32)
    o = _dot(_masked_scores(s, mask, lf, lb), v)
    if st is not None:
        rows = lax.broadcasted_iota(jnp.int32, (T, DV_RET), 0).astype(F32)
        o = o + jnp.exp((rows + 1.0) * lf) * _dot(q, st[0].astype(BF16))
        o = o + jnp.exp((T - rows) * lb) * _dot(q, st[1].astype(BF16))
    mu = jnp.mean(o, axis=-1, keepdims=True)
    c = o - mu
    var = jnp.mean(c * c, axis=-1, keepdims=True)
    return c * lax.rsqrt(var + EPS)


def _ret_states(T, k, v, lf, lb):
    rows = lax.broadcasted_iota(jnp.int32, (T, DK_RET), 0).astype(F32)
    kf = (k.astype(F32) * jnp.exp((T - 1.0 - rows) * lf)).astype(BF16)
    kb = (k.astype(F32) * jnp.exp(rows * lb)).astype(BF16)
    return (lax.dot_general(kf, v, TN, preferred_element_type=F32),
            lax.dot_general(kb, v, TN, preferred_element_type=F32))


def _rope(x, cos, sa, sb):
    half = D_ROPE // 4
    return x * cos + pltpu.roll(x, LANE - half, 1) * sa + pltpu.roll(x, half, 1) * sb


Q_SCALE = (D_NOPE + D_ROPE) ** -0.5 * math.log2(math.e)


def _value_slots(vp):
    first = lax.broadcasted_iota(jnp.int32, vp.shape, 1) < D_VMLA
    return (jnp.where(first, vp, 1.0).astype(BF16), jnp.where(first, 1.0, vp).astype(BF16))


def _attend_pair(q2, k2, v2):
    outs = []
    for qh, kh, vh in zip(q2, k2, v2):
        s = lax.dot_general(qh, kh, NT, preferred_element_type=F32)
        e = jnp.exp2((s - jnp.max(s, axis=-1, keepdims=True)).astype(BF16))
        outs.append(_dot(e, vh))
    first = lax.broadcasted_iota(jnp.int32, outs[0].shape, 1) < D_VMLA
    o = jnp.where(first, outs[0], outs[1])
    l = jnp.where(first, pltpu.roll(outs[0], D_VMLA, 1), pltpu.roll(outs[1], D_VMLA, 1))
    return o / l


def _fourier(T, fu, csw_ref, cts_ref):
    a, b = [], []
    for g in range(F_GROUPS):
        z = _dot(fu[:, g * F_GROUP_W:(g + 1) * F_GROUP_W], csw_ref[...])
        a.append(z[:, :F_GROUP_W])
        b.append(z[:, F_GROUP_W:])
    a = jnp.concatenate(a, axis=1).astype(BF16)
    b = jnp.concatenate(b, axis=1).astype(BF16)
    o = _dot(cts_ref[:, 0:T], a) + _dot(cts_ref[:, T:2 * T], b)
    return o * ((T * F_GROUP_W) ** -0.5)


def _merge(branches, sg, x, mod_ref, wb_ref, wo_ref, fg_ref):
    merged = None
    for n, b in enumerate(branches):
        term = _dot(b, wb_ref[n]) * sg[n]
        merged = term if merged is None else merged + term
    gate = mod_ref[:, 2 * D_MODEL:3 * D_MODEL]
    y = x + gate * _dot(merged.astype(BF16), wo_ref[...])
    if fg_ref is not None:
        y = _rms(y, fg_ref[...])
    return y


def _prompt_kernel(layer, final, *refs):
    (x_ref, mod_ref, g_ref, qg_ref, kvg_ref, w_ref, lg_ref, wq_ref, wk_ref, wv_ref,
     csw_ref, cts_ref, wb_ref, wo_ref) = refs[:14]
    n_in = 14
    fg_ref = None
    if final:
        fg_ref = refs[n_in]
        n_in += 1
    if layer:
        pckv_ref, pkr_ref, pst_ref = refs[n_in:n_in + 3]
        n_in += 3
    y_ref, nckv_ref, nkr_ref, nst_ref, mask_ref = refs[n_in:]
    T = SEQ
    lg = jax.nn.log_sigmoid(lg_ref[...])

    @pl.when(pl.program_id(0) == 0)
    def _():
        _init_masks(T // RET_BLK, lg, mask_ref)

    seqs = range(PROMPT_SPS)
    rows = [slice(s * T, (s + 1) * T) for s in seqs]
    slot = lambda a, i: a[:, i * LANE:(i + 1) * LANE]
    xs = [x_ref[r] for r in rows]
    prs = [_project(x, mod_ref, g_ref, qg_ref, kvg_ref, w_ref) for x in xs]

    for s, pr in zip(seqs, prs):
        if layer:
            nckv_ref[s, 0:layer] = pckv_ref[s]
            nkr_ref[s, 0:layer] = pkr_ref[s]
            nst_ref[s, 0:layer] = pst_ref[s]
        nckv_ref[s, layer] = pr.ckv
        nkr_ref[s, layer] = pr.kr.T[ROPE_LANE0:ROPE_LANE0 + D_ROPE]

    o_r = [[] for _ in seqs]
    for h in range(H_RET):
        lf = lg[0:1, h:h + 1]
        lb = lg[1:2, h:h + 1]
        for s, pr in zip(seqs, prs):
            q = pr.q[:, h * DK_RET:(h + 1) * DK_RET]
            k = pr.k[:, h * DK_RET:(h + 1) * DK_RET]
            v = pr.v[:, h * DV_RET:(h + 1) * DV_RET]
            o_r[s].append(_ret_head(T, q, k, v, mask_ref.at[h], lf, lb))
            sf, sb = _ret_states(T, k, v, lf, lb)
            nst_ref[s, layer, 0, h] = sf
            nst_ref[s, layer, 1, h] = sb

    qs, ks, vs = [], [], []
    for pr in prs:
        ckvb = pr.ckv.astype(BF16)
        q_all = _dot(pr.qn, wq_ref[...]) * Q_SCALE
        k_all = _dot(ckvb, wk_ref[...])
        vs.append(_dot(ckvb, wv_ref[...]))
        qs.append([slot(q_all, h).astype(BF16) for h in range(H_MLA)])
        ks.append([(slot(k_all, h) + pr.kr).astype(BF16) for h in range(H_MLA)])
    o_m = [[] for _ in seqs]
    for p in range(H_MLA // 2):
        for s in seqs:
            o_m[s].append(_attend_pair(qs[s][2 * p:2 * p + 2], ks[s][2 * p:2 * p + 2],
                                       _value_slots(slot(vs[s], p))))

    o_f = [_fourier(T, pr.fu, csw_ref, cts_ref) for pr in prs]

    for s, pr in zip(seqs, prs):
        branches = [(o * gate).astype(BF16) for o, gate in zip(
            (jnp.concatenate(o_r[s], axis=1), jnp.concatenate(o_m[s], axis=1), o_f[s]), pr.sz)]
        y_ref[rows[s]] = _merge(branches, pr.sg, xs[s], mod_ref, wb_ref, wo_ref, fg_ref)


def _prompt_call(layer, x, mod, g, qg, kvg, w_p, logit, wq, wk, wv, csw, cts, wb, wo,
                 final_g=None, prev=None):
    T = SEQ
    row = pl.BlockSpec((PROMPT_SPS * T, D_MODEL), lambda i: (i, 0))
    in_specs = [row, _mod_spec(layer, lambda i: 0)]
    in_specs += [_layer_spec(a, layer) for a in (g, qg, kvg, w_p, logit, wq, wk, wv)]
    in_specs += [_full_spec(csw), _full_spec(cts), _layer_spec(wb, layer), _layer_spec(wo, layer)]
    args = [x, mod, g, qg, kvg, w_p, logit, wq, wk, wv, csw, cts, wb, wo]
    if final_g is not None:
        in_specs.append(_full_spec(final_g))
        args.append(final_g)
    tails = ((SEQ, KV_LORA), (D_ROPE, SEQ), ST_TAIL)
    if layer:
        in_specs += [_grow_spec(layer, t) for t in tails]
        args += list(prev)
    return pl.pallas_call(
        functools.partial(_prompt_kernel, layer, final_g is not None),
        grid=(BATCH // PROMPT_SPS,),
        in_specs=in_specs,
        out_specs=[row] + [_grow_spec(layer + 1, t) for t in tails],
        out_shape=[jax.ShapeDtypeStruct((BATCH * T, D_MODEL), F32)]
        + [jax.ShapeDtypeStruct((BATCH, layer + 1) + t, F32) for t in tails],
        scratch_shapes=[pltpu.VMEM((H_RET, 1, RET_BLK, RET_BLK), F32)],
        compiler_params=_cparams(),
        name="prompt_layer",
    )(*args)


def _in_kernel(x_ref, mod_ref, g_ref, qg_ref, kvg_ref, w_ref,
               qkv_ref, sz_ref, sg_ref, qn_ref, ckv_ref, kr_ref, fu_ref):
    qw = H_RET * DK_RET
    half = TM // IN_CHAINS
    for c in range(IN_CHAINS):
        rows = slice(c * half, (c + 1) * half)
        pr = _project(x_ref[rows], mod_ref, g_ref, qg_ref, kvg_ref, w_ref)
        qkv_ref[rows, 0:qw] = pr.q
        qkv_ref[rows, qw:2 * qw] = pr.k
        qkv_ref[rows, 2 * qw:] = pr.v
        for j in range(3):
            sz_ref[rows, j * BRANCH_W:(j + 1) * BRANCH_W] = pr.sz[j].astype(BF16)
        for j in range(N_BRANCH):
            sg_ref[rows, j * D_MODEL:(j + 1) * D_MODEL] = pr.sg[j].astype(BF16)
        qn_ref[rows] = pr.qn
        ckv_ref[rows] = pr.ckv.astype(BF16)
        kr_ref[rows] = pr.kr
        fu_ref[rows] = pr.fu


def _in_call(layer, x, mod, g, qg, kvg, w_p):
    n = x.shape[0]
    per_b = DEC_SEQ // TM
    row = lambda w: pl.BlockSpec((TM, w), lambda i: (i, 0))
    outs = [(1024, BF16), (3 * BRANCH_W, BF16), (N_BRANCH * D_MODEL, BF16), (Q_LORA, BF16),
            (KV_LORA, BF16), (LANE, F32), (FOURIER_W, BF16)]
    return pl.pallas_call(
        _in_kernel,
        grid=(n // TM,),
        in_specs=[row(D_MODEL), _mod_spec(layer, lambda i: 1 + i // per_b)]
        + [_layer_spec(a, layer) for a in (g, qg, kvg, w_p)],
        out_specs=[row(w) for w, _ in outs],
        out_shape=[jax.ShapeDtypeStruct((n, w), dt) for w, dt in outs],
        compiler_params=_cparams(),
        name="in_proj_s",
    )(x, mod, g, qg, kvg, w_p)


def _ret_fourier_kernel(T, qkv_ref, rz_ref, fz_ref, fu_ref, lg_ref, st_ref, csw_ref, cts_ref,
                        o_ref, mask_ref):
    lg = jax.nn.log_sigmoid(lg_ref[...])

    @pl.when(pl.program_id(0) == 0)
    def _():
        _init_masks(T // RET_BLK, lg, mask_ref)

    qw = H_RET * DK_RET
    for h in range(H_RET):
        q = qkv_ref[:, h * DK_RET:(h + 1) * DK_RET]
        k = qkv_ref[:, qw + h * DK_RET:qw + (h + 1) * DK_RET]
        v = qkv_ref[:, 2 * qw + h * DV_RET:2 * qw + (h + 1) * DV_RET]
        on = _ret_head(T, q, k, v, mask_ref.at[h], lg[0:1, h:h + 1], lg[1:2, h:h + 1],
                       st=(st_ref[0, h], st_ref[1, h]))
        gate = rz_ref[:, h * DV_RET:(h + 1) * DV_RET].astype(F32)
        o_ref[:, h * DV_RET:(h + 1) * DV_RET] = (on * gate).astype(BF16)
    o = _fourier(T, fu_ref[...], csw_ref, cts_ref)
    o_ref[:, RET_W:] = (o * fz_ref[...].astype(F32)).astype(BF16)


def _ret_fourier_call(layer, qkv, sz, fu, logit, state, csw, cts):
    T = DEC_SEQ
    return pl.pallas_call(
        functools.partial(_ret_fourier_kernel, T),
        grid=(DEC_BATCH,),
        in_specs=[pl.BlockSpec((T, 1024), lambda b: (b, 0)),
                  pl.BlockSpec((T, BRANCH_W), lambda b: (b, 0)),
                  pl.BlockSpec((T, BRANCH_W), lambda b: (b, 2)),
                  pl.BlockSpec((T, FOURIER_W), lambda b: (b, 0)),
                  _layer_spec(logit, layer),
                  pl.BlockSpec((None, None) + ST_TAIL, lambda b: (b, layer, 0, 0, 0, 0)),
                  _full_spec(csw), _full_spec(cts)],
        out_specs=pl.BlockSpec((T, RET_W + FOURIER_W), lambda b: (b, 0)),
        out_shape=jax.ShapeDtypeStruct((T * DEC_BATCH, RET_W + FOURIER_W), BF16),
        scratch_shapes=[pltpu.VMEM((H_RET, 3, RET_BLK, RET_BLK), F32)],
        compiler_params=_cparams(),
        name="ret_fourier_s",
    )(qkv, sz, sz, fu, logit, state, csw, cts)


def _mla_kernel(T, qn_ref, ckv_ref, kr_ref, mz_ref, wq_ref, wk_ref, wv_ref, cckv_ref, ckr_ref,
                cos_ref, sa_ref, sb_ref, bm_ref, q_scr, k_scr, v_scr):
    P = PAST_LEN
    qn = qn_ref[...]
    ckv = ckv_ref[...]
    kr = _rope(kr_ref[...], cos_ref[...], sa_ref[...], sb_ref[...])
    cc = cckv_ref[...].astype(BF16)
    for p in range(H_MLA // 2):
        sl = slice(2 * p * LANE, 2 * (p + 1) * LANE)
        q2 = _dot(qn, wq_ref[:, sl])
        kc2 = _dot(cc, wk_ref[:, sl])
        kn2 = _dot(ckv, wk_ref[:, sl])
        for i in range(2):
            h = 2 * p + i
            hl = slice(i * LANE, (i + 1) * LANE)
            qh = _rope(q2[:, hl], cos_ref[...], sa_ref[...], sb_ref[...])
            q_scr[h] = (qh * Q_SCALE).astype(BF16)
            k_scr[h, 0:P] = (kc2[:, hl] + ckr_ref[...]).astype(BF16)
            k_scr[h, P:P + T] = (kn2[:, hl] + kr).astype(BF16)
    for pp in range(H_MLA // 4):
        sl = slice(2 * pp * LANE, 2 * (pp + 1) * LANE)
        for rows, src in ((slice(0, P), cc), (slice(P, P + T), ckv)):
            v2 = _dot(src, wv_ref[:, sl])
            for i in range(2):
                h = 4 * pp + 2 * i
                v_scr[h, rows], v_scr[h + 1, rows] = _value_slots(v2[:, i * LANE:(i + 1) * LANE])

    def body(qb, carry):
        rs = pl.ds(pl.multiple_of(qb * TQ, TQ), TQ)
        for p in range(H_MLA // 2):
            hs = (2 * p, 2 * p + 1)
            o = _attend_pair([q_scr[h, rs] for h in hs], [k_scr[h] for h in hs],
                             [v_scr[h] for h in hs])
            gate = mz_ref[rs, p * LANE:(p + 1) * LANE].astype(F32)
            bm_ref[rs, p * LANE:(p + 1) * LANE] = (o * gate).astype(BF16)
        return carry

    lax.fori_loop(0, T // TQ, body, 0)


def _mla_call(layer, qn, ckv, kr, sz, wq, wk, wv, cache_ckv, cache_kr, cos, sa, sb):
    T = DEC_SEQ
    Tk = T + PAST_LEN
    row = lambda w, c=0: pl.BlockSpec((T, w), lambda b: (b, c))
    return pl.pallas_call(
        functools.partial(_mla_kernel, T),
        grid=(DEC_BATCH,),
        in_specs=[row(Q_LORA), row(KV_LORA), row(LANE), row(BRANCH_W, 1)]
        + [_layer_spec(a, layer) for a in (wq, wk, wv)]
        + [pl.BlockSpec((None, None, PAST_LEN, KV_LORA), lambda b: (b, layer, 0, 0)),
           pl.BlockSpec((None, None, PAST_LEN, LANE), lambda b: (b, layer, 0, 0)),
           _full_spec(cos), _full_spec(sa), _full_spec(sb)],
        out_specs=pl.BlockSpec((T, MLA_W), lambda b: (b, 0)),
        out_shape=jax.ShapeDtypeStruct((T * DEC_BATCH, MLA_W), BF16),
        scratch_shapes=[pltpu.VMEM((H_MLA, T, LANE), BF16),
                        pltpu.VMEM((H_MLA, Tk, LANE), BF16),
                        pltpu.VMEM((H_MLA, Tk, LANE), BF16)],
        compiler_params=_cparams(),
        name="mla_s",
    )(qn, ckv, kr, sz, wq, wk, wv, cache_ckv, cache_kr, cos, sa, sb)


def _merge_kernel(final, *refs):
    br_ref, bm_ref, bf_ref, sg_ref, x_ref, mod_ref, wb_ref, wo_ref = refs[:8]
    fg_ref = refs[8] if final else None
    sg = [sg_ref[:, n * D_MODEL:(n + 1) * D_MODEL].astype(F32) for n in range(N_BRANCH)]
    refs[-1][...] = _merge((br_ref[...], bm_ref[...], bf_ref[...]), sg, x_ref[...],
                           mod_ref, wb_ref, wo_ref, fg_ref)


def _merge_call(layer, brf, bm, sg, x, mod, wb, wo, final_g=None):
    final = final_g is not None
    n = x.shape[0]
    per_b = DEC_SEQ // TM
    row = lambda w, c=0: pl.BlockSpec((TM, w), lambda i: (i, c))
    in_specs = [row(BRANCH_W), row(BRANCH_W), row(BRANCH_W, 1), row(N_BRANCH * D_MODEL), row(D_MODEL),
                _mod_spec(layer, lambda i: 1 + i // per_b),
                _layer_spec(wb, layer), _layer_spec(wo, layer)]
    args = [brf, bm, brf, sg, x, mod, wb, wo]
    if final:
        in_specs.append(_full_spec(final_g))
        args.append(final_g)
    return pl.pallas_call(
        functools.partial(_merge_kernel, final),
        grid=(n // TM,),
        in_specs=in_specs,
        out_specs=row(D_MODEL),
        out_shape=jax.ShapeDtypeStruct((n, D_MODEL), F32),
        compiler_params=_cparams(),
        name="merge_s",
    )(*args)


def _dft_tables(T):
    def cs(n):
        kt = (np.arange(n)[:, None] * np.arange(n)[None, :]) % n
        ang = 2.0 * np.pi * kt.astype(np.float64) / n
        return np.cos(ang), np.sin(ang)
    ct, st = cs(T)
    cw, sw = cs(F_GROUP_W)
    cts = np.concatenate([ct, -st], axis=1).astype(np.float32)
    csw = np.concatenate([cw, sw], axis=1).astype(np.float32)
    return jnp.asarray(csw).astype(BF16), jnp.asarray(cts).astype(BF16)


def _rope_tables(T):
    half = D_ROPE // 2
    nfreq = half // 2
    inv = ROPE_BASE ** (-np.arange(nfreq, dtype=np.float64) / nfreq)
    t = np.arange(T)
    pos = np.stack([t // GRID_W, t % GRID_W], axis=0).astype(np.float64)
    cos = np.ones((T, LANE), np.float64)
    sa = np.zeros((T, LANE), np.float64)
    sb = np.zeros((T, LANE), np.float64)
    for part in range(2):
        ang = pos[part][:, None] * inv[None, :]
        l1 = ROPE_LANE0 + part * half
        l2 = l1 + nfreq
        cos[:, l1:l1 + nfreq] = np.cos(ang)
        cos[:, l2:l2 + nfreq] = np.cos(ang)
        sa[:, l1:l1 + nfreq] = -np.sin(ang)
        sb[:, l2:l2 + nfreq] = np.sin(ang)
    return tuple(jnp.asarray(a.astype(np.float32)) for a in (cos, sa, sb))


KR_SRC = 2176
W_IN_RUNS = ((0, 1024),
             (1024, 512), (2208, 512), (3232, 512),
             (3744, 3072),
             (1536, 384),
             (1920, 256),
             (KR_SRC - ROPE_LANE0, LANE),
             (2720, 512))
W_PREP_TILES = 6
KR_TILE = SEG_KR[0] // LANE


def _w_prep_kernel(tbl_ref, *refs):
    o_ref = refs[-1]
    j = pl.program_id(1)
    lane = lax.broadcasted_iota(jnp.int32, (D_MODEL, LANE), 1)
    rotary = (lane >= ROPE_LANE0) & (lane < ROPE_LANE0 + D_ROPE)
    for t, w_ref in enumerate(refs[:-1]):
        wt = w_ref[0].T
        if t == KR_TILE % W_PREP_TILES:
            wt = jnp.where((j != KR_TILE // W_PREP_TILES) | rotary, wt, 0.0)
        o_ref[:, t * LANE:(t + 1) * LANE] = wt.astype(BF16)


def _permute_w_in(w):
    depth, d, in_w = w.shape
    starts = np.concatenate([src + np.arange(0, width, LANE) for src, width in W_IN_RUNS])
    assert starts.size * LANE == IN_WP and starts.size % W_PREP_TILES == 0
    tbl = jnp.asarray(starts, jnp.int32)
    tile_spec = lambda t: pl.BlockSpec(
        (pl.Element(1), pl.Element(LANE), pl.Element(d)),
        lambda l, j, tbl_ref: (l, pl.multiple_of(tbl_ref[j * W_PREP_TILES + t], D_ROPE), 0))
    wide = W_PREP_TILES * LANE
    return pl.pallas_call(
        _w_prep_kernel,
        grid_spec=pltpu.PrefetchScalarGridSpec(
            num_scalar_prefetch=1,
            grid=(depth, IN_WP // wide),
            in_specs=[tile_spec(t) for t in range(W_PREP_TILES)],
            out_specs=pl.BlockSpec((None, d, wide), lambda l, j, tbl_ref: (l, 0, j)),
        ),
        out_shape=jax.ShapeDtypeStruct((depth, d, IN_WP), BF16),
        compiler_params=_cparams(2),
        name="w_prep",
    )(tbl, *([jnp.swapaxes(w, 1, 2)] * W_PREP_TILES))


def _pad_heads(w, n_heads, width, lo, hi):
    d, k = w.shape[:2]
    wh = w.reshape(d, k, n_heads, width)[..., lo:hi]
    wh = jnp.pad(wh, ((0, 0), (0, 0), (0, 0), (0, LANE - (hi - lo))))
    return wh.reshape(d, k, n_heads * LANE).astype(BF16)


def kernel(x_prompt, x_sample, cache_ckv, cache_krope, state_ret, c, c_ctx, norm_g, w_mod, b_mod,
           w_in, ret_decay_logit, q_norm_g, w_q_up, kv_norm_g, w_kv_up, w_branch, w_out,
           final_norm_g):
    cv = jnp.concatenate([c_ctx[None, :], c, jnp.zeros((MOD_ROWS - 1 - DEC_BATCH, D_MODEL), F32)], axis=0)
    mod = _mod_call(cv, w_mod, b_mod).reshape(DEPTH, MOD_ROWS, 1, 3 * D_MODEL)

    cache_kr = jnp.pad(cache_krope, ((0, 0), (0, 0), (0, 0), (ROPE_LANE0, LANE - ROPE_LANE0 - D_ROPE)))
    rope = _rope_tables(DEC_SEQ)
    dft_p = _dft_tables(SEQ)
    dft_s = _dft_tables(DEC_SEQ)
    logit = jnp.pad(ret_decay_logit, ((0, 0), (0, 8 - 2), (0, LANE - H_RET)))
    g, qg, kvg = norm_g[:, None, :], q_norm_g[:, None, :], kv_norm_g[:, None, :]
    w_p = _permute_w_in(w_in)
    wq = _pad_heads(w_q_up, H_MLA, D_NOPE + D_ROPE, 0, D_NOPE + D_ROPE)
    wk = _pad_heads(w_kv_up, H_MLA, D_NOPE + D_VMLA, 0, D_NOPE)
    wv = w_kv_up.reshape(DEPTH, KV_LORA, H_MLA, D_NOPE + D_VMLA)[..., D_NOPE:]
    wv = wv.reshape(DEPTH, KV_LORA, MLA_W).astype(BF16)
    wb = w_branch.astype(BF16)
    wo = w_out.astype(BF16)
    final_g = final_norm_g[None, :]

    xp = x_prompt.reshape(BATCH * SEQ, D_MODEL)
    xs = x_sample.reshape(DEC_BATCH * DEC_SEQ, D_MODEL)
    new_ctx = None
    for l in range(DEPTH):
        fg = final_g if l == DEPTH - 1 else None
        xp, *new_ctx = _prompt_call(l, xp, mod, g, qg, kvg, w_p, logit, wq, wk, wv, *dft_p, wb, wo,
                                    final_g=fg, prev=new_ctx)

        qkv, sz, sg, qn, ckv, kr, fu = _in_call(l, xs, mod, g, qg, kvg, w_p)
        brf = _ret_fourier_call(l, qkv, sz, fu, logit, state_ret, *dft_s)
        bm = _mla_call(l, qn, ckv, kr, sz, wq, wk, wv, cache_ckv, cache_kr, *rope)
        xs = _merge_call(l, brf, bm, sg, xs, mod, wb, wo, final_g=fg)

    y_prompt = xp.reshape(BATCH, SEQ, D_MODEL)
    y_sample = xs.reshape(DEC_BATCH, DEC_SEQ, D_MODEL)
    new_ckv, new_krope_t, new_ret = new_ctx
    return (y_prompt, y_sample, new_ckv, jnp.swapaxes(new_krope_t, 2, 3), new_ret)
```

```python
import collections
import functools
import math

import numpy as np
import jax
import jax.numpy as jnp
from jax import lax
from jax.experimental import pallas as pl
from jax.experimental.pallas import tpu as pltpu

F32 = jnp.float32
BF16 = jnp.bfloat16

D_MODEL = 1024
BATCH = 32
SEQ = 256
DEPTH = 2
DEC_BATCH = 4
DEC_SEQ = 1024
PAST_LEN = 512
GRID_W = 64
EPS = 1e-6
H_RET = 4
DK_RET = 64
DV_RET = 128
RET_W = H_RET * DV_RET
H_MLA = 8
Q_LORA = 384
KV_LORA = 256
D_NOPE = 64
D_ROPE = 32
D_VMLA = 64
MLA_W = H_MLA * D_VMLA
ROPE_BASE = 10000.0
F_GROUPS = 4
F_GROUP_W = 128
FOURIER_W = F_GROUPS * F_GROUP_W
N_BRANCH = 3
BRANCH_W = 512

LANE = 128
TM = 512
IN_CHAINS = 2
PROMPT_SPS = 2
MOD_ROWS = 8
ROPE_LANE0 = D_NOPE
VMEM_LIMIT = 60 * 1024 * 1024
ST_TAIL = (2, H_RET, DK_RET, DV_RET)

SEG_QKV = (0, 1024)
SEG_Z = (1024, 2560)
SEG_G = (2560, 5632)
SEG_QL = (5632, 6016)
SEG_KV = (6016, 6272)
SEG_KR = (6272, 6400)
SEG_FU = (6400, 6912)
IN_WP = 6912

NT = (((1,), (1,)), ((), ()))
TN = (((0,), (0,)), ((), ()))


def _cparams(n_axes=1):
    return pltpu.CompilerParams(dimension_semantics=("arbitrary",) * n_axes,
                                vmem_limit_bytes=VMEM_LIMIT)


def _layer_spec(a, layer):
    shape = a.shape[1:]
    return pl.BlockSpec((None,) + shape, lambda i: (layer,) + (0,) * len(shape))


def _full_spec(a):
    return pl.BlockSpec(a.shape, lambda i: (0,) * a.ndim)


def _mod_spec(layer, row):
    return pl.BlockSpec((None, None, 1, 3 * D_MODEL), lambda i: (layer, row(i), 0, 0))


def _grow_spec(depth, tail):
    return pl.BlockSpec((PROMPT_SPS, depth) + tail, lambda i: (i,) + (0,) * (1 + len(tail)))


def _dot(a, b):
    return jnp.dot(a, b, preferred_element_type=F32)


def _rms(x, g):
    return x * lax.rsqrt(jnp.mean(x * x, axis=-1, keepdims=True) + EPS) * g


def _mod_kernel(c_ref, w_ref, b_ref, o_ref):
    cv = c_ref[...]
    s = cv * jax.nn.sigmoid(cv)
    o_ref[0] = _dot(s.astype(BF16), w_ref[0].astype(BF16)) + b_ref[0]


def _mod_call(cv, w_mod, b_mod):
    nb = 3 * D_MODEL // 1024
    return pl.pallas_call(
        _mod_kernel,
        grid=(DEPTH, nb),
        in_specs=[
            pl.BlockSpec((MOD_ROWS, D_MODEL), lambda l, j: (0, 0)),
            pl.BlockSpec((1, D_MODEL, 1024), lambda l, j: (l, 0, j)),
            pl.BlockSpec((1, 1, 1024), lambda l, j: (l, 0, j)),
        ],
        out_specs=pl.BlockSpec((1, MOD_ROWS, 1024), lambda l, j: (l, 0, j)),
        out_shape=jax.ShapeDtypeStruct((DEPTH, MOD_ROWS, 3 * D_MODEL), F32),
        compiler_params=_cparams(2),
        name="mod",
    )(cv, w_mod, b_mod.reshape(DEPTH, 1, 3 * D_MODEL))


Proj = collections.namedtuple("Proj", "q k v sz sg qn ckv kr fu")


def _modulate(x, mod_ref, g_ref):
    shift = mod_ref[:, 0:D_MODEL]
    scale = mod_ref[:, D_MODEL:2 * D_MODEL]
    return (_rms(x, g_ref[...]) * (1.0 + scale) + shift).astype(BF16)


def _project(x, mod_ref, g_ref, qg_ref, kvg_ref, w_ref):
    return _project_h(_modulate(x, mod_ref, g_ref), qg_ref, kvg_ref, w_ref)


def _project_h(hb, qg_ref, kvg_ref, w_ref):
    def mm(a, b):
        return _dot(hb, w_ref[:, a:b])

    qw = H_RET * DK_RET
    p = mm(*SEG_QKV)
    q = p[:, 0:qw].astype(BF16)
    k = (p[:, qw:2 * qw] * (DK_RET ** -0.5)).astype(BF16)
    v = p[:, 2 * qw:].astype(BF16)
    sz = []
    for j in range(3):
        a = SEG_Z[0] + j * BRANCH_W
        p = mm(a, a + BRANCH_W)
        sz.append(p * jax.nn.sigmoid(p))
    sg = []
    for j in range(N_BRANCH):
        a = SEG_G[0] + j * D_MODEL
        sg.append(jax.nn.sigmoid(mm(a, a + D_MODEL)))
    qn = _rms(mm(*SEG_QL), qg_ref[...]).astype(BF16)
    ckv = _rms(mm(*SEG_KV), kvg_ref[...])
    kr = mm(*SEG_KR)
    fu = mm(*SEG_FU).astype(BF16)
    return Proj(q, k, v, sz, sg, qn, ckv, kr, fu)


RET_BLK = 256


def _init_masks(n_blk, lg, mask_ref):
    B = RET_BLK
    ii = lax.broadcasted_iota(jnp.int32, (B, B), 0)
    jj = lax.broadcasted_iota(jnp.int32, (B, B), 1)
    d = (ii - jj).astype(F32)
    ad = jnp.abs(d)
    for h in range(H_RET):
        lf = lg[0:1, h:h + 1]
        lb = lg[1:2, h:h + 1]
        mask_ref[h, 0] = jnp.where(d > 0, jnp.exp(ad * lf), jnp.where(d < 0, jnp.exp(ad * lb), 2.0))
        if n_blk > 1:
            mask_ref[h, 1] = jnp.exp((B + d) * lf)
            mask_ref[h, 2] = jnp.exp((B - d) * lb)


def _masked_scores(s, mask, lf, lb):
    B = RET_BLK
    n_blk = s.shape[0] // B
    if n_blk == 1:
        return (s * mask[0]).astype(BF16)
    rows = []
    for i in range(n_blk):
        cols = []
        for j in range(n_blk):
            blk = s[i * B:(i + 1) * B, j * B:(j + 1) * B]
            far = abs(i - j) - 1
            if far < 0:
                m = mask[0]
            else:
                m = mask[1] if j < i else mask[2]
                if far:
                    m = m * jnp.exp((far * B) * (lf if j < i else lb))
            cols.append((blk * m).astype(BF16))
        rows.append(jnp.concatenate(cols, axis=1))
    return jnp.concatenate(rows, axis=0)


def _ret_head(T, q, k, v, mask, lf, lb, st=None):
    s = lax.dot_general(q, k, NT, preferred_element_type=F32)
    o = _dot(_masked_scores(s, mask, lf, lb), v)
    if st is not None:
        rows = lax.broadcasted_iota(jnp.int32, (T, DV_RET), 0).astype(F32)
        o = o + jnp.exp((rows + 1.0) * lf) * _dot(q, st[0].astype(BF16))
        o = o + jnp.exp((T - rows) * lb) * _dot(q, st[1].astype(BF16))
    mu = jnp.mean(o, axis=-1, keepdims=True)
    c = o - mu
    var = jnp.mean(c * c, axis=-1, keepdims=True)
    return c * lax.rsqrt(var + EPS)


def _ret_states(T, k, v, lf, lb):
    rows = lax.broadcasted_iota(jnp.int32, (T, DK_RET), 0).astype(F32)
    kf = (k.astype(F32) * jnp.exp((T - 1.0 - rows) * lf)).astype(BF16)
    kb = (k.astype(F32) * jnp.exp(rows * lb)).astype(BF16)
    return (lax.dot_general(kf, v, TN, preferred_element_type=F32),
            lax.dot_general(kb, v, TN, preferred_element_type=F32))


def _rope(x, cos, sa, sb):
    half = D_ROPE // 4
    return x * cos + pltpu.roll(x, LANE - half, 1) * sa + pltpu.roll(x, half, 1) * sb


Q_SCALE = (D_NOPE + D_ROPE) ** -0.5 * math.log2(math.e)


def _value_slots(vp):
    first = lax.broadcasted_iota(jnp.int32, vp.shape, 1) < D_VMLA
    return (jnp.where(first, vp, 1.0).astype(BF16), jnp.where(first, 1.0, vp).astype(BF16))


def _attend_pair(q2, k2, v2):
    outs = []
    for qh, kh, vh in zip(q2, k2, v2):
        s = lax.dot_general(qh, kh, NT, preferred_element_type=F32)
        e = jnp.exp2((s - jnp.max(s, axis=-1, keepdims=True)).astype(BF16))
        outs.append(_dot(e, vh))
    first = lax.broadcasted_iota(jnp.int32, outs[0].shape, 1) < D_VMLA
    o = jnp.where(first, outs[0], outs[1])
    l = jnp.where(first, pltpu.roll(outs[0], D_VMLA, 1), pltpu.roll(outs[1], D_VMLA, 1))
    return o / l


def _fourier(T, fu, csw_ref, cts_ref):
    a, b = [], []
    for g in range(F_GROUPS):
        z = _dot(fu[:, g * F_GROUP_W:(g + 1) * F_GROUP_W], csw_ref[...])
        a.append(z[:, :F_GROUP_W])
        b.append(z[:, F_GROUP_W:])
    a = jnp.concatenate(a, axis=1).astype(BF16)
    b = jnp.concatenate(b, axis=1).astype(BF16)
    o = _dot(cts_ref[:, 0:T], a) + _dot(cts_ref[:, T:2 * T], b)
    return o * ((T * F_GROUP_W) ** -0.5)


def _merge(branches, sg, x, mod_ref, wb_ref, wo_ref, fg_ref):
    merged = None
    for n, b in enumerate(branches):
        term = _dot(b, wb_ref[n]) * sg[n]
        merged = term if merged is None else merged + term
    gate = mod_ref[:, 2 * D_MODEL:3 * D_MODEL]
    y = x + gate * _dot(merged.astype(BF16), wo_ref[...])
    if fg_ref is not None:
        y = _rms(y, fg_ref[...])
    return y


def _prompt_kernel(layer, final, *refs):
    (x_ref, mod_ref, g_ref, qg_ref, kvg_ref, w_ref, lg_ref, wq_ref, wk_ref, wv_ref,
     csw_ref, cts_ref, wb_ref, wo_ref) = refs[:14]
    n_in = 14
    fg_ref = None
    if final:
        fg_ref = refs[n_in]
        n_in += 1
    if layer:
        pckv_ref, pkr_ref, pst_ref = refs[n_in:n_in + 3]
        n_in += 3
    y_ref, nckv_ref, nkr_ref, nst_ref, mask_ref = refs[n_in:]
    T = SEQ
    lg = jax.nn.log_sigmoid(lg_ref[...])

    @pl.when(pl.program_id(0) == 0)
    def _():
        _init_masks(T // RET_BLK, lg, mask_ref)

    seqs = range(PROMPT_SPS)
    rows = [slice(s * T, (s + 1) * T) for s in seqs]
    slot = lambda a, i: a[:, i * LANE:(i + 1) * LANE]
    xs = [x_ref[r] for r in rows]
    prs = [_project(x, mod_ref, g_ref, qg_ref, kvg_ref, w_ref) for x in xs]

    for s, pr in zip(seqs, prs):
        if layer:
            nckv_ref[s, 0:layer] = pckv_ref[s]
            nkr_ref[s, 0:layer] = pkr_ref[s]
            nst_ref[s, 0:layer] = pst_ref[s]
        nckv_ref[s, layer] = pr.ckv
        nkr_ref[s, layer] = pr.kr.T[ROPE_LANE0:ROPE_LANE0 + D_ROPE]

    qs, ks, vs = [], [], []
    for pr in prs:
        ckvb = pr.ckv.astype(BF16)
        q_all = _dot(pr.qn, wq_ref[...]) * Q_SCALE
        k_all = _dot(ckvb, wk_ref[...])
        vs.append(_dot(ckvb, wv_ref[...]))
        qs.append([slot(q_all, h).astype(BF16) for h in range(H_MLA)])
        ks.append([(slot(k_all, h) + pr.kr).astype(BF16) for h in range(H_MLA)])
    o_m = [[] for _ in seqs]
    for p in range(H_MLA // 2):
        for s in seqs:
            o_m[s].append(_attend_pair(qs[s][2 * p:2 * p + 2], ks[s][2 * p:2 * p + 2],
                                       _value_slots(slot(vs[s], p))))

    o_f = [_fourier(T, pr.fu, csw_ref, cts_ref) for pr in prs]

    o_r = [[] for _ in seqs]
    for h in range(H_RET):
        lf = lg[0:1, h:h + 1]
        lb = lg[1:2, h:h + 1]
        for s, pr in zip(seqs, prs):
            q = pr.q[:, h * DK_RET:(h + 1) * DK_RET]
            k = pr.k[:, h * DK_RET:(h + 1) * DK_RET]
            v = pr.v[:, h * DV_RET:(h + 1) * DV_RET]
            o_r[s].append(_ret_head(T, q, k, v, mask_ref.at[h], lf, lb))
            sf, sb = _ret_states(T, k, v, lf, lb)
            nst_ref[s, layer, 0, h] = sf
            nst_ref[s, layer, 1, h] = sb

    for s, pr in zip(seqs, prs):
        branches =[(o * gate).astype(BF16) for o, gate in zip(
            (jnp.concatenate(o_r[s], axis=1), jnp.concatenate(o_m[s], axis=1), o_f[s]), pr.sz)]
        y_ref[rows[s]] = _merge(branches, pr.sg, xs[s], mod_ref, wb_ref, wo_ref, fg_ref)


def _prompt_call(layer, x, mod, g, qg, kvg, w_p, logit, wq, wk, wv, csw, cts, wb, wo,
                 final_g=None, prev=None):
    T = SEQ
    n_steps = BATCH // PROMPT_SPS
    row = pl.BlockSpec((PROMPT_SPS * T, D_MODEL), lambda i: (i, 0))
    in_specs = [row, _mod_spec(layer, lambda i: 0)]
    in_specs += [_layer_spec(a, layer) for a in (g, qg, kvg, w_p, logit, wq, wk, wv)]
    in_specs += [_full_spec(csw), _full_spec(cts), _layer_spec(wb, layer), _layer_spec(wo, layer)]
    args = [x, mod, g, qg, kvg, w_p, logit, wq, wk, wv, csw, cts, wb, wo]
    if final_g is not None:
        in_specs.append(_full_spec(final_g))
        args.append(final_g)
    tails = ((SEQ, KV_LORA), (D_ROPE, SEQ), ST_TAIL)
    if layer:
        in_specs += [_grow_spec(layer, t) for t in tails]
        args += list(prev)
    return pl.pallas_call(
        functools.partial(_prompt_kernel, layer, final_g is not None),
        grid=(n_steps,),
        in_specs=in_specs,
        out_specs=[row] + [_grow_spec(layer + 1, t) for t in tails],
        out_shape=[jax.ShapeDtypeStruct((BATCH * T, D_MODEL), F32)]
        + [jax.ShapeDtypeStruct((BATCH, layer + 1) + t, F32) for t in tails],
        scratch_shapes=[pltpu.VMEM((H_RET, 1, RET_BLK, RET_BLK), F32)],
        compiler_params=_cparams(),
        name="prompt_layer",
    )(*args)


def _in_kernel(x_ref, mod_ref, g_ref, qg_ref, kvg_ref, w_ref,
               qkv_ref, sz_ref, sg_ref, qn_ref, ckv_ref, kr_ref, fu_ref):
    qw = H_RET * DK_RET
    half = TM // IN_CHAINS
    for c in range(IN_CHAINS):
        rows = slice(c * half, (c + 1) * half)
        pr = _project(x_ref[rows], mod_ref, g_ref, qg_ref, kvg_ref, w_ref)
        qkv_ref[rows, 0:qw] = pr.q
        qkv_ref[rows, qw:2 * qw] = pr.k
        qkv_ref[rows, 2 * qw:] = pr.v
        for j in range(3):
            sz_ref[rows, j * BRANCH_W:(j + 1) * BRANCH_W] = pr.sz[j].astype(BF16)
        for j in range(N_BRANCH):
            sg_ref[rows, j * D_MODEL:(j + 1) * D_MODEL] = pr.sg[j].astype(BF16)
        qn_ref[rows] = pr.qn
        ckv_ref[rows] = pr.ckv.astype(BF16)
        kr_ref[rows] = pr.kr
        fu_ref[rows] = pr.fu


def _in_call(layer, x, mod, g, qg, kvg, w_p):
    n = x.shape[0]
    per_b = DEC_SEQ // TM
    row = lambda w: pl.BlockSpec((TM, w), lambda i: (i, 0))
    outs = [(1024, BF16), (3 * BRANCH_W, BF16), (N_BRANCH * D_MODEL, BF16), (Q_LORA, BF16),
            (KV_LORA, BF16), (LANE, F32), (FOURIER_W, BF16)]
    return pl.pallas_call(
        _in_kernel,
        grid=(n // TM,),
        in_specs=[row(D_MODEL), _mod_spec(layer, lambda i: 1 + i // per_b)]
        + [_layer_spec(a, layer) for a in (g, qg, kvg, w_p)],
        out_specs=[row(w) for w, _ in outs],
        out_shape=[jax.ShapeDtypeStruct((n, w), dt) for w, dt in outs],
        compiler_params=_cparams(),
        name="in_proj_s",
    )(x, mod, g, qg, kvg, w_p)


def _mix_kernel(T, qkv_ref, sz_ref, fu_ref, qn_ref, ckv_ref, kr_ref, lg_ref, st_ref, csw_ref,
                cts_ref, wq_ref, wk_ref, wv_ref, cckv_ref, ckr_ref, cos_ref, sa_ref, sb_ref,
                o_ref, mask_ref):
    lg = jax.nn.log_sigmoid(lg_ref[...])

    @pl.when(pl.program_id(0) == 0)
    def _():
        _init_masks(T // RET_BLK, lg, mask_ref)

    def put(col, val):
        gate = sz_ref[:, col:col + val.shape[1]].astype(F32)
        o_ref[:, col:col + val.shape[1]] = (val * gate).astype(BF16)

    qn = qn_ref[...]
    ckv = ckv_ref[...]
    cos, sa, sb = cos_ref[...], sa_ref[...], sb_ref[...]
    kr = _rope(kr_ref[...], cos, sa, sb)
    cc = cckv_ref[...].astype(BF16)
    slot = lambda a, i: a[:, i * LANE:(i + 1) * LANE]
    for p in range(H_MLA // 2):
        sl = slice(2 * p * LANE, 2 * (p + 1) * LANE)
        q2 = _dot(qn, wq_ref[:, sl])
        kc2 = _dot(cc, wk_ref[:, sl])
        kn2 = _dot(ckv, wk_ref[:, sl])
        if p % 2 == 0:
            vsl = slice(p * LANE, (p + 2) * LANE)
            v4 = jnp.concatenate([_dot(cc, wv_ref[:, vsl]), _dot(ckv, wv_ref[:, vsl])], axis=0)
        qs = [(_rope(slot(q2, i), cos, sa, sb) * Q_SCALE).astype(BF16) for i in range(2)]
        ks = [jnp.concatenate([(slot(kc2, i) + ckr_ref[...]).astype(BF16),
                               (slot(kn2, i) + kr).astype(BF16)], axis=0) for i in range(2)]
        put(BRANCH_W + p * LANE, _attend_pair(qs, ks, _value_slots(slot(v4, p % 2))))

    put(2 * BRANCH_W, _fourier(T, fu_ref[...], csw_ref, cts_ref))

    qw = H_RET * DK_RET
    for h in range(H_RET):
        q = qkv_ref[:, h * DK_RET:(h + 1) * DK_RET]
        k = qkv_ref[:, qw + h * DK_RET:qw + (h + 1) * DK_RET]
        v = qkv_ref[:, 2 * qw + h * DV_RET:2 * qw + (h + 1) * DV_RET]
        put(h * DV_RET, _ret_head(T, q, k, v, mask_ref.at[h], lg[0:1, h:h + 1], lg[1:2, h:h + 1],
                                  st=(st_ref[0, h], st_ref[1, h])))


def _mix_call(layer, qkv, sz, fu, qn, ckv, kr, logit, state, csw, cts, wq, wk, wv,
              cache_ckv, cache_kr, cos, sa, sb):
    T = DEC_SEQ
    row = lambda w: pl.BlockSpec((T, w), lambda b: (b, 0))
    cached = lambda tail: pl.BlockSpec((None, None) + tail, lambda b: (b, layer) + (0,) * len(tail))
    return pl.pallas_call(
        functools.partial(_mix_kernel, T),
        grid=(DEC_BATCH,),
        in_specs=[row(1024), row(3 * BRANCH_W), row(FOURIER_W), row(Q_LORA), row(KV_LORA), row(LANE),
                  _layer_spec(logit, layer), cached(ST_TAIL), _full_spec(csw), _full_spec(cts)]
        + [_layer_spec(a, layer) for a in (wq, wk, wv)]
        + [cached((PAST_LEN, KV_LORA)), cached((PAST_LEN, LANE)),
           _full_spec(cos), _full_spec(sa), _full_spec(sb)],
        out_specs=row(3 * BRANCH_W),
        out_shape=jax.ShapeDtypeStruct((T * DEC_BATCH, 3 * BRANCH_W), BF16),
        scratch_shapes=[pltpu.VMEM((H_RET, 3, RET_BLK, RET_BLK), F32)],
        compiler_params=_cparams(),
        name="mix_s",
    )(qkv, sz, fu, qn, ckv, kr, logit, state, csw, cts, wq, wk, wv, cache_ckv, cache_kr,
      cos, sa, sb)


def _merge_kernel(final, *refs):
    br_ref, bm_ref, bf_ref, sg_ref, x_ref, mod_ref, wb_ref, wo_ref = refs[:8]
    fg_ref = refs[8] if final else None
    sg = [sg_ref[:, n * D_MODEL:(n + 1) * D_MODEL].astype(F32) for n in range(N_BRANCH)]
    refs[-1][...] = _merge((br_ref[...], bm_ref[...], bf_ref[...]), sg, x_ref[...],
                           mod_ref, wb_ref, wo_ref, fg_ref)


def _merge_call(layer, branches, sg, x, mod, wb, wo, final_g=None):
    final = final_g is not None
    n = x.shape[0]
    per_b = DEC_SEQ // TM
    row = lambda w, c=0: pl.BlockSpec((TM, w), lambda i: (i, c))
    in_specs = [row(BRANCH_W, 0), row(BRANCH_W, 1), row(BRANCH_W, 2),
                row(N_BRANCH * D_MODEL), row(D_MODEL),
                _mod_spec(layer, lambda i: 1 + i // per_b),
                _layer_spec(wb, layer), _layer_spec(wo, layer)]
    args = [branches, branches, branches, sg, x, mod, wb, wo]
    if final:
        in_specs.append(_full_spec(final_g))
        args.append(final_g)
    return pl.pallas_call(
        functools.partial(_merge_kernel, final),
        grid=(n // TM,),
        in_specs=in_specs,
        out_specs=row(D_MODEL),
        out_shape=jax.ShapeDtypeStruct((n, D_MODEL), F32),
        compiler_params=_cparams(),
        name="merge_s",
    )(*args)


def _dft_tables(T):
    def cs(n):
        kt = (np.arange(n)[:, None] * np.arange(n)[None, :]) % n
        ang = 2.0 * np.pi * kt.astype(np.float64) / n
        return np.cos(ang), np.sin(ang)
    ct, st = cs(T)
    cw, sw = cs(F_GROUP_W)
    cts = np.concatenate([ct, -st], axis=1).astype(np.float32)
    csw = np.concatenate([cw, sw], axis=1).astype(np.float32)
    return jnp.asarray(csw).astype(BF16), jnp.asarray(cts).astype(BF16)


def _rope_tables(T):
    half = D_ROPE // 2
    nfreq = half // 2
    inv = ROPE_BASE ** (-np.arange(nfreq, dtype=np.float64) / nfreq)
    t = np.arange(T)
    pos = np.stack([t // GRID_W, t % GRID_W], axis=0).astype(np.float64)
    cos = np.ones((T, LANE), np.float64)
    sa = np.zeros((T, LANE), np.float64)
    sb = np.zeros((T, LANE), np.float64)
    for part in range(2):
        ang = pos[part][:, None] * inv[None, :]
        l1 = ROPE_LANE0 + part * half
        l2 = l1 + nfreq
        cos[:, l1:l1 + nfreq] = np.cos(ang)
        cos[:, l2:l2 + nfreq] = np.cos(ang)
        sa[:, l1:l1 + nfreq] = -np.sin(ang)
        sb[:, l2:l2 + nfreq] = np.sin(ang)
    return tuple(jnp.asarray(a.astype(np.float32)) for a in (cos, sa, sb))


KR_SRC = 2176
W_IN_RUNS = ((0, 1024),
             (1024, 512), (2208, 512), (3232, 512),
             (3744, 3072),
             (1536, 384),
             (1920, 256),
             (KR_SRC - ROPE_LANE0, LANE),
             (2720, 512))
W_PREP_TILES = 6
KR_TILE = SEG_KR[0] // LANE


def _w_prep_kernel(tbl_ref, *refs):
    o_ref = refs[-1]
    j = pl.program_id(1)
    lane = lax.broadcasted_iota(jnp.int32, (D_MODEL, LANE), 1)
    rotary = (lane >= ROPE_LANE0) & (lane < ROPE_LANE0 + D_ROPE)
    for t, w_ref in enumerate(refs[:-1]):
        wt = w_ref[0].T
        if t == KR_TILE % W_PREP_TILES:
            wt = jnp.where((j != KR_TILE // W_PREP_TILES) | rotary, wt, 0.0)
        o_ref[:, t * LANE:(t + 1) * LANE] = wt.astype(BF16)


def _permute_w_in(w):
    depth, d, in_w = w.shape
    starts = np.concatenate([src + np.arange(0, width, LANE) for src, width in W_IN_RUNS])
    assert starts.size * LANE == IN_WP and starts.size % W_PREP_TILES == 0
    tbl = jnp.asarray(starts, jnp.int32)
    tile_spec = lambda t: pl.BlockSpec(
        (pl.Element(1), pl.Element(LANE), pl.Element(d)),
        lambda l, j, tbl_ref: (l, pl.multiple_of(tbl_ref[j * W_PREP_TILES + t], D_ROPE), 0))
    wide = W_PREP_TILES * LANE
    return pl.pallas_call(
        _w_prep_kernel,
        grid_spec=pltpu.PrefetchScalarGridSpec(
            num_scalar_prefetch=1,
            grid=(depth, IN_WP // wide),
            in_specs=[tile_spec(t) for t in range(W_PREP_TILES)],
            out_specs=pl.BlockSpec((None, d, wide), lambda l, j, tbl_ref: (l, 0, j)),
        ),
        out_shape=jax.ShapeDtypeStruct((depth, d, IN_WP), BF16),
        compiler_params=_cparams(2),
        name="w_prep",
    )(tbl, *([jnp.swapaxes(w, 1, 2)] * W_PREP_TILES))


def _pad_heads(w, n_heads, width, lo, hi):
    d, k = w.shape[:2]
    wh = w.reshape(d, k, n_heads, width)[..., lo:hi]
    wh = jnp.pad(wh, ((0, 0), (0, 0), (0, 0), (0, LANE - (hi - lo))))
    return wh.reshape(d, k, n_heads * LANE).astype(BF16)


def kernel(x_prompt, x_sample, cache_ckv, cache_krope, state_ret, c, c_ctx, norm_g, w_mod, b_mod,
           w_in, ret_decay_logit, q_norm_g, w_q_up, kv_norm_g, w_kv_up, w_branch, w_out,
           final_norm_g):
    cv = jnp.concatenate([c_ctx[None, :], c, jnp.zeros((MOD_ROWS - 1 - DEC_BATCH, D_MODEL), F32)], axis=0)
    mod = _mod_call(cv, w_mod, b_mod).reshape(DEPTH, MOD_ROWS, 1, 3 * D_MODEL)

    cache_kr = jnp.pad(cache_krope, ((0, 0), (0, 0), (0, 0), (ROPE_LANE0, LANE - ROPE_LANE0 - D_ROPE)))
    rope = _rope_tables(DEC_SEQ)
    dft_p = _dft_tables(SEQ)
    dft_s = _dft_tables(DEC_SEQ)
    logit = jnp.pad(ret_decay_logit, ((0, 0), (0, 8 - 2), (0, LANE - H_RET)))
    g, qg, kvg = norm_g[:, None, :], q_norm_g[:, None, :], kv_norm_g[:, None, :]
    w_p = _permute_w_in(w_in)
    wq = _pad_heads(w_q_up, H_MLA, D_NOPE + D_ROPE, 0, D_NOPE + D_ROPE)
    wk = _pad_heads(w_kv_up, H_MLA, D_NOPE + D_VMLA, 0, D_NOPE)
    wv = w_kv_up.reshape(DEPTH, KV_LORA, H_MLA, D_NOPE + D_VMLA)[..., D_NOPE:]
    wv = wv.reshape(DEPTH, KV_LORA, MLA_W).astype(BF16)
    wb = w_branch.astype(BF16)
    wo = w_out.astype(BF16)
    final_g = final_norm_g[None, :]

    xp = x_prompt.reshape(BATCH * SEQ, D_MODEL)
    xs = x_sample.reshape(DEC_BATCH * DEC_SEQ, D_MODEL)
    new_ctx = None
    for l in range(DEPTH):
        fg = final_g if l == DEPTH - 1 else None
        xp, *new_ctx = _prompt_call(l, xp, mod, g, qg, kvg, w_p, logit, wq, wk, wv, *dft_p, wb, wo,
                                    final_g=fg, prev=new_ctx)

        qkv, sz, sg, qn, ckv, kr, fu = _in_call(l, xs, mod, g, qg, kvg, w_p)
        branches = _mix_call(l, qkv, sz, fu, qn, ckv, kr, logit, state_ret, *dft_s, wq, wk, wv,
                             cache_ckv, cache_kr, *rope)
        xs = _merge_call(l, branches, sg, xs, mod, wb, wo, final_g=fg)

    y_prompt = xp.reshape(BATCH, SEQ, D_MODEL)
    y_sample = xs.reshape(DEC_BATCH, DEC_SEQ, D_MODEL)
    new_ckv, new_krope_t, new_ret = new_ctx
    return (y_prompt, y_sample, new_ckv, jnp.swapaxes(new_krope_t, 2, 3), new_ret)
```

```python
import collections
import functools
import math

import numpy as np
import jax
import jax.numpy as jnp
from jax import lax
from jax.experimental import pallas as pl
from jax.experimental.pallas import tpu as pltpu

F32 = jnp.float32
BF16 = jnp.bfloat16

D_MODEL = 1024
BATCH = 32
SEQ = 256
DEPTH = 2
DEC_BATCH = 4
DEC_SEQ = 1024
PAST_LEN = 512
GRID_W = 64
EPS = 1e-6
H_RET = 4
DK_RET = 64
DV_RET = 128
RET_W = H_RET * DV_RET
H_MLA = 8
Q_LORA = 384
KV_LORA = 256
D_NOPE = 64
D_ROPE = 32
D_VMLA = 64
MLA_W = H_MLA * D_VMLA
ROPE_BASE = 10000.0
F_GROUPS = 4
F_GROUP_W = 128
FOURIER_W = F_GROUPS * F_GROUP_W
N_BRANCH = 3
BRANCH_W = 512

LANE = 128
TM = 512
IN_CHAINS = 2
PROMPT_SPS = 2
SAMPLE_SPS = 2
MOD_ROWS = 8
ROPE_LANE0 = D_NOPE
VMEM_LIMIT = 60 * 1024 * 1024
ST_TAIL = (2, H_RET, DK_RET, DV_RET)

SEG_QKV = (0, 1024)
SEG_Z = (1024, 2560)
SEG_G = (2560, 5632)
SEG_QL = (5632, 6016)
SEG_KV = (6016, 6272)
SEG_KR = (6272, 6400)
SEG_FU = (6400, 6912)
IN_WP = 6912

NT = (((1,), (1,)), ((), ()))
TN = (((0,), (0,)), ((), ()))


def _cparams(n_axes=1):
    return pltpu.CompilerParams(dimension_semantics=("arbitrary",) * n_axes,
                                vmem_limit_bytes=VMEM_LIMIT)


def _layer_spec(a, layer):
    shape = a.shape[1:]
    return pl.BlockSpec((None,) + shape, lambda i: (layer,) + (0,) * len(shape))


def _full_spec(a):
    return pl.BlockSpec(a.shape, lambda i: (0,) * a.ndim)


def _mod_spec(layer, row):
    return pl.BlockSpec((None, None, 1, 3 * D_MODEL), lambda i: (layer, row(i), 0, 0))


def _grow_spec(depth, tail):
    return pl.BlockSpec((PROMPT_SPS, depth) + tail, lambda i: (i,) + (0,) * (1 + len(tail)))


def _dot(a, b):
    return jnp.dot(a, b, preferred_element_type=F32)


def _rms(x, g):
    return x * lax.rsqrt(jnp.mean(x * x, axis=-1, keepdims=True) + EPS) * g


def _mod_kernel(c_ref, w_ref, b_ref, o_ref):
    cv = c_ref[...]
    s = cv * jax.nn.sigmoid(cv)
    o_ref[0] = _dot(s.astype(BF16), w_ref[0].astype(BF16)) + b_ref[0]


def _mod_call(cv, w_mod, b_mod):
    nb = 3 * D_MODEL // 1024
    return pl.pallas_call(
        _mod_kernel,
        grid=(DEPTH, nb),
        in_specs=[
            pl.BlockSpec((MOD_ROWS, D_MODEL), lambda l, j: (0, 0)),
            pl.BlockSpec((1, D_MODEL, 1024), lambda l, j: (l, 0, j)),
            pl.BlockSpec((1, 1, 1024), lambda l, j: (l, 0, j)),
        ],
        out_specs=pl.BlockSpec((1, MOD_ROWS, 1024), lambda l, j: (l, 0, j)),
        out_shape=jax.ShapeDtypeStruct((DEPTH, MOD_ROWS, 3 * D_MODEL), F32),
        compiler_params=_cparams(2),
        name="mod",
    )(cv, w_mod, b_mod.reshape(DEPTH, 1, 3 * D_MODEL))


Proj = collections.namedtuple("Proj", "q k v sz sg qn ckv kr fu")


def _modulate(x, mod_ref, g_ref):
    shift = mod_ref[:, 0:D_MODEL]
    scale = mod_ref[:, D_MODEL:2 * D_MODEL]
    return (_rms(x, g_ref[...]) * (1.0 + scale) + shift).astype(BF16)


def _project(x, mod_ref, g_ref, qg_ref, kvg_ref, w_ref):
    return _project_h(_modulate(x, mod_ref, g_ref), qg_ref, kvg_ref, w_ref)


def _project_h(hb, qg_ref, kvg_ref, w_ref):
    def mm(a, b):
        return _dot(hb, w_ref[:, a:b])

    qw = H_RET * DK_RET
    p = mm(*SEG_QKV)
    q = p[:, 0:qw].astype(BF16)
    k = (p[:, qw:2 * qw] * (DK_RET ** -0.5)).astype(BF16)
    v = p[:, 2 * qw:].astype(BF16)
    sz = []
    for j in range(3):
        a = SEG_Z[0] + j * BRANCH_W
        p = mm(a, a + BRANCH_W)
        sz.append(p * jax.nn.sigmoid(p))
    sg = []
    for j in range(N_BRANCH):
        a = SEG_G[0] + j * D_MODEL
        sg.append(jax.nn.sigmoid(mm(a, a + D_MODEL)))
    qn = _rms(mm(*SEG_QL), qg_ref[...]).astype(BF16)
    ckv = _rms(mm(*SEG_KV), kvg_ref[...])
    kr = mm(*SEG_KR)
    fu = mm(*SEG_FU).astype(BF16)
    return Proj(q, k, v, sz, sg, qn, ckv, kr, fu)


RET_BLK = 256


def _init_masks(n_blk, lg, mask_ref):
    B = RET_BLK
    ii = lax.broadcasted_iota(jnp.int32, (B, B), 0)
    jj = lax.broadcasted_iota(jnp.int32, (B, B), 1)
    d = (ii - jj).astype(F32)
    ad = jnp.abs(d)
    for h in range(H_RET):
        lf = lg[0:1, h:h + 1]
        lb = lg[1:2, h:h + 1]
        mask_ref[h, 0] = jnp.where(d > 0, jnp.exp(ad * lf), jnp.where(d < 0, jnp.exp(ad * lb), 2.0))
        if n_blk > 1:
            mask_ref[h, 1] = jnp.exp((B + d) * lf)
            mask_ref[h, 2] = jnp.exp((B - d) * lb)


def _masked_scores(s, mask, lf, lb):
    B = RET_BLK
    n_blk = s.shape[0] // B
    if n_blk == 1:
        return (s * mask[0]).astype(BF16)
    rows = []
    for i in range(n_blk):
        cols = []
        for j in range(n_blk):
            blk = s[i * B:(i + 1) * B, j * B:(j + 1) * B]
            far = abs(i - j) - 1
            if far < 0:
                m = mask[0]
            else:
                m = mask[1] if j < i else mask[2]
                if far:
                    m = m * jnp.exp((far * B) * (lf if j < i else lb))
            cols.append((blk * m).astype(BF16))
        rows.append(jnp.concatenate(cols, axis=1))
    return jnp.concatenate(rows, axis=0)


def _ret_head(T, q, k, v, mask, lf, lb, st=None):
    s = lax.dot_general(q, k, NT, preferred_element_type=F32)
    o = _dot(_masked_scores(s, mask, lf, lb), v)
    if st is not None:
        rows = lax.broadcasted_iota(jnp.int32, (T, DV_RET), 0).astype(F32)
        o = o + jnp.exp((rows + 1.0) * lf) * _dot(q, st[0].astype(BF16))
        o = o + jnp.exp((T - rows) * lb) * _dot(q, st[1].astype(BF16))
    mu = jnp.mean(o, axis=-1, keepdims=True)
    c = o - mu
    var = jnp.mean(c * c, axis=-1, keepdims=True)
    return c * lax.rsqrt(var + EPS)


def _ret_states(T, k, v, lf, lb):
    rows = lax.broadcasted_iota(jnp.int32, (T, DK_RET), 0).astype(F32)
    kf = (k.astype(F32) * jnp.exp((T - 1.0 - rows) * lf)).astype(BF16)
    kb = (k.astype(F32) * jnp.exp(rows * lb)).astype(BF16)
    return (lax.dot_general(kf, v, TN, preferred_element_type=F32),
            lax.dot_general(kb, v, TN, preferred_element_type=F32))


def _rope(x, cos, sa, sb):
    half = D_ROPE // 4
    return x * cos + pltpu.roll(x, LANE - half, 1) * sa + pltpu.roll(x, half, 1) * sb


Q_SCALE = (D_NOPE + D_ROPE) ** -0.5 * math.log2(math.e)


def _value_slots(vp):
    first = lax.broadcasted_iota(jnp.int32, vp.shape, 1) < D_VMLA
    return (jnp.where(first, vp, 1.0).astype(BF16), jnp.where(first, 1.0, vp).astype(BF16))


def _attend_pair(q2, k2, v2):
    outs = []
    for qh, kh, vh in zip(q2, k2, v2):
        s = lax.dot_general(qh, kh, NT, preferred_element_type=F32)
        e = jnp.exp2((s - jnp.max(s, axis=-1, keepdims=True)).astype(BF16))
        outs.append(_dot(e, vh))
    first = lax.broadcasted_iota(jnp.int32, outs[0].shape, 1) < D_VMLA
    o = jnp.where(first, outs[0], outs[1])
    l = jnp.where(first, pltpu.roll(outs[0], D_VMLA, 1), pltpu.roll(outs[1], D_VMLA, 1))
    return o / l


def _fourier(T, fu, csw_ref, cts_ref):
    a, b = [], []
    for g in range(F_GROUPS):
        z = _dot(fu[:, g * F_GROUP_W:(g + 1) * F_GROUP_W], csw_ref[...])
        a.append(z[:, :F_GROUP_W])
        b.append(z[:, F_GROUP_W:])
    a = jnp.concatenate(a, axis=1).astype(BF16)
    b = jnp.concatenate(b, axis=1).astype(BF16)
    o = _dot(cts_ref[:, 0:T], a) + _dot(cts_ref[:, T:2 * T], b)
    return o * ((T * F_GROUP_W) ** -0.5)


def _merge(branches, sg, x, mod_ref, wb_ref, wo_ref, fg_ref):
    merged = None
    for n, b in enumerate(branches):
        term = _dot(b, wb_ref[n]) * sg[n]
        merged = term if merged is None else merged + term
    gate = mod_ref[:, 2 * D_MODEL:3 * D_MODEL]
    y = x + gate * _dot(merged.astype(BF16), wo_ref[...])
    if fg_ref is not None:
        y = _rms(y, fg_ref[...])
    return y


def _prompt_kernel(layer, final, *refs):
    (x_ref, mod_ref, g_ref, qg_ref, kvg_ref, w_ref, lg_ref, wq_ref, wk_ref, wv_ref,
     csw_ref, cts_ref, wb_ref, wo_ref) = refs[:14]
    n_in = 14
    fg_ref = None
    if final:
        fg_ref = refs[n_in]
        n_in += 1
    if layer:
        pckv_ref, pkr_ref, pst_ref = refs[n_in:n_in + 3]
        n_in += 3
    y_ref, nckv_ref, nkr_ref, nst_ref, mask_ref = refs[n_in:]
    T = SEQ
    lg = jax.nn.log_sigmoid(lg_ref[...])

    @pl.when(pl.program_id(0) == 0)
    def _():
        _init_masks(T // RET_BLK, lg, mask_ref)

    seqs = range(PROMPT_SPS)
    rows = [slice(s * T, (s + 1) * T) for s in seqs]
    slot = lambda a, i: a[:, i * LANE:(i + 1) * LANE]
    xs = [x_ref[r] for r in rows]
    prs = [_project(x, mod_ref, g_ref, qg_ref, kvg_ref, w_ref) for x in xs]

    for s, pr in zip(seqs, prs):
        if layer:
            nckv_ref[s, 0:layer] = pckv_ref[s]
            nkr_ref[s, 0:layer] = pkr_ref[s]
            nst_ref[s, 0:layer] = pst_ref[s]
        nckv_ref[s, layer] = pr.ckv
        nkr_ref[s, layer] = pr.kr.T[ROPE_LANE0:ROPE_LANE0 + D_ROPE]

    qs, ks, vs = [], [], []
    for pr in prs:
        ckvb = pr.ckv.astype(BF16)
        q_all = _dot(pr.qn, wq_ref[...]) * Q_SCALE
        k_all = _dot(ckvb, wk_ref[...])
        vs.append(_dot(ckvb, wv_ref[...]))
        qs.append([slot(q_all, h).astype(BF16) for h in range(H_MLA)])
        ks.append([(slot(k_all, h) + pr.kr).astype(BF16) for h in range(H_MLA)])
    o_m = [[] for _ in seqs]
    for p in range(H_MLA // 2):
        for s in seqs:
            o_m[s].append(_attend_pair(qs[s][2 * p:2 * p + 2], ks[s][2 * p:2 * p + 2],
                                       _value_slots(slot(vs[s], p))))

    o_f = [_fourier(T, pr.fu, csw_ref, cts_ref) for pr in prs]

    o_r = [[] for _ in seqs]
    for h in range(H_RET):
        lf = lg[0:1, h:h + 1]
        lb = lg[1:2, h:h + 1]
        for s, pr in zip(seqs, prs):
            q = pr.q[:, h * DK_RET:(h + 1) * DK_RET]
            k = pr.k[:, h * DK_RET:(h + 1) * DK_RET]
            v = pr.v[:, h * DV_RET:(h + 1) * DV_RET]
            o_r[s].append(_ret_head(T, q, k, v, mask_ref.at[h], lf, lb))
            sf, sb = _ret_states(T, k, v, lf, lb)
            nst_ref[s, layer, 0, h] = sf
            nst_ref[s, layer, 1, h] = sb

    for s, pr in zip(seqs, prs):
        branches = [(o * gate).astype(BF16) for o, gate in zip(
            (jnp.concatenate(o_r[s], axis=1), jnp.concatenate(o_m[s], axis=1), o_f[s]), pr.sz)]
        y_ref[rows[s]] = _merge(branches, pr.sg, xs[s], mod_ref, wb_ref, wo_ref, fg_ref)


def _prompt_call(layer, x, mod, g, qg, kvg, w_p, logit, wq, wk, wv, csw, cts, wb, wo,
                 final_g=None, prev=None):
    T = SEQ
    n_steps = BATCH // PROMPT_SPS
    row = pl.BlockSpec((PROMPT_SPS * T, D_MODEL), lambda i: (i, 0))
    in_specs = [row, _mod_spec(layer, lambda i: 0)]
    in_specs += [_layer_spec(a, layer) for a in (g, qg, kvg, w_p, logit, wq, wk, wv)]
    in_specs += [_full_spec(csw), _full_spec(cts), _layer_spec(wb, layer), _layer_spec(wo, layer)]
    args = [x, mod, g, qg, kvg, w_p, logit, wq, wk, wv, csw, cts, wb, wo]
    if final_g is not None:
        in_specs.append(_full_spec(final_g))
        args.append(final_g)
    tails = ((SEQ, KV_LORA), (D_ROPE, SEQ), ST_TAIL)
    if layer:
        in_specs += [_grow_spec(layer, t) for t in tails]
        args += list(prev)
    return pl.pallas_call(
        functools.partial(_prompt_kernel, layer, final_g is not None),
        grid=(n_steps,),
        in_specs=in_specs,
        out_specs=[row] + [_grow_spec(layer + 1, t) for t in tails],
        out_shape=[jax.ShapeDtypeStruct((BATCH * T, D_MODEL), F32)]
        + [jax.ShapeDtypeStruct((BATCH, layer + 1) + t, F32) for t in tails],
        scratch_shapes=[pltpu.VMEM((H_RET, 1, RET_BLK, RET_BLK), F32)],
        compiler_params=_cparams(),
        name="prompt_layer",
    )(*args)


def _in_kernel(x_ref, mod_ref, g_ref, qg_ref, kvg_ref, w_ref,
               qkv_ref, sz_ref, sg_ref, qn_ref, ckv_ref, kr_ref, fu_ref):
    qw = H_RET * DK_RET
    half = TM // IN_CHAINS
    for c in range(IN_CHAINS):
        rows = slice(c * half, (c + 1) * half)
        pr = _project(x_ref[rows], mod_ref, g_ref, qg_ref, kvg_ref, w_ref)
        qkv_ref[rows, 0:qw] = pr.q
        qkv_ref[rows, qw:2 * qw] = pr.k
        qkv_ref[rows, 2 * qw:] = pr.v
        for j in range(3):
            sz_ref[rows, j * BRANCH_W:(j + 1) * BRANCH_W] = pr.sz[j].astype(BF16)
        for j in range(N_BRANCH):
            sg_ref[rows, j * D_MODEL:(j + 1) * D_MODEL] = pr.sg[j].astype(BF16)
        qn_ref[rows] = pr.qn
        ckv_ref[rows] = pr.ckv.astype(BF16)
        kr_ref[rows] = pr.kr
        fu_ref[rows] = pr.fu


def _in_call(layer, x, mod, g, qg, kvg, w_p):
    n = x.shape[0]
    per_b = DEC_SEQ // TM
    row = lambda w: pl.BlockSpec((TM, w), lambda i: (i, 0))
    outs = [(1024, BF16), (3 * BRANCH_W, BF16), (N_BRANCH * D_MODEL, BF16), (Q_LORA, BF16),
            (KV_LORA, BF16), (LANE, F32), (FOURIER_W, BF16)]
    return pl.pallas_call(
        _in_kernel,
        grid=(n // TM,),
        in_specs=[row(D_MODEL), _mod_spec(layer, lambda i: 1 + i // per_b)]
        + [_layer_spec(a, layer) for a in (g, qg, kvg, w_p)],
        out_specs=[row(w) for w, _ in outs],
        out_shape=[jax.ShapeDtypeStruct((n, w), dt) for w, dt in outs],
        compiler_params=_cparams(),
        name="in_proj_s",
    )(x, mod, g, qg, kvg, w_p)


def _sample_rows(width, col=0):
    return pl.BlockSpec((SAMPLE_SPS * DEC_SEQ, width), lambda b: (b, col))


def _sample_cached(layer, tail):
    return pl.BlockSpec((SAMPLE_SPS, None) + tail, lambda b: (b, layer) + (0,) * len(tail))


def _ret_fourier_kernel(T, qkv_ref, rz_ref, fz_ref, fu_ref, lg_ref, st_ref, csw_ref, cts_ref,
                        o_ref, mask_ref):
    lg = jax.nn.log_sigmoid(lg_ref[...])

    @pl.when(pl.program_id(0) == 0)
    def _():
        _init_masks(T // RET_BLK, lg, mask_ref)

    qw = H_RET * DK_RET
    for s in range(SAMPLE_SPS):
        rows = slice(s * T, (s + 1) * T)
        for h in range(H_RET):
            q = qkv_ref[rows, h * DK_RET:(h + 1) * DK_RET]
            k = qkv_ref[rows, qw + h * DK_RET:qw + (h + 1) * DK_RET]
            v = qkv_ref[rows, 2 * qw + h * DV_RET:2 * qw + (h + 1) * DV_RET]
            on = _ret_head(T, q, k, v, mask_ref.at[h], lg[0:1, h:h + 1], lg[1:2, h:h + 1],
                           st=(st_ref[s, 0, h], st_ref[s, 1, h]))
            gate = rz_ref[rows, h * DV_RET:(h + 1) * DV_RET].astype(F32)
            o_ref[rows, h * DV_RET:(h + 1) * DV_RET] = (on * gate).astype(BF16)
        o = _fourier(T, fu_ref[rows], csw_ref, cts_ref)
        o_ref[rows, RET_W:] = (o * fz_ref[rows].astype(F32)).astype(BF16)


def _ret_fourier_call(layer, qkv, sz, fu, logit, state, csw, cts):
    T = DEC_SEQ
    return pl.pallas_call(
        functools.partial(_ret_fourier_kernel, T),
        grid=(DEC_BATCH // SAMPLE_SPS,),
        in_specs=[_sample_rows(1024), _sample_rows(BRANCH_W, 0), _sample_rows(BRANCH_W, 2),
                  _sample_rows(FOURIER_W), _layer_spec(logit, layer),
                  _sample_cached(layer, ST_TAIL), _full_spec(csw), _full_spec(cts)],
        out_specs=_sample_rows(RET_W + FOURIER_W),
        out_shape=jax.ShapeDtypeStruct((T * DEC_BATCH, RET_W + FOURIER_W), BF16),
        scratch_shapes=[pltpu.VMEM((H_RET, 3, RET_BLK, RET_BLK), F32)],
        compiler_params=_cparams(),
        name="ret_fourier_s",
    )(qkv, sz, sz, fu, logit, state, csw, cts)


def _mla_kernel(T, qn_ref, ckv_ref, kr_ref, mz_ref, wq_ref, wk_ref, wv_ref, cckv_ref, ckr_ref,
                cos_ref, sa_ref, sb_ref, bm_ref):
    cos, sa, sb = cos_ref[...], sa_ref[...], sb_ref[...]
    slot = lambda a, i: a[:, i * LANE:(i + 1) * LANE]
    for s in range(SAMPLE_SPS):
        rows = slice(s * T, (s + 1) * T)
        qn = qn_ref[rows]
        ckv = ckv_ref[rows]
        kr = _rope(kr_ref[rows], cos, sa, sb)
        cc = cckv_ref[s].astype(BF16)
        for p in range(H_MLA // 2):
            sl = slice(2 * p * LANE, 2 * (p + 1) * LANE)
            q2 = _dot(qn, wq_ref[:, sl])
            kc2 = _dot(cc, wk_ref[:, sl])
            kn2 = _dot(ckv, wk_ref[:, sl])
            if p % 2 == 0:
                vsl = slice(p * LANE, (p + 2) * LANE)
                v4 = jnp.concatenate([_dot(cc, wv_ref[:, vsl]), _dot(ckv, wv_ref[:, vsl])], axis=0)
            qs = [(_rope(slot(q2, i), cos, sa, sb) * Q_SCALE).astype(BF16) for i in range(2)]
            ks = [jnp.concatenate([(slot(kc2, i) + ckr_ref[s]).astype(BF16),
                                   (slot(kn2, i) + kr).astype(BF16)], axis=0) for i in range(2)]
            o = _attend_pair(qs, ks, _value_slots(slot(v4, p % 2)))
            gate = mz_ref[rows, p * LANE:(p + 1) * LANE].astype(F32)
            bm_ref[rows, p * LANE:(p + 1) * LANE] = (o * gate).astype(BF16)


def _mla_call(layer, qn, ckv, kr, sz, wq, wk, wv, cache_ckv, cache_kr, cos, sa, sb):
    T = DEC_SEQ
    return pl.pallas_call(
        functools.partial(_mla_kernel, T),
        grid=(DEC_BATCH // SAMPLE_SPS,),
        in_specs=[_sample_rows(Q_LORA), _sample_rows(KV_LORA), _sample_rows(LANE),
                  _sample_rows(BRANCH_W, 1)]
        + [_layer_spec(a, layer) for a in (wq, wk, wv)]
        + [_sample_cached(layer, (PAST_LEN, KV_LORA)), _sample_cached(layer, (PAST_LEN, LANE)),
           _full_spec(cos), _full_spec(sa), _full_spec(sb)],
        out_specs=_sample_rows(MLA_W),
        out_shape=jax.ShapeDtypeStruct((T * DEC_BATCH, MLA_W), BF16),
        compiler_params=_cparams(),
        name="mla_s",
    )(qn, ckv, kr, sz, wq, wk, wv, cache_ckv, cache_kr, cos, sa, sb)


def _merge_kernel(final, *refs):
    br_ref, bm_ref, bf_ref, sg_ref, x_ref, mod_ref, wb_ref, wo_ref = refs[:8]
    fg_ref = refs[8] if final else None
    sg = [sg_ref[:, n * D_MODEL:(n + 1) * D_MODEL].astype(F32) for n in range(N_BRANCH)]
    refs[-1][...] = _merge((br_ref[...], bm_ref[...], bf_ref[...]), sg, x_ref[...],
                           mod_ref, wb_ref, wo_ref, fg_ref)


def _merge_call(layer, brf, bm, sg, x, mod, wb, wo, final_g=None):
    final = final_g is not None
    n = x.shape[0]
    per_b = DEC_SEQ // TM
    row = lambda w, c=0: pl.BlockSpec((TM, w), lambda i: (i, c))
    in_specs = [row(BRANCH_W), row(BRANCH_W), row(BRANCH_W, 1), row(N_BRANCH * D_MODEL), row(D_MODEL),
                _mod_spec(layer, lambda i: 1 + i // per_b),
                _layer_spec(wb, layer), _layer_spec(wo, layer)]
    args = [brf, bm, brf, sg, x, mod, wb, wo]
    if final:
        in_specs.append(_full_spec(final_g))
        args.append(final_g)
    return pl.pallas_call(
        functools.partial(_merge_kernel, final),
        grid=(n // TM,),
        in_specs=in_specs,
        out_specs=row(D_MODEL),
        out_shape=jax.ShapeDtypeStruct((n, D_MODEL), F32),
        compiler_params=_cparams(),
        name="merge_s",
    )(*args)


def _dft_tables(T):
    def cs(n):
        kt = (np.arange(n)[:, None] * np.arange(n)[None, :]) % n
        ang = 2.0 * np.pi * kt.astype(np.float64) / n
        return np.cos(ang), np.sin(ang)
    ct, st = cs(T)
    cw, sw = cs(F_GROUP_W)
    cts = np.concatenate([ct, -st], axis=1).astype(np.float32)
    csw = np.concatenate([cw, sw], axis=1).astype(np.float32)
    return jnp.asarray(csw).astype(BF16), jnp.asarray(cts).astype(BF16)


def _rope_tables(T):
    half = D_ROPE // 2
    nfreq = half // 2
    inv = ROPE_BASE ** (-np.arange(nfreq, dtype=np.float64) / nfreq)
    t = np.arange(T)
    pos = np.stack([t // GRID_W, t % GRID_W], axis=0).astype(np.float64)
    cos = np.ones((T, LANE), np.float64)
    sa = np.zeros((T, LANE), np.float64)
    sb = np.zeros((T, LANE), np.float64)
    for part in range(2):
        ang = pos[part][:, None] * inv[None, :]
        l1 = ROPE_LANE0 + part * half
        l2 = l1 + nfreq
        cos[:, l1:l1 + nfreq] = np.cos(ang)
        cos[:, l2:l2 + nfreq] = np.cos(ang)
        sa[:, l1:l1 + nfreq] = -np.sin(ang)
        sb[:, l2:l2 + nfreq] = np.sin(ang)
    return tuple(jnp.asarray(a.astype(np.float32)) for a in (cos, sa, sb))


KR_SRC = 2176
W_IN_RUNS = ((0, 1024),
             (1024, 512), (2208, 512), (3232, 512),
             (3744, 3072),
             (1536, 384),
             (1920, 256),
             (KR_SRC - ROPE_LANE0, LANE),
             (2720, 512))
W_PREP_TILES = 6
KR_TILE = SEG_KR[0] // LANE


def _w_prep_kernel(tbl_ref, *refs):
    o_ref = refs[-1]
    j = pl.program_id(1)
    lane = lax.broadcasted_iota(jnp.int32, (D_MODEL, LANE), 1)
    rotary = (lane >= ROPE_LANE0) & (lane < ROPE_LANE0 + D_ROPE)
    for t, w_ref in enumerate(refs[:-1]):
        wt = w_ref[0].T
        if t == KR_TILE % W_PREP_TILES:
            wt = jnp.where((j != KR_TILE // W_PREP_TILES) | rotary, wt, 0.0)
        o_ref[:, t * LANE:(t + 1) * LANE] = wt.astype(BF16)


def _permute_w_in(w):
    depth, d, in_w = w.shape
    starts = np.concatenate([src + np.arange(0, width, LANE) for src, width in W_IN_RUNS])
    assert starts.size * LANE == IN_WP and starts.size % W_PREP_TILES == 0
    tbl = jnp.asarray(starts, jnp.int32)
    tile_spec = lambda t: pl.BlockSpec(
        (pl.Element(1), pl.Element(LANE), pl.Element(d)),
        lambda l, j, tbl_ref: (l, pl.multiple_of(tbl_ref[j * W_PREP_TILES + t], D_ROPE), 0))
    wide = W_PREP_TILES * LANE
    return pl.pallas_call(
        _w_prep_kernel,
        grid_spec=pltpu.PrefetchScalarGridSpec(
            num_scalar_prefetch=1,
            grid=(depth, IN_WP // wide),
            in_specs=[tile_spec(t) for t in range(W_PREP_TILES)],
            out_specs=pl.BlockSpec((None, d, wide), lambda l, j, tbl_ref: (l, 0, j)),
        ),
        out_shape=jax.ShapeDtypeStruct((depth, d, IN_WP), BF16),
        compiler_params=_cparams(2),
        name="w_prep",
    )(tbl, *([jnp.swapaxes(w, 1, 2)] * W_PREP_TILES))


def _pad_heads(w, n_heads, width, lo, hi):
    d, k = w.shape[:2]
    wh = w.reshape(d, k, n_heads, width)[..., lo:hi]
    wh = jnp.pad(wh, ((0, 0), (0, 0), (0, 0), (0, LANE - (hi - lo))))
    return wh.reshape(d, k, n_heads * LANE).astype(BF16)


def kernel(x_prompt, x_sample, cache_ckv, cache_krope, state_ret, c, c_ctx, norm_g, w_mod, b_mod,
           w_in, ret_decay_logit, q_norm_g, w_q_up, kv_norm_g, w_kv_up, w_branch, w_out,
           final_norm_g):
    cv = jnp.concatenate([c_ctx[None, :], c, jnp.zeros((MOD_ROWS - 1 - DEC_BATCH, D_MODEL), F32)], axis=0)
    mod = _mod_call(cv, w_mod, b_mod).reshape(DEPTH, MOD_ROWS, 1, 3 * D_MODEL)

    cache_kr = jnp.pad(cache_krope, ((0, 0), (0, 0), (0, 0), (ROPE_LANE0, LANE - ROPE_LANE0 - D_ROPE)))
    rope = _rope_tables(DEC_SEQ)
    dft_p = _dft_tables(SEQ)
    dft_s = _dft_tables(DEC_SEQ)
    logit = jnp.pad(ret_decay_logit, ((0, 0), (0, 8 - 2), (0, LANE - H_RET)))
    g, qg, kvg = norm_g[:, None, :], q_norm_g[:, None, :], kv_norm_g[:, None, :]
    w_p = _permute_w_in(w_in)
    wq = _pad_heads(w_q_up, H_MLA, D_NOPE + D_ROPE, 0, D_NOPE + D_ROPE)
    wk = _pad_heads(w_kv_up, H_MLA, D_NOPE + D_VMLA, 0, D_NOPE)
    wv = w_kv_up.reshape(DEPTH, KV_LORA, H_MLA, D_NOPE + D_VMLA)[..., D_NOPE:]
    wv = wv.reshape(DEPTH, KV_LORA, MLA_W).astype(BF16)
    wb = w_branch.astype(BF16)
    wo = w_out.astype(BF16)
    final_g = final_norm_g[None, :]

    xp = x_prompt.reshape(BATCH * SEQ, D_MODEL)
    xs = x_sample.reshape(DEC_BATCH * DEC_SEQ, D_MODEL)
    new_ctx = None
    for l in range(DEPTH):
        fg = final_g if l == DEPTH - 1 else None
        xp, *new_ctx = _prompt_call(l, xp, mod, g, qg, kvg, w_p, logit, wq, wk, wv, *dft_p, wb, wo,
                                    final_g=fg, prev=new_ctx)

        qkv, sz, sg, qn, ckv, kr, fu = _in_call(l, xs, mod, g, qg, kvg, w_p)
        brf = _ret_fourier_call(l, qkv, sz, fu, logit, state_ret, *dft_s)
        bm = _mla_call(l, qn, ckv, kr, sz, wq, wk, wv, cache_ckv, cache_kr, *rope)
        xs = _merge_call(l, brf, bm, sg, xs, mod, wb, wo, final_g=fg)

    y_prompt = xp.reshape(BATCH, SEQ, D_MODEL)
    y_sample = xs.reshape(DEC_BATCH, DEC_SEQ, D_MODEL)
    new_ckv, new_krope_t, new_ret = new_ctx
    return (y_prompt, y_sample, new_ckv, jnp.swapaxes(new_krope_t, 2, 3), new_ret)
```

```python
import collections
import functools
import math

import numpy as np
import jax
import jax.numpy as jnp
from jax import lax
from jax.experimental import pallas as pl
from jax.experimental.pallas import tpu as pltpu

F32 = jnp.float32
BF16 = jnp.bfloat16

D_MODEL = 1024
BATCH = 32
SEQ = 256
DEPTH = 2
DEC_BATCH = 4
DEC_SEQ = 1024
PAST_LEN = 512
GRID_W = 64
EPS = 1e-6
H_RET = 4
DK_RET = 64
DV_RET = 128
RET_W = H_RET * DV_RET
H_MLA = 8
Q_LORA = 384
KV_LORA = 256
D_NOPE = 64
D_ROPE = 32
D_VMLA = 64
MLA_W = H_MLA * D_VMLA
ROPE_BASE = 10000.0
F_GROUPS = 4
F_GROUP_W = 128
FOURIER_W = F_GROUPS * F_GROUP_W
N_BRANCH = 3
BRANCH_W = 512

LANE = 128
TM = 512
IN_CHAINS = 2
PROMPT_SPS = 2
SAMPLE_SPS = 1
MOD_ROWS = 8
ROPE_LANE0 = D_NOPE
VMEM_LIMIT = 60 * 1024 * 1024
ST_TAIL = (2, H_RET, DK_RET, DV_RET)

SEG_QKV = (0, 1024)
SEG_Z = (1024, 2560)
SEG_G = (2560, 5632)
SEG_QL = (5632, 6016)
SEG_KR = (6016, 6144)
SEG_KV = (6144, 6400)
SEG_FU = (6400, 6912)
IN_WP = 6912

NT = (((1,), (1,)), ((), ()))
TN = (((0,), (0,)), ((), ()))


def _cparams(n_axes=1):
    return pltpu.CompilerParams(dimension_semantics=("arbitrary",) * n_axes,
                                vmem_limit_bytes=VMEM_LIMIT)


def _layer_spec(a, layer):
    shape = a.shape[1:]
    return pl.BlockSpec((None,) + shape, lambda i: (layer,) + (0,) * len(shape))


def _full_spec(a):
    return pl.BlockSpec(a.shape, lambda i: (0,) * a.ndim)


def _mod_spec(layer, row):
    return pl.BlockSpec((None, None, 1, 3 * D_MODEL), lambda i: (layer, row(i), 0, 0))


def _grow_spec(depth, tail):
    return pl.BlockSpec((PROMPT_SPS, depth) + tail, lambda i: (i,) + (0,) * (1 + len(tail)))


def _dot(a, b):
    return jnp.dot(a, b, preferred_element_type=F32)


def _rms(x, g):
    return x * lax.rsqrt(jnp.mean(x * x, axis=-1, keepdims=True) + EPS) * g


def _mod_kernel(c_ref, w_ref, b_ref, o_ref):
    cv = c_ref[...]
    s = cv * jax.nn.sigmoid(cv)
    o_ref[0] = _dot(s.astype(BF16), w_ref[0].astype(BF16)) + b_ref[0]


def _mod_call(cv, w_mod, b_mod):
    nb = 3 * D_MODEL // 1024
    return pl.pallas_call(
        _mod_kernel,
        grid=(DEPTH, nb),
        in_specs=[
            pl.BlockSpec((MOD_ROWS, D_MODEL), lambda l, j: (0, 0)),
            pl.BlockSpec((1, D_MODEL, 1024), lambda l, j: (l, 0, j)),
            pl.BlockSpec((1, 1, 1024), lambda l, j: (l, 0, j)),
        ],
        out_specs=pl.BlockSpec((1, MOD_ROWS, 1024), lambda l, j: (l, 0, j)),
        out_shape=jax.ShapeDtypeStruct((DEPTH, MOD_ROWS, 3 * D_MODEL), F32),
        compiler_params=_cparams(2),
        name="mod",
    )(cv, w_mod, b_mod.reshape(DEPTH, 1, 3 * D_MODEL))


Proj = collections.namedtuple("Proj", "q k v sz sg qn ckv kr fu")


def _modulate(x, mod_ref, g_ref):
    shift = mod_ref[:, 0:D_MODEL]
    scale = mod_ref[:, D_MODEL:2 * D_MODEL]
    return (_rms(x, g_ref[...]) * (1.0 + scale) + shift).astype(BF16)


def _project(x, mod_ref, g_ref, qg_ref, kvg_ref, w_ref):
    return _project_h(_modulate(x, mod_ref, g_ref), qg_ref, kvg_ref, w_ref)


def _project_h(hb, qg_ref, kvg_ref, w_ref):
    def mm(a, b):
        return _dot(hb, w_ref[:, a:b])

    qw = H_RET * DK_RET
    p = mm(*SEG_QKV)
    q = p[:, 0:qw].astype(BF16)
    k = (p[:, qw:2 * qw] * (DK_RET ** -0.5)).astype(BF16)
    v = p[:, 2 * qw:].astype(BF16)
    sz = []
    for j in range(3):
        a = SEG_Z[0] + j * BRANCH_W
        p = mm(a, a + BRANCH_W)
        sz.append(p * jax.nn.sigmoid(p))
    sg = []
    for j in range(N_BRANCH):
        a = SEG_G[0] + j * D_MODEL
        sg.append(jax.nn.sigmoid(mm(a, a + D_MODEL)))
    p = mm(SEG_QL[0], SEG_KR[1])
    qn = _rms(p[:, :Q_LORA], qg_ref[...]).astype(BF16)
    kr = p[:, Q_LORA:]
    ckv = _rms(mm(*SEG_KV), kvg_ref[...])
    fu = mm(*SEG_FU).astype(BF16)
    return Proj(q, k, v, sz, sg, qn, ckv, kr, fu)


RET_BLK = 256


def _init_masks(n_blk, lg, mask_ref):
    B = RET_BLK
    ii = lax.broadcasted_iota(jnp.int32, (B, B), 0)
    jj = lax.broadcasted_iota(jnp.int32, (B, B), 1)
    d = (ii - jj).astype(F32)
    ad = jnp.abs(d)
    for h in range(H_RET):
        lf = lg[0:1, h:h + 1]
        lb = lg[1:2, h:h + 1]
        mask_ref[h, 0] = jnp.where(d > 0, jnp.exp(ad * lf), jnp.where(d < 0, jnp.exp(ad * lb), 2.0))
        if n_blk > 1:
            mask_ref[h, 1] = jnp.exp((B + d) * lf)
            mask_ref[h, 2] = jnp.exp((B - d) * lb)


def _masked_scores(s, mask, lf, lb):
    B = RET_BLK
    n_blk = s.shape[0] // B
    if n_blk == 1:
        return (s * mask[0]).astype(BF16)
    rows = []
    for i in range(n_blk):
        cols = []
        for j in range(n_blk):
            blk = s[i * B:(i + 1) * B, j * B:(j + 1) * B]
            far = abs(i - j) - 1
            if far < 0:
                m = mask[0]
            else:
                m = mask[1] if j < i else mask[2]
                if far:
                    m = m * jnp.exp((far * B) * (lf if j < i else lb))
            cols.append((blk * m).astype(BF16))
        rows.append(jnp.concatenate(cols, axis=1))
    return jnp.concatenate(rows, axis=0)


def _ret_head(T, q, k, v, mask, lf, lb, st=None):
    s = lax.dot_general(q, k, NT, preferred_element_type=F32)
    o = _dot(_masked_scores(s, mask, lf, lb), v)
    if st is not None:
        c = _dot(q, jnp.concatenate([st[0], st[1]], axis=1).astype(BF16))
        rows = lax.broadcasted_iota(jnp.int32, (T, DV_RET), 0).astype(F32)
        o = o + jnp.exp((rows + 1.0) * lf) * c[:, :DV_RET]
        o = o + jnp.exp((T - rows) * lb) * c[:, DV_RET:]
    mu = jnp.mean(o, axis=-1, keepdims=True)
    c = o - mu
    var = jnp.mean(c * c, axis=-1, keepdims=True)
    return c * lax.rsqrt(var + EPS)


def _ret_states(T, k, v, lf, lb):
    rows = lax.broadcasted_iota(jnp.int32, (T, DK_RET), 0).astype(F32)
    kf = (k.astype(F32) * jnp.exp((T - 1.0 - rows) * lf)).astype(BF16)
    kb = (k.astype(F32) * jnp.exp(rows * lb)).astype(BF16)
    return (lax.dot_general(kf, v, TN, preferred_element_type=F32),
            lax.dot_general(kb, v, TN, preferred_element_type=F32))


def _rope(x, cos, sa, sb):
    half = D_ROPE // 4
    return x * cos + pltpu.roll(x, LANE - half, 1) * sa + pltpu.roll(x, half, 1) * sb


Q_SCALE = (D_NOPE + D_ROPE) ** -0.5 * math.log2(math.e)


def _value_slots(vp):
    first = lax.broadcasted_iota(jnp.int32, vp.shape, 1) < D_VMLA
    return (jnp.where(first, vp, 1.0).astype(BF16), jnp.where(first, 1.0, vp).astype(BF16))


def _attend_pair(q2, k2, v2):
    outs = []
    for qh, kh, vh in zip(q2, k2, v2):
        s = lax.dot_general(qh, kh, NT, preferred_element_type=F32)
        e = jnp.exp2((s - jnp.max(s, axis=-1, keepdims=True)).astype(BF16))
        outs.append(_dot(e, vh))
    first = lax.broadcasted_iota(jnp.int32, outs[0].shape, 1) < D_VMLA
    o = jnp.where(first, outs[0], outs[1])
    l = jnp.where(first, pltpu.roll(outs[0], D_VMLA, 1), pltpu.roll(outs[1], D_VMLA, 1))
    return o / l


def _fourier(T, fu, csw_ref, cts_ref):
    a, b = [], []
    for g in range(F_GROUPS):
        z = _dot(fu[:, g * F_GROUP_W:(g + 1) * F_GROUP_W], csw_ref[...])
        a.append(z[:, :F_GROUP_W])
        b.append(z[:, F_GROUP_W:])
    a = jnp.concatenate(a, axis=1).astype(BF16)
    b = jnp.concatenate(b, axis=1).astype(BF16)
    o = _dot(cts_ref[:, 0:T], a) + _dot(cts_ref[:, T:2 * T], b)
    return o * ((T * F_GROUP_W) ** -0.5)


def _merge(branches, sg, x, mod_ref, wb_ref, wo_ref, fg_ref):
    merged = None
    for n, b in enumerate(branches):
        term = _dot(b, wb_ref[n]) * sg[n]
        merged = term if merged is None else merged + term
    gate = mod_ref[:, 2 * D_MODEL:3 * D_MODEL]
    y = x + gate * _dot(merged.astype(BF16), wo_ref[...])
    if fg_ref is not None:
        y = _rms(y, fg_ref[...])
    return y


def _prompt_kernel(layer, final, *refs):
    (x_ref, mod_ref, g_ref, qg_ref, kvg_ref, w_ref, lg_ref, wq_ref, wk_ref, wv_ref,
     csw_ref, cts_ref, wb_ref, wo_ref) = refs[:14]
    n_in = 14
    fg_ref = None
    if final:
        fg_ref = refs[n_in]
        n_in += 1
    if layer:
        pckv_ref, pkr_ref, pst_ref = refs[n_in:n_in + 3]
        n_in += 3
    y_ref, nckv_ref, nkr_ref, nst_ref, mask_ref = refs[n_in:]
    T = SEQ
    lg = jax.nn.log_sigmoid(lg_ref[...])

    @pl.when(pl.program_id(0) == 0)
    def _():
        _init_masks(T // RET_BLK, lg, mask_ref)

    seqs = range(PROMPT_SPS)
    rows = [slice(s * T, (s + 1) * T) for s in seqs]
    slot = lambda a, i: a[:, i * LANE:(i + 1) * LANE]
    xs = [x_ref[r] for r in rows]
    prs = [_project(x, mod_ref, g_ref, qg_ref, kvg_ref, w_ref) for x in xs]

    for s, pr in zip(seqs, prs):
        if layer:
            nckv_ref[s, 0:layer] = pckv_ref[s]
            nkr_ref[s, 0:layer] = pkr_ref[s]
            nst_ref[s, 0:layer] = pst_ref[s]
        nckv_ref[s, layer] = pr.ckv
        nkr_ref[s, layer] = pr.kr.T[ROPE_LANE0:ROPE_LANE0 + D_ROPE]

    qs, ks, vs = [], [], []
    for pr in prs:
        ckvb = pr.ckv.astype(BF16)
        q_all = _dot(pr.qn, wq_ref[...]) * Q_SCALE
        k_all = _dot(ckvb, wk_ref[...])
        vs.append(_dot(ckvb, wv_ref[...]))
        qs.append([slot(q_all, h).astype(BF16) for h in range(H_MLA)])
        ks.append([(slot(k_all, h) + pr.kr).astype(BF16) for h in range(H_MLA)])
    o_m = [[] for _ in seqs]
    for p in range(H_MLA // 2):
        for s in seqs:
            o_m[s].append(_attend_pair(qs[s][2 * p:2 * p + 2], ks[s][2 * p:2 * p + 2],
                                       _value_slots(slot(vs[s], p))))

    o_f = [_fourier(T, pr.fu, csw_ref, cts_ref) for pr in prs]

    o_r = [[] for _ in seqs]
    for h in range(H_RET):
        lf = lg[0:1, h:h + 1]
        lb = lg[1:2, h:h + 1]
        for s, pr in zip(seqs, prs):
            q = pr.q[:, h * DK_RET:(h + 1) * DK_RET]
            k = pr.k[:, h * DK_RET:(h + 1) * DK_RET]
            v = pr.v[:, h * DV_RET:(h + 1) * DV_RET]
            o_r[s].append(_ret_head(T, q, k, v, mask_ref.at[h], lf, lb))
            sf, sb = _ret_states(T, k, v, lf, lb)
            nst_ref[s, layer, 0, h] = sf
            nst_ref[s, layer, 1, h] = sb

    for s, pr in zip(seqs, prs):
        branches = [(o * gate).astype(BF16) for o, gate in zip(
            (jnp.concatenate(o_r[s], axis=1), jnp.concatenate(o_m[s], axis=1), o_f[s]), pr.sz)]
        y_ref[rows[s]] = _merge(branches, pr.sg, xs[s], mod_ref, wb_ref, wo_ref, fg_ref)


def _prompt_call(layer, x, mod, g, qg, kvg, w_p, logit, wq, wk, wv, csw, cts, wb, wo,
                 final_g=None, prev=None):
    T = SEQ
    n_steps = BATCH // PROMPT_SPS
    row = pl.BlockSpec((PROMPT_SPS * T, D_MODEL), lambda i: (i, 0))
    in_specs = [row, _mod_spec(layer, lambda i: 0)]
    in_specs += [_layer_spec(a, layer) for a in (g, qg, kvg, w_p, logit, wq, wk, wv)]
    in_specs += [_full_spec(csw), _full_spec(cts), _layer_spec(wb, layer), _layer_spec(wo, layer)]
    args = [x, mod, g, qg, kvg, w_p, logit, wq, wk, wv, csw, cts, wb, wo]
    if final_g is not None:
        in_specs.append(_full_spec(final_g))
        args.append(final_g)
    tails = ((SEQ, KV_LORA), (D_ROPE, SEQ), ST_TAIL)
    if layer:
        in_specs += [_grow_spec(layer, t) for t in tails]
        args += list(prev)
    return pl.pallas_call(
        functools.partial(_prompt_kernel, layer, final_g is not None),
        grid=(n_steps,),
        in_specs=in_specs,
        out_specs=[row] + [_grow_spec(layer + 1, t) for t in tails],
        out_shape=[jax.ShapeDtypeStruct((BATCH * T, D_MODEL), F32)]
        + [jax.ShapeDtypeStruct((BATCH, layer + 1) + t, F32) for t in tails],
        scratch_shapes=[pltpu.VMEM((H_RET, 1, RET_BLK, RET_BLK), F32)],
        compiler_params=_cparams(),
        name="prompt_layer",
    )(*args)


def _in_kernel(x_ref, mod_ref, g_ref, qg_ref, kvg_ref, w_ref,
               qkv_ref, sz_ref, sg_ref, qn_ref, ckv_ref, kr_ref, fu_ref):
    qw = H_RET * DK_RET
    half = TM // IN_CHAINS
    for c in range(IN_CHAINS):
        rows = slice(c * half, (c + 1) * half)
        pr = _project(x_ref[rows], mod_ref, g_ref, qg_ref, kvg_ref, w_ref)
        qkv_ref[rows, 0:qw] = pr.q
        qkv_ref[rows, qw:2 * qw] = pr.k
        qkv_ref[rows, 2 * qw:] = pr.v
        for j in range(3):
            sz_ref[rows, j * BRANCH_W:(j + 1) * BRANCH_W] = pr.sz[j].astype(BF16)
        for j in range(N_BRANCH):
            sg_ref[rows, j * D_MODEL:(j + 1) * D_MODEL] = pr.sg[j].astype(BF16)
        qn_ref[rows] = pr.qn
        ckv_ref[rows] = pr.ckv.astype(BF16)
        kr_ref[rows] = pr.kr
        fu_ref[rows] = pr.fu


def _in_call(layer, x, mod, g, qg, kvg, w_p):
    n = x.shape[0]
    per_b = DEC_SEQ // TM
    row = lambda w: pl.BlockSpec((TM, w), lambda i: (i, 0))
    outs = [(1024, BF16), (3 * BRANCH_W, BF16), (N_BRANCH * D_MODEL, BF16), (Q_LORA, BF16),
            (KV_LORA, BF16), (LANE, F32), (FOURIER_W, BF16)]
    return pl.pallas_call(
        _in_kernel,
        grid=(n // TM,),
        in_specs=[row(D_MODEL), _mod_spec(layer, lambda i: 1 + i // per_b)]
        + [_layer_spec(a, layer) for a in (g, qg, kvg, w_p)],
        out_specs=[row(w) for w, _ in outs],
        out_shape=[jax.ShapeDtypeStruct((n, w), dt) for w, dt in outs],
        compiler_params=_cparams(),
        name="in_proj_s",
    )(x, mod, g, qg, kvg, w_p)


def _sample_rows(width, col=0):
    return pl.BlockSpec((SAMPLE_SPS * DEC_SEQ, width), lambda b: (b, col))


def _sample_cached(layer, tail):
    return pl.BlockSpec((SAMPLE_SPS, None) + tail, lambda b: (b, layer) + (0,) * len(tail))


def _ret_fourier_kernel(T, qkv_ref, rz_ref, fz_ref, fu_ref, lg_ref, st_ref, csw_ref, cts_ref,
                        o_ref, mask_ref):
    lg = jax.nn.log_sigmoid(lg_ref[...])

    @pl.when(pl.program_id(0) == 0)
    def _():
        _init_masks(T // RET_BLK, lg, mask_ref)

    qw = H_RET * DK_RET
    for s in range(SAMPLE_SPS):
        rows = slice(s * T, (s + 1) * T)
        for h in range(H_RET):
            q = qkv_ref[rows, h * DK_RET:(h + 1) * DK_RET]
            k = qkv_ref[rows, qw + h * DK_RET:qw + (h + 1) * DK_RET]
            v = qkv_ref[rows, 2 * qw + h * DV_RET:2 * qw + (h + 1) * DV_RET]
            on = _ret_head(T, q, k, v, mask_ref.at[h], lg[0:1, h:h + 1], lg[1:2, h:h + 1],
                           st=(st_ref[s, 0, h], st_ref[s, 1, h]))
            gate = rz_ref[rows, h * DV_RET:(h + 1) * DV_RET].astype(F32)
            o_ref[rows, h * DV_RET:(h + 1) * DV_RET] = (on * gate).astype(BF16)
        o = _fourier(T, fu_ref[rows], csw_ref, cts_ref)
        o_ref[rows, RET_W:] = (o * fz_ref[rows].astype(F32)).astype(BF16)


def _ret_fourier_call(layer, qkv, sz, fu, logit, state, csw, cts):
    T = DEC_SEQ
    return pl.pallas_call(
        functools.partial(_ret_fourier_kernel, T),
        grid=(DEC_BATCH // SAMPLE_SPS,),
        in_specs=[_sample_rows(1024), _sample_rows(BRANCH_W, 0), _sample_rows(BRANCH_W, 2),
                  _sample_rows(FOURIER_W), _layer_spec(logit, layer),
                  _sample_cached(layer, ST_TAIL), _full_spec(csw), _full_spec(cts)],
        out_specs=_sample_rows(RET_W + FOURIER_W),
        out_shape=jax.ShapeDtypeStruct((T * DEC_BATCH, RET_W + FOURIER_W), BF16),
        scratch_shapes=[pltpu.VMEM((H_RET, 3, RET_BLK, RET_BLK), F32)],
        compiler_params=_cparams(),
        name="ret_fourier_s",
    )(qkv, sz, sz, fu, logit, state, csw, cts)


def _mla_kernel(T, qn_ref, ckv_ref, kr_ref, mz_ref, wq_ref, wk_ref, wv_ref, cckv_ref, ckr_ref,
                cos_ref, sa_ref, sb_ref, bm_ref):
    cos, sa, sb = cos_ref[...], sa_ref[...], sb_ref[...]
    slot = lambda a, i: a[:, i * LANE:(i + 1) * LANE]
    for s in range(SAMPLE_SPS):
        rows = slice(s * T, (s + 1) * T)
        qn = qn_ref[rows]
        ckv = ckv_ref[rows]
        kr = _rope(kr_ref[rows], cos, sa, sb)
        cc = cckv_ref[s].astype(BF16)
        for p in range(H_MLA // 2):
            sl = slice(2 * p * LANE, 2 * (p + 1) * LANE)
            q2 = _dot(qn, wq_ref[:, sl])
            kc2 = _dot(cc, wk_ref[:, sl])
            kn2 = _dot(ckv, wk_ref[:, sl])
            if p % 2 == 0:
                vsl = slice(p * LANE, (p + 2) * LANE)
                v4 = jnp.concatenate([_dot(cc, wv_ref[:, vsl]), _dot(ckv, wv_ref[:, vsl])], axis=0)
            qs = [(_rope(slot(q2, i), cos, sa, sb) * Q_SCALE).astype(BF16) for i in range(2)]
            ks = [jnp.concatenate([(slot(kc2, i) + ckr_ref[s]).astype(BF16),
                                   (slot(kn2, i) + kr).astype(BF16)], axis=0) for i in range(2)]
            o = _attend_pair(qs, ks, _value_slots(slot(v4, p % 2)))
            gate = mz_ref[rows, p * LANE:(p + 1) * LANE].astype(F32)
            bm_ref[rows, p * LANE:(p + 1) * LANE] = (o * gate).astype(BF16)


def _mla_call(layer, qn, ckv, kr, sz, wq, wk, wv, cache_ckv, cache_kr, cos, sa, sb):
    T = DEC_SEQ
    return pl.pallas_call(
        functools.partial(_mla_kernel, T),
        grid=(DEC_BATCH // SAMPLE_SPS,),
        in_specs=[_sample_rows(Q_LORA), _sample_rows(KV_LORA), _sample_rows(LANE),
                  _sample_rows(BRANCH_W, 1)]
        + [_layer_spec(a, layer) for a in (wq, wk, wv)]
        + [_sample_cached(layer, (PAST_LEN, KV_LORA)), _sample_cached(layer, (PAST_LEN, LANE)),
           _full_spec(cos), _full_spec(sa), _full_spec(sb)],
        out_specs=_sample_rows(MLA_W),
        out_shape=jax.ShapeDtypeStruct((T * DEC_BATCH, MLA_W), BF16),
        compiler_params=_cparams(),
        name="mla_s",
    )(qn, ckv, kr, sz, wq, wk, wv, cache_ckv, cache_kr, cos, sa, sb)


def _merge_kernel(final, *refs):
    br_ref, bm_ref, bf_ref, sg_ref, x_ref, mod_ref, wb_ref, wo_ref = refs[:8]
    fg_ref = refs[8] if final else None
    sg = [sg_ref[:, n * D_MODEL:(n + 1) * D_MODEL].astype(F32) for n in range(N_BRANCH)]
    refs[-1][...] = _merge((br_ref[...], bm_ref[...], bf_ref[...]), sg, x_ref[...],
                           mod_ref, wb_ref, wo_ref, fg_ref)


def _merge_call(layer, brf, bm, sg, x, mod, wb, wo, final_g=None):
    final = final_g is not None
    n = x.shape[0]
    per_b = DEC_SEQ // TM
    row = lambda w, c=0: pl.BlockSpec((TM, w), lambda i: (i, c))
    in_specs = [row(BRANCH_W), row(BRANCH_W), row(BRANCH_W, 1), row(N_BRANCH * D_MODEL), row(D_MODEL),
                _mod_spec(layer, lambda i: 1 + i // per_b),
                _layer_spec(wb, layer), _layer_spec(wo, layer)]
    args = [brf, bm, brf, sg, x, mod, wb, wo]
    if final:
        in_specs.append(_full_spec(final_g))
        args.append(final_g)
    return pl.pallas_call(
        functools.partial(_merge_kernel, final),
        grid=(n // TM,),
        in_specs=in_specs,
        out_specs=row(D_MODEL),
        out_shape=jax.ShapeDtypeStruct((n, D_MODEL), F32),
        compiler_params=_cparams(),
        name="merge_s",
    )(*args)


def _dft_tables(T):
    def cs(n):
        kt = (np.arange(n)[:, None] * np.arange(n)[None, :]) % n
        ang = 2.0 * np.pi * kt.astype(np.float64) / n
        return np.cos(ang), np.sin(ang)
    ct, st = cs(T)
    cw, sw = cs(F_GROUP_W)
    cts = np.concatenate([ct, -st], axis=1).astype(np.float32)
    csw = np.concatenate([cw, sw], axis=1).astype(np.float32)
    return jnp.asarray(csw).astype(BF16), jnp.asarray(cts).astype(BF16)


def _rope_tables(T):
    half = D_ROPE // 2
    nfreq = half // 2
    inv = ROPE_BASE ** (-np.arange(nfreq, dtype=np.float64) / nfreq)
    t = np.arange(T)
    pos = np.stack([t // GRID_W, t % GRID_W], axis=0).astype(np.float64)
    cos = np.ones((T, LANE), np.float64)
    sa = np.zeros((T, LANE), np.float64)
    sb = np.zeros((T, LANE), np.float64)
    for part in range(2):
        ang = pos[part][:, None] * inv[None, :]
        l1 = ROPE_LANE0 + part * half
        l2 = l1 + nfreq
        cos[:, l1:l1 + nfreq] = np.cos(ang)
        cos[:, l2:l2 + nfreq] = np.cos(ang)
        sa[:, l1:l1 + nfreq] = -np.sin(ang)
        sb[:, l2:l2 + nfreq] = np.sin(ang)
    return tuple(jnp.asarray(a.astype(np.float32)) for a in (cos, sa, sb))


KR_SRC = 2176
W_IN_RUNS = ((0, 1024),
             (1024, 512), (2208, 512), (3232, 512),
             (3744, 3072),
             (1536, 384),
             (KR_SRC - ROPE_LANE0, LANE),
             (1920, 256),
             (2720, 512))
W_PREP_TILES = 6
KR_TILE = SEG_KR[0] // LANE


def _w_prep_kernel(tbl_ref, *refs):
    o_ref = refs[-1]
    j = pl.program_id(1)
    lane = lax.broadcasted_iota(jnp.int32, (D_MODEL, LANE), 1)
    rotary = (lane >= ROPE_LANE0) & (lane < ROPE_LANE0 + D_ROPE)
    for t, w_ref in enumerate(refs[:-1]):
        wt = w_ref[0].T
        if t == KR_TILE % W_PREP_TILES:
            wt = jnp.where((j != KR_TILE // W_PREP_TILES) | rotary, wt, 0.0)
        o_ref[:, t * LANE:(t + 1) * LANE] = wt.astype(BF16)


def _permute_w_in(w):
    depth, d, in_w = w.shape
    starts = np.concatenate([src + np.arange(0, width, LANE) for src, width in W_IN_RUNS])
    assert starts.size * LANE == IN_WP and starts.size % W_PREP_TILES == 0
    tbl = jnp.asarray(starts, jnp.int32)
    tile_spec = lambda t: pl.BlockSpec(
        (pl.Element(1), pl.Element(LANE), pl.Element(d)),
        lambda l, j, tbl_ref: (l, pl.multiple_of(tbl_ref[j * W_PREP_TILES + t], D_ROPE), 0))
    wide = W_PREP_TILES * LANE
    return pl.pallas_call(
        _w_prep_kernel,
        grid_spec=pltpu.PrefetchScalarGridSpec(
            num_scalar_prefetch=1,
            grid=(depth, IN_WP // wide),
            in_specs=[tile_spec(t) for t in range(W_PREP_TILES)],
            out_specs=pl.BlockSpec((None, d, wide), lambda l, j, tbl_ref: (l, 0, j)),
        ),
        out_shape=jax.ShapeDtypeStruct((depth, d, IN_WP), BF16),
        compiler_params=_cparams(2),
        name="w_prep",
    )(tbl, *([jnp.swapaxes(w, 1, 2)] * W_PREP_TILES))


def _pad_heads(w, n_heads, width, lo, hi):
    d, k = w.shape[:2]
    wh = w.reshape(d, k, n_heads, width)[..., lo:hi]
    wh = jnp.pad(wh, ((0, 0), (0, 0), (0, 0), (0, LANE - (hi - lo))))
    return wh.reshape(d, k, n_heads * LANE).astype(BF16)


def kernel(x_prompt, x_sample, cache_ckv, cache_krope, state_ret, c, c_ctx, norm_g, w_mod, b_mod,
           w_in, ret_decay_logit, q_norm_g, w_q_up, kv_norm_g, w_kv_up, w_branch, w_out,
           final_norm_g):
    cv = jnp.concatenate([c_ctx[None, :], c, jnp.zeros((MOD_ROWS - 1 - DEC_BATCH, D_MODEL), F32)], axis=0)
    mod = _mod_call(cv, w_mod, b_mod).reshape(DEPTH, MOD_ROWS, 1, 3 * D_MODEL)

    cache_kr = jnp.pad(cache_krope, ((0, 0), (0, 0), (0, 0), (ROPE_LANE0, LANE - ROPE_LANE0 - D_ROPE)))
    rope = _rope_tables(DEC_SEQ)
    dft_p = _dft_tables(SEQ)
    dft_s = _dft_tables(DEC_SEQ)
    logit = jnp.pad(ret_decay_logit, ((0, 0), (0, 8 - 2), (0, LANE - H_RET)))
    g, qg, kvg = norm_g[:, None, :], q_norm_g[:, None, :], kv_norm_g[:, None, :]
    w_p = _permute_w_in(w_in)
    wq = _pad_heads(w_q_up, H_MLA, D_NOPE + D_ROPE, 0, D_NOPE + D_ROPE)
    wk = _pad_heads(w_kv_up, H_MLA, D_NOPE + D_VMLA, 0, D_NOPE)
    wv = w_kv_up.reshape(DEPTH, KV_LORA, H_MLA, D_NOPE + D_VMLA)[..., D_NOPE:]
    wv = wv.reshape(DEPTH, KV_LORA, MLA_W).astype(BF16)
    wb = w_branch.astype(BF16)
    wo = w_out.astype(BF16)
    final_g = final_norm_g[None, :]

    xp = x_prompt.reshape(BATCH * SEQ, D_MODEL)
    xs = x_sample.reshape(DEC_BATCH * DEC_SEQ, D_MODEL)
    new_ctx = None
    for l in range(DEPTH):
        fg = final_g if l == DEPTH - 1 else None
        xp, *new_ctx = _prompt_call(l, xp, mod, g, qg, kvg, w_p, logit, wq, wk, wv, *dft_p, wb, wo,
                                    final_g=fg, prev=new_ctx)

        qkv, sz, sg, qn, ckv, kr, fu = _in_call(l, xs, mod, g, qg, kvg, w_p)
        brf = _ret_fourier_call(l, qkv, sz, fu, logit, state_ret, *dft_s)
        bm = _mla_call(l, qn, ckv, kr, sz, wq, wk, wv, cache_ckv, cache_kr, *rope)
        xs = _merge_call(l, brf, bm, sg, xs, mod, wb, wo, final_g=fg)

    y_prompt = xp.reshape(BATCH, SEQ, D_MODEL)
    y_sample = xs.reshape(DEC_BATCH, DEC_SEQ, D_MODEL)
    new_ckv, new_krope_t, new_ret = new_ctx
    return (y_prompt, y_sample, new_ckv, jnp.swapaxes(new_krope_t, 2, 3), new_ret)
```

```python
import collections
import functools
import math

import numpy as np
import jax
import jax.numpy as jnp
from jax import lax
from jax.experimental import pallas as pl
from jax.experimental.pallas import tpu as pltpu

F32 = jnp.float32
BF16 = jnp.bfloat16

D_MODEL = 1024
BATCH = 32
SEQ = 256
DEPTH = 2
DEC_BATCH = 4
DEC_SEQ = 1024
PAST_LEN = 512
GRID_W = 64
EPS = 1e-6
H_RET = 4
DK_RET = 64
DV_RET = 128
RET_W = H_RET * DV_RET
H_MLA = 8
Q_LORA = 384
KV_LORA = 256
D_NOPE = 64
D_ROPE = 32
D_VMLA = 64
MLA_W = H_MLA * D_VMLA
ROPE_BASE = 10000.0
F_GROUPS = 4
F_GROUP_W = 128
FOURIER_W = F_GROUPS * F_GROUP_W
N_BRANCH = 3
BRANCH_W = 512

LANE = 128
TM = 512
MERGE_TM = 1024
IN_CHAINS = 2
PROMPT_SPS = 2
SAMPLE_SPS = 1
MOD_ROWS = 8
ROPE_LANE0 = D_NOPE
VMEM_LIMIT = 60 * 1024 * 1024
ST_TAIL = (2, H_RET, DK_RET, DV_RET)

SEG_QKV = (0, 1024)
SEG_Z = (1024, 2560)
SEG_G = (2560, 5632)
SEG_QL = (5632, 6016)
SEG_KR = (6016, 6144)
SEG_KV = (6144, 6400)
SEG_FU = (6400, 6912)
IN_WP = 6912

NT = (((1,), (1,)), ((), ()))
TN = (((0,), (0,)), ((), ()))


def _cparams(n_axes=1):
    return pltpu.CompilerParams(dimension_semantics=("arbitrary",) * n_axes,
                                vmem_limit_bytes=VMEM_LIMIT)


def _layer_spec(a, layer):
    shape = a.shape[1:]
    return pl.BlockSpec((None,) + shape, lambda i: (layer,) + (0,) * len(shape))


def _full_spec(a):
    return pl.BlockSpec(a.shape, lambda i: (0,) * a.ndim)


def _mod_spec(layer, row):
    return pl.BlockSpec((None, None, 1, 3 * D_MODEL), lambda i: (layer, row(i), 0, 0))


def _grow_spec(depth, tail):
    return pl.BlockSpec((PROMPT_SPS, depth) + tail, lambda i: (i,) + (0,) * (1 + len(tail)))


def _dot(a, b):
    return jnp.dot(a, b, preferred_element_type=F32)


def _rms(x, g):
    return x * lax.rsqrt(jnp.mean(x * x, axis=-1, keepdims=True) + EPS) * g


def _mod_kernel(c_ref, w_ref, b_ref, o_ref):
    cv = c_ref[...]
    s = cv * jax.nn.sigmoid(cv)
    o_ref[0] = _dot(s.astype(BF16), w_ref[0].astype(BF16)) + b_ref[0]


def _mod_call(cv, w_mod, b_mod):
    nb = 3 * D_MODEL // 1024
    return pl.pallas_call(
        _mod_kernel,
        grid=(DEPTH, nb),
        in_specs=[
            pl.BlockSpec((MOD_ROWS, D_MODEL), lambda l, j: (0, 0)),
            pl.BlockSpec((1, D_MODEL, 1024), lambda l, j: (l, 0, j)),
            pl.BlockSpec((1, 1, 1024), lambda l, j: (l, 0, j)),
        ],
        out_specs=pl.BlockSpec((1, MOD_ROWS, 1024), lambda l, j: (l, 0, j)),
        out_shape=jax.ShapeDtypeStruct((DEPTH, MOD_ROWS, 3 * D_MODEL), F32),
        compiler_params=_cparams(2),
        name="mod",
    )(cv, w_mod, b_mod.reshape(DEPTH, 1, 3 * D_MODEL))


Proj = collections.namedtuple("Proj", "q k v sz sg qn ckv kr fu")


def _modulate(x, mod_ref, g_ref):
    shift = mod_ref[:, 0:D_MODEL]
    scale = mod_ref[:, D_MODEL:2 * D_MODEL]
    return (_rms(x, g_ref[...]) * (1.0 + scale) + shift).astype(BF16)


def _project(x, mod_ref, g_ref, qg_ref, kvg_ref, w_ref):
    return _project_h(_modulate(x, mod_ref, g_ref), qg_ref, kvg_ref, w_ref)


def _project_h(hb, qg_ref, kvg_ref, w_ref):
    def mm(a, b):
        return _dot(hb, w_ref[:, a:b])

    qw = H_RET * DK_RET
    p = mm(*SEG_QKV)
    q = p[:, 0:qw].astype(BF16)
    k = (p[:, qw:2 * qw] * (DK_RET ** -0.5)).astype(BF16)
    v = p[:, 2 * qw:].astype(BF16)
    sz = []
    for j in range(3):
        a = SEG_Z[0] + j * BRANCH_W
        p = mm(a, a + BRANCH_W)
        sz.append(p * jax.nn.sigmoid(p))
    sg = []
    for j in range(N_BRANCH):
        a = SEG_G[0] + j * D_MODEL
        sg.append(jax.nn.sigmoid(mm(a, a + D_MODEL)))
    p = mm(SEG_QL[0], SEG_KR[1])
    qn = _rms(p[:, :Q_LORA], qg_ref[...]).astype(BF16)
    kr = p[:, Q_LORA:]
    ckv = _rms(mm(*SEG_KV), kvg_ref[...])
    fu = mm(*SEG_FU).astype(BF16)
    return Proj(q, k, v, sz, sg, qn, ckv, kr, fu)


RET_BLK = 256


def _init_masks(n_blk, lg, mask_ref):
    B = RET_BLK
    ii = lax.broadcasted_iota(jnp.int32, (B, B), 0)
    jj = lax.broadcasted_iota(jnp.int32, (B, B), 1)
    d = (ii - jj).astype(F32)
    ad = jnp.abs(d)
    for h in range(H_RET):
        lf = lg[0:1, h:h + 1]
        lb = lg[1:2, h:h + 1]
        mask_ref[h, 0] = jnp.where(d > 0, jnp.exp(ad * lf), jnp.where(d < 0, jnp.exp(ad * lb), 2.0))
        if n_blk > 1:
            mask_ref[h, 1] = jnp.exp((B + d) * lf)
            mask_ref[h, 2] = jnp.exp((B - d) * lb)


def _masked_scores(s, mask, lf, lb):
    B = RET_BLK
    n_blk = s.shape[0] // B
    if n_blk == 1:
        return (s * mask[0]).astype(BF16)
    rows = []
    for i in range(n_blk):
        cols = []
        for j in range(n_blk):
            blk = s[i * B:(i + 1) * B, j * B:(j + 1) * B]
            far = abs(i - j) - 1
            if far < 0:
                m = mask[0]
            else:
                m = mask[1] if j < i else mask[2]
                if far:
                    m = m * jnp.exp((far * B) * (lf if j < i else lb))
            cols.append((blk * m).astype(BF16))
        rows.append(jnp.concatenate(cols, axis=1))
    return jnp.concatenate(rows, axis=0)


def _ret_head(T, q, k, v, mask, lf, lb, st=None):
    s = lax.dot_general(q, k, NT, preferred_element_type=F32)
    o = _dot(_masked_scores(s, mask, lf, lb), v)
    if st is not None:
        c = _dot(q, jnp.concatenate([st[0], st[1]], axis=1).astype(BF16))
        rows = lax.broadcasted_iota(jnp.int32, (T, DV_RET), 0).astype(F32)
        o = o + jnp.exp((rows + 1.0) * lf) * c[:, :DV_RET]
        o = o + jnp.exp((T - rows) * lb) * c[:, DV_RET:]
    mu = jnp.mean(o, axis=-1, keepdims=True)
    c = o - mu
    var = jnp.mean(c * c, axis=-1, keepdims=True)
    return c * lax.rsqrt(var + EPS)


def _ret_states(T, k, v, lf, lb):
    rows = lax.broadcasted_iota(jnp.int32, (T, DK_RET), 0).astype(F32)
    kf = (k.astype(F32) * jnp.exp((T - 1.0 - rows) * lf)).astype(BF16)
    kb = (k.astype(F32) * jnp.exp(rows * lb)).astype(BF16)
    return (lax.dot_general(kf, v, TN, preferred_element_type=F32),
            lax.dot_general(kb, v, TN, preferred_element_type=F32))


def _rope(x, cos, sa, sb):
    half = D_ROPE // 4
    return x * cos + pltpu.roll(x, LANE - half, 1) * sa + pltpu.roll(x, half, 1) * sb


Q_SCALE = (D_NOPE + D_ROPE) ** -0.5 * math.log2(math.e)


def _value_slots(vp):
    first = lax.broadcasted_iota(jnp.int32, vp.shape, 1) < D_VMLA
    return (jnp.where(first, vp, 1.0).astype(BF16), jnp.where(first, 1.0, vp).astype(BF16))


def _attend_pair(q2, k2, v2):
    outs = []
    for qh, kh, vh in zip(q2, k2, v2):
        s = lax.dot_general(qh, kh, NT, preferred_element_type=F32)
        e = jnp.exp2((s - jnp.max(s, axis=-1, keepdims=True)).astype(BF16))
        outs.append(_dot(e, vh))
    first = lax.broadcasted_iota(jnp.int32, outs[0].shape, 1) < D_VMLA
    o = jnp.where(first, outs[0], outs[1])
    l = jnp.where(first, pltpu.roll(outs[0], D_VMLA, 1), pltpu.roll(outs[1], D_VMLA, 1))
    return o / l


def _fourier(T, fu, csw_ref, cts_ref):
    a, b = [], []
    for g in range(F_GROUPS):
        z = _dot(fu[:, g * F_GROUP_W:(g + 1) * F_GROUP_W], csw_ref[...])
        a.append(z[:, :F_GROUP_W])
        b.append(z[:, F_GROUP_W:])
    a = jnp.concatenate(a, axis=1).astype(BF16)
    b = jnp.concatenate(b, axis=1).astype(BF16)
    o = _dot(cts_ref[:, 0:T], a) + _dot(cts_ref[:, T:2 * T], b)
    return o * ((T * F_GROUP_W) ** -0.5)


def _merge(branches, sg, x, mod_ref, wb_ref, wo_ref, fg_ref):
    merged = None
    for n, b in enumerate(branches):
        term = _dot(b, wb_ref[n]) * sg[n]
        merged = term if merged is None else merged + term
    gate = mod_ref[:, 2 * D_MODEL:3 * D_MODEL]
    y = x + gate * _dot(merged.astype(BF16), wo_ref[...])
    if fg_ref is not None:
        y = _rms(y, fg_ref[...])
    return y


def _prompt_kernel(layer, final, *refs):
    (x_ref, mod_ref, g_ref, qg_ref, kvg_ref, w_ref, lg_ref, wq_ref, wk_ref, wv_ref,
     csw_ref, cts_ref, wb_ref, wo_ref) = refs[:14]
    n_in = 14
    fg_ref = None
    if final:
        fg_ref = refs[n_in]
        n_in += 1
    if layer:
        pckv_ref, pkr_ref, pst_ref = refs[n_in:n_in + 3]
        n_in += 3
    y_ref, nckv_ref, nkr_ref, nst_ref, mask_ref = refs[n_in:]
    T = SEQ
    lg = jax.nn.log_sigmoid(lg_ref[...])

    @pl.when(pl.program_id(0) == 0)
    def _():
        _init_masks(T // RET_BLK, lg, mask_ref)

    seqs = range(PROMPT_SPS)
    rows = [slice(s * T, (s + 1) * T) for s in seqs]
    slot = lambda a, i: a[:, i * LANE:(i + 1) * LANE]
    xs = [x_ref[r] for r in rows]
    prs = [_project(x, mod_ref, g_ref, qg_ref, kvg_ref, w_ref) for x in xs]

    for s, pr in zip(seqs, prs):
        if layer:
            nckv_ref[s, 0:layer] = pckv_ref[s]
            nkr_ref[s, 0:layer] = pkr_ref[s]
            nst_ref[s, 0:layer] = pst_ref[s]
        nckv_ref[s, layer] = pr.ckv
        nkr_ref[s, layer] = pr.kr.T[ROPE_LANE0:ROPE_LANE0 + D_ROPE]

    qs, ks, vs = [], [], []
    for pr in prs:
        ckvb = pr.ckv.astype(BF16)
        q_all = _dot(pr.qn, wq_ref[...]) * Q_SCALE
        k_all = _dot(ckvb, wk_ref[...])
        vs.append(_dot(ckvb, wv_ref[...]))
        qs.append([slot(q_all, h).astype(BF16) for h in range(H_MLA)])
        ks.append([(slot(k_all, h) + pr.kr).astype(BF16) for h in range(H_MLA)])
    o_m = [[] for _ in seqs]
    for p in range(H_MLA // 2):
        for s in seqs:
            o_m[s].append(_attend_pair(qs[s][2 * p:2 * p + 2], ks[s][2 * p:2 * p + 2],
                                       _value_slots(slot(vs[s], p))))

    o_f = [_fourier(T, pr.fu, csw_ref, cts_ref) for pr in prs]

    o_r = [[] for _ in seqs]
    for h in range(H_RET):
        lf = lg[0:1, h:h + 1]
        lb = lg[1:2, h:h + 1]
        for s, pr in zip(seqs, prs):
            q = pr.q[:, h * DK_RET:(h + 1) * DK_RET]
            k = pr.k[:, h * DK_RET:(h + 1) * DK_RET]
            v = pr.v[:, h * DV_RET:(h + 1) * DV_RET]
            o_r[s].append(_ret_head(T, q, k, v, mask_ref.at[h], lf, lb))
            sf, sb = _ret_states(T, k, v, lf, lb)
            nst_ref[s, layer, 0, h] = sf
            nst_ref[s, layer, 1, h] = sb

    for s, pr in zip(seqs, prs):
        branches = [(o * gate).astype(BF16) for o, gate in zip(
            (jnp.concatenate(o_r[s], axis=1), jnp.concatenate(o_m[s], axis=1), o_f[s]), pr.sz)]
        y_ref[rows[s]] = _merge(branches, pr.sg, xs[s], mod_ref, wb_ref, wo_ref, fg_ref)


def _prompt_call(layer, x, mod, g, qg, kvg, w_p, logit, wq, wk, wv, csw, cts, wb, wo,
                 final_g=None, prev=None):
    T = SEQ
    n_steps = BATCH // PROMPT_SPS
    row = pl.BlockSpec((PROMPT_SPS * T, D_MODEL), lambda i: (i, 0))
    in_specs = [row, _mod_spec(layer, lambda i: 0)]
    in_specs += [_layer_spec(a, layer) for a in (g, qg, kvg, w_p, logit, wq, wk, wv)]
    in_specs += [_full_spec(csw), _full_spec(cts), _layer_spec(wb, layer), _layer_spec(wo, layer)]
    args = [x, mod, g, qg, kvg, w_p, logit, wq, wk, wv, csw, cts, wb, wo]
    if final_g is not None:
        in_specs.append(_full_spec(final_g))
        args.append(final_g)
    tails = ((SEQ, KV_LORA), (D_ROPE, SEQ), ST_TAIL)
    if layer:
        in_specs += [_grow_spec(layer, t) for t in tails]
        args += list(prev)
    return pl.pallas_call(
        functools.partial(_prompt_kernel, layer, final_g is not None),
        grid=(n_steps,),
        in_specs=in_specs,
        out_specs=[row] + [_grow_spec(layer + 1, t) for t in tails],
        out_shape=[jax.ShapeDtypeStruct((BATCH * T, D_MODEL), F32)]
        + [jax.ShapeDtypeStruct((BATCH, layer + 1) + t, F32) for t in tails],
        scratch_shapes=[pltpu.VMEM((H_RET, 1, RET_BLK, RET_BLK), F32)],
        compiler_params=_cparams(),
        name="prompt_layer",
    )(*args)


def _in_kernel(x_ref, mod_ref, g_ref, qg_ref, kvg_ref, w_ref,
               qkv_ref, sz_ref, sg_ref, qn_ref, ckv_ref, kr_ref, fu_ref):
    qw = H_RET * DK_RET
    half = TM // IN_CHAINS
    for c in range(IN_CHAINS):
        rows = slice(c * half, (c + 1) * half)
        pr = _project(x_ref[rows], mod_ref, g_ref, qg_ref, kvg_ref, w_ref)
        qkv_ref[rows, 0:qw] = pr.q
        qkv_ref[rows, qw:2 * qw] = pr.k
        qkv_ref[rows, 2 * qw:] = pr.v
        for j in range(3):
            sz_ref[rows, j * BRANCH_W:(j + 1) * BRANCH_W] = pr.sz[j].astype(BF16)
        for j in range(N_BRANCH):
            sg_ref[rows, j * D_MODEL:(j + 1) * D_MODEL] = pr.sg[j].astype(BF16)
        qn_ref[rows] = pr.qn
        ckv_ref[rows] = pr.ckv.astype(BF16)
        kr_ref[rows] = pr.kr
        fu_ref[rows] = pr.fu


def _in_call(layer, x, mod, g, qg, kvg, w_p):
    n = x.shape[0]
    per_b = DEC_SEQ // TM
    row = lambda w: pl.BlockSpec((TM, w), lambda i: (i, 0))
    outs = [(1024, BF16), (3 * BRANCH_W, BF16), (N_BRANCH * D_MODEL, BF16), (Q_LORA, BF16),
            (KV_LORA, BF16), (LANE, F32), (FOURIER_W, BF16)]
    return pl.pallas_call(
        _in_kernel,
        grid=(n // TM,),
        in_specs=[row(D_MODEL), _mod_spec(layer, lambda i: 1 + i // per_b)]
        + [_layer_spec(a, layer) for a in (g, qg, kvg, w_p)],
        out_specs=[row(w) for w, _ in outs],
        out_shape=[jax.ShapeDtypeStruct((n, w), dt) for w, dt in outs],
        compiler_params=_cparams(),
        name="in_proj_s",
    )(x, mod, g, qg, kvg, w_p)


def _sample_rows(width, col=0):
    return pl.BlockSpec((SAMPLE_SPS * DEC_SEQ, width), lambda b: (b, col))


def _sample_cached(layer, tail):
    return pl.BlockSpec((SAMPLE_SPS, None) + tail, lambda b: (b, layer) + (0,) * len(tail))


def _ret_fourier_kernel(T, qkv_ref, rz_ref, fz_ref, fu_ref, lg_ref, st_ref, csw_ref, cts_ref,
                        o_ref, mask_ref):
    lg = jax.nn.log_sigmoid(lg_ref[...])

    @pl.when(pl.program_id(0) == 0)
    def _():
        _init_masks(T // RET_BLK, lg, mask_ref)

    qw = H_RET * DK_RET
    for s in range(SAMPLE_SPS):
        rows = slice(s * T, (s + 1) * T)
        for h in range(H_RET):
            q = qkv_ref[rows, h * DK_RET:(h + 1) * DK_RET]
            k = qkv_ref[rows, qw + h * DK_RET:qw + (h + 1) * DK_RET]
            v = qkv_ref[rows, 2 * qw + h * DV_RET:2 * qw + (h + 1) * DV_RET]
            on = _ret_head(T, q, k, v, mask_ref.at[h], lg[0:1, h:h + 1], lg[1:2, h:h + 1],
                           st=(st_ref[s, 0, h], st_ref[s, 1, h]))
            gate = rz_ref[rows, h * DV_RET:(h + 1) * DV_RET].astype(F32)
            o_ref[rows, h * DV_RET:(h + 1) * DV_RET] = (on * gate).astype(BF16)
        o = _fourier(T, fu_ref[rows], csw_ref, cts_ref)
        o_ref[rows, RET_W:] = (o * fz_ref[rows].astype(F32)).astype(BF16)


def _ret_fourier_call(layer, qkv, sz, fu, logit, state, csw, cts):
    T = DEC_SEQ
    return pl.pallas_call(
        functools.partial(_ret_fourier_kernel, T),
        grid=(DEC_BATCH // SAMPLE_SPS,),
        in_specs=[_sample_rows(1024), _sample_rows(BRANCH_W, 0), _sample_rows(BRANCH_W, 2),
                  _sample_rows(FOURIER_W), _layer_spec(logit, layer),
                  _sample_cached(layer, ST_TAIL), _full_spec(csw), _full_spec(cts)],
        out_specs=_sample_rows(RET_W + FOURIER_W),
        out_shape=jax.ShapeDtypeStruct((T * DEC_BATCH, RET_W + FOURIER_W), BF16),
        scratch_shapes=[pltpu.VMEM((H_RET, 3, RET_BLK, RET_BLK), F32)],
        compiler_params=_cparams(),
        name="ret_fourier_s",
    )(qkv, sz, sz, fu, logit, state, csw, cts)


def _mla_kernel(T, qn_ref, ckv_ref, kr_ref, mz_ref, wq_ref, wk_ref, wv_ref, cckv_ref, ckr_ref,
                cos_ref, sa_ref, sb_ref, bm_ref):
    cos, sa, sb = cos_ref[...], sa_ref[...], sb_ref[...]
    slot = lambda a, i: a[:, i * LANE:(i + 1) * LANE]
    for s in range(SAMPLE_SPS):
        rows = slice(s * T, (s + 1) * T)
        qn = qn_ref[rows]
        ckv = ckv_ref[rows]
        kr = _rope(kr_ref[rows], cos, sa, sb)
        cc = cckv_ref[s].astype(BF16)
        for p in range(H_MLA // 2):
            sl = slice(2 * p * LANE, 2 * (p + 1) * LANE)
            q2 = _dot(qn, wq_ref[:, sl])
            kc2 = _dot(cc, wk_ref[:, sl])
            kn2 = _dot(ckv, wk_ref[:, sl])
            if p % 2 == 0:
                vsl = slice(p * LANE, (p + 2) * LANE)
                v4 = jnp.concatenate([_dot(cc, wv_ref[:, vsl]), _dot(ckv, wv_ref[:, vsl])], axis=0)
            qs = [(_rope(slot(q2, i), cos, sa, sb) * Q_SCALE).astype(BF16) for i in range(2)]
            ks = [jnp.concatenate([(slot(kc2, i) + ckr_ref[s]).astype(BF16),
                                   (slot(kn2, i) + kr).astype(BF16)], axis=0) for i in range(2)]
            o = _attend_pair(qs, ks, _value_slots(slot(v4, p % 2)))
            gate = mz_ref[rows, p * LANE:(p + 1) * LANE].astype(F32)
            bm_ref[rows, p * LANE:(p + 1) * LANE] = (o * gate).astype(BF16)


def _mla_call(layer, qn, ckv, kr, sz, wq, wk, wv, cache_ckv, cache_kr, cos, sa, sb):
    T = DEC_SEQ
    return pl.pallas_call(
        functools.partial(_mla_kernel, T),
        grid=(DEC_BATCH // SAMPLE_SPS,),
        in_specs=[_sample_rows(Q_LORA), _sample_rows(KV_LORA), _sample_rows(LANE),
                  _sample_rows(BRANCH_W, 1)]
        + [_layer_spec(a, layer) for a in (wq, wk, wv)]
        + [_sample_cached(layer, (PAST_LEN, KV_LORA)), _sample_cached(layer, (PAST_LEN, LANE)),
           _full_spec(cos), _full_spec(sa), _full_spec(sb)],
        out_specs=_sample_rows(MLA_W),
        out_shape=jax.ShapeDtypeStruct((T * DEC_BATCH, MLA_W), BF16),
        compiler_params=_cparams(),
        name="mla_s",
    )(qn, ckv, kr, sz, wq, wk, wv, cache_ckv, cache_kr, cos, sa, sb)


def _merge_kernel(final, *refs):
    br_ref, bm_ref, bf_ref, sg_ref, x_ref, mod_ref, wb_ref, wo_ref = refs[:8]
    fg_ref = refs[8] if final else None
    sg = [sg_ref[:, n * D_MODEL:(n + 1) * D_MODEL].astype(F32) for n in range(N_BRANCH)]
    refs[-1][...] = _merge((br_ref[...], bm_ref[...], bf_ref[...]), sg, x_ref[...],
                           mod_ref, wb_ref, wo_ref, fg_ref)


def _merge_call(layer, brf, bm, sg, x, mod, wb, wo, final_g=None):
    final = final_g is not None
    n = x.shape[0]
    per_b = DEC_SEQ // MERGE_TM
    row = lambda w, c=0: pl.BlockSpec((MERGE_TM, w), lambda i: (i, c))
    in_specs = [row(BRANCH_W), row(BRANCH_W), row(BRANCH_W, 1), row(N_BRANCH * D_MODEL), row(D_MODEL),
                _mod_spec(layer, lambda i: 1 + i // per_b),
                _layer_spec(wb, layer), _layer_spec(wo, layer)]
    args = [brf, bm, brf, sg, x, mod, wb, wo]
    if final:
        in_specs.append(_full_spec(final_g))
        args.append(final_g)
    return pl.pallas_call(
        functools.partial(_merge_kernel, final),
        grid=(n // MERGE_TM,),
        in_specs=in_specs,
        out_specs=row(D_MODEL),
        out_shape=jax.ShapeDtypeStruct((n, D_MODEL), F32),
        compiler_params=_cparams(),
        name="merge_s",
    )(*args)


def _dft_tables(T):
    def cs(n):
        kt = (np.arange(n)[:, None] * np.arange(n)[None, :]) % n
        ang = 2.0 * np.pi * kt.astype(np.float64) / n
        return np.cos(ang), np.sin(ang)
    ct, st = cs(T)
    cw, sw = cs(F_GROUP_W)
    cts = np.concatenate([ct, -st], axis=1).astype(np.float32)
    csw = np.concatenate([cw, sw], axis=1).astype(np.float32)
    return jnp.asarray(csw).astype(BF16), jnp.asarray(cts).astype(BF16)


def _rope_tables(T):
    half = D_ROPE // 2
    nfreq = half // 2
    inv = ROPE_BASE ** (-np.arange(nfreq, dtype=np.float64) / nfreq)
    t = np.arange(T)
    pos = np.stack([t // GRID_W, t % GRID_W], axis=0).astype(np.float64)
    cos = np.ones((T, LANE), np.float64)
    sa = np.zeros((T, LANE), np.float64)
    sb = np.zeros((T, LANE), np.float64)
    for part in range(2):
        ang = pos[part][:, None] * inv[None, :]
        l1 = ROPE_LANE0 + part * half
        l2 = l1 + nfreq
        cos[:, l1:l1 + nfreq] = np.cos(ang)
        cos[:, l2:l2 + nfreq] = np.cos(ang)
        sa[:, l1:l1 + nfreq] = -np.sin(ang)
        sb[:, l2:l2 + nfreq] = np.sin(ang)
    return tuple(jnp.asarray(a.astype(np.float32)) for a in (cos, sa, sb))


KR_SRC = 2176
W_IN_RUNS = ((0, 1024),
             (1024, 512), (2208, 512), (3232, 512),
             (3744, 3072),
             (1536, 384),
             (KR_SRC - ROPE_LANE0, LANE),
             (1920, 256),
             (2720, 512))
W_PREP_TILES = 6
KR_TILE = SEG_KR[0] // LANE


def _w_prep_kernel(tbl_ref, *refs):
    o_ref = refs[-1]
    j = pl.program_id(1)
    lane = lax.broadcasted_iota(jnp.int32, (D_MODEL, LANE), 1)
    rotary = (lane >= ROPE_LANE0) & (lane < ROPE_LANE0 + D_ROPE)
    for t, w_ref in enumerate(refs[:-1]):
        wt = w_ref[0].T
        if t == KR_TILE % W_PREP_TILES:
            wt = jnp.where((j != KR_TILE // W_PREP_TILES) | rotary, wt, 0.0)
        o_ref[:, t * LANE:(t + 1) * LANE] = wt.astype(BF16)


def _permute_w_in(w):
    depth, d, in_w = w.shape
    starts = np.concatenate([src + np.arange(0, width, LANE) for src, width in W_IN_RUNS])
    assert starts.size * LANE == IN_WP and starts.size % W_PREP_TILES == 0
    tbl = jnp.asarray(starts, jnp.int32)
    tile_spec = lambda t: pl.BlockSpec(
        (pl.Element(1), pl.Element(LANE), pl.Element(d)),
        lambda l, j, tbl_ref: (l, pl.multiple_of(tbl_ref[j * W_PREP_TILES + t], D_ROPE), 0))
    wide = W_PREP_TILES * LANE
    return pl.pallas_call(
        _w_prep_kernel,
        grid_spec=pltpu.PrefetchScalarGridSpec(
            num_scalar_prefetch=1,
            grid=(depth, IN_WP // wide),
            in_specs=[tile_spec(t) for t in range(W_PREP_TILES)],
            out_specs=pl.BlockSpec((None, d, wide), lambda l, j, tbl_ref: (l, 0, j)),
        ),
        out_shape=jax.ShapeDtypeStruct((depth, d, IN_WP), BF16),
        compiler_params=_cparams(2),
        name="w_prep",
    )(tbl, *([jnp.swapaxes(w, 1, 2)] * W_PREP_TILES))


def _pad_heads(w, n_heads, width, lo, hi):
    d, k = w.shape[:2]
    wh = w.reshape(d, k, n_heads, width)[..., lo:hi]
    wh = jnp.pad(wh, ((0, 0), (0, 0), (0, 0), (0, LANE - (hi - lo))))
    return wh.reshape(d, k, n_heads * LANE).astype(BF16)


def kernel(x_prompt, x_sample, cache_ckv, cache_krope, state_ret, c, c_ctx, norm_g, w_mod, b_mod,
           w_in, ret_decay_logit, q_norm_g, w_q_up, kv_norm_g, w_kv_up, w_branch, w_out,
           final_norm_g):
    cv = jnp.concatenate([c_ctx[None, :], c, jnp.zeros((MOD_ROWS - 1 - DEC_BATCH, D_MODEL), F32)], axis=0)
    mod = _mod_call(cv, w_mod, b_mod).reshape(DEPTH, MOD_ROWS, 1, 3 * D_MODEL)

    cache_kr = jnp.pad(cache_krope, ((0, 0), (0, 0), (0, 0), (ROPE_LANE0, LANE - ROPE_LANE0 - D_ROPE)))
    rope = _rope_tables(DEC_SEQ)
    dft_p = _dft_tables(SEQ)
    dft_s = _dft_tables(DEC_SEQ)
    logit = jnp.pad(ret_decay_logit, ((0, 0), (0, 8 - 2), (0, LANE - H_RET)))
    g, qg, kvg = norm_g[:, None, :], q_norm_g[:, None, :], kv_norm_g[:, None, :]
    w_p = _permute_w_in(w_in)
    wq = _pad_heads(w_q_up, H_MLA, D_NOPE + D_ROPE, 0, D_NOPE + D_ROPE)
    wk = _pad_heads(w_kv_up, H_MLA, D_NOPE + D_VMLA, 0, D_NOPE)
    wv = w_kv_up.reshape(DEPTH, KV_LORA, H_MLA, D_NOPE + D_VMLA)[..., D_NOPE:]
    wv = wv.reshape(DEPTH, KV_LORA, MLA_W).astype(BF16)
    wb = w_branch.astype(BF16)
    wo = w_out.astype(BF16)
    final_g = final_norm_g[None, :]

    xp = x_prompt.reshape(BATCH * SEQ, D_MODEL)
    xs = x_sample.reshape(DEC_BATCH * DEC_SEQ, D_MODEL)
    new_ctx = None
    for l in range(DEPTH):
        fg = final_g if l == DEPTH - 1 else None
        xp, *new_ctx = _prompt_call(l, xp, mod, g, qg, kvg, w_p, logit, wq, wk, wv, *dft_p, wb, wo,
                                    final_g=fg, prev=new_ctx)

        qkv, sz, sg, qn, ckv, kr, fu = _in_call(l, xs, mod, g, qg, kvg, w_p)
        brf = _ret_fourier_call(l, qkv, sz, fu, logit, state_ret, *dft_s)
        bm = _mla_call(l, qn, ckv, kr, sz, wq, wk, wv, cache_ckv, cache_kr, *rope)
        xs = _merge_call(l, brf, bm, sg, xs, mod, wb, wo, final_g=fg)

    y_prompt = xp.reshape(BATCH, SEQ, D_MODEL)
    y_sample = xs.reshape(DEC_BATCH, DEC_SEQ, D_MODEL)
    new_ckv, new_krope_t, new_ret = new_ctx
    return (y_prompt, y_sample, new_ckv, jnp.swapaxes(new_krope_t, 2, 3), new_ret)
```

```python
import collections
import functools
import math

import numpy as np
import jax
import jax.numpy as jnp
from jax import lax
from jax.experimental import pallas as pl
from jax.experimental.pallas import tpu as pltpu

F32 = jnp.float32
BF16 = jnp.bfloat16

D_MODEL = 1024
BATCH = 32
SEQ = 256
DEPTH = 2
DEC_BATCH = 4
DEC_SEQ = 1024
PAST_LEN = 512
GRID_W = 64
EPS = 1e-6
H_RET = 4
DK_RET = 64
DV_RET = 128
RET_W = H_RET * DV_RET
H_MLA = 8
Q_LORA = 384
KV_LORA = 256
D_NOPE = 64
D_ROPE = 32
D_VMLA = 64
MLA_W = H_MLA * D_VMLA
ROPE_BASE = 10000.0
F_GROUPS = 4
F_GROUP_W = 128
FOURIER_W = F_GROUPS * F_GROUP_W
N_BRANCH = 3
BRANCH_W = 512

LANE = 128
TM = 512
MERGE_TM = 1024
IN_CHAINS = 2
PROMPT_SPS = 2
SAMPLE_SPS = 1
MOD_ROWS = 8
ROPE_LANE0 = D_NOPE
VMEM_LIMIT = 60 * 1024 * 1024
ST_TAIL = (2, H_RET, DK_RET, DV_RET)

SEG_QKV = (0, 1024)
SEG_Z = (1024, 2560)
SEG_G = (2560, 5632)
SEG_QL = (5632, 6016)
SEG_KR = (6016, 6144)
SEG_KV = (6144, 6400)
SEG_FU = (6400, 6912)
IN_WP = 6912

NT = (((1,), (1,)), ((), ()))
TN = (((0,), (0,)), ((), ()))


def _cparams(n_axes=1):
    return pltpu.CompilerParams(dimension_semantics=("arbitrary",) * n_axes,
                                vmem_limit_bytes=VMEM_LIMIT)


def _layer_spec(a, layer):
    shape = a.shape[1:]
    return pl.BlockSpec((None,) + shape, lambda i: (layer,) + (0,) * len(shape))


def _full_spec(a):
    return pl.BlockSpec(a.shape, lambda i: (0,) * a.ndim)


def _mod_spec(layer, row):
    return pl.BlockSpec((None, None, 1, 3 * D_MODEL), lambda i: (layer, row(i), 0, 0))


def _grow_spec(depth, tail):
    return pl.BlockSpec((PROMPT_SPS, depth) + tail, lambda i: (i,) + (0,) * (1 + len(tail)))


def _dot(a, b):
    return jnp.dot(a, b, preferred_element_type=F32)


def _rms(x, g):
    return x * lax.rsqrt(jnp.mean(x * x, axis=-1, keepdims=True) + EPS) * g


def _mod_kernel(c_ref, w_ref, b_ref, o_ref):
    cv = c_ref[...]
    s = cv * jax.nn.sigmoid(cv)
    o_ref[0] = _dot(s.astype(BF16), w_ref[0].astype(BF16)) + b_ref[0]


def _mod_call(cv, w_mod, b_mod):
    nb = 3 * D_MODEL // 1024
    return pl.pallas_call(
        _mod_kernel,
        grid=(DEPTH, nb),
        in_specs=[
            pl.BlockSpec((MOD_ROWS, D_MODEL), lambda l, j: (0, 0)),
            pl.BlockSpec((1, D_MODEL, 1024), lambda l, j: (l, 0, j)),
            pl.BlockSpec((1, 1, 1024), lambda l, j: (l, 0, j)),
        ],
        out_specs=pl.BlockSpec((1, MOD_ROWS, 1024), lambda l, j: (l, 0, j)),
        out_shape=jax.ShapeDtypeStruct((DEPTH, MOD_ROWS, 3 * D_MODEL), F32),
        compiler_params=_cparams(2),
        name="mod",
    )(cv, w_mod, b_mod.reshape(DEPTH, 1, 3 * D_MODEL))


Proj = collections.namedtuple("Proj", "q k v sz sg qn ckv kr fu")


def _modulate(x, mod_ref, g_ref):
    shift = mod_ref[:, 0:D_MODEL]
    scale = mod_ref[:, D_MODEL:2 * D_MODEL]
    return (_rms(x, g_ref[...]) * (1.0 + scale) + shift).astype(BF16)


def _project(x, mod_ref, g_ref, qg_ref, kvg_ref, w_ref):
    return _project_h(_modulate(x, mod_ref, g_ref), qg_ref, kvg_ref, w_ref)


def _project_h(hb, qg_ref, kvg_ref, w_ref):
    def mm(a, b):
        return _dot(hb, w_ref[:, a:b])

    qw = H_RET * DK_RET
    p = mm(*SEG_QKV)
    q = p[:, 0:qw].astype(BF16)
    k = (p[:, qw:2 * qw] * (DK_RET ** -0.5)).astype(BF16)
    v = p[:, 2 * qw:].astype(BF16)
    sz = []
    for j in range(3):
        a = SEG_Z[0] + j * BRANCH_W
        p = mm(a, a + BRANCH_W)
        sz.append(p * jax.nn.sigmoid(p))
    sg = []
    for j in range(N_BRANCH):
        a = SEG_G[0] + j * D_MODEL
        sg.append(jax.nn.sigmoid(mm(a, a + D_MODEL)))
    p = mm(SEG_QL[0], SEG_KR[1])
    qn = _rms(p[:, :Q_LORA], qg_ref[...]).astype(BF16)
    kr = p[:, Q_LORA:]
    ckv = _rms(mm(*SEG_KV), kvg_ref[...])
    fu = mm(*SEG_FU).astype(BF16)
    return Proj(q, k, v, sz, sg, qn, ckv, kr, fu)


RET_BLK = 256


def _init_masks(n_blk, lg, mask_ref):
    B = RET_BLK
    ii = lax.broadcasted_iota(jnp.int32, (B, B), 0)
    jj = lax.broadcasted_iota(jnp.int32, (B, B), 1)
    d = (ii - jj).astype(F32)
    ad = jnp.abs(d)
    for h in range(H_RET):
        lf = lg[0:1, h:h + 1]
        lb = lg[1:2, h:h + 1]
        mask_ref[h, 0] = jnp.where(d > 0, jnp.exp(ad * lf), jnp.where(d < 0, jnp.exp(ad * lb), 2.0))
        if n_blk > 1:
            mask_ref[h, 1] = jnp.exp((B + d) * lf)
            mask_ref[h, 2] = jnp.exp((B - d) * lb)


def _masked_scores(s, mask, lf, lb):
    B = RET_BLK
    n_blk = s.shape[0] // B
    if n_blk == 1:
        return (s * mask[0]).astype(BF16)
    rows = []
    for i in range(n_blk):
        cols = []
        for j in range(n_blk):
            blk = s[i * B:(i + 1) * B, j * B:(j + 1) * B]
            far = abs(i - j) - 1
            if far < 0:
                m = mask[0]
            else:
                m = mask[1] if j < i else mask[2]
                if far:
                    m = m * jnp.exp((far * B) * (lf if j < i else lb))
            cols.append((blk * m).astype(BF16))
        rows.append(jnp.concatenate(cols, axis=1))
    return jnp.concatenate(rows, axis=0)


def _ret_head(T, q, k, v, mask, lf, lb, st=None):
    s = lax.dot_general(q, k, NT, preferred_element_type=F32)
    o = _dot(_masked_scores(s, mask, lf, lb), v)
    if st is not None:
        c = _dot(q, jnp.concatenate([st[0], st[1]], axis=1).astype(BF16))
        rows = lax.broadcasted_iota(jnp.int32, (T, DV_RET), 0).astype(F32)
        o = o + jnp.exp((rows + 1.0) * lf) * c[:, :DV_RET]
        o = o + jnp.exp((T - rows) * lb) * c[:, DV_RET:]
    mu = jnp.mean(o, axis=-1, keepdims=True)
    c = o - mu
    var = jnp.mean(c * c, axis=-1, keepdims=True)
    return c * lax.rsqrt(var + EPS)


def _ret_states(T, k, v, lf, lb):
    rows = lax.broadcasted_iota(jnp.int32, (T, DK_RET), 0).astype(F32)
    kf = (k.astype(F32) * jnp.exp((T - 1.0 - rows) * lf)).astype(BF16)
    kb = (k.astype(F32) * jnp.exp(rows * lb)).astype(BF16)
    return (lax.dot_general(kf, v, TN, preferred_element_type=F32),
            lax.dot_general(kb, v, TN, preferred_element_type=F32))


def _rope(x, cos, sa, sb):
    half = D_ROPE // 4
    return x * cos + pltpu.roll(x, LANE - half, 1) * sa + pltpu.roll(x, half, 1) * sb


Q_SCALE = (D_NOPE + D_ROPE) ** -0.5 * math.log2(math.e)


def _value_slots(vp):
    first = lax.broadcasted_iota(jnp.int32, vp.shape, 1) < D_VMLA
    return (jnp.where(first, vp, 1.0).astype(BF16), jnp.where(first, 1.0, vp).astype(BF16))


def _attend_pair(q2, k2, v2):
    outs = []
    for qh, kh, vh in zip(q2, k2, v2):
        s = lax.dot_general(qh, kh, NT, preferred_element_type=F32)
        e = jnp.exp2((s - jnp.max(s, axis=-1, keepdims=True)).astype(BF16))
        outs.append(_dot(e, vh))
    first = lax.broadcasted_iota(jnp.int32, outs[0].shape, 1) < D_VMLA
    o = jnp.where(first, outs[0], outs[1])
    l = jnp.where(first, pltpu.roll(outs[0], D_VMLA, 1), pltpu.roll(outs[1], D_VMLA, 1))
    return o / l


def _fourier(T, fu, csw_ref, cts_ref):
    a, b = [], []
    for g in range(F_GROUPS):
        z = _dot(fu[:, g * F_GROUP_W:(g + 1) * F_GROUP_W], csw_ref[...])
        a.append(z[:, :F_GROUP_W])
        b.append(z[:, F_GROUP_W:])
    a = jnp.concatenate(a, axis=1).astype(BF16)
    b = jnp.concatenate(b, axis=1).astype(BF16)
    o = _dot(cts_ref[:, 0:T], a) + _dot(cts_ref[:, T:2 * T], b)
    return o * ((T * F_GROUP_W) ** -0.5)


def _merge(branches, sg, x, mod_ref, wb_ref, wo_ref, fg_ref):
    merged = None
    for n, b in enumerate(branches):
        term = _dot(b, wb_ref[n]) * sg[n]
        merged = term if merged is None else merged + term
    gate = mod_ref[:, 2 * D_MODEL:3 * D_MODEL]
    y = x + gate * _dot(merged.astype(BF16), wo_ref[...])
    if fg_ref is not None:
        y = _rms(y, fg_ref[...])
    return y


def _prompt_kernel(layer, final, *refs):
    (x_ref, mod_ref, g_ref, qg_ref, kvg_ref, w_ref, lg_ref, wq_ref, wk_ref, wv_ref,
     csw_ref, cts_ref, wb_ref, wo_ref) = refs[:14]
    n_in = 14
    fg_ref = None
    if final:
        fg_ref = refs[n_in]
        n_in += 1
    if layer:
        pckv_ref, pkr_ref, pst_ref = refs[n_in:n_in + 3]
        n_in += 3
    y_ref, nckv_ref, nkr_ref, nst_ref, mask_ref = refs[n_in:]
    T = SEQ
    lg = jax.nn.log_sigmoid(lg_ref[...])

    @pl.when(pl.program_id(0) == 0)
    def _():
        _init_masks(T // RET_BLK, lg, mask_ref)

    seqs = range(PROMPT_SPS)
    rows = [slice(s * T, (s + 1) * T) for s in seqs]
    slot = lambda a, i: a[:, i * LANE:(i + 1) * LANE]
    xs = [x_ref[r] for r in rows]
    prs = [_project(x, mod_ref, g_ref, qg_ref, kvg_ref, w_ref) for x in xs]

    for s, pr in zip(seqs, prs):
        if layer:
            nckv_ref[s, 0:layer] = pckv_ref[s]
            nkr_ref[s, 0:layer] = pkr_ref[s]
            nst_ref[s, 0:layer] = pst_ref[s]
        nckv_ref[s, layer] = pr.ckv
        nkr_ref[s, layer] = pr.kr.T[ROPE_LANE0:ROPE_LANE0 + D_ROPE]

    qs, ks, vs = [], [], []
    for pr in prs:
        ckvb = pr.ckv.astype(BF16)
        q_all = _dot(pr.qn, wq_ref[...]) * Q_SCALE
        k_all = _dot(ckvb, wk_ref[...])
        vs.append(_dot(ckvb, wv_ref[...]))
        qs.append([slot(q_all, h).astype(BF16) for h in range(H_MLA)])
        ks.append([(slot(k_all, h) + pr.kr).astype(BF16) for h in range(H_MLA)])
    o_m = [[] for _ in seqs]
    for p in range(H_MLA // 2):
        for s in seqs:
            o_m[s].append(_attend_pair(qs[s][2 * p:2 * p + 2], ks[s][2 * p:2 * p + 2],
                                       _value_slots(slot(vs[s], p))))

    o_f = [_fourier(T, pr.fu, csw_ref, cts_ref) for pr in prs]

    o_r = [[] for _ in seqs]
    for h in range(H_RET):
        lf = lg[0:1, h:h + 1]
        lb = lg[1:2, h:h + 1]
        for s, pr in zip(seqs, prs):
            q = pr.q[:, h * DK_RET:(h + 1) * DK_RET]
            k = pr.k[:, h * DK_RET:(h + 1) * DK_RET]
            v = pr.v[:, h * DV_RET:(h + 1) * DV_RET]
            o_r[s].append(_ret_head(T, q, k, v, mask_ref.at[h], lf, lb))
            sf, sb = _ret_states(T, k, v, lf, lb)
            nst_ref[s, layer, 0, h] = sf
            nst_ref[s, layer, 1, h] = sb

    for s, pr in zip(seqs, prs):
        branches = [(o * gate).astype(BF16) for o, gate in zip(
            (jnp.concatenate(o_r[s], axis=1), jnp.concatenate(o_m[s], axis=1), o_f[s]), pr.sz)]
        y_ref[rows[s]] = _merge(branches, pr.sg, xs[s], mod_ref, wb_ref, wo_ref, fg_ref)


def _prompt_call(layer, x, mod, g, qg, kvg, w_p, logit, wq, wk, wv, csw, cts, wb, wo,
                 final_g=None, prev=None):
    T = SEQ
    n_steps = BATCH // PROMPT_SPS
    row = pl.BlockSpec((PROMPT_SPS * T, D_MODEL), lambda i: (i, 0))
    in_specs = [row, _mod_spec(layer, lambda i: 0)]
    in_specs += [_layer_spec(a, layer) for a in (g, qg, kvg, w_p, logit, wq, wk, wv)]
    in_specs += [_full_spec(csw), _full_spec(cts), _layer_spec(wb, layer), _layer_spec(wo, layer)]
    args = [x, mod, g, qg, kvg, w_p, logit, wq, wk, wv, csw, cts, wb, wo]
    if final_g is not None:
        in_specs.append(_full_spec(final_g))
        args.append(final_g)
    tails = ((SEQ, KV_LORA), (D_ROPE, SEQ), ST_TAIL)
    if layer:
        in_specs += [_grow_spec(layer, t) for t in tails]
        args += list(prev)
    return pl.pallas_call(
        functools.partial(_prompt_kernel, layer, final_g is not None),
        grid=(n_steps,),
        in_specs=in_specs,
        out_specs=[row] + [_grow_spec(layer + 1, t) for t in tails],
        out_shape=[jax.ShapeDtypeStruct((BATCH * T, D_MODEL), F32)]
        + [jax.ShapeDtypeStruct((BATCH, layer + 1) + t, F32) for t in tails],
        scratch_shapes=[pltpu.VMEM((H_RET, 1, RET_BLK, RET_BLK), F32)],
        compiler_params=_cparams(),
        name="prompt_layer",
    )(*args)


def _in_kernel(x_ref, mod_ref, g_ref, qg_ref, kvg_ref, w_ref,
               qkv_ref, sz_ref, sg_ref, qn_ref, ckv_ref, kr_ref, fu_ref):
    qw = H_RET * DK_RET
    half = TM // IN_CHAINS
    for c in range(IN_CHAINS):
        rows = slice(c * half, (c + 1) * half)
        pr = _project(x_ref[rows], mod_ref, g_ref, qg_ref, kvg_ref, w_ref)
        qkv_ref[rows, 0:qw] = pr.q
        qkv_ref[rows, qw:2 * qw] = pr.k
        qkv_ref[rows, 2 * qw:] = pr.v
        for j in range(3):
            sz_ref[rows, j * BRANCH_W:(j + 1) * BRANCH_W] = pr.sz[j].astype(BF16)
        for j in range(N_BRANCH):
            sg_ref[rows, j * D_MODEL:(j + 1) * D_MODEL] = pr.sg[j].astype(BF16)
        qn_ref[rows] = pr.qn
        ckv_ref[rows] = pr.ckv.astype(BF16)
        kr_ref[rows] = pr.kr
        fu_ref[rows] = pr.fu


def _in_call(layer, x, mod, g, qg, kvg, w_p):
    n = x.shape[0]
    per_b = DEC_SEQ // TM
    row = lambda w: pl.BlockSpec((TM, w), lambda i: (i, 0))
    outs = [(1024, BF16), (3 * BRANCH_W, BF16), (N_BRANCH * D_MODEL, BF16), (Q_LORA, BF16),
            (KV_LORA, BF16), (LANE, F32), (FOURIER_W, BF16)]
    return pl.pallas_call(
        _in_kernel,
        grid=(n // TM,),
        in_specs=[row(D_MODEL), _mod_spec(layer, lambda i: 1 + i // per_b)]
        + [_layer_spec(a, layer) for a in (g, qg, kvg, w_p)],
        out_specs=[row(w) for w, _ in outs],
        out_shape=[jax.ShapeDtypeStruct((n, w), dt) for w, dt in outs],
        compiler_params=_cparams(),
        name="in_proj_s",
    )(x, mod, g, qg, kvg, w_p)


def _sample_rows(width, col=0):
    return pl.BlockSpec((SAMPLE_SPS * DEC_SEQ, width), lambda b: (b, col))


def _sample_cached(layer, tail):
    return pl.BlockSpec((SAMPLE_SPS, None) + tail, lambda b: (b, layer) + (0,) * len(tail))


def _ret_fourier_kernel(T, qkv_ref, rz_ref, fz_ref, fu_ref, lg_ref, st_ref, csw_ref, cts_ref,
                        o_ref, mask_ref):
    lg = jax.nn.log_sigmoid(lg_ref[...])

    @pl.when(pl.program_id(0) == 0)
    def _():
        _init_masks(T // RET_BLK, lg, mask_ref)

    qw = H_RET * DK_RET
    for s in range(SAMPLE_SPS):
        rows = slice(s * T, (s + 1) * T)
        for h in range(H_RET):
            q = qkv_ref[rows, h * DK_RET:(h + 1) * DK_RET]
            k = qkv_ref[rows, qw + h * DK_RET:qw + (h + 1) * DK_RET]
            v = qkv_ref[rows, 2 * qw + h * DV_RET:2 * qw + (h + 1) * DV_RET]
            on = _ret_head(T, q, k, v, mask_ref.at[h], lg[0:1, h:h + 1], lg[1:2, h:h + 1],
                           st=(st_ref[s, 0, h], st_ref[s, 1, h]))
            gate = rz_ref[rows, h * DV_RET:(h + 1) * DV_RET].astype(F32)
            o_ref[rows, h * DV_RET:(h + 1) * DV_RET] = (on * gate).astype(BF16)
        o = _fourier(T, fu_ref[rows], csw_ref, cts_ref)
        o_ref[rows, RET_W:] = (o * fz_ref[rows].astype(F32)).astype(BF16)


def _ret_fourier_call(layer, qkv, sz, fu, logit, state, csw, cts):
    T = DEC_SEQ
    return pl.pallas_call(
        functools.partial(_ret_fourier_kernel, T),
        grid=(DEC_BATCH // SAMPLE_SPS,),
        in_specs=[_sample_rows(1024), _sample_rows(BRANCH_W, 0), _sample_rows(BRANCH_W, 2),
                  _sample_rows(FOURIER_W), _layer_spec(logit, layer),
                  _sample_cached(layer, ST_TAIL), _full_spec(csw), _full_spec(cts)],
        out_specs=_sample_rows(RET_W + FOURIER_W),
        out_shape=jax.ShapeDtypeStruct((T * DEC_BATCH, RET_W + FOURIER_W), BF16),
        scratch_shapes=[pltpu.VMEM((H_RET, 3, RET_BLK, RET_BLK), F32)],
        compiler_params=_cparams(),
        name="ret_fourier_s",
    )(qkv, sz, sz, fu, logit, state, csw, cts)


def _mla_kernel(T, qn_ref, ckv_ref, kr_ref, mz_ref, wq_ref, wk_ref, wv_ref, cckv_ref, ckr_ref,
                cos_ref, sa_ref, sb_ref, bm_ref):
    cos, sa, sb = cos_ref[...], sa_ref[...], sb_ref[...]
    slot = lambda a, i: a[:, i * LANE:(i + 1) * LANE]
    for s in range(SAMPLE_SPS):
        rows = slice(s * T, (s + 1) * T)
        qn = qn_ref[rows]
        ckv = ckv_ref[rows]
        kr = _rope(kr_ref[rows], cos, sa, sb)
        cc = cckv_ref[s].astype(BF16)
        for p in range(H_MLA // 2):
            sl = slice(2 * p * LANE, 2 * (p + 1) * LANE)
            q2 = _dot(qn, wq_ref[:, sl])
            kc2 = _dot(cc, wk_ref[:, sl])
            kn2 = _dot(ckv, wk_ref[:, sl])
            if p % 2 == 0:
                vsl = slice(p * LANE, (p + 2) * LANE)
                v4 = jnp.concatenate([_dot(cc, wv_ref[:, vsl]), _dot(ckv, wv_ref[:, vsl])], axis=0)
            qs = [(_rope(slot(q2, i), cos, sa, sb) * Q_SCALE).astype(BF16) for i in range(2)]
            ks = [jnp.concatenate([(slot(kc2, i) + ckr_ref[s]).astype(BF16),
                                   (slot(kn2, i) + kr).astype(BF16)], axis=0) for i in range(2)]
            o = _attend_pair(qs, ks, _value_slots(slot(v4, p % 2)))
            gate = mz_ref[rows, p * LANE:(p + 1) * LANE].astype(F32)
            bm_ref[rows, p * LANE:(p + 1) * LANE] = (o * gate).astype(BF16)


def _mla_call(layer, qn, ckv, kr, sz, wq, wk, wv, cache_ckv, cache_kr, cos, sa, sb):
    T = DEC_SEQ
    return pl.pallas_call(
        functools.partial(_mla_kernel, T),
        grid=(DEC_BATCH // SAMPLE_SPS,),
        in_specs=[_sample_rows(Q_LORA), _sample_rows(KV_LORA), _sample_rows(LANE),
                  _sample_rows(BRANCH_W, 1)]
        + [_layer_spec(a, layer) for a in (wq, wk, wv)]
        + [_sample_cached(layer, (PAST_LEN, KV_LORA)), _sample_cached(layer, (PAST_LEN, LANE)),
           _full_spec(cos), _full_spec(sa), _full_spec(sb)],
        out_specs=_sample_rows(MLA_W),
        out_shape=jax.ShapeDtypeStruct((T * DEC_BATCH, MLA_W), BF16),
        compiler_params=_cparams(),
        name="mla_s",
    )(qn, ckv, kr, sz, wq, wk, wv, cache_ckv, cache_kr, cos, sa, sb)


def _merge_kernel(final, *refs):
    br_ref, bm_ref, bf_ref, sg_ref, x_ref, mod_ref, wb_ref, wo_ref = refs[:8]
    fg_ref = refs[8] if final else None
    sg = [sg_ref[:, n * D_MODEL:(n + 1) * D_MODEL].astype(F32) for n in range(N_BRANCH)]
    refs[-1][...] = _merge((br_ref[...], bm_ref[...], bf_ref[...]), sg, x_ref[...],
                           mod_ref, wb_ref, wo_ref, fg_ref)


def _merge_call(layer, brf, bm, sg, x, mod, wb, wo, final_g=None):
    final = final_g is not None
    n = x.shape[0]
    per_b = DEC_SEQ // MERGE_TM
    row = lambda w, c=0: pl.BlockSpec((MERGE_TM, w), lambda i: (i, c))
    in_specs = [row(BRANCH_W), row(BRANCH_W), row(BRANCH_W, 1), row(N_BRANCH * D_MODEL), row(D_MODEL),
                _mod_spec(layer, lambda i: 1 + i // per_b),
                _layer_spec(wb, layer), _layer_spec(wo, layer)]
    args = [brf, bm, brf, sg, x, mod, wb, wo]
    if final:
        in_specs.append(_full_spec(final_g))
        args.append(final_g)
    return pl.pallas_call(
        functools.partial(_merge_kernel, final),
        grid=(n // MERGE_TM,),
        in_specs=in_specs,
        out_specs=row(D_MODEL),
        out_shape=jax.ShapeDtypeStruct((n, D_MODEL), F32),
        compiler_params=_cparams(),
        name="merge_s",
    )(*args)


def _dft_tables(T):
    def cs(n):
        kt = (np.arange(n)[:, None] * np.arange(n)[None, :]) % n
        ang = 2.0 * np.pi * kt.astype(np.float64) / n
        return np.cos(ang), np.sin(ang)
    ct, st = cs(T)
    cw, sw = cs(F_GROUP_W)
    cts = np.concatenate([ct, -st], axis=1).astype(np.float32)
    csw = np.concatenate([cw, sw], axis=1).astype(np.float32)
    return jnp.asarray(csw).astype(BF16), jnp.asarray(cts).astype(BF16)


def _rope_tables(T):
    half = D_ROPE // 2
    nfreq = half // 2
    inv = ROPE_BASE ** (-np.arange(nfreq, dtype=np.float64) / nfreq)
    t = np.arange(T)
    pos = np.stack([t // GRID_W, t % GRID_W], axis=0).astype(np.float64)
    cos = np.ones((T, LANE), np.float64)
    sa = np.zeros((T, LANE), np.float64)
    sb = np.zeros((T, LANE), np.float64)
    for part in range(2):
        ang = pos[part][:, None] * inv[None, :]
        l1 = ROPE_LANE0 + part * half
        l2 = l1 + nfreq
        cos[:, l1:l1 + nfreq] = np.cos(ang)
        cos[:, l2:l2 + nfreq] = np.cos(ang)
        sa[:, l1:l1 + nfreq] = -np.sin(ang)
        sb[:, l2:l2 + nfreq] = np.sin(ang)
    return tuple(jnp.asarray(a.astype(np.float32)) for a in (cos, sa, sb))


KR_SRC = 2176
W_IN_RUNS = ((0, 1024),
             (1024, 512), (2208, 512), (3232, 512),
             (3744, 3072),
             (1536, 384),
             (KR_SRC - ROPE_LANE0, LANE),
             (1920, 256),
             (2720, 512))
W_PREP_TILES = 9
KR_TILE = SEG_KR[0] // LANE


def _w_prep_kernel(tbl_ref, *refs):
    o_ref = refs[-1]
    j = pl.program_id(1)
    lane = lax.broadcasted_iota(jnp.int32, (D_MODEL, LANE), 1)
    rotary = (lane >= ROPE_LANE0) & (lane < ROPE_LANE0 + D_ROPE)
    for t, w_ref in enumerate(refs[:-1]):
        wt = w_ref[0].T
        if t == KR_TILE % W_PREP_TILES:
            wt = jnp.where((j != KR_TILE // W_PREP_TILES) | rotary, wt, 0.0)
        o_ref[:, t * LANE:(t + 1) * LANE] = wt.astype(BF16)


def _permute_w_in(w):
    depth, d, in_w = w.shape
    starts = np.concatenate([src + np.arange(0, width, LANE) for src, width in W_IN_RUNS])
    assert starts.size * LANE == IN_WP and starts.size % W_PREP_TILES == 0
    tbl = jnp.asarray(starts, jnp.int32)
    tile_spec = lambda t: pl.BlockSpec(
        (pl.Element(1), pl.Element(LANE), pl.Element(d)),
        lambda l, j, tbl_ref: (l, pl.multiple_of(tbl_ref[j * W_PREP_TILES + t], D_ROPE), 0))
    wide = W_PREP_TILES * LANE
    return pl.pallas_call(
        _w_prep_kernel,
        grid_spec=pltpu.PrefetchScalarGridSpec(
            num_scalar_prefetch=1,
            grid=(depth, IN_WP // wide),
            in_specs=[tile_spec(t) for t in range(W_PREP_TILES)],
            out_specs=pl.BlockSpec((None, d, wide), lambda l, j, tbl_ref: (l, 0, j)),
        ),
        out_shape=jax.ShapeDtypeStruct((depth, d, IN_WP), BF16),
        compiler_params=_cparams(2),
        name="w_prep",
    )(tbl, *([jnp.swapaxes(w, 1, 2)] * W_PREP_TILES))


def _pad_heads(w, n_heads, width, lo, hi):
    d, k = w.shape[:2]
    wh = w.reshape(d, k, n_heads, width)[..., lo:hi]
    wh = jnp.pad(wh, ((0, 0), (0, 0), (0, 0), (0, LANE - (hi - lo))))
    return wh.reshape(d, k, n_heads * LANE).astype(BF16)


def kernel(x_prompt, x_sample, cache_ckv, cache_krope, state_ret, c, c_ctx, norm_g, w_mod, b_mod,
           w_in, ret_decay_logit, q_norm_g, w_q_up, kv_norm_g, w_kv_up, w_branch, w_out,
           final_norm_g):
    cv = jnp.concatenate([c_ctx[None, :], c, jnp.zeros((MOD_ROWS - 1 - DEC_BATCH, D_MODEL), F32)], axis=0)
    mod = _mod_call(cv, w_mod, b_mod).reshape(DEPTH, MOD_ROWS, 1, 3 * D_MODEL)

    cache_kr = jnp.pad(cache_krope, ((0, 0), (0, 0), (0, 0), (ROPE_LANE0, LANE - ROPE_LANE0 - D_ROPE)))
    rope = _rope_tables(DEC_SEQ)
    dft_p = _dft_tables(SEQ)
    dft_s = _dft_tables(DEC_SEQ)
    logit = jnp.pad(ret_decay_logit, ((0, 0), (0, 8 - 2), (0, LANE - H_RET)))
    g, qg, kvg = norm_g[:, None, :], q_norm_g[:, None, :], kv_norm_g[:, None, :]
    w_p = _permute_w_in(w_in)
    wq = _pad_heads(w_q_up, H_MLA, D_NOPE + D_ROPE, 0, D_NOPE + D_ROPE)
    wk = _pad_heads(w_kv_up, H_MLA, D_NOPE + D_VMLA, 0, D_NOPE)
    wv = w_kv_up.reshape(DEPTH, KV_LORA, H_MLA, D_NOPE + D_VMLA)[..., D_NOPE:]
    wv = wv.reshape(DEPTH, KV_LORA, MLA_W).astype(BF16)
    wb = w_branch.astype(BF16)
    wo = w_out.astype(BF16)
    final_g = final_norm_g[None, :]

    xp = x_prompt.reshape(BATCH * SEQ, D_MODEL)
    xs = x_sample.reshape(DEC_BATCH * DEC_SEQ, D_MODEL)
    new_ctx = None
    for l in range(DEPTH):
        fg = final_g if l == DEPTH - 1 else None
        xp, *new_ctx = _prompt_call(l, xp, mod, g, qg, kvg, w_p, logit, wq, wk, wv, *dft_p, wb, wo,
                                    final_g=fg, prev=new_ctx)

        qkv, sz, sg, qn, ckv, kr, fu = _in_call(l, xs, mod, g, qg, kvg, w_p)
        brf = _ret_fourier_call(l, qkv, sz, fu, logit, state_ret, *dft_s)
        bm = _mla_call(l, qn, ckv, kr, sz, wq, wk, wv, cache_ckv, cache_kr, *rope)
        xs = _merge_call(l, brf, bm, sg, xs, mod, wb, wo, final_g=fg)

    y_prompt = xp.reshape(BATCH, SEQ, D_MODEL)
    y_sample = xs.reshape(DEC_BATCH, DEC_SEQ, D_MODEL)
    new_ckv, new_krope_t, new_ret = new_ctx
    return (y_prompt, y_sample, new_ckv, jnp.swapaxes(new_krope_t, 2, 3), new_ret)
```

```python
import collections
import functools
import math

import numpy as np
import jax
import jax.numpy as jnp
from jax import lax
from jax.experimental import pallas as pl
from jax.experimental.pallas import tpu as pltpu

F32 = jnp.float32
BF16 = jnp.bfloat16

D_MODEL = 1024
BATCH = 32
SEQ = 256
DEPTH = 2
DEC_BATCH = 4
DEC_SEQ = 1024
PAST_LEN = 512
GRID_W = 64
EPS = 1e-6
H_RET = 4
DK_RET = 64
DV_RET = 128
RET_W = H_RET * DV_RET
H_MLA = 8
Q_LORA = 384
KV_LORA = 256
D_NOPE = 64
D_ROPE = 32
D_VMLA = 64
MLA_W = H_MLA * D_VMLA
ROPE_BASE = 10000.0
F_GROUPS = 4
F_GROUP_W = 128
FOURIER_W = F_GROUPS * F_GROUP_W
N_BRANCH = 3
BRANCH_W = 512

LANE = 128
TM = 512
MERGE_TM = 1024
IN_CHAINS = 2
PROMPT_SPS = 2
SAMPLE_SPS = 1
MOD_ROWS = 8
ROPE_LANE0 = D_NOPE
VMEM_LIMIT = 60 * 1024 * 1024
ST_TAIL = (2, H_RET, DK_RET, DV_RET)

SEG_QKV = (0, 1024)
SEG_Z = (1024, 2560)
SEG_G = (2560, 5632)
SEG_QL = (5632, 6016)
SEG_KR = (6016, 6144)
SEG_KV = (6144, 6400)
SEG_FU = (6400, 6912)
IN_WP = 6912

NT = (((1,), (1,)), ((), ()))
TN = (((0,), (0,)), ((), ()))


def _cparams(n_axes=1):
    return pltpu.CompilerParams(dimension_semantics=("arbitrary",) * n_axes,
                                vmem_limit_bytes=VMEM_LIMIT)


def _layer_spec(a, layer):
    shape = a.shape[1:]
    return pl.BlockSpec((None,) + shape, lambda i: (layer,) + (0,) * len(shape))


def _full_spec(a):
    return pl.BlockSpec(a.shape, lambda i: (0,) * a.ndim)


def _mod_spec(layer, row):
    return pl.BlockSpec((None, None, 1, 3 * D_MODEL), lambda i: (layer, row(i), 0, 0))


def _grow_spec(depth, tail):
    return pl.BlockSpec((PROMPT_SPS, depth) + tail, lambda i: (i,) + (0,) * (1 + len(tail)))


def _dot(a, b):
    return jnp.dot(a, b, preferred_element_type=F32)


def _rms(x, g):
    return x * lax.rsqrt(jnp.mean(x * x, axis=-1, keepdims=True) + EPS) * g


def _mod_kernel(c_ref, w_ref, b_ref, o_ref):
    cv = c_ref[...]
    s = cv * jax.nn.sigmoid(cv)
    o_ref[0] = _dot(s.astype(BF16), w_ref[0].astype(BF16)) + b_ref[0]


def _mod_call(cv, w_mod, b_mod):
    nb = 3 * D_MODEL // 1024
    return pl.pallas_call(
        _mod_kernel,
        grid=(DEPTH, nb),
        in_specs=[
            pl.BlockSpec((MOD_ROWS, D_MODEL), lambda l, j: (0, 0)),
            pl.BlockSpec((1, D_MODEL, 1024), lambda l, j: (l, 0, j)),
            pl.BlockSpec((1, 1, 1024), lambda l, j: (l, 0, j)),
        ],
        out_specs=pl.BlockSpec((1, MOD_ROWS, 1024), lambda l, j: (l, 0, j)),
        out_shape=jax.ShapeDtypeStruct((DEPTH, MOD_ROWS, 3 * D_MODEL), F32),
        compiler_params=_cparams(2),
        name="mod",
    )(cv, w_mod, b_mod.reshape(DEPTH, 1, 3 * D_MODEL))


Proj = collections.namedtuple("Proj", "q k v sz sg qn ckv kr fu")


def _modulate(x, mod_ref, g_ref):
    shift = mod_ref[:, 0:D_MODEL]
    scale = mod_ref[:, D_MODEL:2 * D_MODEL]
    return (_rms(x, g_ref[...]) * (1.0 + scale) + shift).astype(BF16)


def _project(x, mod_ref, g_ref, qg_ref, kvg_ref, w_ref):
    return _project_h(_modulate(x, mod_ref, g_ref), qg_ref, kvg_ref, w_ref)


def _project_h(hb, qg_ref, kvg_ref, w_ref):
    def mm(a, b):
        return _dot(hb, w_ref[:, a:b])

    qw = H_RET * DK_RET
    p = mm(*SEG_QKV)
    q = p[:, 0:qw].astype(BF16)
    k = (p[:, qw:2 * qw] * (DK_RET ** -0.5)).astype(BF16)
    v = p[:, 2 * qw:].astype(BF16)
    sz = []
    for j in range(3):
        a = SEG_Z[0] + j * BRANCH_W
        p = mm(a, a + BRANCH_W)
        sz.append(p * jax.nn.sigmoid(p))
    sg = []
    for j in range(N_BRANCH):
        a = SEG_G[0] + j * D_MODEL
        sg.append(jax.nn.sigmoid(mm(a, a + D_MODEL)))
    p = mm(SEG_QL[0], SEG_KR[1])
    qn = _rms(p[:, :Q_LORA], qg_ref[...]).astype(BF16)
    kr = p[:, Q_LORA:]
    ckv = _rms(mm(*SEG_KV), kvg_ref[...])
    fu = mm(*SEG_FU).astype(BF16)
    return Proj(q, k, v, sz, sg, qn, ckv, kr, fu)


RET_BLK = 256


def _init_masks(n_blk, lg, mask_ref):
    B = RET_BLK
    ii = lax.broadcasted_iota(jnp.int32, (B, B), 0)
    jj = lax.broadcasted_iota(jnp.int32, (B, B), 1)
    d = (ii - jj).astype(F32)
    ad = jnp.abs(d)
    for h in range(H_RET):
        lf = lg[0:1, h:h + 1]
        lb = lg[1:2, h:h + 1]
        mask_ref[h, 0] = jnp.where(d > 0, jnp.exp(ad * lf), jnp.where(d < 0, jnp.exp(ad * lb), 2.0))
        if n_blk > 1:
            mask_ref[h, 1] = jnp.exp((B + d) * lf)
            mask_ref[h, 2] = jnp.exp((B - d) * lb)


def _masked_scores(s, mask, lf, lb):
    B = RET_BLK
    n_blk = s.shape[0] // B
    if n_blk == 1:
        return (s * mask[0]).astype(BF16)
    rows = []
    for i in range(n_blk):
        cols = []
        for j in range(n_blk):
            blk = s[i * B:(i + 1) * B, j * B:(j + 1) * B]
            far = abs(i - j) - 1
            if far < 0:
                m = mask[0]
            else:
                m = mask[1] if j < i else mask[2]
                if far:
                    m = m * jnp.exp((far * B) * (lf if j < i else lb))
            cols.append((blk * m).astype(BF16))
        rows.append(jnp.concatenate(cols, axis=1))
    return jnp.concatenate(rows, axis=0)


def _ret_head(T, q, k, v, mask, lf, lb, st=None):
    s = lax.dot_general(q, k, NT, preferred_element_type=F32)
    o = _dot(_masked_scores(s, mask, lf, lb), v)
    if st is not None:
        c = _dot(q, jnp.concatenate([st[0], st[1]], axis=1).astype(BF16))
        rows = lax.broadcasted_iota(jnp.int32, (T, DV_RET), 0).astype(F32)
        o = o + jnp.exp((rows + 1.0) * lf) * c[:, :DV_RET]
        o = o + jnp.exp((T - rows) * lb) * c[:, DV_RET:]
    mu = jnp.mean(o, axis=-1, keepdims=True)
    c = o - mu
    var = jnp.mean(c * c, axis=-1, keepdims=True)
    return c * lax.rsqrt(var + EPS)


def _ret_states(T, k, v, lf, lb):
    rows = lax.broadcasted_iota(jnp.int32, (T, DK_RET), 0).astype(F32)
    kf = (k.astype(F32) * jnp.exp((T - 1.0 - rows) * lf)).astype(BF16)
    kb = (k.astype(F32) * jnp.exp(rows * lb)).astype(BF16)
    return (lax.dot_general(kf, v, TN, preferred_element_type=F32),
            lax.dot_general(kb, v, TN, preferred_element_type=F32))


def _rope(x, cos, sa, sb):
    half = D_ROPE // 4
    return x * cos + pltpu.roll(x, LANE - half, 1) * sa + pltpu.roll(x, half, 1) * sb


Q_SCALE = (D_NOPE + D_ROPE) ** -0.5 * math.log2(math.e)


def _value_slots(vp):
    first = lax.broadcasted_iota(jnp.int32, vp.shape, 1) < D_VMLA
    return (jnp.where(first, vp, 1.0).astype(BF16), jnp.where(first, 1.0, vp).astype(BF16))


def _attend_pair(q2, k2, v2):
    outs = []
    for qh, kh, vh in zip(q2, k2, v2):
        s = lax.dot_general(qh, kh, NT, preferred_element_type=F32)
        e = jnp.exp2((s - jnp.max(s, axis=-1, keepdims=True)).astype(BF16))
        outs.append(_dot(e, vh))
    first = lax.broadcasted_iota(jnp.int32, outs[0].shape, 1) < D_VMLA
    o = jnp.where(first, outs[0], outs[1])
    l = jnp.where(first, pltpu.roll(outs[0], D_VMLA, 1), pltpu.roll(outs[1], D_VMLA, 1))
    return o / l


def _fourier(T, fu, csw_ref, cts_ref):
    a, b = [], []
    for g in range(F_GROUPS):
        z = _dot(fu[:, g * F_GROUP_W:(g + 1) * F_GROUP_W], csw_ref[...])
        a.append(z[:, :F_GROUP_W])
        b.append(z[:, F_GROUP_W:])
    a = jnp.concatenate(a, axis=1).astype(BF16)
    b = jnp.concatenate(b, axis=1).astype(BF16)
    o = _dot(cts_ref[:, 0:T], a) + _dot(cts_ref[:, T:2 * T], b)
    return o * ((T * F_GROUP_W) ** -0.5)


def _merge(branches, sg, x, mod_ref, wb_ref, wo_ref, fg_ref):
    merged = None
    for n, b in enumerate(branches):
        term = _dot(b, wb_ref[n]) * sg[n]
        merged = term if merged is None else merged + term
    gate = mod_ref[:, 2 * D_MODEL:3 * D_MODEL]
    y = x + gate * _dot(merged.astype(BF16), wo_ref[...])
    if fg_ref is not None:
        y = _rms(y, fg_ref[...])
    return y


def _prompt_kernel(layer, final, *refs):
    (x_ref, mod_ref, g_ref, qg_ref, kvg_ref, w_ref, lg_ref, wq_ref, wk_ref, wv_ref,
     csw_ref, cts_ref, wb_ref, wo_ref) = refs[:14]
    n_in = 14
    fg_ref = None
    if final:
        fg_ref = refs[n_in]
        n_in += 1
    if layer:
        pckv_ref, pkr_ref, pst_ref = refs[n_in:n_in + 3]
        n_in += 3
    y_ref, nckv_ref, nkr_ref, nst_ref, mask_ref = refs[n_in:]
    T = SEQ
    lg = jax.nn.log_sigmoid(lg_ref[...])

    @pl.when(pl.program_id(0) == 0)
    def _():
        _init_masks(T // RET_BLK, lg, mask_ref)

    seqs = range(PROMPT_SPS)
    rows = [slice(s * T, (s + 1) * T) for s in seqs]
    slot = lambda a, i: a[:, i * LANE:(i + 1) * LANE]
    xs = [x_ref[r] for r in rows]
    prs = [_project(x, mod_ref, g_ref, qg_ref, kvg_ref, w_ref) for x in xs]

    for s, pr in zip(seqs, prs):
        if layer:
            nckv_ref[s, 0:layer] = pckv_ref[s]
            nkr_ref[s, 0:layer] = pkr_ref[s]
            nst_ref[s, 0:layer] = pst_ref[s]
        nckv_ref[s, layer] = pr.ckv
        nkr_ref[s, layer] = pr.kr.T[ROPE_LANE0:ROPE_LANE0 + D_ROPE]

    qs, ks, vs = [], [], []
    for pr in prs:
        ckvb = pr.ckv.astype(BF16)
        q_all = _dot(pr.qn, wq_ref[...]) * Q_SCALE
        k_all = _dot(ckvb, wk_ref[...])
        vs.append(_dot(ckvb, wv_ref[...]))
        qs.append([slot(q_all, h).astype(BF16) for h in range(H_MLA)])
        ks.append([(slot(k_all, h) + pr.kr).astype(BF16) for h in range(H_MLA)])
    o_m = [[] for _ in seqs]
    for p in range(H_MLA // 2):
        for s in seqs:
            o_m[s].append(_attend_pair(qs[s][2 * p:2 * p + 2], ks[s][2 * p:2 * p + 2],
                                       _value_slots(slot(vs[s], p))))

    o_f = [_fourier(T, pr.fu, csw_ref, cts_ref) for pr in prs]

    o_r = [[] for _ in seqs]
    for h in range(H_RET):
        lf = lg[0:1, h:h + 1]
        lb = lg[1:2, h:h + 1]
        for s, pr in zip(seqs, prs):
            q = pr.q[:, h * DK_RET:(h + 1) * DK_RET]
            k = pr.k[:, h * DK_RET:(h + 1) * DK_RET]
            v = pr.v[:, h * DV_RET:(h + 1) * DV_RET]
            o_r[s].append(_ret_head(T, q, k, v, mask_ref.at[h], lf, lb))
            sf, sb = _ret_states(T, k, v, lf, lb)
            nst_ref[s, layer, 0, h] = sf
            nst_ref[s, layer, 1, h] = sb

    for s, pr in zip(seqs, prs):
        branches = [(o * gate).astype(BF16) for o, gate in zip(
            (jnp.concatenate(o_r[s], axis=1), jnp.concatenate(o_m[s], axis=1), o_f[s]), pr.sz)]
        y_ref[rows[s]] = _merge(branches, pr.sg, xs[s], mod_ref, wb_ref, wo_ref, fg_ref)


def _prompt_call(layer, x, mod, g, qg, kvg, w_p, logit, wq, wk, wv, csw, cts, wb, wo,
                 final_g=None, prev=None):
    T = SEQ
    n_steps = BATCH // PROMPT_SPS
    row = pl.BlockSpec((PROMPT_SPS * T, D_MODEL), lambda i: (i, 0))
    in_specs = [row, _mod_spec(layer, lambda i: 0)]
    in_specs += [_layer_spec(a, layer) for a in (g, qg, kvg, w_p, logit, wq, wk, wv)]
    in_specs += [_full_spec(csw), _full_spec(cts), _layer_spec(wb, layer), _layer_spec(wo, layer)]
    args = [x, mod, g, qg, kvg, w_p, logit, wq, wk, wv, csw, cts, wb, wo]
    if final_g is not None:
        in_specs.append(_full_spec(final_g))
        args.append(final_g)
    tails = ((SEQ, KV_LORA), (D_ROPE, SEQ), ST_TAIL)
    if layer:
        in_specs += [_grow_spec(layer, t) for t in tails]
        args += list(prev)
    return pl.pallas_call(
        functools.partial(_prompt_kernel, layer, final_g is not None),
        grid=(n_steps,),
        in_specs=in_specs,
        out_specs=[row] + [_grow_spec(layer + 1, t) for t in tails],
        out_shape=[jax.ShapeDtypeStruct((BATCH * T, D_MODEL), F32)]
        + [jax.ShapeDtypeStruct((BATCH, layer + 1) + t, F32) for t in tails],
        scratch_shapes=[pltpu.VMEM((H_RET, 1, RET_BLK, RET_BLK), F32)],
        compiler_params=_cparams(),
        name="prompt_layer",
    )(*args)


def _in_kernel(x_ref, mod_ref, g_ref, qg_ref, kvg_ref, w_ref,
               qkv_ref, sz_ref, sg_ref, qn_ref, ckv_ref, kr_ref, fu_ref):
    qw = H_RET * DK_RET
    half = TM // IN_CHAINS
    for c in range(IN_CHAINS):
        rows = slice(c * half, (c + 1) * half)
        pr = _project(x_ref[rows], mod_ref, g_ref, qg_ref, kvg_ref, w_ref)
        qkv_ref[rows, 0:qw] = pr.q
        qkv_ref[rows, qw:2 * qw] = pr.k
        qkv_ref[rows, 2 * qw:] = pr.v
        for j in range(3):
            sz_ref[rows, j * BRANCH_W:(j + 1) * BRANCH_W] = pr.sz[j].astype(BF16)
        for j in range(N_BRANCH):
            sg_ref[rows, j * D_MODEL:(j + 1) * D_MODEL] = pr.sg[j].astype(BF16)
        qn_ref[rows] = pr.qn
        ckv_ref[rows] = pr.ckv.astype(BF16)
        kr_ref[rows] = pr.kr
        fu_ref[rows] = pr.fu


def _in_call(layer, x, mod, g, qg, kvg, w_p):
    n = x.shape[0]
    per_b = DEC_SEQ // TM
    row = lambda w: pl.BlockSpec((TM, w), lambda i: (i, 0))
    outs = [(1024, BF16), (3 * BRANCH_W, BF16), (N_BRANCH * D_MODEL, BF16), (Q_LORA, BF16),
            (KV_LORA, BF16), (LANE, F32), (FOURIER_W, BF16)]
    return pl.pallas_call(
        _in_kernel,
        grid=(n // TM,),
        in_specs=[row(D_MODEL), _mod_spec(layer, lambda i: 1 + i // per_b)]
        + [_layer_spec(a, layer) for a in (g, qg, kvg, w_p)],
        out_specs=[row(w) for w, _ in outs],
        out_shape=[jax.ShapeDtypeStruct((n, w), dt) for w, dt in outs],
        compiler_params=_cparams(),
        name="in_proj_s",
    )(x, mod, g, qg, kvg, w_p)


def _sample_rows(width, col=0):
    return pl.BlockSpec((SAMPLE_SPS * DEC_SEQ, width), lambda b: (b, col))


def _sample_cached(layer, tail):
    return pl.BlockSpec((SAMPLE_SPS, None) + tail, lambda b: (b, layer) + (0,) * len(tail))


def _ret_fourier_kernel(T, qkv_ref, rz_ref, fz_ref, fu_ref, lg_ref, st_ref, csw_ref, cts_ref,
                        o_ref, mask_ref):
    lg = jax.nn.log_sigmoid(lg_ref[...])

    @pl.when(pl.program_id(0) == 0)
    def _():
        _init_masks(T // RET_BLK, lg, mask_ref)

    qw = H_RET * DK_RET
    for s in range(SAMPLE_SPS):
        rows = slice(s * T, (s + 1) * T)
        for h in range(H_RET):
            q = qkv_ref[rows, h * DK_RET:(h + 1) * DK_RET]
            k = qkv_ref[rows, qw + h * DK_RET:qw + (h + 1) * DK_RET]
            v = qkv_ref[rows, 2 * qw + h * DV_RET:2 * qw + (h + 1) * DV_RET]
            on = _ret_head(T, q, k, v, mask_ref.at[h], lg[0:1, h:h + 1], lg[1:2, h:h + 1],
                           st=(st_ref[s, 0, h], st_ref[s, 1, h]))
            gate = rz_ref[rows, h * DV_RET:(h + 1) * DV_RET].astype(F32)
            o_ref[rows, h * DV_RET:(h + 1) * DV_RET] = (on * gate).astype(BF16)
        o = _fourier(T, fu_ref[rows], csw_ref, cts_ref)
        o_ref[rows, RET_W:] = (o * fz_ref[rows].astype(F32)).astype(BF16)


def _ret_fourier_call(layer, qkv, sz, fu, logit, state, csw, cts):
    T = DEC_SEQ
    return pl.pallas_call(
        functools.partial(_ret_fourier_kernel, T),
        grid=(DEC_BATCH // SAMPLE_SPS,),
        in_specs=[_sample_rows(1024), _sample_rows(BRANCH_W, 0), _sample_rows(BRANCH_W, 2),
                  _sample_rows(FOURIER_W), _layer_spec(logit, layer),
                  _sample_cached(layer, ST_TAIL), _full_spec(csw), _full_spec(cts)],
        out_specs=_sample_rows(RET_W + FOURIER_W),
        out_shape=jax.ShapeDtypeStruct((T * DEC_BATCH, RET_W + FOURIER_W), BF16),
        scratch_shapes=[pltpu.VMEM((H_RET, 3, RET_BLK, RET_BLK), F32)],
        compiler_params=_cparams(),
        name="ret_fourier_s",
    )(qkv, sz, sz, fu, logit, state, csw, cts)


def _mla_kernel(T, qn_ref, ckv_ref, kr_ref, mz_ref, wq_ref, wk_ref, wv_ref, cckv_ref, ckr_ref,
                cos_ref, sa_ref, sb_ref, bm_ref):
    cos, sa, sb = cos_ref[...], sa_ref[...], sb_ref[...]
    slot = lambda a, i: a[:, i * LANE:(i + 1) * LANE]
    for s in range(SAMPLE_SPS):
        rows = slice(s * T, (s + 1) * T)
        qn = qn_ref[rows]
        ckv = ckv_ref[rows]
        kr = _rope(kr_ref[rows], cos, sa, sb)
        cc = cckv_ref[s].astype(BF16)
        for p in range(H_MLA // 2):
            sl = slice(2 * p * LANE, 2 * (p + 1) * LANE)
            q2 = _dot(qn, wq_ref[:, sl])
            kc2 = _dot(cc, wk_ref[:, sl])
            kn2 = _dot(ckv, wk_ref[:, sl])
            if p % 2 == 0:
                vsl = slice(p * LANE, (p + 2) * LANE)
                v4 = jnp.concatenate([_dot(cc, wv_ref[:, vsl]), _dot(ckv, wv_ref[:, vsl])], axis=0)
            qs = [(_rope(slot(q2, i), cos, sa, sb) * Q_SCALE).astype(BF16) for i in range(2)]
            ks = [jnp.concatenate([(slot(kc2, i) + ckr_ref[s]).astype(BF16),
                                   (slot(kn2, i) + kr).astype(BF16)], axis=0) for i in range(2)]
            o = _attend_pair(qs, ks, _value_slots(slot(v4, p % 2)))
            gate = mz_ref[rows, p * LANE:(p + 1) * LANE].astype(F32)
            bm_ref[rows, p * LANE:(p + 1) * LANE] = (o * gate).astype(BF16)


def _mla_call(layer, qn, ckv, kr, sz, wq, wk, wv, cache_ckv, cache_kr, cos, sa, sb):
    T = DEC_SEQ
    return pl.pallas_call(
        functools.partial(_mla_kernel, T),
        grid=(DEC_BATCH // SAMPLE_SPS,),
        in_specs=[_sample_rows(Q_LORA), _sample_rows(KV_LORA), _sample_rows(LANE),
                  _sample_rows(BRANCH_W, 1)]
        + [_layer_spec(a, layer) for a in (wq, wk, wv)]
        + [_sample_cached(layer, (PAST_LEN, KV_LORA)), _sample_cached(layer, (PAST_LEN, LANE)),
           _full_spec(cos), _full_spec(sa), _full_spec(sb)],
        out_specs=_sample_rows(MLA_W),
        out_shape=jax.ShapeDtypeStruct((T * DEC_BATCH, MLA_W), BF16),
        compiler_params=_cparams(),
        name="mla_s",
    )(qn, ckv, kr, sz, wq, wk, wv, cache_ckv, cache_kr, cos, sa, sb)


def _merge_kernel(final, *refs):
    br_ref, bm_ref, bf_ref, sg_ref, x_ref, mod_ref, wb_ref, wo_ref = refs[:8]
    fg_ref = refs[8] if final else None
    sg = [sg_ref[:, n * D_MODEL:(n + 1) * D_MODEL].astype(F32) for n in range(N_BRANCH)]
    refs[-1][...] = _merge((br_ref[...], bm_ref[...], bf_ref[...]), sg, x_ref[...],
                           mod_ref, wb_ref, wo_ref, fg_ref)


def _merge_call(layer, brf, bm, sg, x, mod, wb, wo, final_g=None):
    final = final_g is not None
    n = x.shape[0]
    per_b = DEC_SEQ // MERGE_TM
    row = lambda w, c=0: pl.BlockSpec((MERGE_TM, w), lambda i: (i, c))
    in_specs = [row(BRANCH_W), row(BRANCH_W), row(BRANCH_W, 1), row(N_BRANCH * D_MODEL), row(D_MODEL),
                _mod_spec(layer, lambda i: 1 + i // per_b),
                _layer_spec(wb, layer), _layer_spec(wo, layer)]
    args = [brf, bm, brf, sg, x, mod, wb, wo]
    if final:
        in_specs.append(_full_spec(final_g))
        args.append(final_g)
    return pl.pallas_call(
        functools.partial(_merge_kernel, final),
        grid=(n // MERGE_TM,),
        in_specs=in_specs,
        out_specs=row(D_MODEL),
        out_shape=jax.ShapeDtypeStruct((n, D_MODEL), F32),
        compiler_params=_cparams(),
        name="merge_s",
    )(*args)


def _dft_tables(T):
    def cs(n):
        kt = (np.arange(n)[:, None] * np.arange(n)[None, :]) % n
        ang = 2.0 * np.pi * kt.astype(np.float64) / n
        return np.cos(ang), np.sin(ang)
    ct, st = cs(T)
    cw, sw = cs(F_GROUP_W)
    cts = np.concatenate([ct, -st], axis=1).astype(np.float32)
    csw = np.concatenate([cw, sw], axis=1).astype(np.float32)
    return jnp.asarray(csw).astype(BF16), jnp.asarray(cts).astype(BF16)


def _rope_tables(T):
    half = D_ROPE // 2
    nfreq = half // 2
    inv = ROPE_BASE ** (-np.arange(nfreq, dtype=np.float64) / nfreq)
    t = np.arange(T)
    pos = np.stack([t // GRID_W, t % GRID_W], axis=0).astype(np.float64)
    cos = np.ones((T, LANE), np.float64)
    sa = np.zeros((T, LANE), np.float64)
    sb = np.zeros((T, LANE), np.float64)
    for part in range(2):
        ang = pos[part][:, None] * inv[None, :]
        l1 = ROPE_LANE0 + part * half
        l2 = l1 + nfreq
        cos[:, l1:l1 + nfreq] = np.cos(ang)
        cos[:, l2:l2 + nfreq] = np.cos(ang)
        sa[:, l1:l1 + nfreq] = -np.sin(ang)
        sb[:, l2:l2 + nfreq] = np.sin(ang)
    return tuple(jnp.asarray(a.astype(np.float32)) for a in (cos, sa, sb))


KR_SRC = 2176
W_IN_RUNS = ((0, 1024),
             (1024, 512), (2208, 512), (3232, 512),
             (3744, 3072),
             (1536, 384),
             (KR_SRC - ROPE_LANE0, LANE),
             (1920, 256),
             (2720, 512))
W_PREP_TILES = 18
KR_TILE = SEG_KR[0] // LANE


def _w_prep_kernel(tbl_ref, *refs):
    o_ref = refs[-1]
    j = pl.program_id(1)
    lane = lax.broadcasted_iota(jnp.int32, (D_MODEL, LANE), 1)
    rotary = (lane >= ROPE_LANE0) & (lane < ROPE_LANE0 + D_ROPE)
    for t, w_ref in enumerate(refs[:-1]):
        wt = w_ref[0].T
        if t == KR_TILE % W_PREP_TILES:
            wt = jnp.where((j != KR_TILE // W_PREP_TILES) | rotary, wt, 0.0)
        o_ref[:, t * LANE:(t + 1) * LANE] = wt.astype(BF16)


def _permute_w_in(w):
    depth, d, in_w = w.shape
    starts = np.concatenate([src + np.arange(0, width, LANE) for src, width in W_IN_RUNS])
    assert starts.size * LANE == IN_WP and starts.size % W_PREP_TILES == 0
    tbl = jnp.asarray(starts, jnp.int32)
    tile_spec = lambda t: pl.BlockSpec(
        (pl.Element(1), pl.Element(LANE), pl.Element(d)),
        lambda l, j, tbl_ref: (l, pl.multiple_of(tbl_ref[j * W_PREP_TILES + t], D_ROPE), 0))
    wide = W_PREP_TILES * LANE
    return pl.pallas_call(
        _w_prep_kernel,
        grid_spec=pltpu.PrefetchScalarGridSpec(
            num_scalar_prefetch=1,
            grid=(depth, IN_WP // wide),
            in_specs=[tile_spec(t) for t in range(W_PREP_TILES)],
            out_specs=pl.BlockSpec((None, d, wide), lambda l, j, tbl_ref: (l, 0, j)),
        ),
        out_shape=jax.ShapeDtypeStruct((depth, d, IN_WP), BF16),
        compiler_params=_cparams(2),
        name="w_prep",
    )(tbl, *([jnp.swapaxes(w, 1, 2)] * W_PREP_TILES))


def _pad_heads(w, n_heads, width, lo, hi):
    d, k = w.shape[:2]
    wh = w.reshape(d, k, n_heads, width)[..., lo:hi]
    wh = jnp.pad(wh, ((0, 0), (0, 0), (0, 0), (0, LANE - (hi - lo))))
    return wh.reshape(d, k, n_heads * LANE).astype(BF16)


def kernel(x_prompt, x_sample, cache_ckv, cache_krope, state_ret, c, c_ctx, norm_g, w_mod, b_mod,
           w_in, ret_decay_logit, q_norm_g, w_q_up, kv_norm_g, w_kv_up, w_branch, w_out,
           final_norm_g):
    cv = jnp.concatenate([c_ctx[None, :], c, jnp.zeros((MOD_ROWS - 1 - DEC_BATCH, D_MODEL), F32)], axis=0)
    mod = _mod_call(cv, w_mod, b_mod).reshape(DEPTH, MOD_ROWS, 1, 3 * D_MODEL)

    cache_kr = jnp.pad(cache_krope, ((0, 0), (0, 0), (0, 0), (ROPE_LANE0, LANE - ROPE_LANE0 - D_ROPE)))
    rope = _rope_tables(DEC_SEQ)
    dft_p = _dft_tables(SEQ)
    dft_s = _dft_tables(DEC_SEQ)
    logit = jnp.pad(ret_decay_logit, ((0, 0), (0, 8 - 2), (0, LANE - H_RET)))
    g, qg, kvg = norm_g[:, None, :], q_norm_g[:, None, :], kv_norm_g[:, None, :]
    w_p = _permute_w_in(w_in)
    wq = _pad_heads(w_q_up, H_MLA, D_NOPE + D_ROPE, 0, D_NOPE + D_ROPE)
    wk = _pad_heads(w_kv_up, H_MLA, D_NOPE + D_VMLA, 0, D_NOPE)
    wv = w_kv_up.reshape(DEPTH, KV_LORA, H_MLA, D_NOPE + D_VMLA)[..., D_NOPE:]
    wv = wv.reshape(DEPTH, KV_LORA, MLA_W).astype(BF16)
    wb = w_branch.astype(BF16)
    wo = w_out.astype(BF16)
    final_g = final_norm_g[None, :]

    xp = x_prompt.reshape(BATCH * SEQ, D_MODEL)
    xs = x_sample.reshape(DEC_BATCH * DEC_SEQ, D_MODEL)
    new_ctx = None
    for l in range(DEPTH):
        fg = final_g if l == DEPTH - 1 else None
        xp, *new_ctx = _prompt_call(l, xp, mod, g, qg, kvg, w_p, logit, wq, wk, wv, *dft_p, wb, wo,
                                    final_g=fg, prev=new_ctx)

        qkv, sz, sg, qn, ckv, kr, fu = _in_call(l, xs, mod, g, qg, kvg, w_p)
        brf = _ret_fourier_call(l, qkv, sz, fu, logit, state_ret, *dft_s)
        bm = _mla_call(l, qn, ckv, kr, sz, wq, wk, wv, cache_ckv, cache_kr, *rope)
        xs = _merge_call(l, brf, bm, sg, xs, mod, wb, wo, final_g=fg)

    y_prompt = xp.reshape(BATCH, SEQ, D_MODEL)
    y_sample = xs.reshape(DEC_BATCH, DEC_SEQ, D_MODEL)
    new_ckv, new_krope_t, new_ret = new_ctx
    return (y_prompt, y_sample, new_ckv, jnp.swapaxes(new_krope_t, 2, 3), new_ret)
```

```python
import collections
import functools
import math

import numpy as np
import jax
import jax.numpy as jnp
from jax import lax
from jax.experimental import pallas as pl
from jax.experimental.pallas import tpu as pltpu

F32 = jnp.float32
BF16 = jnp.bfloat16

D_MODEL = 1024
BATCH = 32
SEQ = 256
DEPTH = 2
DEC_BATCH = 4
DEC_SEQ = 1024
PAST_LEN = 512
GRID_W = 64
EPS = 1e-6
H_RET = 4
DK_RET = 64
DV_RET = 128
RET_W = H_RET * DV_RET
H_MLA = 8
Q_LORA = 384
KV_LORA = 256
D_NOPE = 64
D_ROPE = 32
D_VMLA = 64
MLA_W = H_MLA * D_VMLA
ROPE_BASE = 10000.0
F_GROUPS = 4
F_GROUP_W = 128
FOURIER_W = F_GROUPS * F_GROUP_W
N_BRANCH = 3
BRANCH_W = 512

LANE = 128
TM = 512
MERGE_TM = 1024
IN_CHAINS = 2
PROMPT_SPS = 2
SAMPLE_SPS = 1
MOD_ROWS = 8
ROPE_LANE0 = D_NOPE
VMEM_LIMIT = 60 * 1024 * 1024
ST_TAIL = (2, H_RET, DK_RET, DV_RET)

SEG_QKV = (0, 1024)
SEG_Z = (1024, 2560)
SEG_G = (2560, 5632)
SEG_QL = (5632, 6016)
SEG_KR = (6016, 6144)
SEG_KV = (6144, 6400)
SEG_FU = (6400, 6912)
IN_WP = 6912

NT = (((1,), (1,)), ((), ()))
TN = (((0,), (0,)), ((), ()))


def _cparams(n_axes=1):
    return pltpu.CompilerParams(dimension_semantics=("arbitrary",) * n_axes,
                                vmem_limit_bytes=VMEM_LIMIT)


def _layer_spec(a, layer):
    shape = a.shape[1:]
    return pl.BlockSpec((None,) + shape, lambda i: (layer,) + (0,) * len(shape))


def _full_spec(a):
    return pl.BlockSpec(a.shape, lambda i: (0,) * a.ndim)


def _mod_spec(layer, row):
    return pl.BlockSpec((None, None, 1, 3 * D_MODEL), lambda i: (layer, row(i), 0, 0))


def _grow_spec(depth, tail):
    return pl.BlockSpec((PROMPT_SPS, depth) + tail, lambda i: (i,) + (0,) * (1 + len(tail)))


def _dot(a, b):
    return jnp.dot(a, b, preferred_element_type=F32)


def _rms(x, g):
    return x * lax.rsqrt(jnp.mean(x * x, axis=-1, keepdims=True) + EPS) * g


def _mod_kernel(c_ref, w_ref, b_ref, o_ref):
    cv = c_ref[...]
    s = cv * jax.nn.sigmoid(cv)
    o_ref[0] = _dot(s.astype(BF16), w_ref[0].astype(BF16)) + b_ref[0]


def _mod_call(cv, w_mod, b_mod):
    nb = 3 * D_MODEL // 1024
    return pl.pallas_call(
        _mod_kernel,
        grid=(DEPTH, nb),
        in_specs=[
            pl.BlockSpec((MOD_ROWS, D_MODEL), lambda l, j: (0, 0)),
            pl.BlockSpec((1, D_MODEL, 1024), lambda l, j: (l, 0, j)),
            pl.BlockSpec((1, 1, 1024), lambda l, j: (l, 0, j)),
        ],
        out_specs=pl.BlockSpec((1, MOD_ROWS, 1024), lambda l, j: (l, 0, j)),
        out_shape=jax.ShapeDtypeStruct((DEPTH, MOD_ROWS, 3 * D_MODEL), F32),
        compiler_params=_cparams(2),
        name="mod",
    )(cv, w_mod, b_mod.reshape(DEPTH, 1, 3 * D_MODEL))


Proj = collections.namedtuple("Proj", "q k v sz sg qn ckv kr fu")


def _modulate(x, mod_ref, g_ref):
    shift = mod_ref[:, 0:D_MODEL]
    scale = mod_ref[:, D_MODEL:2 * D_MODEL]
    return (_rms(x, g_ref[...]) * (1.0 + scale) + shift).astype(BF16)


def _project(x, mod_ref, g_ref, qg_ref, kvg_ref, w_ref):
    return _project_h(_modulate(x, mod_ref, g_ref), qg_ref, kvg_ref, w_ref)


def _project_h(hb, qg_ref, kvg_ref, w_ref):
    def mm(a, b):
        return _dot(hb, w_ref[:, a:b])

    qw = H_RET * DK_RET
    p = mm(*SEG_QKV)
    q = p[:, 0:qw].astype(BF16)
    k = (p[:, qw:2 * qw] * (DK_RET ** -0.5)).astype(BF16)
    v = p[:, 2 * qw:].astype(BF16)
    sz = []
    for j in range(3):
        a = SEG_Z[0] + j * BRANCH_W
        p = mm(a, a + BRANCH_W)
        sz.append(p * jax.nn.sigmoid(p))
    sg = []
    for j in range(N_BRANCH):
        a = SEG_G[0] + j * D_MODEL
        sg.append(jax.nn.sigmoid(mm(a, a + D_MODEL)))
    p = mm(SEG_QL[0], SEG_KR[1])
    qn = _rms(p[:, :Q_LORA], qg_ref[...]).astype(BF16)
    kr = p[:, Q_LORA:]
    ckv = _rms(mm(*SEG_KV), kvg_ref[...])
    fu = mm(*SEG_FU).astype(BF16)
    return Proj(q, k, v, sz, sg, qn, ckv, kr, fu)


RET_BLK = 256


def _init_masks(n_blk, lg, mask_ref):
    B = RET_BLK
    ii = lax.broadcasted_iota(jnp.int32, (B, B), 0)
    jj = lax.broadcasted_iota(jnp.int32, (B, B), 1)
    d = (ii - jj).astype(F32)
    ad = jnp.abs(d)
    for h in range(H_RET):
        lf = lg[0:1, h:h + 1]
        lb = lg[1:2, h:h + 1]
        mask_ref[h, 0] = jnp.where(d > 0, jnp.exp(ad * lf), jnp.where(d < 0, jnp.exp(ad * lb), 2.0))
        if n_blk > 1:
            mask_ref[h, 1] = jnp.exp((B + d) * lf)
            mask_ref[h, 2] = jnp.exp((B - d) * lb)


def _masked_scores(s, mask, lf, lb):
    B = RET_BLK
    n_blk = s.shape[0] // B
    if n_blk == 1:
        return (s * mask[0]).astype(BF16)
    rows = []
    for i in range(n_blk):
        cols = []
        for j in range(n_blk):
            blk = s[i * B:(i + 1) * B, j * B:(j + 1) * B]
            far = abs(i - j) - 1
            if far < 0:
                m = mask[0]
            else:
                m = mask[1] if j < i else mask[2]
                if far:
                    m = m * jnp.exp((far * B) * (lf if j < i else lb))
            cols.append((blk * m).astype(BF16))
        rows.append(jnp.concatenate(cols, axis=1))
    return jnp.concatenate(rows, axis=0)


def _ret_head(T, q, k, v, mask, lf, lb, st=None):
    s = lax.dot_general(q, k, NT, preferred_element_type=F32)
    o = _dot(_masked_scores(s, mask, lf, lb), v)
    if st is not None:
        c = _dot(q, jnp.concatenate([st[0], st[1]], axis=1).astype(BF16))
        rows = lax.broadcasted_iota(jnp.int32, (T, DV_RET), 0).astype(F32)
        o = o + jnp.exp((rows + 1.0) * lf) * c[:, :DV_RET]
        o = o + jnp.exp((T - rows) * lb) * c[:, DV_RET:]
    mu = jnp.mean(o, axis=-1, keepdims=True)
    c = o - mu
    var = jnp.mean(c * c, axis=-1, keepdims=True)
    return c * lax.rsqrt(var + EPS)


def _ret_states(T, k, v, lf, lb):
    rows = lax.broadcasted_iota(jnp.int32, (T, DK_RET), 0).astype(F32)
    kf = (k.astype(F32) * jnp.exp((T - 1.0 - rows) * lf)).astype(BF16)
    kb = (k.astype(F32) * jnp.exp(rows * lb)).astype(BF16)
    return (lax.dot_general(kf, v, TN, preferred_element_type=F32),
            lax.dot_general(kb, v, TN, preferred_element_type=F32))


def _rope(x, cos, sa, sb):
    half = D_ROPE // 4
    return x * cos + pltpu.roll(x, LANE - half, 1) * sa + pltpu.roll(x, half, 1) * sb


Q_SCALE = (D_NOPE + D_ROPE) ** -0.5 * math.log2(math.e)


def _value_slots(vp):
    first = lax.broadcasted_iota(jnp.int32, vp.shape, 1) < D_VMLA
    return (jnp.where(first, vp, 1.0).astype(BF16), jnp.where(first, 1.0, vp).astype(BF16))


def _attend_pair(q2, k2, v2):
    outs = []
    for qh, kh, vh in zip(q2, k2, v2):
        s = lax.dot_general(qh, kh, NT, preferred_element_type=F32)
        e = jnp.exp2((s - jnp.max(s, axis=-1, keepdims=True)).astype(BF16))
        outs.append(_dot(e, vh))
    first = lax.broadcasted_iota(jnp.int32, outs[0].shape, 1) < D_VMLA
    o = jnp.where(first, outs[0], outs[1])
    l = jnp.where(first, pltpu.roll(outs[0], D_VMLA, 1), pltpu.roll(outs[1], D_VMLA, 1))
    return o / l


def _fourier(T, fu, csw_ref, cts_ref):
    a, b = [], []
    for g in range(F_GROUPS):
        z = _dot(fu[:, g * F_GROUP_W:(g + 1) * F_GROUP_W], csw_ref[...])
        a.append(z[:, :F_GROUP_W])
        b.append(z[:, F_GROUP_W:])
    a = jnp.concatenate(a, axis=1).astype(BF16)
    b = jnp.concatenate(b, axis=1).astype(BF16)
    o = _dot(cts_ref[:, 0:T], a) + _dot(cts_ref[:, T:2 * T], b)
    return o * ((T * F_GROUP_W) ** -0.5)


def _merge(branches, sg, x, mod_ref, wb_ref, wo_ref, fg_ref):
    merged = None
    for n, b in enumerate(branches):
        term = _dot(b, wb_ref[n]) * sg[n]
        merged = term if merged is None else merged + term
    gate = mod_ref[:, 2 * D_MODEL:3 * D_MODEL]
    y = x + gate * _dot(merged.astype(BF16), wo_ref[...])
    if fg_ref is not None:
        y = _rms(y, fg_ref[...])
    return y


def _prompt_kernel(layer, final, *refs):
    (x_ref, mod_ref, g_ref, qg_ref, kvg_ref, w_ref, lg_ref, wq_ref, wk_ref, wv_ref,
     csw_ref, cts_ref, wb_ref, wo_ref) = refs[:14]
    n_in = 14
    fg_ref = None
    if final:
        fg_ref = refs[n_in]
        n_in += 1
    if layer:
        pckv_ref, pkr_ref, pst_ref = refs[n_in:n_in + 3]
        n_in += 3
    y_ref, nckv_ref, nkr_ref, nst_ref, mask_ref = refs[n_in:]
    T = SEQ
    lg = jax.nn.log_sigmoid(lg_ref[...])

    @pl.when(pl.program_id(0) == 0)
    def _():
        _init_masks(T // RET_BLK, lg, mask_ref)

    seqs = range(PROMPT_SPS)
    rows = [slice(s * T, (s + 1) * T) for s in seqs]
    slot = lambda a, i: a[:, i * LANE:(i + 1) * LANE]
    xs = [x_ref[r] for r in rows]
    prs = [_project(x, mod_ref, g_ref, qg_ref, kvg_ref, w_ref) for x in xs]

    for s, pr in zip(seqs, prs):
        if layer:
            nckv_ref[s, 0:layer] = pckv_ref[s]
            nkr_ref[s, 0:layer] = pkr_ref[s]
            nst_ref[s, 0:layer] = pst_ref[s]
        nckv_ref[s, layer] = pr.ckv
        nkr_ref[s, layer] = pr.kr.T[ROPE_LANE0:ROPE_LANE0 + D_ROPE]

    qs, ks, vs = [], [], []
    for pr in prs:
        ckvb = pr.ckv.astype(BF16)
        q_all = _dot(pr.qn, wq_ref[...]) * Q_SCALE
        k_all = _dot(ckvb, wk_ref[...])
        vs.append(_dot(ckvb, wv_ref[...]))
        qs.append([slot(q_all, h).astype(BF16) for h in range(H_MLA)])
        ks.append([(slot(k_all, h) + pr.kr).astype(BF16) for h in range(H_MLA)])
    o_m = [[] for _ in seqs]
    for p in range(H_MLA // 2):
        for s in seqs:
            o_m[s].append(_attend_pair(qs[s][2 * p:2 * p + 2], ks[s][2 * p:2 * p + 2],
                                       _value_slots(slot(vs[s], p))))

    o_f = [_fourier(T, pr.fu, csw_ref, cts_ref) for pr in prs]

    o_r = [[] for _ in seqs]
    for h in range(H_RET):
        lf = lg[0:1, h:h + 1]
        lb = lg[1:2, h:h + 1]
        for s, pr in zip(seqs, prs):
            q = pr.q[:, h * DK_RET:(h + 1) * DK_RET]
            k = pr.k[:, h * DK_RET:(h + 1) * DK_RET]
            v = pr.v[:, h * DV_RET:(h + 1) * DV_RET]
            o_r[s].append(_ret_head(T, q, k, v, mask_ref.at[h], lf, lb))
            sf, sb = _ret_states(T, k, v, lf, lb)
            nst_ref[s, layer, 0, h] = sf
            nst_ref[s, layer, 1, h] = sb

    for s, pr in zip(seqs, prs):
        branches = [(o * gate).astype(BF16) for o, gate in zip(
            (jnp.concatenate(o_r[s], axis=1), jnp.concatenate(o_m[s], axis=1), o_f[s]), pr.sz)]
        y_ref[rows[s]] = _merge(branches, pr.sg, xs[s], mod_ref, wb_ref, wo_ref, fg_ref)


def _prompt_call(layer, x, mod, g, qg, kvg, w_p, logit, wq, wk, wv, csw, cts, wb, wo,
                 final_g=None, prev=None):
    T = SEQ
    n_steps = BATCH // PROMPT_SPS
    row = pl.BlockSpec((PROMPT_SPS * T, D_MODEL), lambda i: (i, 0))
    in_specs = [row, _mod_spec(layer, lambda i: 0)]
    in_specs += [_layer_spec(a, layer) for a in (g, qg, kvg, w_p, logit, wq, wk, wv)]
    in_specs += [_full_spec(csw), _full_spec(cts), _layer_spec(wb, layer), _layer_spec(wo, layer)]
    args = [x, mod, g, qg, kvg, w_p, logit, wq, wk, wv, csw, cts, wb, wo]
    if final_g is not None:
        in_specs.append(_full_spec(final_g))
        args.append(final_g)
    tails = ((SEQ, KV_LORA), (D_ROPE, SEQ), ST_TAIL)
    if layer:
        in_specs += [_grow_spec(layer, t) for t in tails]
        args += list(prev)
    return pl.pallas_call(
        functools.partial(_prompt_kernel, layer, final_g is not None),
        grid=(n_steps,),
        in_specs=in_specs,
        out_specs=[row] + [_grow_spec(layer + 1, t) for t in tails],
        out_shape=[jax.ShapeDtypeStruct((BATCH * T, D_MODEL), F32)]
        + [jax.ShapeDtypeStruct((BATCH, layer + 1) + t, F32) for t in tails],
        scratch_shapes=[pltpu.VMEM((H_RET, 1, RET_BLK, RET_BLK), F32)],
        compiler_params=_cparams(),
        name="prompt_layer",
    )(*args)


def _in_kernel(x_ref, mod_ref, g_ref, qg_ref, kvg_ref, w_ref,
               qkv_ref, sz_ref, sg_ref, qn_ref, ckv_ref, kr_ref, fu_ref):
    qw = H_RET * DK_RET
    half = TM // IN_CHAINS
    for c in range(IN_CHAINS):
        rows = slice(c * half, (c + 1) * half)
        pr = _project(x_ref[rows], mod_ref, g_ref, qg_ref, kvg_ref, w_ref)
        qkv_ref[rows, 0:qw] = pr.q
        qkv_ref[rows, qw:2 * qw] = pr.k
        qkv_ref[rows, 2 * qw:] = pr.v
        for j in range(3):
            sz_ref[rows, j * BRANCH_W:(j + 1) * BRANCH_W] = pr.sz[j].astype(BF16)
        for j in range(N_BRANCH):
            sg_ref[rows, j * D_MODEL:(j + 1) * D_MODEL] = pr.sg[j].astype(BF16)
        qn_ref[rows] = pr.qn
        ckv_ref[rows] = pr.ckv.astype(BF16)
        kr_ref[rows] = pr.kr
        fu_ref[rows] = pr.fu


def _in_call(layer, x, mod, g, qg, kvg, w_p):
    n = x.shape[0]
    per_b = DEC_SEQ // TM
    row = lambda w: pl.BlockSpec((TM, w), lambda i: (i, 0))
    outs = [(1024, BF16), (3 * BRANCH_W, BF16), (N_BRANCH * D_MODEL, BF16), (Q_LORA, BF16),
            (KV_LORA, BF16), (LANE, F32), (FOURIER_W, BF16)]
    return pl.pallas_call(
        _in_kernel,
        grid=(n // TM,),
        in_specs=[row(D_MODEL), _mod_spec(layer, lambda i: 1 + i // per_b)]
        + [_layer_spec(a, layer) for a in (g, qg, kvg, w_p)],
        out_specs=[row(w) for w, _ in outs],
        out_shape=[jax.ShapeDtypeStruct((n, w), dt) for w, dt in outs],
        compiler_params=_cparams(),
        name="in_proj_s",
    )(x, mod, g, qg, kvg, w_p)


def _sample_rows(width, col=0):
    return pl.BlockSpec((SAMPLE_SPS * DEC_SEQ, width), lambda b: (b, col))


def _sample_cached(layer, tail):
    return pl.BlockSpec((SAMPLE_SPS, None) + tail, lambda b: (b, layer) + (0,) * len(tail))


def _ret_fourier_kernel(T, qkv_ref, rz_ref, fz_ref, fu_ref, lg_ref, st_ref, csw_ref, cts_ref,
                        o_ref, mask_ref):
    lg = jax.nn.log_sigmoid(lg_ref[...])

    @pl.when(pl.program_id(0) == 0)
    def _():
        _init_masks(T // RET_BLK, lg, mask_ref)

    qw = H_RET * DK_RET
    for s in range(SAMPLE_SPS):
        rows = slice(s * T, (s + 1) * T)
        for h in range(H_RET):
            q = qkv_ref[rows, h * DK_RET:(h + 1) * DK_RET]
            k = qkv_ref[rows, qw + h * DK_RET:qw + (h + 1) * DK_RET]
            v = qkv_ref[rows, 2 * qw + h * DV_RET:2 * qw + (h + 1) * DV_RET]
            on = _ret_head(T, q, k, v, mask_ref.at[h], lg[0:1, h:h + 1], lg[1:2, h:h + 1],
                           st=(st_ref[s, 0, h], st_ref[s, 1, h]))
            gate = rz_ref[rows, h * DV_RET:(h + 1) * DV_RET].astype(F32)
            o_ref[rows, h * DV_RET:(h + 1) * DV_RET] = (on * gate).astype(BF16)
        o = _fourier(T, fu_ref[rows], csw_ref, cts_ref)
        o_ref[rows, RET_W:] = (o * fz_ref[rows].astype(F32)).astype(BF16)


def _ret_fourier_call(layer, qkv, sz, fu, logit, state, csw, cts):
    T = DEC_SEQ
    return pl.pallas_call(
        functools.partial(_ret_fourier_kernel, T),
        grid=(DEC_BATCH // SAMPLE_SPS,),
        in_specs=[_sample_rows(1024), _sample_rows(BRANCH_W, 0), _sample_rows(BRANCH_W, 2),
                  _sample_rows(FOURIER_W), _layer_spec(logit, layer),
                  _sample_cached(layer, ST_TAIL), _full_spec(csw), _full_spec(cts)],
        out_specs=_sample_rows(RET_W + FOURIER_W),
        out_shape=jax.ShapeDtypeStruct((T * DEC_BATCH, RET_W + FOURIER_W), BF16),
        scratch_shapes=[pltpu.VMEM((H_RET, 3, RET_BLK, RET_BLK), F32)],
        compiler_params=_cparams(),
        name="ret_fourier_s",
    )(qkv, sz, sz, fu, logit, state, csw, cts)


def _mla_kernel(T, qn_ref, ckv_ref, kr_ref, mz_ref, wq_ref, wk_ref, wv_ref, cckv_ref, ckr_ref,
                cos_ref, sa_ref, sb_ref, bm_ref):
    cos, sa, sb = cos_ref[...], sa_ref[...], sb_ref[...]
    slot = lambda a, i: a[:, i * LANE:(i + 1) * LANE]
    for s in range(SAMPLE_SPS):
        rows = slice(s * T, (s + 1) * T)
        qn = qn_ref[rows]
        ckv = ckv_ref[rows]
        kr = _rope(kr_ref[rows], cos, sa, sb)
        cc = cckv_ref[s].astype(BF16)
        for p in range(H_MLA // 2):
            sl = slice(2 * p * LANE, 2 * (p + 1) * LANE)
            q2 = _dot(qn, wq_ref[:, sl])
            kc2 = _dot(cc, wk_ref[:, sl])
            kn2 = _dot(ckv, wk_ref[:, sl])
            if p % 2 == 0:
                vsl = slice(p * LANE, (p + 2) * LANE)
                v4 = jnp.concatenate([_dot(cc, wv_ref[:, vsl]), _dot(ckv, wv_ref[:, vsl])], axis=0)
            qs = [(_rope(slot(q2, i), cos, sa, sb) * Q_SCALE).astype(BF16) for i in range(2)]
            ks = [jnp.concatenate([(slot(kc2, i) + ckr_ref[s]).astype(BF16),
                                   (slot(kn2, i) + kr).astype(BF16)], axis=0) for i in range(2)]
            o = _attend_pair(qs, ks, _value_slots(slot(v4, p % 2)))
            gate = mz_ref[rows, p * LANE:(p + 1) * LANE].astype(F32)
            bm_ref[rows, p * LANE:(p + 1) * LANE] = (o * gate).astype(BF16)


def _mla_call(layer, qn, ckv, kr, sz, wq, wk, wv, cache_ckv, cache_kr, cos, sa, sb):
    T = DEC_SEQ
    return pl.pallas_call(
        functools.partial(_mla_kernel, T),
        grid=(DEC_BATCH // SAMPLE_SPS,),
        in_specs=[_sample_rows(Q_LORA), _sample_rows(KV_LORA), _sample_rows(LANE),
                  _sample_rows(BRANCH_W, 1)]
        + [_layer_spec(a, layer) for a in (wq, wk, wv)]
        + [_sample_cached(layer, (PAST_LEN, KV_LORA)), _sample_cached(layer, (PAST_LEN, LANE)),
           _full_spec(cos), _full_spec(sa), _full_spec(sb)],
        out_specs=_sample_rows(MLA_W),
        out_shape=jax.ShapeDtypeStruct((T * DEC_BATCH, MLA_W), BF16),
        compiler_params=_cparams(),
        name="mla_s",
    )(qn, ckv, kr, sz, wq, wk, wv, cache_ckv, cache_kr, cos, sa, sb)


def _merge_kernel(final, *refs):
    br_ref, bm_ref, bf_ref, sg_ref, x_ref, mod_ref, wb_ref, wo_ref = refs[:8]
    fg_ref = refs[8] if final else None
    sg = [sg_ref[:, n * D_MODEL:(n + 1) * D_MODEL].astype(F32) for n in range(N_BRANCH)]
    refs[-1][...] = _merge((br_ref[...], bm_ref[...], bf_ref[...]), sg, x_ref[...],
                           mod_ref, wb_ref, wo_ref, fg_ref)


def _merge_call(layer, brf, bm, sg, x, mod, wb, wo, final_g=None):
    final = final_g is not None
    n = x.shape[0]
    per_b = DEC_SEQ // MERGE_TM
    row = lambda w, c=0: pl.BlockSpec((MERGE_TM, w), lambda i: (i, c))
    in_specs = [row(BRANCH_W), row(BRANCH_W), row(BRANCH_W, 1), row(N_BRANCH * D_MODEL), row(D_MODEL),
                _mod_spec(layer, lambda i: 1 + i // per_b),
                _layer_spec(wb, layer), _layer_spec(wo, layer)]
    args = [brf, bm, brf, sg, x, mod, wb, wo]
    if final:
        in_specs.append(_full_spec(final_g))
        args.append(final_g)
    return pl.pallas_call(
        functools.partial(_merge_kernel, final),
        grid=(n // MERGE_TM,),
        in_specs=in_specs,
        out_specs=row(D_MODEL),
        out_shape=jax.ShapeDtypeStruct((n, D_MODEL), F32),
        compiler_params=_cparams(),
        name="merge_s",
    )(*args)


def _dft_tables(T):
    def cs(n):
        kt = (np.arange(n)[:, None] * np.arange(n)[None, :]) % n
        ang = 2.0 * np.pi * kt.astype(np.float64) / n
        return np.cos(ang), np.sin(ang)
    ct, st = cs(T)
    cw, sw = cs(F_GROUP_W)
    cts = np.concatenate([ct, -st], axis=1).astype(np.float32)
    csw = np.concatenate([cw, sw], axis=1).astype(np.float32)
    return jnp.asarray(csw).astype(BF16), jnp.asarray(cts).astype(BF16)


def _rope_tables(T):
    half = D_ROPE // 2
    nfreq = half // 2
    inv = ROPE_BASE ** (-np.arange(nfreq, dtype=np.float64) / nfreq)
    t = np.arange(T)
    pos = np.stack([t // GRID_W, t % GRID_W], axis=0).astype(np.float64)
    cos = np.ones((T, LANE), np.float64)
    sa = np.zeros((T, LANE), np.float64)
    sb = np.zeros((T, LANE), np.float64)
    for part in range(2):
        ang = pos[part][:, None] * inv[None, :]
        l1 = ROPE_LANE0 + part * half
        l2 = l1 + nfreq
        cos[:, l1:l1 + nfreq] = np.cos(ang)
        cos[:, l2:l2 + nfreq] = np.cos(ang)
        sa[:, l1:l1 + nfreq] = -np.sin(ang)
        sb[:, l2:l2 + nfreq] = np.sin(ang)
    return tuple(jnp.asarray(a.astype(np.float32)) for a in (cos, sa, sb))


KR_SRC = 2176
W_IN_RUNS = ((0, 1024),
             (1024, 512), (2208, 512), (3232, 512),
             (3744, 3072),
             (1536, 384),
             (KR_SRC - ROPE_LANE0, LANE),
             (1920, 256),
             (2720, 512))
W_PREP_TILES = 27
KR_TILE = SEG_KR[0] // LANE


def _w_prep_kernel(tbl_ref, *refs):
    o_ref = refs[-1]
    j = pl.program_id(1)
    lane = lax.broadcasted_iota(jnp.int32, (D_MODEL, LANE), 1)
    rotary = (lane >= ROPE_LANE0) & (lane < ROPE_LANE0 + D_ROPE)
    for t, w_ref in enumerate(refs[:-1]):
        wt = w_ref[0].T
        if t == KR_TILE % W_PREP_TILES:
            wt = jnp.where((j != KR_TILE // W_PREP_TILES) | rotary, wt, 0.0)
        o_ref[:, t * LANE:(t + 1) * LANE] = wt.astype(BF16)


def _permute_w_in(w):
    depth, d, in_w = w.shape
    starts = np.concatenate([src + np.arange(0, width, LANE) for src, width in W_IN_RUNS])
    assert starts.size * LANE == IN_WP and starts.size % W_PREP_TILES == 0
    tbl = jnp.asarray(starts, jnp.int32)
    tile_spec = lambda t: pl.BlockSpec(
        (pl.Element(1), pl.Element(LANE), pl.Element(d)),
        lambda l, j, tbl_ref: (l, pl.multiple_of(tbl_ref[j * W_PREP_TILES + t], D_ROPE), 0))
    wide = W_PREP_TILES * LANE
    return pl.pallas_call(
        _w_prep_kernel,
        grid_spec=pltpu.PrefetchScalarGridSpec(
            num_scalar_prefetch=1,
            grid=(depth, IN_WP // wide),
            in_specs=[tile_spec(t) for t in range(W_PREP_TILES)],
            out_specs=pl.BlockSpec((None, d, wide), lambda l, j, tbl_ref: (l, 0, j)),
        ),
        out_shape=jax.ShapeDtypeStruct((depth, d, IN_WP), BF16),
        compiler_params=_cparams(2),
        name="w_prep",
    )(tbl, *([jnp.swapaxes(w, 1, 2)] * W_PREP_TILES))


def _pad_heads(w, n_heads, width, lo, hi):
    d, k = w.shape[:2]
    wh = w.reshape(d, k, n_heads, width)[..., lo:hi]
    wh = jnp.pad(wh, ((0, 0), (0, 0), (0, 0), (0, LANE - (hi - lo))))
    return wh.reshape(d, k, n_heads * LANE).astype(BF16)


def kernel(x_prompt, x_sample, cache_ckv, cache_krope, state_ret, c, c_ctx, norm_g, w_mod, b_mod,
           w_in, ret_decay_logit, q_norm_g, w_q_up, kv_norm_g, w_kv_up, w_branch, w_out,
           final_norm_g):
    cv = jnp.concatenate([c_ctx[None, :], c, jnp.zeros((MOD_ROWS - 1 - DEC_BATCH, D_MODEL), F32)], axis=0)
    mod = _mod_call(cv, w_mod, b_mod).reshape(DEPTH, MOD_ROWS, 1, 3 * D_MODEL)

    cache_kr = jnp.pad(cache_krope, ((0, 0), (0, 0), (0, 0), (ROPE_LANE0, LANE - ROPE_LANE0 - D_ROPE)))
    rope = _rope_tables(DEC_SEQ)
    dft_p = _dft_tables(SEQ)
    dft_s = _dft_tables(DEC_SEQ)
    logit = jnp.pad(ret_decay_logit, ((0, 0), (0, 8 - 2), (0, LANE - H_RET)))
    g, qg, kvg = norm_g[:, None, :], q_norm_g[:, None, :], kv_norm_g[:, None, :]
    w_p = _permute_w_in(w_in)
    wq = _pad_heads(w_q_up, H_MLA, D_NOPE + D_ROPE, 0, D_NOPE + D_ROPE)
    wk = _pad_heads(w_kv_up, H_MLA, D_NOPE + D_VMLA, 0, D_NOPE)
    wv = w_kv_up.reshape(DEPTH, KV_LORA, H_MLA, D_NOPE + D_VMLA)[..., D_NOPE:]
    wv = wv.reshape(DEPTH, KV_LORA, MLA_W).astype(BF16)
    wb = w_branch.astype(BF16)
    wo = w_out.astype(BF16)
    final_g = final_norm_g[None, :]

    xp = x_prompt.reshape(BATCH * SEQ, D_MODEL)
    xs = x_sample.reshape(DEC_BATCH * DEC_SEQ, D_MODEL)
    new_ctx = None
    for l in range(DEPTH):
        fg = final_g if l == DEPTH - 1 else None
        xp, *new_ctx = _prompt_call(l, xp, mod, g, qg, kvg, w_p, logit, wq, wk, wv, *dft_p, wb, wo,
                                    final_g=fg, prev=new_ctx)

        qkv, sz, sg, qn, ckv, kr, fu = _in_call(l, xs, mod, g, qg, kvg, w_p)
        brf = _ret_fourier_call(l, qkv, sz, fu, logit, state_ret, *dft_s)
        bm = _mla_call(l, qn, ckv, kr, sz, wq, wk, wv, cache_ckv, cache_kr, *rope)
        xs = _merge_call(l, brf, bm, sg, xs, mod, wb, wo, final_g=fg)

    y_prompt = xp.reshape(BATCH, SEQ, D_MODEL)
    y_sample = xs.reshape(DEC_BATCH, DEC_SEQ, D_MODEL)
    new_ckv, new_krope_t, new_ret = new_ctx
    return (y_prompt, y_sample, new_ckv, jnp.swapaxes(new_krope_t, 2, 3), new_ret)
```

```python
import collections
import functools
import math

import numpy as np
import jax
import jax.numpy as jnp
from jax import lax
from jax.experimental import pallas as pl
from jax.experimental.pallas import tpu as pltpu

F32 = jnp.float32
BF16 = jnp.bfloat16

D_MODEL = 1024
BATCH = 32
SEQ = 256
DEPTH = 2
DEC_BATCH = 4
DEC_SEQ = 1024
PAST_LEN = 512
GRID_W = 64
EPS = 1e-6
H_RET = 4
DK_RET = 64
DV_RET = 128
RET_W = H_RET * DV_RET
H_MLA = 8
Q_LORA = 384
KV_LORA = 256
D_NOPE = 64
D_ROPE = 32
D_VMLA = 64
MLA_W = H_MLA * D_VMLA
ROPE_BASE = 10000.0
F_GROUPS = 4
F_GROUP_W = 128
FOURIER_W = F_GROUPS * F_GROUP_W
N_BRANCH = 3
BRANCH_W = 512

LANE = 128
TM = 512
MERGE_TM = 512
MERGE_IN_BUFFERS = 3
IN_CHAINS = 2
PROMPT_SPS = 2
SAMPLE_SPS = 1
MOD_ROWS = 8
ROPE_LANE0 = D_NOPE
VMEM_LIMIT = 60 * 1024 * 1024
ST_TAIL = (2, H_RET, DK_RET, DV_RET)

SEG_QKV = (0, 1024)
SEG_Z = (1024, 2560)
SEG_G = (2560, 5632)
SEG_QL = (5632, 6016)
SEG_KR = (6016, 6144)
SEG_KV = (6144, 6400)
SEG_FU = (6400, 6912)
IN_WP = 6912

NT = (((1,), (1,)), ((), ()))
TN = (((0,), (0,)), ((), ()))


def _cparams(n_axes=1):
    return pltpu.CompilerParams(dimension_semantics=("arbitrary",) * n_axes,
                                vmem_limit_bytes=VMEM_LIMIT)


def _layer_spec(a, layer):
    shape = a.shape[1:]
    return pl.BlockSpec((None,) + shape, lambda i: (layer,) + (0,) * len(shape))


def _full_spec(a):
    return pl.BlockSpec(a.shape, lambda i: (0,) * a.ndim)


def _mod_spec(layer, row):
    return pl.BlockSpec((None, None, 1, 3 * D_MODEL), lambda i: (layer, row(i), 0, 0))


def _grow_spec(depth, tail):
    return pl.BlockSpec((PROMPT_SPS, depth) + tail, lambda i: (i,) + (0,) * (1 + len(tail)))


def _dot(a, b):
    return jnp.dot(a, b, preferred_element_type=F32)


def _rms(x, g):
    return x * lax.rsqrt(jnp.mean(x * x, axis=-1, keepdims=True) + EPS) * g


def _mod_kernel(c_ref, w_ref, b_ref, o_ref):
    cv = c_ref[...]
    s = cv * jax.nn.sigmoid(cv)
    o_ref[0] = _dot(s.astype(BF16), w_ref[0].astype(BF16)) + b_ref[0]


def _mod_call(cv, w_mod, b_mod):
    nb = 3 * D_MODEL // 1024
    return pl.pallas_call(
        _mod_kernel,
        grid=(DEPTH, nb),
        in_specs=[
            pl.BlockSpec((MOD_ROWS, D_MODEL), lambda l, j: (0, 0)),
            pl.BlockSpec((1, D_MODEL, 1024), lambda l, j: (l, 0, j)),
            pl.BlockSpec((1, 1, 1024), lambda l, j: (l, 0, j)),
        ],
        out_specs=pl.BlockSpec((1, MOD_ROWS, 1024), lambda l, j: (l, 0, j)),
        out_shape=jax.ShapeDtypeStruct((DEPTH, MOD_ROWS, 3 * D_MODEL), F32),
        compiler_params=_cparams(2),
        name="mod",
    )(cv, w_mod, b_mod.reshape(DEPTH, 1, 3 * D_MODEL))


Proj = collections.namedtuple("Proj", "q k v sz sg qn ckv kr fu")


def _modulate(x, mod_ref, g_ref):
    shift = mod_ref[:, 0:D_MODEL]
    scale = mod_ref[:, D_MODEL:2 * D_MODEL]
    return (_rms(x, g_ref[...]) * (1.0 + scale) + shift).astype(BF16)


def _project(x, mod_ref, g_ref, qg_ref, kvg_ref, w_ref):
    return _project_h(_modulate(x, mod_ref, g_ref), qg_ref, kvg_ref, w_ref)


def _project_h(hb, qg_ref, kvg_ref, w_ref):
    def mm(a, b):
        return _dot(hb, w_ref[:, a:b])

    qw = H_RET * DK_RET
    p = mm(*SEG_QKV)
    q = p[:, 0:qw].astype(BF16)
    k = (p[:, qw:2 * qw] * (DK_RET ** -0.5)).astype(BF16)
    v = p[:, 2 * qw:].astype(BF16)
    sz = []
    for j in range(3):
        a = SEG_Z[0] + j * BRANCH_W
        p = mm(a, a + BRANCH_W)
        sz.append(p * jax.nn.sigmoid(p))
    sg = []
    for j in range(N_BRANCH):
        a = SEG_G[0] + j * D_MODEL
        sg.append(jax.nn.sigmoid(mm(a, a + D_MODEL)))
    p = mm(SEG_QL[0], SEG_KR[1])
    qn = _rms(p[:, :Q_LORA], qg_ref[...]).astype(BF16)
    kr = p[:, Q_LORA:]
    ckv = _rms(mm(*SEG_KV), kvg_ref[...])
    fu = mm(*SEG_FU).astype(BF16)
    return Proj(q, k, v, sz, sg, qn, ckv, kr, fu)


RET_BLK = 256


def _init_masks(n_blk, lg, mask_ref):
    B = RET_BLK
    ii = lax.broadcasted_iota(jnp.int32, (B, B), 0)
    jj = lax.broadcasted_iota(jnp.int32, (B, B), 1)
    d = (ii - jj).astype(F32)
    ad = jnp.abs(d)
    for h in range(H_RET):
        lf = lg[0:1, h:h + 1]
        lb = lg[1:2, h:h + 1]
        mask_ref[h, 0] = jnp.where(d > 0, jnp.exp(ad * lf), jnp.where(d < 0, jnp.exp(ad * lb), 2.0))
        if n_blk > 1:
            mask_ref[h, 1] = jnp.exp((B + d) * lf)
            mask_ref[h, 2] = jnp.exp((B - d) * lb)


def _masked_scores(s, mask, lf, lb):
    B = RET_BLK
    n_blk = s.shape[0] // B
    if n_blk == 1:
        return (s * mask[0]).astype(BF16)
    rows = []
    for i in range(n_blk):
        cols = []
        for j in range(n_blk):
            blk = s[i * B:(i + 1) * B, j * B:(j + 1) * B]
            far = abs(i - j) - 1
            if far < 0:
                m = mask[0]
            else:
                m = mask[1] if j < i else mask[2]
                if far:
                    m = m * jnp.exp((far * B) * (lf if j < i else lb))
            cols.append((blk * m).astype(BF16))
        rows.append(jnp.concatenate(cols, axis=1))
    return jnp.concatenate(rows, axis=0)


def _ret_head(T, q, k, v, mask, lf, lb, st=None):
    s = lax.dot_general(q, k, NT, preferred_element_type=F32)
    o = _dot(_masked_scores(s, mask, lf, lb), v)
    if st is not None:
        c = _dot(q, jnp.concatenate([st[0], st[1]], axis=1).astype(BF16))
        rows = lax.broadcasted_iota(jnp.int32, (T, DV_RET), 0).astype(F32)
        o = o + jnp.exp((rows + 1.0) * lf) * c[:, :DV_RET]
        o = o + jnp.exp((T - rows) * lb) * c[:, DV_RET:]
    mu = jnp.mean(o, axis=-1, keepdims=True)
    c = o - mu
    var = jnp.mean(c * c, axis=-1, keepdims=True)
    return c * lax.rsqrt(var + EPS)


def _ret_states(T, k, v, lf, lb):
    rows = lax.broadcasted_iota(jnp.int32, (T, DK_RET), 0).astype(F32)
    kf = (k.astype(F32) * jnp.exp((T - 1.0 - rows) * lf)).astype(BF16)
    kb = (k.astype(F32) * jnp.exp(rows * lb)).astype(BF16)
    return (lax.dot_general(kf, v, TN, preferred_element_type=F32),
            lax.dot_general(kb, v, TN, preferred_element_type=F32))


def _rope(x, cos, sa, sb):
    half = D_ROPE // 4
    return x * cos + pltpu.roll(x, LANE - half, 1) * sa + pltpu.roll(x, half, 1) * sb


Q_SCALE = (D_NOPE + D_ROPE) ** -0.5 * math.log2(math.e)


def _value_slots(vp):
    first = lax.broadcasted_iota(jnp.int32, vp.shape, 1) < D_VMLA
    return (jnp.where(first, vp, 1.0).astype(BF16), jnp.where(first, 1.0, vp).astype(BF16))


def _attend_pair(q2, k2, v2):
    outs = []
    for qh, kh, vh in zip(q2, k2, v2):
        s = lax.dot_general(qh, kh, NT, preferred_element_type=F32)
        e = jnp.exp2((s - jnp.max(s, axis=-1, keepdims=True)).astype(BF16))
        outs.append(_dot(e, vh))
    first = lax.broadcasted_iota(jnp.int32, outs[0].shape, 1) < D_VMLA
    o = jnp.where(first, outs[0], outs[1])
    l = jnp.where(first, pltpu.roll(outs[0], D_VMLA, 1), pltpu.roll(outs[1], D_VMLA, 1))
    return o / l


def _fourier(T, fu, csw_ref, cts_ref):
    a, b = [], []
    for g in range(F_GROUPS):
        z = _dot(fu[:, g * F_GROUP_W:(g + 1) * F_GROUP_W], csw_ref[...])
        a.append(z[:, :F_GROUP_W])
        b.append(z[:, F_GROUP_W:])
    a = jnp.concatenate(a, axis=1).astype(BF16)
    b = jnp.concatenate(b, axis=1).astype(BF16)
    o = _dot(cts_ref[:, 0:T], a) + _dot(cts_ref[:, T:2 * T], b)
    return o * ((T * F_GROUP_W) ** -0.5)


def _merge(branches, sg, x, mod_ref, wb_ref, wo_ref, fg_ref):
    merged = None
    for n, b in enumerate(branches):
        term = _dot(b, wb_ref[n]) * sg[n]
        merged = term if merged is None else merged + term
    gate = mod_ref[:, 2 * D_MODEL:3 * D_MODEL]
    y = x + gate * _dot(merged.astype(BF16), wo_ref[...])
    if fg_ref is not None:
        y = _rms(y, fg_ref[...])
    return y


def _prompt_kernel(layer, final, *refs):
    (x_ref, mod_ref, g_ref, qg_ref, kvg_ref, w_ref, lg_ref, wq_ref, wk_ref, wv_ref,
     csw_ref, cts_ref, wb_ref, wo_ref) = refs[:14]
    n_in = 14
    fg_ref = None
    if final:
        fg_ref = refs[n_in]
        n_in += 1
    if layer:
        pckv_ref, pkr_ref, pst_ref = refs[n_in:n_in + 3]
        n_in += 3
    y_ref, nckv_ref, nkr_ref, nst_ref, mask_ref = refs[n_in:]
    T = SEQ
    lg = jax.nn.log_sigmoid(lg_ref[...])

    @pl.when(pl.program_id(0) == 0)
    def _():
        _init_masks(T // RET_BLK, lg, mask_ref)

    seqs = range(PROMPT_SPS)
    rows = [slice(s * T, (s + 1) * T) for s in seqs]
    slot = lambda a, i: a[:, i * LANE:(i + 1) * LANE]
    xs = [x_ref[r] for r in rows]
    prs = [_project(x, mod_ref, g_ref, qg_ref, kvg_ref, w_ref) for x in xs]

    for s, pr in zip(seqs, prs):
        if layer:
            nckv_ref[s, 0:layer] = pckv_ref[s]
            nkr_ref[s, 0:layer] = pkr_ref[s]
            nst_ref[s, 0:layer] = pst_ref[s]
        nckv_ref[s, layer] = pr.ckv
        nkr_ref[s, layer] = pr.kr.T[ROPE_LANE0:ROPE_LANE0 + D_ROPE]

    qs, ks, vs = [], [], []
    for pr in prs:
        ckvb = pr.ckv.astype(BF16)
        q_all = _dot(pr.qn, wq_ref[...]) * Q_SCALE
        k_all = _dot(ckvb, wk_ref[...])
        vs.append(_dot(ckvb, wv_ref[...]))
        qs.append([slot(q_all, h).astype(BF16) for h in range(H_MLA)])
        ks.append([(slot(k_all, h) + pr.kr).astype(BF16) for h in range(H_MLA)])
    o_m = [[] for _ in seqs]
    for p in range(H_MLA // 2):
        for s in seqs:
            o_m[s].append(_attend_pair(qs[s][2 * p:2 * p + 2], ks[s][2 * p:2 * p + 2],
                                       _value_slots(slot(vs[s], p))))

    o_f = [_fourier(T, pr.fu, csw_ref, cts_ref) for pr in prs]

    o_r = [[] for _ in seqs]
    for h in range(H_RET):
        lf = lg[0:1, h:h + 1]
        lb = lg[1:2, h:h + 1]
        for s, pr in zip(seqs, prs):
            q = pr.q[:, h * DK_RET:(h + 1) * DK_RET]
            k = pr.k[:, h * DK_RET:(h + 1) * DK_RET]
            v = pr.v[:, h * DV_RET:(h + 1) * DV_RET]
            o_r[s].append(_ret_head(T, q, k, v, mask_ref.at[h], lf, lb))
            sf, sb = _ret_states(T, k, v, lf, lb)
            nst_ref[s, layer, 0, h] = sf
            nst_ref[s, layer, 1, h] = sb

    for s, pr in zip(seqs, prs):
        branches = [(o * gate).astype(BF16) for o, gate in zip(
            (jnp.concatenate(o_r[s], axis=1), jnp.concatenate(o_m[s], axis=1), o_f[s]), pr.sz)]
        y_ref[rows[s]] = _merge(branches, pr.sg, xs[s], mod_ref, wb_ref, wo_ref, fg_ref)


def _prompt_call(layer, x, mod, g, qg, kvg, w_p, logit, wq, wk, wv, csw, cts, wb, wo,
                 final_g=None, prev=None):
    T = SEQ
    n_steps = BATCH // PROMPT_SPS
    row = pl.BlockSpec((PROMPT_SPS * T, D_MODEL), lambda i: (i, 0))
    in_specs = [row, _mod_spec(layer, lambda i: 0)]
    in_specs += [_layer_spec(a, layer) for a in (g, qg, kvg, w_p, logit, wq, wk, wv)]
    in_specs += [_full_spec(csw), _full_spec(cts), _layer_spec(wb, layer), _layer_spec(wo, layer)]
    args = [x, mod, g, qg, kvg, w_p, logit, wq, wk, wv, csw, cts, wb, wo]
    if final_g is not None:
        in_specs.append(_full_spec(final_g))
        args.append(final_g)
    tails = ((SEQ, KV_LORA), (D_ROPE, SEQ), ST_TAIL)
    if layer:
        in_specs += [_grow_spec(layer, t) for t in tails]
        args += list(prev)
    return pl.pallas_call(
        functools.partial(_prompt_kernel, layer, final_g is not None),
        grid=(n_steps,),
        in_specs=in_specs,
        out_specs=[row] + [_grow_spec(layer + 1, t) for t in tails],
        out_shape=[jax.ShapeDtypeStruct((BATCH * T, D_MODEL), F32)]
        + [jax.ShapeDtypeStruct((BATCH, layer + 1) + t, F32) for t in tails],
        scratch_shapes=[pltpu.VMEM((H_RET, 1, RET_BLK, RET_BLK), F32)],
        compiler_params=_cparams(),
        name="prompt_layer",
    )(*args)


def _in_kernel(x_ref, mod_ref, g_ref, qg_ref, kvg_ref, w_ref,
               qkv_ref, sz_ref, sg_ref, qn_ref, ckv_ref, kr_ref, fu_ref):
    qw = H_RET * DK_RET
    half = TM // IN_CHAINS
    for c in range(IN_CHAINS):
        rows = slice(c * half, (c + 1) * half)
        pr = _project(x_ref[rows], mod_ref, g_ref, qg_ref, kvg_ref, w_ref)
        qkv_ref[rows, 0:qw] = pr.q
        qkv_ref[rows, qw:2 * qw] = pr.k
        qkv_ref[rows, 2 * qw:] = pr.v
        for j in range(3):
            sz_ref[rows, j * BRANCH_W:(j + 1) * BRANCH_W] = pr.sz[j].astype(BF16)
        for j in range(N_BRANCH):
            sg_ref[rows, j * D_MODEL:(j + 1) * D_MODEL] = pr.sg[j].astype(BF16)
        qn_ref[rows] = pr.qn
        ckv_ref[rows] = pr.ckv.astype(BF16)
        kr_ref[rows] = pr.kr
        fu_ref[rows] = pr.fu


def _in_call(layer, x, mod, g, qg, kvg, w_p):
    n = x.shape[0]
    per_b = DEC_SEQ // TM
    row = lambda w: pl.BlockSpec((TM, w), lambda i: (i, 0))
    outs = [(1024, BF16), (3 * BRANCH_W, BF16), (N_BRANCH * D_MODEL, BF16), (Q_LORA, BF16),
            (KV_LORA, BF16), (LANE, F32), (FOURIER_W, BF16)]
    return pl.pallas_call(
        _in_kernel,
        grid=(n // TM,),
        in_specs=[row(D_MODEL), _mod_spec(layer, lambda i: 1 + i // per_b)]
        + [_layer_spec(a, layer) for a in (g, qg, kvg, w_p)],
        out_specs=[row(w) for w, _ in outs],
        out_shape=[jax.ShapeDtypeStruct((n, w), dt) for w, dt in outs],
        compiler_params=_cparams(),
        name="in_proj_s",
    )(x, mod, g, qg, kvg, w_p)


def _sample_rows(width, col=0):
    return pl.BlockSpec((SAMPLE_SPS * DEC_SEQ, width), lambda b: (b, col))


def _sample_cached(layer, tail):
    return pl.BlockSpec((SAMPLE_SPS, None) + tail, lambda b: (b, layer) + (0,) * len(tail))


def _ret_fourier_kernel(T, qkv_ref, rz_ref, fz_ref, fu_ref, lg_ref, st_ref, csw_ref, cts_ref,
                        o_ref, mask_ref):
    lg = jax.nn.log_sigmoid(lg_ref[...])

    @pl.when(pl.program_id(0) == 0)
    def _():
        _init_masks(T // RET_BLK, lg, mask_ref)

    qw = H_RET * DK_RET
    for s in range(SAMPLE_SPS):
        rows = slice(s * T, (s + 1) * T)
        for h in range(H_RET):
            q = qkv_ref[rows, h * DK_RET:(h + 1) * DK_RET]
            k = qkv_ref[rows, qw + h * DK_RET:qw + (h + 1) * DK_RET]
            v = qkv_ref[rows, 2 * qw + h * DV_RET:2 * qw + (h + 1) * DV_RET]
            on = _ret_head(T, q, k, v, mask_ref.at[h], lg[0:1, h:h + 1], lg[1:2, h:h + 1],
                           st=(st_ref[s, 0, h], st_ref[s, 1, h]))
            gate = rz_ref[rows, h * DV_RET:(h + 1) * DV_RET].astype(F32)
            o_ref[rows, h * DV_RET:(h + 1) * DV_RET] = (on * gate).astype(BF16)
        o = _fourier(T, fu_ref[rows], csw_ref, cts_ref)
        o_ref[rows, RET_W:] = (o * fz_ref[rows].astype(F32)).astype(BF16)


def _ret_fourier_call(layer, qkv, sz, fu, logit, state, csw, cts):
    T = DEC_SEQ
    return pl.pallas_call(
        functools.partial(_ret_fourier_kernel, T),
        grid=(DEC_BATCH // SAMPLE_SPS,),
        in_specs=[_sample_rows(1024), _sample_rows(BRANCH_W, 0), _sample_rows(BRANCH_W, 2),
                  _sample_rows(FOURIER_W), _layer_spec(logit, layer),
                  _sample_cached(layer, ST_TAIL), _full_spec(csw), _full_spec(cts)],
        out_specs=_sample_rows(RET_W + FOURIER_W),
        out_shape=jax.ShapeDtypeStruct((T * DEC_BATCH, RET_W + FOURIER_W), BF16),
        scratch_shapes=[pltpu.VMEM((H_RET, 3, RET_BLK, RET_BLK), F32)],
        compiler_params=_cparams(),
        name="ret_fourier_s",
    )(qkv, sz, sz, fu, logit, state, csw, cts)


def _mla_kernel(T, qn_ref, ckv_ref, kr_ref, mz_ref, wq_ref, wk_ref, wv_ref, cckv_ref, ckr_ref,
                cos_ref, sa_ref, sb_ref, bm_ref):
    cos, sa, sb = cos_ref[...], sa_ref[...], sb_ref[...]
    slot = lambda a, i: a[:, i * LANE:(i + 1) * LANE]
    for s in range(SAMPLE_SPS):
        rows = slice(s * T, (s + 1) * T)
        qn = qn_ref[rows]
        ckv = ckv_ref[rows]
        kr = _rope(kr_ref[rows], cos, sa, sb)
        cc = cckv_ref[s].astype(BF16)
        for p in range(H_MLA // 2):
            sl = slice(2 * p * LANE, 2 * (p + 1) * LANE)
            q2 = _dot(qn, wq_ref[:, sl])
            kc2 = _dot(cc, wk_ref[:, sl])
            kn2 = _dot(ckv, wk_ref[:, sl])
            if p % 2 == 0:
                vsl = slice(p * LANE, (p + 2) * LANE)
                v4 = jnp.concatenate([_dot(cc, wv_ref[:, vsl]), _dot(ckv, wv_ref[:, vsl])], axis=0)
            qs = [(_rope(slot(q2, i), cos, sa, sb) * Q_SCALE).astype(BF16) for i in range(2)]
            ks = [jnp.concatenate([(slot(kc2, i) + ckr_ref[s]).astype(BF16),
                                   (slot(kn2, i) + kr).astype(BF16)], axis=0) for i in range(2)]
            o = _attend_pair(qs, ks, _value_slots(slot(v4, p % 2)))
            gate = mz_ref[rows, p * LANE:(p + 1) * LANE].astype(F32)
            bm_ref[rows, p * LANE:(p + 1) * LANE] = (o * gate).astype(BF16)


def _mla_call(layer, qn, ckv, kr, sz, wq, wk, wv, cache_ckv, cache_kr, cos, sa, sb):
    T = DEC_SEQ
    return pl.pallas_call(
        functools.partial(_mla_kernel, T),
        grid=(DEC_BATCH // SAMPLE_SPS,),
        in_specs=[_sample_rows(Q_LORA), _sample_rows(KV_LORA), _sample_rows(LANE),
                  _sample_rows(BRANCH_W, 1)]
        + [_layer_spec(a, layer) for a in (wq, wk, wv)]
        + [_sample_cached(layer, (PAST_LEN, KV_LORA)), _sample_cached(layer, (PAST_LEN, LANE)),
           _full_spec(cos), _full_spec(sa), _full_spec(sb)],
        out_specs=_sample_rows(MLA_W),
        out_shape=jax.ShapeDtypeStruct((T * DEC_BATCH, MLA_W), BF16),
        compiler_params=_cparams(),
        name="mla_s",
    )(qn, ckv, kr, sz, wq, wk, wv, cache_ckv, cache_kr, cos, sa, sb)


def _gate_copy(sg_hbm, sg_buf, sem, block):
    slot = block % MERGE_IN_BUFFERS
    return pltpu.make_async_copy(sg_hbm.at[pl.ds(block * MERGE_TM, MERGE_TM)],
                                 sg_buf.at[slot], sem.at[slot])


def _merge_kernel(final, *refs):
    br_ref, bm_ref, bf_ref, sg_hbm, x_ref, mod_ref, wb_ref, wo_ref = refs[:8]
    fg_ref = refs[8] if final else None
    y_ref, sg_buf, sem = refs[-3:]
    i = pl.program_id(0)
    ahead = MERGE_IN_BUFFERS - 1

    @pl.when(i == 0)
    def _():
        for j in range(ahead):
            _gate_copy(sg_hbm, sg_buf, sem, j).start()

    @pl.when(i + ahead < pl.num_programs(0))
    def _():
        _gate_copy(sg_hbm, sg_buf, sem, i + ahead).start()

    _gate_copy(sg_hbm, sg_buf, sem, i).wait()
    slot = i % MERGE_IN_BUFFERS
    sg = [sg_buf[slot, :, n * D_MODEL:(n + 1) * D_MODEL].astype(F32) for n in range(N_BRANCH)]
    y_ref[...] = _merge((br_ref[...], bm_ref[...], bf_ref[...]), sg, x_ref[...],
                        mod_ref, wb_ref, wo_ref, fg_ref)


def _merge_call(layer, brf, bm, sg, x, mod, wb, wo, final_g=None):
    final = final_g is not None
    n = x.shape[0]
    per_b = DEC_SEQ // MERGE_TM
    row = lambda w, c=0: pl.BlockSpec((MERGE_TM, w), lambda i: (i, c))
    assert n // MERGE_TM >= MERGE_IN_BUFFERS - 1
    in_specs = [row(BRANCH_W), row(BRANCH_W), row(BRANCH_W, 1),
                pl.BlockSpec(memory_space=pl.ANY), row(D_MODEL),
                _mod_spec(layer, lambda i: 1 + i // per_b),
                _layer_spec(wb, layer), _layer_spec(wo, layer)]
    args = [brf, bm, brf, sg, x, mod, wb, wo]
    if final:
        in_specs.append(_full_spec(final_g))
        args.append(final_g)
    return pl.pallas_call(
        functools.partial(_merge_kernel, final),
        grid=(n // MERGE_TM,),
        in_specs=in_specs,
        out_specs=row(D_MODEL),
        out_shape=jax.ShapeDtypeStruct((n, D_MODEL), F32),
        scratch_shapes=[pltpu.VMEM((MERGE_IN_BUFFERS, MERGE_TM, N_BRANCH * D_MODEL), BF16),
                        pltpu.SemaphoreType.DMA((MERGE_IN_BUFFERS,))],
        compiler_params=_cparams(),
        name="merge_s",
    )(*args)


def _dft_tables(T):
    def cs(n):
        kt = (np.arange(n)[:, None] * np.arange(n)[None, :]) % n
        ang = 2.0 * np.pi * kt.astype(np.float64) / n
        return np.cos(ang), np.sin(ang)
    ct, st = cs(T)
    cw, sw = cs(F_GROUP_W)
    cts = np.concatenate([ct, -st], axis=1).astype(np.float32)
    csw = np.concatenate([cw, sw], axis=1).astype(np.float32)
    return jnp.asarray(csw).astype(BF16), jnp.asarray(cts).astype(BF16)


def _rope_tables(T):
    half = D_ROPE // 2
    nfreq = half // 2
    inv = ROPE_BASE ** (-np.arange(nfreq, dtype=np.float64) / nfreq)
    t = np.arange(T)
    pos = np.stack([t // GRID_W, t % GRID_W], axis=0).astype(np.float64)
    cos = np.ones((T, LANE), np.float64)
    sa = np.zeros((T, LANE), np.float64)
    sb = np.zeros((T, LANE), np.float64)
    for part in range(2):
        ang = pos[part][:, None] * inv[None, :]
        l1 = ROPE_LANE0 + part * half
        l2 = l1 + nfreq
        cos[:, l1:l1 + nfreq] = np.cos(ang)
        cos[:, l2:l2 + nfreq] = np.cos(ang)
        sa[:, l1:l1 + nfreq] = -np.sin(ang)
        sb[:, l2:l2 + nfreq] = np.sin(ang)
    return tuple(jnp.asarray(a.astype(np.float32)) for a in (cos, sa, sb))


KR_SRC = 2176
W_IN_RUNS = ((0, 1024),
             (1024, 512), (2208, 512), (3232, 512),
             (3744, 3072),
             (1536, 384),
             (KR_SRC - ROPE_LANE0, LANE),
             (1920, 256),
             (2720, 512))
W_PREP_TILES = 18
KR_TILE = SEG_KR[0] // LANE


def _w_prep_kernel(tbl_ref, *refs):
    o_ref = refs[-1]
    j = pl.program_id(1)
    lane = lax.broadcasted_iota(jnp.int32, (D_MODEL, LANE), 1)
    rotary = (lane >= ROPE_LANE0) & (lane < ROPE_LANE0 + D_ROPE)
    for t, w_ref in enumerate(refs[:-1]):
        wt = w_ref[0].T
        if t == KR_TILE % W_PREP_TILES:
            wt = jnp.where((j != KR_TILE // W_PREP_TILES) | rotary, wt, 0.0)
        o_ref[:, t * LANE:(t + 1) * LANE] = wt.astype(BF16)


def _permute_w_in(w):
    depth, d, in_w = w.shape
    starts = np.concatenate([src + np.arange(0, width, LANE) for src, width in W_IN_RUNS])
    assert starts.size * LANE == IN_WP and starts.size % W_PREP_TILES == 0
    tbl = jnp.asarray(starts, jnp.int32)
    tile_spec = lambda t: pl.BlockSpec(
        (pl.Element(1), pl.Element(LANE), pl.Element(d)),
        lambda l, j, tbl_ref: (l, pl.multiple_of(tbl_ref[j * W_PREP_TILES + t], D_ROPE), 0))
    wide = W_PREP_TILES * LANE
    return pl.pallas_call(
        _w_prep_kernel,
        grid_spec=pltpu.PrefetchScalarGridSpec(
            num_scalar_prefetch=1,
            grid=(depth, IN_WP // wide),
            in_specs=[tile_spec(t) for t in range(W_PREP_TILES)],
            out_specs=pl.BlockSpec((None, d, wide), lambda l, j, tbl_ref: (l, 0, j)),
        ),
        out_shape=jax.ShapeDtypeStruct((depth, d, IN_WP), BF16),
        compiler_params=_cparams(2),
        name="w_prep",
    )(tbl, *([jnp.swapaxes(w, 1, 2)] * W_PREP_TILES))


def _pad_heads(w, n_heads, width, lo, hi):
    d, k = w.shape[:2]
    wh = w.reshape(d, k, n_heads, width)[..., lo:hi]
    wh = jnp.pad(wh, ((0, 0), (0, 0), (0, 0), (0, LANE - (hi - lo))))
    return wh.reshape(d, k, n_heads * LANE).astype(BF16)


def kernel(x_prompt, x_sample, cache_ckv, cache_krope, state_ret, c, c_ctx, norm_g, w_mod, b_mod,
           w_in, ret_decay_logit, q_norm_g, w_q_up, kv_norm_g, w_kv_up, w_branch, w_out,
           final_norm_g):
    cv = jnp.concatenate([c_ctx[None, :], c, jnp.zeros((MOD_ROWS - 1 - DEC_BATCH, D_MODEL), F32)], axis=0)
    mod = _mod_call(cv, w_mod, b_mod).reshape(DEPTH, MOD_ROWS, 1, 3 * D_MODEL)

    cache_kr = jnp.pad(cache_krope, ((0, 0), (0, 0), (0, 0), (ROPE_LANE0, LANE - ROPE_LANE0 - D_ROPE)))
    rope = _rope_tables(DEC_SEQ)
    dft_p = _dft_tables(SEQ)
    dft_s = _dft_tables(DEC_SEQ)
    logit = jnp.pad(ret_decay_logit, ((0, 0), (0, 8 - 2), (0, LANE - H_RET)))
    g, qg, kvg = norm_g[:, None, :], q_norm_g[:, None, :], kv_norm_g[:, None, :]
    w_p = _permute_w_in(w_in)
    wq = _pad_heads(w_q_up, H_MLA, D_NOPE + D_ROPE, 0, D_NOPE + D_ROPE)
    wk = _pad_heads(w_kv_up, H_MLA, D_NOPE + D_VMLA, 0, D_NOPE)
    wv = w_kv_up.reshape(DEPTH, KV_LORA, H_MLA, D_NOPE + D_VMLA)[..., D_NOPE:]
    wv = wv.reshape(DEPTH, KV_LORA, MLA_W).astype(BF16)
    wb = w_branch.astype(BF16)
    wo = w_out.astype(BF16)
    final_g = final_norm_g[None, :]

    xp = x_prompt.reshape(BATCH * SEQ, D_MODEL)
    xs = x_sample.reshape(DEC_BATCH * DEC_SEQ, D_MODEL)
    new_ctx = None
    for l in range(DEPTH):
        fg = final_g if l == DEPTH - 1 else None
        xp, *new_ctx = _prompt_call(l, xp, mod, g, qg, kvg, w_p, logit, wq, wk, wv, *dft_p, wb, wo,
                                    final_g=fg, prev=new_ctx)

        qkv, sz, sg, qn, ckv, kr, fu = _in_call(l, xs, mod, g, qg, kvg, w_p)
        brf = _ret_fourier_call(l, qkv, sz, fu, logit, state_ret, *dft_s)
        bm = _mla_call(l, qn, ckv, kr, sz, wq, wk, wv, cache_ckv, cache_kr, *rope)
        xs = _merge_call(l, brf, bm, sg, xs, mod, wb, wo, final_g=fg)

    y_prompt = xp.reshape(BATCH, SEQ, D_MODEL)
    y_sample = xs.reshape(DEC_BATCH, DEC_SEQ, D_MODEL)
    new_ckv, new_krope_t, new_ret = new_ctx
    return (y_prompt, y_sample, new_ckv, jnp.swapaxes(new_krope_t, 2, 3), new_ret)
```
